```python
import math
import jax
import jax.numpy as jnp
from jax import lax
import numpy as np

D_MODEL = 1024
BATCH = 2
SEQ = 16384
DEPTH = 4

HEAD_DIM = 64
N_HEADS = D_MODEL // HEAD_DIM
ATTN_WIDTH = N_HEADS * HEAD_DIM
ATTN_SCALE = HEAD_DIM ** -0.5
N_MIXERS = 4
Q_BLOCK = 128
ROW_CHUNK = 128
REL_BUCKETS = 32
REL_MAX_DIST = 2048
DILATED_PAIRS = ((128, 1), (512, 4), (2048, 16))
SB_CLIP = 60.0
NSA_KV_HEADS = 4
NSA_GROUP = N_HEADS // NSA_KV_HEADS
NSA_KV_WIDTH = NSA_KV_HEADS * HEAD_DIM
NSA_CMP_LEN = 32
NSA_CMP_STRIDE = 16
NSA_CMP_HIDDEN = 256
NSA_SEL_BLOCK = 64
NSA_TOP_N = 16
NSA_WINDOW = 512
MOBA_BLOCK = 256
MOBA_TOP_K = 3
N_EXPERTS = 16
N_GROUPS = 4
EXPERTS_PER_GROUP = N_EXPERTS // N_GROUPS
MOE_TOP_K = 2
D_EXPERT = 256
MOE_BLOCK = 256
DEEPNORM_ALPHA = (2 * DEPTH) ** 0.25
DEEPNORM_BETA = (8 * DEPTH) ** -0.25
LN_EPS = 1e-5
NEG_INF = -1e30
FORCED_SCORE = 1e9

kernel_name = 'hybrid_sparse_attn_grouped_moe_trunk'


def layer_norm(x, g, b):
    xf = x.astype(jnp.float32)
    mu = xf.mean(-1, keepdims=True)
    var = jnp.square(xf - mu).mean(-1, keepdims=True)
    return ((xf - mu) * lax.rsqrt(var + LN_EPS) * g + b).astype(x.dtype)


def rel_bucket(dist):
    n = jnp.maximum(dist, 0)
    exact = REL_BUCKETS // 2
    logf = jnp.log(jnp.maximum(n, 1).astype(jnp.float32) / exact) / math.log(REL_MAX_DIST / exact)
    large = jnp.minimum(exact + (logf * (REL_BUCKETS - exact)).astype(jnp.int32), REL_BUCKETS - 1)
    return jnp.where(n < exact, n, large)


def rel_bias(table, dist):
    return jnp.moveaxis(table[rel_bucket(dist)], -1, 0).astype(jnp.float32)


def masked_softmax(logits, mask):
    logits = jnp.where(mask, logits.astype(jnp.float32), NEG_INF)
    m = logits.max(-1, keepdims=True)
    p = jnp.where(mask, jnp.exp(logits - m), 0.0)
    denom = jnp.maximum(p.sum(-1, keepdims=True), 1e-30)
    return p / denom, m + jnp.log(denom)


def merge_parts(o, lse):
    w = jax.nn.softmax(lse.astype(jnp.float32), axis=1)
    return jnp.einsum('npr,nprd->nrd', w, o.astype(jnp.float32))


def split_heads(t, h):
    b, s, _ = t.shape
    return t.reshape(b, s, h, HEAD_DIM).transpose(0, 2, 1, 3)


def merge_heads(t):
    b, h, s, d = t.shape
    return t.transpose(0, 2, 1, 3).reshape(b, s, h * d)


def dispatch_attention(q_src, pos_src, gid, k_groups, v_groups, kstart, bias_tbl, tbl_idx):
    n_src, n_slots = gid.shape
    n_groups, blk, _ = k_groups.shape
    R = q_src.shape[1]
    N = n_src * n_slots
    flat_g = gid.reshape(N)
    order = jnp.argsort(flat_g).astype(jnp.int32)
    sorted_g = flat_g[order]
    valid = sorted_g < n_groups
    sg = jnp.minimum(sorted_g, n_groups - 1)
    counts = jnp.zeros((n_groups + 1,), jnp.int32).at[flat_g].add(1)[:n_groups]
    padded = (counts + ROW_CHUNK - 1) // ROW_CHUNK * ROW_CHUNK
    pad_end = jnp.cumsum(padded)
    pad_start = pad_end - padded
    start = jnp.cumsum(counts) - counts
    n_chunks = -(-N // ROW_CHUNK) + n_groups
    P = n_chunks * ROW_CHUNK
    dest = jnp.where(valid, pad_start[sg] + jnp.arange(N) - start[sg], P)
    buf = jnp.full((P,), N, jnp.int32).at[dest].set(order, mode='drop')
    chunk_group = jnp.minimum(jnp.searchsorted(pad_end, jnp.arange(n_chunks) * ROW_CHUNK, side='right'),
                              n_groups - 1).astype(jnp.int32)
    q_pad = jnp.concatenate([q_src, jnp.zeros((1, R, HEAD_DIM), q_src.dtype)], axis=0)
    pos_pad = jnp.concatenate([pos_src, jnp.full((1,), -1, pos_src.dtype)], axis=0)
    off = jnp.arange(blk)

    def chunk(args):
        rows, g = args
        src = rows // n_slots
        qi = q_pad[src]
        dist = pos_pad[src][:, None] - (kstart[g] + off)[None, :]
        bias = bias_tbl[tbl_idx[g]][:, rel_bucket(dist)].astype(jnp.float32)
        logits = jnp.einsum('crd,kd->rck', qi, k_groups[g]).astype(jnp.float32) * ATTN_SCALE + bias
        p, lse = masked_softmax(logits, (dist >= 0)[None])
        o = jnp.einsum('rck,kd->crd', p, v_groups[g].astype(jnp.float32))
        return o, lse[..., 0].T

    o, lse = lax.map(chunk, (buf.reshape(n_chunks, ROW_CHUNK), chunk_group))
    o_all = jnp.zeros((N + 1, R, HEAD_DIM), jnp.float32).at[buf].set(o.reshape(P, R, HEAD_DIM))[:N]
    lse_all = jnp.full((N + 1, R), NEG_INF, jnp.float32).at[buf].set(lse.reshape(P, R))[:N]
    return o_all.reshape(n_src, n_slots, R, HEAD_DIM), lse_all.reshape(n_src, n_slots, R)


def dilated_group(q, k, v, table, window, dilation):
    Bsz, H, S_, _ = q.shape
    L = S_ // dilation
    nb = -(-L // Q_BLOCK)
    Lp = nb * Q_BLOCK
    span = window // dilation
    n_prev = -(-span // Q_BLOCK)

    def streams(t):
        t = t.reshape(Bsz, H, L, dilation, HEAD_DIM).transpose(0, 1, 3, 2, 4)
        t = jnp.pad(t, ((0, 0), (0, 0), (0, 0), (0, Lp - L), (0, 0)))
        return t.reshape(Bsz, H, dilation, nb, Q_BLOCK, HEAD_DIM)

    def band(t):
        tp = jnp.pad(t, ((0, 0), (0, 0), (0, 0), (n_prev, 0), (0, 0), (0, 0)))
        return jnp.concatenate([tp[:, :, :, o:o + nb] for o in range(n_prev + 1)], axis=4)

    qs = streams(q)
    kb, vb = band(streams(k)), band(streams(v))
    a = jnp.arange(Q_BLOCK)
    c = jnp.arange((n_prev + 1) * Q_BLOCK)
    steps = n_prev * Q_BLOCK + a[:, None] - c[None, :]
    key_idx = jnp.arange(nb)[:, None] * Q_BLOCK - n_prev * Q_BLOCK + c[None, :]
    mask = ((steps >= 0) & (steps <= span))[None] & (key_idx >= 0)[:, None, :]
    bias = rel_bias(table, steps * dilation)
    logits = jnp.einsum('bhrnqd,bhrnkd->bhrnqk', qs, kb).astype(jnp.float32) * ATTN_SCALE + bias[:, None, None]
    p, lse = masked_softmax(logits, mask)
    o = jnp.einsum('bhrnqk,bhrnkd->bhrnqd', p.astype(v.dtype), vb)
    o = o.reshape(Bsz, H, dilation, Lp, HEAD_DIM)[:, :, :, :L].transpose(0, 1, 3, 2, 4).reshape(Bsz, H, S_, HEAD_DIM)
    lse = lse[..., 0].reshape(Bsz, H, dilation, Lp)[..., :L].transpose(0, 1, 3, 2).reshape(Bsz, H, S_)
    return o, lse


def mixer_dilated(x, w_in, w_out, table):
    proj = x @ w_in
    gw = 3 * ATTN_WIDTH
    outs, lses = [], []
    for g, (window, dilation) in enumerate(DILATED_PAIRS):
        q, k, v = [split_heads(t, N_HEADS) for t in jnp.split(proj[..., g * gw:(g + 1) * gw], 3, axis=-1)]
        o, lse = dilated_group(q, k, v, table, window, dilation)
        outs.append(o)
        lses.append(lse)
    wts = jax.nn.softmax(jnp.stack(lses), axis=0)
    o = jnp.einsum('gbhs,gbhsd->bhsd', wts, jnp.stack(outs).astype(jnp.float32))
    return merge_heads(o.astype(x.dtype)) @ w_out


def mixer_stick_breaking(x, w_in, w_out):
    q, k, v = [split_heads(t, N_HEADS) for t in jnp.split(x @ w_in, 3, axis=-1)]
    Bsz, H, S_, _ = q.shape
    nb = S_ // Q_BLOCK
    nbp = nb + nb % 2
    pad = ((0, 0), (0, 0), (0, (nbp - nb) * Q_BLOCK), (0, 0))

    def blocks(t):
        return jnp.pad(t, pad).reshape(Bsz, H, nbp, Q_BLOCK, HEAD_DIM)

    qb, kb, vb = blocks(q * ATTN_SCALE), blocks(k), blocks(v)
    a = jnp.arange(Q_BLOCK)
    strict = a[None, :] < a[:, None]
    suffix = (a[:, None] >= a[None, :]).astype(jnp.float32)

    z = jnp.clip(jnp.einsum('bhnqd,bhnkd->bhnqk', qb, kb).astype(jnp.float32), -SB_CLIP, SB_CLIP)
    sp = jnp.where(strict, jnp.log1p(jnp.exp(z)), 0.0)
    r_in = jnp.einsum('bhnqj,js->bhnqs', sp, suffix)
    att = jnp.where(strict, jnp.exp(z - r_in), 0.0)
    acc_d = jnp.einsum('bhnqs,bhnsd->bhnqd', att, vb.astype(jnp.float32))
    r_d = r_in[..., 0]

    def pair(p):
        qb2 = nbp - 1 - p
        init_r2 = lax.dynamic_index_in_dim(r_d, qb2, axis=2, keepdims=False)
        init_a2 = lax.dynamic_index_in_dim(acc_d, qb2, axis=2, keepdims=False)

        def step(carry, kstep):
            r_sum, acc, acc_first = carry
            start2 = kstep == p
            acc_first = jnp.where(start2, acc, acc_first)
            r_sum = jnp.where(start2, init_r2, r_sum)
            acc = jnp.where(start2, init_a2, acc)
            first = kstep < p
            qi = jnp.where(first, p, qb2)
            kj = jnp.where(first, p - 1 - kstep, nbp - 2 - kstep)
            qblk = lax.dynamic_index_in_dim(qb, qi, axis=2, keepdims=False)
            kblk = lax.dynamic_index_in_dim(kb, kj, axis=2, keepdims=False)
            vblk = lax.dynamic_index_in_dim(vb, kj, axis=2, keepdims=False)
            zz = jnp.clip(jnp.einsum('bhqd,bhkd->bhqk', qblk, kblk).astype(jnp.float32), -SB_CLIP, SB_CLIP)
            rr = jnp.einsum('bhqj,js->bhqs', jnp.log1p(jnp.exp(zz)), suffix)
            pv = jnp.einsum('bhqs,bhsd->bhqd', jnp.exp(zz - rr), vblk.astype(jnp.float32))
            acc = acc + jnp.exp(-r_sum)[..., None] * pv
            r_sum = r_sum + rr[..., 0]
            return (r_sum, acc, acc_first), None

        init = (lax.dynamic_index_in_dim(r_d, p, axis=2, keepdims=False),
                lax.dynamic_index_in_dim(acc_d, p, axis=2, keepdims=False),
                jnp.zeros((Bsz, H, Q_BLOCK, HEAD_DIM), jnp.float32))
        (_, acc_b, acc_a), _ = lax.scan(step, init, jnp.arange(nbp - 1))
        return acc_a, acc_b

    out_a, out_b = lax.map(pair, jnp.arange(nbp // 2))
    o = jnp.concatenate([out_a, out_b[::-1]], axis=0)
    o = o.transpose(1, 2, 0, 3, 4).reshape(Bsz, H, nbp * Q_BLOCK, HEAD_DIM)[:, :, :S_]
    return merge_heads(o.astype(x.dtype)) @ w_out


def compress_blocks(t, pos, w1, w2):
    Bsz, G, S_, _ = t.shape
    halves = t.reshape(Bsz, G, S_ // NSA_CMP_STRIDE, NSA_CMP_STRIDE, HEAD_DIM)
    blocks = jnp.concatenate([halves[:, :, :-1], halves[:, :, 1:]], axis=3) + pos
    n_cmp = blocks.shape[2]
    flat = blocks.reshape(Bsz, G, n_cmp, NSA_CMP_LEN * HEAD_DIM)
    return jax.nn.gelu(flat @ w1) @ w2


def mixer_nsa(x, w_in, w_out, table, pos_k, w1_k, w2_k, pos_v, w1_v, w2_v):
    Bsz, S_, _ = x.shape
    G, R = NSA_KV_HEADS, NSA_GROUP
    proj = x @ w_in
    q = proj[..., :ATTN_WIDTH].reshape(Bsz, S_, G, R, HEAD_DIM).transpose(0, 2, 3, 1, 4)
    kv = jnp.split(proj[..., ATTN_WIDTH:ATTN_WIDTH + 6 * NSA_KV_WIDTH], 6, axis=-1)
    k_cmp, v_cmp, k_sel, v_sel, k_win, v_win = [split_heads(t, G) for t in kv]
    gates = jax.nn.sigmoid(proj[..., ATTN_WIDTH + 6 * NSA_KV_WIDTH:].astype(jnp.float32))
    gates = gates.reshape(Bsz, S_, 3, G, R).transpose(2, 0, 3, 4, 1)

    kc = compress_blocks(k_cmp, pos_k, w1_k, w2_k)
    vc = compress_blocks(v_cmp, pos_v, w1_v, w2_v)
    n_cmp = kc.shape[2]
    cmp_end = jnp.arange(n_cmp) * NSA_CMP_STRIDE + NSA_CMP_LEN - 1
    n_sel_blocks = S_ // NSA_SEL_BLOCK
    n_top = min(NSA_TOP_N, n_sel_blocks)
    offs = np.arange(1 - NSA_CMP_LEN // NSA_CMP_STRIDE, NSA_SEL_BLOCK // NSA_CMP_STRIDE)
    ov = np.arange(n_sel_blocks)[:, None] * (NSA_SEL_BLOCK // NSA_CMP_STRIDE) + offs[None]
    ov_valid = jnp.asarray((ov >= 0) & (ov < n_cmp))
    ov_idx = jnp.asarray(np.clip(ov, 0, n_cmp - 1))
    kwin = jnp.pad(k_win, ((0, 0), (0, 0), (NSA_WINDOW, 0), (0, 0)))
    vwin = jnp.pad(v_win, ((0, 0), (0, 0), (NSA_WINDOW, 0), (0, 0)))
    table_gr = table.T.reshape(G, R, REL_BUCKETS)
    nb = S_ // Q_BLOCK
    win_len = NSA_WINDOW + Q_BLOCK
    a = jnp.arange(Q_BLOCK)
    win_steps = NSA_WINDOW + a[:, None] - jnp.arange(win_len)[None, :]
    win_bias = rel_bias(table, win_steps).reshape(G, R, Q_BLOCK, win_len)
    sel_j = jnp.arange(n_sel_blocks)
    qb = q.reshape(Bsz, G, R, nb, Q_BLOCK, HEAD_DIM).transpose(3, 0, 1, 2, 4, 5)

    def block(args):
        qi, i = args
        qpos = i * Q_BLOCK + a
        cdist = qpos[:, None] - cmp_end[None, :]
        lc = jnp.einsum('bgrqd,bgcd->bgrqc', qi, kc).astype(jnp.float32) * ATTN_SCALE
        lc = lc + rel_bias(table, cdist).reshape(G, R, Q_BLOCK, n_cmp)
        pc, _ = masked_softmax(lc, cdist >= 0)
        o_cmp = jnp.einsum('bgrqc,bgcd->bgrqd', pc, vc.astype(jnp.float32))
        imp = (pc.sum(2)[..., ov_idx] * ov_valid).sum(-1)
        cur = qpos // NSA_SEL_BLOCK
        forced = (sel_j[None] == 0) | (sel_j[None] == cur[:, None]) | (sel_j[None] == cur[:, None] - 1)
        causal = sel_j[None] * NSA_SEL_BLOCK <= qpos[:, None]
        score = jnp.where(forced, FORCED_SCORE, jnp.where(causal, imp, NEG_INF))
        sel = lax.top_k(score, n_top)[1]
        kw = lax.dynamic_slice_in_dim(kwin, i * Q_BLOCK, win_len, axis=2)
        vw = lax.dynamic_slice_in_dim(vwin, i * Q_BLOCK, win_len, axis=2)
        wmask = (win_steps < NSA_WINDOW) & (win_steps >= 0) & (qpos[:, None] - win_steps >= 0)
        lw = jnp.einsum('bgrqd,bgkd->bgrqk', qi, kw).astype(jnp.float32) * ATTN_SCALE + win_bias
        pw, _ = masked_softmax(lw, wmask)
        o_win = jnp.einsum('bgrqk,bgkd->bgrqd', pw, vw.astype(jnp.float32))
        return o_cmp, o_win, sel

    o_cmp, o_win, sel = lax.map(block, (qb, jnp.arange(nb)))

    def unblock(t):
        return t.transpose(1, 2, 3, 0, 4, 5).reshape(Bsz, G, R, S_, HEAD_DIM)

    o_cmp, o_win = unblock(o_cmp), unblock(o_win)
    sel = sel.transpose(1, 2, 0, 3, 4).reshape(Bsz, G, S_, n_top)
    qpos_all = jnp.arange(S_)
    valid = sel * NSA_SEL_BLOCK <= qpos_all[:, None]
    n_groups = Bsz * G * n_sel_blocks
    bg = jnp.arange(Bsz * G).reshape(Bsz, G, 1, 1)
    gid = jnp.where(valid, bg * n_sel_blocks + sel, n_groups).reshape(Bsz * G * S_, n_top)
    q_src = q.transpose(0, 1, 3, 2, 4).reshape(Bsz * G * S_, R, HEAD_DIM)
    o_s, lse_s = dispatch_attention(
        q_src, jnp.tile(qpos_all, Bsz * G), gid,
        k_sel.reshape(n_groups, NSA_SEL_BLOCK, HEAD_DIM), v_sel.reshape(n_groups, NSA_SEL_BLOCK, HEAD_DIM),
        jnp.tile(jnp.arange(n_sel_blocks) * NSA_SEL_BLOCK, Bsz * G), table_gr,
        (jnp.arange(n_groups) // n_sel_blocks) % G)
    o_sel = merge_parts(o_s, lse_s).reshape(Bsz, G, S_, R, HEAD_DIM).transpose(0, 1, 3, 2, 4)
    out = gates[0][..., None] * o_cmp + gates[1][..., None] * o_sel + gates[2][..., None] * o_win
    return out.transpose(0, 3, 1, 2, 4).reshape(Bsz, S_, ATTN_WIDTH).astype(x.dtype) @ w_out


def mixer_moba(x, w_in, w_out, table):
    Bsz, S_, _ = x.shape
    H = N_HEADS
    q, k, v = [split_heads(t, H) for t in jnp.split(x @ w_in, 3, axis=-1)]
    nblk = -(-S_ // MOBA_BLOCK)
    Sp = nblk * MOBA_BLOCK
    pad = ((0, 0), (0, 0), (0, Sp - S_), (0, 0))
    qblk = jnp.pad(q, pad).reshape(Bsz, H, nblk, MOBA_BLOCK, HEAD_DIM)
    kblk = jnp.pad(k, pad).reshape(Bsz, H, nblk, MOBA_BLOCK, HEAD_DIM)
    vblk = jnp.pad(v, pad).reshape(Bsz, H, nblk, MOBA_BLOCK, HEAD_DIM)
    c = jnp.arange(MOBA_BLOCK)
    own_dist = c[:, None] - c[None, :]
    lo = jnp.einsum('bhnqd,bhnkd->bhnqk', qblk, kblk).astype(jnp.float32) * ATTN_SCALE + rel_bias(table, own_dist)[:, None]
    p_own, lse_own = masked_softmax(lo, own_dist >= 0)
    o_own = jnp.einsum('bhnqk,bhnkd->bhnqd', p_own, vblk.astype(jnp.float32)).reshape(Bsz, H, Sp, HEAD_DIM)[:, :, :S_]
    lse_own = lse_own[..., 0].reshape(Bsz, H, Sp)[:, :, :S_]
    kmean = kblk.mean(axis=3)
    n_top = max(1, min(MOBA_TOP_K, nblk - 1))
    cur = jnp.arange(S_) // MOBA_BLOCK
    gate = jnp.einsum('bhsd,bhnd->bhsn', q, kmean).astype(jnp.float32)
    gate = jnp.where(jnp.arange(nblk)[None, :] < cur[:, None], gate, NEG_INF)
    sel = lax.top_k(gate, n_top)[1]
    valid = sel < cur[:, None]
    n_groups = Bsz * H * nblk
    bh = jnp.arange(Bsz * H).reshape(Bsz, H, 1, 1)
    gid = jnp.where(valid, bh * nblk + sel, n_groups).reshape(Bsz * H * S_, n_top)
    o_sel, lse_sel = dispatch_attention(
        q.reshape(Bsz * H * S_, 1, HEAD_DIM), jnp.tile(jnp.arange(S_), Bsz * H), gid,
        kblk.reshape(n_groups, MOBA_BLOCK, HEAD_DIM), vblk.reshape(n_groups, MOBA_BLOCK, HEAD_DIM),
        jnp.tile(jnp.arange(nblk) * MOBA_BLOCK, Bsz * H), table.T[:, None, :],
        (jnp.arange(n_groups) // nblk) % H)
    o_all = jnp.concatenate([o_own.reshape(Bsz * H * S_, 1, 1, HEAD_DIM), o_sel], axis=1)
    lse_all = jnp.concatenate([lse_own.reshape(Bsz * H * S_, 1, 1), lse_sel], axis=1)
    o = merge_parts(o_all, lse_all).reshape(Bsz, H, S_, HEAD_DIM)
    return merge_heads(o.astype(x.dtype)) @ w_out


def moe_ffn(x, router_w, router_b, w_gate, w_up, w_down):
    Bsz, S_, D_ = x.shape
    T = Bsz * S_
    xt = x.reshape(T, D_)
    scores = jax.nn.sigmoid((xt @ router_w).astype(jnp.float32))
    biased = (scores + router_b.astype(jnp.float32)).reshape(T, N_GROUPS, EXPERTS_PER_GROUP)
    group_score = lax.top_k(biased, MOE_TOP_K)[0].sum(-1)
    best_group = jnp.argmax(group_score, axis=-1).astype(jnp.int32)
    in_group = jnp.take_along_axis(biased, best_group[:, None, None], axis=1)[:, 0]
    local = lax.top_k(in_group, MOE_TOP_K)[1]
    expert_idx = best_group[:, None] * EXPERTS_PER_GROUP + local
    gate = jnp.take_along_axis(scores, expert_idx, axis=1)
    gate = gate / gate.sum(-1, keepdims=True)
    n_assign = T * MOE_TOP_K
    flat_e = expert_idx.reshape(n_assign)
    order = jnp.argsort(flat_e)
    sorted_e = flat_e[order]
    sorted_tok = (order // MOE_TOP_K).astype(jnp.int32)
    sorted_w = gate.reshape(n_assign)[order]
    counts = jnp.zeros((N_EXPERTS,), jnp.int32).at[flat_e].add(1)
    padded = (counts + MOE_BLOCK - 1) // MOE_BLOCK * MOE_BLOCK
    pad_end = jnp.cumsum(padded)
    pad_start = pad_end - padded
    start = jnp.cumsum(counts) - counts
    dest = pad_start[sorted_e] + jnp.arange(n_assign) - start[sorted_e]
    n_blocks = -(-n_assign // MOE_BLOCK) + N_EXPERTS
    P = n_blocks * MOE_BLOCK
    buf_tok = jnp.full((P,), T, jnp.int32).at[dest].set(sorted_tok)
    buf_w = jnp.zeros((P,), x.dtype).at[dest].set(sorted_w.astype(x.dtype))
    block_expert = jnp.minimum(jnp.searchsorted(pad_end, jnp.arange(n_blocks) * MOE_BLOCK, side='right'), N_EXPERTS - 1)
    x_pad = jnp.concatenate([xt, jnp.zeros((1, D_), xt.dtype)], axis=0)
    xb = x_pad[buf_tok].reshape(n_blocks, MOE_BLOCK, D_)

    def expert_block(args):
        xi, e = args
        return (jax.nn.silu(xi @ w_gate[e]) * (xi @ w_up[e])) @ w_down[e]

    yb = lax.map(expert_block, (xb, block_expert)).reshape(P, D_) * buf_w[:, None]
    y = jnp.zeros((T + 1, D_), yb.dtype).at[buf_tok].add(yb)[:T]
    return y.reshape(Bsz, S_, D_)


def setup_inputs(seed: int = 0) -> dict:
    key = jax.random.key(seed)
    ks = jax.random.split(key, 26)
    n_occ = [len(range(m, DEPTH, N_MIXERS)) for m in range(N_MIXERS)]
    beta = DEEPNORM_BETA

    def normal(k, shape, scale):
        return jax.random.normal(k, shape, jnp.float32) * scale

    def col_scale(pieces):
        return jnp.asarray(np.concatenate([np.full((w,), s, np.float32) for w, s in pieces]))

    qkv_pieces = [(ATTN_WIDTH, 1.0), (ATTN_WIDTH, 1.0), (ATTN_WIDTH, beta)]
    dil_scale = col_scale(qkv_pieces * len(DILATED_PAIRS))
    qkv_scale = col_scale(qkv_pieces)
    nsa_scale = col_scale([(ATTN_WIDTH, 1.0)] + [(NSA_KV_WIDTH, 1.0), (NSA_KV_WIDTH, beta)] * 3 + [(3 * N_HEADS, 1.0)])
    din = D_MODEL ** -0.5
    out_scale = ATTN_WIDTH ** -0.5 * beta
    cmp_in = NSA_CMP_LEN * HEAD_DIM
    return {
        'x': normal(ks[0], (BATCH, SEQ, D_MODEL), 1.0),
        'rel_table': normal(ks[1], (REL_BUCKETS, N_HEADS), 0.1),
        'router_w': normal(ks[2], (D_MODEL, N_EXPERTS), din),
        'router_b': normal(ks[3], (N_EXPERTS,), 0.01),
        'ln1_g': 1.0 + normal(ks[4], (DEPTH, D_MODEL), 0.01),
        'ln1_b': normal(ks[5], (DEPTH, D_MODEL), 0.01),
        'ln2_g': 1.0 + normal(ks[6], (DEPTH, D_MODEL), 0.01),
        'ln2_b': normal(ks[7], (DEPTH, D_MODEL), 0.01),
        'exp_w_gate': normal(ks[8], (DEPTH, N_EXPERTS, D_MODEL, D_EXPERT), din),
        'exp_w_up': normal(ks[9], (DEPTH, N_EXPERTS, D_MODEL, D_EXPERT), din),
        'exp_w_down': normal(ks[10], (DEPTH, N_EXPERTS, D_EXPERT, D_MODEL), D_EXPERT ** -0.5 * beta),
        'dil_w_in': normal(ks[11], (n_occ[0], D_MODEL, 3 * len(DILATED_PAIRS) * ATTN_WIDTH), din) * dil_scale,
        'dil_w_out': normal(ks[12], (n_occ[0], ATTN_WIDTH, D_MODEL), out_scale),
        'sb_w_in': normal(ks[13], (n_occ[1], D_MODEL, 3 * ATTN_WIDTH), din) * qkv_scale,
        'sb_w_out': normal(ks[14], (n_occ[1], ATTN_WIDTH, D_MODEL), out_scale),
        'nsa_w_in': normal(ks[15], (n_occ[2], D_MODEL, ATTN_WIDTH + 6 * NSA_KV_WIDTH + 3 * N_HEADS), din) * nsa_scale,
        'nsa_w_out': normal(ks[16], (n_occ[2], ATTN_WIDTH, D_MODEL), out_scale),
        'nsa_cmp_pos_k': normal(ks[17], (n_occ[2], NSA_CMP_LEN, HEAD_DIM), 0.1),
        'nsa_cmp_w1_k': normal(ks[18], (n_occ[2], cmp_in, NSA_CMP_HIDDEN), cmp_in ** -0.5),
        'nsa_cmp_w2_k': normal(ks[19], (n_occ[2], NSA_CMP_HIDDEN, HEAD_DIM), NSA_CMP_HIDDEN ** -0.5),
        'nsa_cmp_pos_v': normal(ks[20], (n_occ[2], NSA_CMP_LEN, HEAD_DIM), 0.1),
        'nsa_cmp_w1_v': normal(ks[21], (n_occ[2], cmp_in, NSA_CMP_HIDDEN), cmp_in ** -0.5),
        'nsa_cmp_w2_v': normal(ks[22], (n_occ[2], NSA_CMP_HIDDEN, HEAD_DIM), NSA_CMP_HIDDEN ** -0.5),
        'moba_w_in': normal(ks[23], (n_occ[3], D_MODEL, 3 * ATTN_WIDTH), din) * qkv_scale,
        'moba_w_out': normal(ks[24], (n_occ[3], ATTN_WIDTH, D_MODEL), out_scale),
    }


def reference(x, rel_table, router_w, router_b, ln1_g, ln1_b, ln2_g, ln2_b,
              exp_w_gate, exp_w_up, exp_w_down, dil_w_in, dil_w_out, sb_w_in, sb_w_out,
              nsa_w_in, nsa_w_out, nsa_cmp_pos_k, nsa_cmp_w1_k, nsa_cmp_w2_k,
              nsa_cmp_pos_v, nsa_cmp_w1_v, nsa_cmp_w2_v, moba_w_in, moba_w_out):
    for layer in range(DEPTH):
        kind, occ = layer % N_MIXERS, layer // N_MIXERS
        if kind == 0:
            mixed = mixer_dilated(x, dil_w_in[occ], dil_w_out[occ], rel_table)
        elif kind == 1:
            mixed = mixer_stick_breaking(x, sb_w_in[occ], sb_w_out[occ])
        elif kind == 2:
            mixed = mixer_nsa(x, nsa_w_in[occ], nsa_w_out[occ], rel_table,
                              nsa_cmp_pos_k[occ], nsa_cmp_w1_k[occ], nsa_cmp_w2_k[occ],
                              nsa_cmp_pos_v[occ], nsa_cmp_w1_v[occ], nsa_cmp_w2_v[occ])
        else:
            mixed = mixer_moba(x, moba_w_in[occ], moba_w_out[occ], rel_table)
        x = layer_norm(DEEPNORM_ALPHA * x + mixed, ln1_g[layer], ln1_b[layer])
        ffn = moe_ffn(x, router_w, router_b, exp_w_gate[layer], exp_w_up[layer], exp_w_down[layer])
        x = layer_norm(DEEPNORM_ALPHA * x + ffn, ln2_g[layer], ln2_b[layer])
    return x
```

```python
import functools
import math

import numpy as np
import jax
import jax.numpy as jnp
from jax import lax
from jax.experimental import pallas as pl
from jax.experimental.pallas import tpu as pltpu

F32 = jnp.float32
BF16 = jnp.bfloat16

D_MODEL = 1024
HEAD_DIM = 64
N_HEADS = 16
LANES = 128
PAIRS = N_HEADS // 2
ATTN_SCALE = HEAD_DIM ** -0.5
REL_BUCKETS = 32
REL_MAX_DIST = 2048
DILATED_PAIRS = ((128, 1), (512, 4), (2048, 16))
DIL_BLOCK = 128
SB_CLIP = 60.0
NSA_KV_HEADS = 4
NSA_GROUP = N_HEADS // NSA_KV_HEADS
NSA_CMP_LEN = 32
NSA_CMP_STRIDE = 16
NSA_SEL_BLOCK = 64
NSA_TOP_N = 16
NSA_WINDOW = 512
MOBA_BLOCK = 256
MOBA_TOP_K = 3
N_EXPERTS = 16
N_GROUPS = 4
EXPERTS_PER_GROUP = N_EXPERTS // N_GROUPS
DEPTH = 4
DEEPNORM_ALPHA = (2 * DEPTH) ** 0.25
LN_EPS = 1e-5
NEG_INF = -1e30
FORCED_SCORE = 1e9
TILE = 256
MAIN_TILES = 10
CMP_CHUNK = 128
CMP_TILES = 19
VMEM_LIMIT = 56 * 1024 * 1024


def _bucket_thresholds():
    n = np.arange(0, 2 * REL_MAX_DIST)
    exact = REL_BUCKETS // 2
    logf = np.log(np.maximum(n, 1).astype(np.float64) / exact) / math.log(REL_MAX_DIST / exact)
    large = np.minimum(exact + (logf * (REL_BUCKETS - exact)).astype(np.int64), REL_BUCKETS - 1)
    bucket = np.where(n < exact, n, large)
    return tuple(int(np.argmax(bucket >= k)) for k in range(1, REL_BUCKETS))


BUCKET_THRESHOLDS = _bucket_thresholds()


def _params(sem, vmem=VMEM_LIMIT):
    return pltpu.CompilerParams(dimension_semantics=sem, vmem_limit_bytes=vmem)


def _dot(a, b):
    return jnp.dot(a, b, preferred_element_type=F32)


def _dot_nt(a, b):
    return lax.dot_general(a, b, (((1,), (1,)), ((), ())), preferred_element_type=F32)


def _split_bf16(x):
    hi = x.astype(BF16)
    lo = (x - hi.astype(F32)).astype(BF16)
    return hi, lo


def _matmul_kernel(x_ref, w_ref, o_ref, *, act):
    y = _dot(x_ref[...].astype(BF16), w_ref[...])
    if act == 'sigmoid':
        y = 1.0 / (1.0 + jnp.exp(-y))
    o_ref[...] = y.astype(o_ref.dtype)


def _matmul(x, w, out_dtype, act=None):
    m, k = x.shape
    n = w.shape[1]
    tm = 512
    tn = 512 if n % 512 == 0 else 128
    return pl.pallas_call(
        functools.partial(_matmul_kernel, act=act),
        out_shape=jax.ShapeDtypeStruct((m, n), out_dtype),
        grid=(m // tm, n // tn),
        in_specs=[pl.BlockSpec((tm, k), lambda i, j: (i, 0)),
                  pl.BlockSpec((k, tn), lambda i, j: (0, j))],
        out_specs=pl.BlockSpec((tm, tn), lambda i, j: (i, j)),
        compiler_params=_params(("parallel", "arbitrary")),
        name="proj_matmul",
    )(x, w)


def _layer_norm(z, g, b):
    mu = jnp.mean(z, axis=-1, keepdims=True)
    zc = z - mu
    var = jnp.mean(zc * zc, axis=-1, keepdims=True)
    return zc * lax.rsqrt(var + LN_EPS) * g + b


def _outproj_kernel(*refs, n_in, merge):
    ins = refs[:n_in]
    w_ref, x_ref, g_ref, b_ref, o_ref = refs[n_in:]
    if merge == 'lse3':
        o1, o2, o3, l1, l2, l3 = [r[...] for r in ins]
        mx = jnp.maximum(jnp.maximum(l1, l2), l3)
        e1, e2, e3 = jnp.exp(l1 - mx), jnp.exp(l2 - mx), jnp.exp(l3 - mx)
        a = (e1 * o1 + e2 * o2 + e3 * o3) / (e1 + e2 + e3)
    elif merge == 'sum3':
        a = ins[0][...].astype(F32) + ins[1][...].astype(F32) + ins[2][...].astype(F32)
    else:
        a = ins[0][...]
    y = _dot(a.astype(BF16), w_ref[...])
    o_ref[...] = _layer_norm(DEEPNORM_ALPHA * x_ref[...] + y, g_ref[...], b_ref[...])


def _outproj_ln(ins, w, x, g, b, merge=None):
    m, d = x.shape
    ka = ins[0].shape[1]
    tm = 256
    n_in = len(ins)
    return pl.pallas_call(
        functools.partial(_outproj_kernel, n_in=n_in, merge=merge),
        out_shape=jax.ShapeDtypeStruct((m, d), F32),
        grid=(m // tm,),
        in_specs=[pl.BlockSpec((tm, ka), lambda i: (i, 0)) for _ in ins]
        + [pl.BlockSpec((ka, d), lambda i: (0, 0)),
           pl.BlockSpec((tm, d), lambda i: (i, 0)),
           pl.BlockSpec((1, d), lambda i: (0, 0)),
           pl.BlockSpec((1, d), lambda i: (0, 0))],
        out_specs=pl.BlockSpec((tm, d), lambda i: (i, 0)),
        compiler_params=_params(("parallel",)),
        name="outproj_ln",
    )(*ins, w, x, g.reshape(1, d), b.reshape(1, d))


def _bias_kernel(tbl_ref, o_ref, *, rows, cols, off0, tstride, cmul, lo, hi, mults):
    t = pl.program_id(0)
    if len(set(mults)) == 1:
        mult = mults[0]
    else:
        mult = jnp.int32(mults[-1])
        for idx in range(len(mults) - 2, -1, -1):
            mult = jnp.where(t == idx, jnp.int32(mults[idx]), mult)
    base = off0 + tstride * t

    def strip(i, carry):
        r0 = pl.multiple_of(i * 8, 8)
        a = lax.broadcasted_iota(jnp.int32, (8, cols), 0) + r0
        c = lax.broadcasted_iota(jnp.int32, (8, cols), 1)
        steps = base + a - cmul * c
        valid = (steps >= lo) & (steps <= hi)
        dist = steps * mult
        for h in range(N_HEADS):
            acc = jnp.full((8, cols), tbl_ref[0, h], F32)
            for k, thr in enumerate(BUCKET_THRESHOLDS):
                acc = jnp.where(dist >= thr, tbl_ref[k + 1, h], acc)
            o_ref[h, 0, pl.ds(r0, 8), :] = jnp.where(valid, acc, NEG_INF)
        return carry

    lax.fori_loop(0, rows // 8, strip, 0)


def _bias_tiles(table, n_tiles, rows, cols, off0, tstride, cmul, lo, hi, mults):
    return pl.pallas_call(
        functools.partial(_bias_kernel, rows=rows, cols=cols, off0=off0, tstride=tstride, cmul=cmul,
                          lo=lo, hi=hi, mults=tuple(mults)),
        out_shape=jax.ShapeDtypeStruct((N_HEADS, n_tiles, rows, cols), F32),
        grid=(n_tiles,),
        in_specs=[pl.BlockSpec(memory_space=pltpu.SMEM)],
        out_specs=pl.BlockSpec((N_HEADS, 1, rows, cols), lambda t: (0, t, 0, 0)),
        compiler_params=_params(("arbitrary",)),
        name="bias_tiles",
    )(table)


def _dil_kernel(q_ref, kp_ref, kc_ref, vp_ref, vc_ref, bias_ref, o_ref, lse_ref):
    j = pl.program_id(3)
    q2 = q_ref[0]
    k2 = jnp.concatenate([kp_ref[0], kc_ref[0]], axis=0)
    v2 = jnp.concatenate([vp_ref[0], vc_ref[0]], axis=0)
    colk = lax.broadcasted_iota(jnp.int32, (DIL_BLOCK, 2 * DIL_BLOCK), 1)
    first = jnp.where((colk < DIL_BLOCK) & (j == 0), NEG_INF, 0.0)
    lane = lax.broadcasted_iota(jnp.int32, (DIL_BLOCK, LANES), 1)
    outs, lses = [], []
    for h in range(2):
        s = _dot_nt(q2[:, h * LANES:(h + 1) * LANES], k2) + bias_ref[h, 0] + first
        m = jnp.max(s, axis=-1, keepdims=True)
        p = jnp.exp(s - m)
        l = jnp.maximum(jnp.sum(p, axis=-1, keepdims=True), 1e-30)
        outs.append(_dot(p.astype(BF16), v2) / l)
        lses.append(m + jnp.log(l))
    low = lane < HEAD_DIM
    o_ref[0] = jnp.where(low, outs[0], outs[1])
    lse_ref[0] = jnp.where(low, lses[0], lses[1])


def _dilated_group(proj, bias, g, dilation, bsz, seq):
    width = proj.shape[-1]
    length = seq // dilation
    nb = length // DIL_BLOCK
    wb = width // LANES
    pv = proj.reshape(bsz, length, dilation * width)
    q_blk = g * N_HEADS // 2
    k_blk = 3 * N_HEADS + g * 2 * PAIRS
    v_blk = k_blk + PAIRS
    kv_spec = lambda base, prev: pl.BlockSpec(
        (1, DIL_BLOCK, LANES),
        (lambda hp, b, r, j: (b, jnp.maximum(j - 1, 0), r * wb + base + hp)) if prev
        else (lambda hp, b, r, j: (b, j, r * wb + base + hp)))
    out_spec = pl.BlockSpec((1, DIL_BLOCK, LANES), lambda hp, b, r, j: (b, j, r * PAIRS + hp))
    o, lse = pl.pallas_call(
        _dil_kernel,
        out_shape=[jax.ShapeDtypeStruct((bsz, length, dilation * D_MODEL), F32)] * 2,
        grid=(PAIRS, bsz, dilation, nb),
        in_specs=[pl.BlockSpec((1, DIL_BLOCK, 2 * LANES), lambda hp, b, r, j: (b, j, r * (wb // 2) + q_blk + hp)),
                  kv_spec(k_blk, True), kv_spec(k_blk, False), kv_spec(v_blk, True), kv_spec(v_blk, False),
                  pl.BlockSpec((2, 1, DIL_BLOCK, 2 * DIL_BLOCK), lambda hp, b, r, j: (hp, g, 0, 0))],
        out_specs=[out_spec, out_spec],
        compiler_params=_params(("parallel", "parallel", "parallel", "arbitrary")),
        name="dilated_attn",
    )(pv, pv, pv, pv, pv, bias)
    return o.reshape(bsz * seq, D_MODEL), lse.reshape(bsz * seq, D_MODEL)


def _pad_heads_cols(w, parity_of_head):
    k, n = w.shape
    nh = n // HEAD_DIM
    wh = w.reshape(k, nh, HEAD_DIM)
    z = jnp.zeros_like(wh)
    par = jnp.asarray([parity_of_head(h) for h in range(nh)], jnp.int32)[None, :, None]
    lo = jnp.where(par == 0, wh, z)
    hi = jnp.where(par == 1, wh, z)
    return jnp.concatenate([lo, hi], axis=-1).reshape(k, nh * LANES)


def _pad_heads_rows(w, parity_of_head):
    return _pad_heads_cols(w.T, parity_of_head).T


def _mixer_dilated(x, w_in, w_out, bias_dil, bsz, seq):
    aw = N_HEADS * HEAD_DIM
    par = lambda h: h % 2
    qs, kvs = [], []
    for g in range(len(DILATED_PAIRS)):
        base = g * 3 * aw
        qs.append(_pad_heads_cols(w_in[:, base:base + aw] * ATTN_SCALE, par))
        kvs.append(w_in[:, base + aw:base + 3 * aw])
    w_all = jnp.concatenate(qs + kvs, axis=1).astype(BF16)
    proj = _matmul(x, w_all, BF16).reshape(bsz, seq, -1)
    parts = [_dilated_group(proj, bias_dil, g, d, bsz, seq) for g, (_, d) in enumerate(DILATED_PAIRS)]
    return [p[0] for p in parts] + [p[1] for p in parts], w_out.astype(BF16)


def _sb_kernel(q_ref, k_ref, v_ref, o_ref):
    i = pl.program_id(2)
    t = TILE
    row = lax.broadcasted_iota(jnp.int32, (t, t), 0)
    col = lax.broadcasted_iota(jnp.int32, (t, t), 1)
    suffix = jnp.where(row >= col, 1.0, 0.0).astype(BF16)
    strict = col < row
    q2 = q_ref[0]
    outs = []
    for h in range(2):
        qh = q2[:, h * LANES:(h + 1) * LANES]

        def block(kj, masked):
            start = pl.multiple_of(kj * t, t)
            kb = k_ref[0, pl.ds(start, t), :]
            vb = v_ref[0, pl.ds(start, t), :]
            z = jnp.clip(_dot_nt(qh, kb), -SB_CLIP, SB_CLIP)
            sp = jnp.log(1.0 + jnp.exp(z))
            if masked:
                sp = jnp.where(strict, sp, 0.0)
            hi, lo = _split_bf16(sp)
            rr = _dot(hi, suffix) + _dot(lo, suffix)
            att = jnp.exp(z - rr)
            if masked:
                att = jnp.where(strict, att, 0.0)
            return _dot(att.astype(BF16), vb), rr[:, 0:1]

        def diagonal(kj, carry):
            acc_d, r_d = block(kj, True)
            return r_d, acc_d

        r0, acc0 = lax.fori_loop(i, i + 1, diagonal, (jnp.zeros((t, 1), F32), jnp.zeros((t, LANES), F32)))

        def step(n, carry):
            r_sum, acc = carry
            pv, r_blk = block(i - 1 - n, False)
            return r_sum + r_blk, acc + jnp.exp(-r_sum) * pv

        _, acc = lax.fori_loop(0, i, step, (r0, acc0))
        outs.append(acc)
    lane = lax.broadcasted_iota(jnp.int32, (t, LANES), 1)
    o_ref[0] = jnp.where(lane < HEAD_DIM, outs[0], outs[1]).astype(o_ref.dtype)


def _mixer_stick_breaking(x, w_in, w_out, bsz, seq):
    aw = N_HEADS * HEAD_DIM
    wq = _pad_heads_cols(w_in[:, :aw] * ATTN_SCALE, lambda h: h % 2)
    w_all = jnp.concatenate([wq, w_in[:, aw:]], axis=1).astype(BF16)
    proj = _matmul(x, w_all, BF16).reshape(bsz, seq, -1)
    kb = N_HEADS
    o = pl.pallas_call(
        _sb_kernel,
        out_shape=jax.ShapeDtypeStruct((bsz, seq, D_MODEL), BF16),
        grid=(bsz, PAIRS, seq // TILE),
        in_specs=[pl.BlockSpec((1, TILE, 2 * LANES), lambda b, hp, i: (b, i, hp)),
                  pl.BlockSpec((1, seq, LANES), lambda b, hp, i: (b, 0, kb + hp)),
                  pl.BlockSpec((1, seq, LANES), lambda b, hp, i: (b, 0, kb + PAIRS + hp))],
        out_specs=pl.BlockSpec((1, TILE, LANES), lambda b, hp, i: (b, i, hp)),
        compiler_params=_params(("parallel", "parallel", "arbitrary")),
        name="stick_breaking_attn",
    )(proj, proj, proj)
    return [o.reshape(bsz * seq, D_MODEL)], w_out.astype(BF16)


def _flash_chunks(q_heads, k_ref, v_ref, bias_ref, nsel_of_head, lo, hi, i, blocks_per_chunk, n_bias_tiles):
    nh = len(q_heads)
    distinct = []
    for ns in nsel_of_head:
        if ns is not None and all(ns is not d for d in distinct):
            distinct.append(ns)

    def body(ch, carry):
        start = pl.multiple_of(ch * TILE, TILE)
        kb = k_ref[0, pl.ds(start, TILE), :]
        vb = v_ref[0, pl.ds(start, TILE), :]
        tile = jnp.minimum(i - ch, n_bias_tiles - 1)
        adds = []
        for ns in distinct:
            nblk = ns.shape[1]
            rj = lax.broadcasted_iota(jnp.int32, (nblk, TILE), 0)
            ck = lax.broadcasted_iota(jnp.int32, (nblk, TILE), 1)
            blk_of_key = ch * blocks_per_chunk + ck // (TILE // blocks_per_chunk)
            expand = jnp.where(rj == blk_of_key, NEG_INF, 0.0).astype(BF16)
            adds.append(_dot(ns, expand))
        new = []
        for h in range(nh):
            m, l, acc = carry[h]
            s = _dot_nt(q_heads[h], kb) + bias_ref[h, tile]
            ns = nsel_of_head[h]
            if ns is not None:
                s = s + adds[[d is ns for d in distinct].index(True)]
            m_new = jnp.maximum(m, jnp.max(s, axis=-1, keepdims=True))
            alpha = jnp.exp(m - m_new)
            p = jnp.exp(s - m_new)
            l = alpha * l + jnp.sum(p, axis=-1, keepdims=True)
            acc = alpha * acc + _dot(p.astype(BF16), vb)
            new.append((m_new, l, acc))
        return tuple(new)

    init = tuple((jnp.full((TILE, 1), NEG_INF, F32), jnp.zeros((TILE, 1), F32), jnp.zeros((TILE, LANES), F32))
                 for _ in range(nh))
    res = lax.fori_loop(lo, hi, body, init)
    return [(acc, l) for (_, l, acc) in res]


def _kmean_kernel(k_ref, o_ref):
    rows = k_ref.shape[1]
    kf = k_ref[0].astype(F32).reshape(rows // MOBA_BLOCK, MOBA_BLOCK, k_ref.shape[2])
    o_ref[0] = jnp.sum(kf, axis=1) * (1.0 / MOBA_BLOCK)


def _moba_kernel(q_ref, k_ref, v_ref, km_ref, bias_ref, o_ref):
    i = pl.program_id(2)
    nblk = km_ref.shape[1]
    km_hi, km_lo = _split_bf16(km_ref[0])
    q2 = q_ref[0]
    blk = lax.broadcasted_iota(jnp.int32, (TILE, nblk), 1)
    past = blk < i
    q_heads, nsels = [], []
    for h in range(2):
        qh = q2[:, h * LANES:(h + 1) * LANES]
        gate = jnp.where(past, _dot_nt(qh, km_hi) + _dot_nt(qh, km_lo), NEG_INF)
        allowed = blk == i
        for _ in range(MOBA_TOP_K):
            mx = jnp.max(gate, axis=-1, keepdims=True)
            first = jnp.min(jnp.where(gate == mx, blk, nblk), axis=-1, keepdims=True)
            pick = blk == first
            allowed = allowed | (pick & past)
            gate = jnp.where(pick, -jnp.inf, gate)
        q_heads.append(qh)
        nsels.append(jnp.where(allowed, 0.0, 1.0).astype(BF16))
    res = _flash_chunks(q_heads, k_ref, v_ref, bias_ref, nsels, 0, i + 1, i, 1, MAIN_TILES)
    lane = lax.broadcasted_iota(jnp.int32, (TILE, LANES), 1)
    o0 = res[0][0] / res[0][1]
    o1 = res[1][0] / res[1][1]
    o_ref[0] = jnp.where(lane < HEAD_DIM, o0, o1).astype(o_ref.dtype)


def _mixer_moba(x, w_in, w_out, bias_main, bsz, seq):
    aw = N_HEADS * HEAD_DIM
    nblk = seq // MOBA_BLOCK
    assert nblk - 1 >= MOBA_TOP_K
    wq = _pad_heads_cols(w_in[:, :aw] * ATTN_SCALE, lambda h: h % 2)
    w_all = jnp.concatenate([wq, w_in[:, aw:]], axis=1).astype(BF16)
    proj = _matmul(x, w_all, BF16).reshape(bsz, seq, -1)
    kb = N_HEADS
    rows = 8 * MOBA_BLOCK
    kmean = pl.pallas_call(
        _kmean_kernel,
        out_shape=jax.ShapeDtypeStruct((bsz, nblk, D_MODEL), F32),
        grid=(bsz, seq // rows),
        in_specs=[pl.BlockSpec((1, rows, D_MODEL), lambda b, i: (b, i, kb * LANES // D_MODEL))],
        out_specs=pl.BlockSpec((1, 8, D_MODEL), lambda b, i: (b, i, 0)),
        compiler_params=_params(("parallel", "parallel")),
        name="moba_kmean",
    )(proj)
    o = pl.pallas_call(
        _moba_kernel,
        out_shape=jax.ShapeDtypeStruct((bsz, seq, D_MODEL), BF16),
        grid=(bsz, PAIRS, seq // TILE),
        in_specs=[pl.BlockSpec((1, TILE, 2 * LANES), lambda b, hp, i: (b, i, hp)),
                  pl.BlockSpec((1, seq, LANES), lambda b, hp, i: (b, 0, kb + hp)),
                  pl.BlockSpec((1, seq, LANES), lambda b, hp, i: (b, 0, kb + PAIRS + hp)),
                  pl.BlockSpec((1, nblk, LANES), lambda b, hp, i: (b, 0, hp)),
                  pl.BlockSpec((2, MAIN_TILES, TILE, TILE), lambda b, hp, i: (hp, 0, 0, 0))],
        out_specs=pl.BlockSpec((1, TILE, LANES), lambda b, hp, i: (b, i, hp)),
        compiler_params=_params(("parallel", "parallel", "arbitrary")),
        name="moba_attn",
    )(proj, proj, proj, kmean, bias_main)
    return [o.reshape(bsz * seq, D_MODEL)], w_out.astype(BF16)


def _gelu_tanh(x):
    return 0.5 * x * (1.0 + jnp.tanh(math.sqrt(2.0 / math.pi) * (x + 0.044715 * (x * x * x))))


def _compress_kernel(a_ref, pos_ref, w1_ref, w2_ref, o_ref):
    nc = a_ref.shape[2]
    half = NSA_CMP_STRIDE * HEAD_DIM
    a = a_ref[0, 0].astype(F32)
    top = _dot((a + pos_ref[0:1, :]).astype(BF16), w1_ref[0:half, :])
    bot = _dot((a + pos_ref[1:2, :]).astype(BF16), w1_ref[half:2 * half, :])
    hid = top + pltpu.roll(bot, nc - 1, 0)
    out = _dot(_gelu_tanh(hid).astype(BF16), w2_ref[0])
    rowi = lax.broadcasted_iota(jnp.int32, out.shape, 0)
    o_ref[0, 0] = jnp.where(rowi < nc - 1, out, 0.0).astype(o_ref.dtype)


def _compress(tok, pos, w1, w2, bsz, seq):
    nc = seq // NSA_CMP_STRIDE
    half = NSA_CMP_STRIDE * HEAD_DIM
    pos2 = pos.reshape(2, half)
    z = jnp.zeros_like(w2)
    w2p = jnp.stack([jnp.concatenate([w2, z], axis=1), jnp.concatenate([z, w2], axis=1)]).astype(BF16)
    return pl.pallas_call(
        _compress_kernel,
        out_shape=jax.ShapeDtypeStruct((bsz, NSA_KV_HEADS, nc, LANES), BF16),
        grid=(bsz, NSA_KV_HEADS),
        in_specs=[pl.BlockSpec((1, 1, nc, half), lambda b, g: (b, g, 0, 0)),
                  pl.BlockSpec((2, half), lambda b, g: (0, 0)),
                  pl.BlockSpec((2 * half, w1.shape[1]), lambda b, g: (0, 0)),
                  pl.BlockSpec((1, w2.shape[0], LANES), lambda b, g: (g % 2, 0, 0))],
        out_specs=pl.BlockSpec((1, 1, nc, LANES), lambda b, g: (b, g, 0, 0)),
        compiler_params=_params(("parallel", "parallel")),
        name="nsa_compress",
    )(tok, pos2, w1.astype(BF16), w2p)


def _nsa_cmp_kernel(q_ref, kc_ref, vc_ref, bias_ref, gate_ref, o_ref, sel_ref, *, n_top):
    i = pl.program_id(2)
    nc = kc_ref.shape[2]
    n_chunks = nc // CMP_CHUNK
    nsb = sel_ref.shape[3]
    q4 = q_ref[0]
    gates = gate_ref[0, 0, 0]
    lane = lax.broadcasted_iota(jnp.int32, (TILE, LANES), 1)
    g_par = pl.program_id(1) % 2
    keep = (lane >= g_par * HEAD_DIM) & (lane < (g_par + 1) * HEAD_DIM)
    psum = [jnp.zeros((TILE, CMP_CHUNK), F32) for _ in range(n_chunks)]
    for r in range(NSA_GROUP):
        qh = q4[:, r * LANES:(r + 1) * LANES]
        ss = []
        for c in range(n_chunks):
            tile = jnp.clip(i - (CMP_CHUNK * NSA_CMP_STRIDE // TILE) * c, -1, CMP_TILES - 2) + 1
            ss.append(_dot_nt(qh, kc_ref[0, 0, c * CMP_CHUNK:(c + 1) * CMP_CHUNK, :]) + bias_ref[r, tile])
        m = ss[0].max(axis=-1, keepdims=True)
        for c in range(1, n_chunks):
            m = jnp.maximum(m, ss[c].max(axis=-1, keepdims=True))
        ps = [jnp.where(s > 0.5 * NEG_INF, jnp.exp(s - m), 0.0) for s in ss]
        l = ps[0].sum(axis=-1, keepdims=True)
        for c in range(1, n_chunks):
            l = l + ps[c].sum(axis=-1, keepdims=True)
        inv = 1.0 / jnp.maximum(l, 1e-30)
        acc = jnp.zeros((TILE, LANES), F32)
        for c in range(n_chunks):
            pc = ps[c] * inv
            psum[c] = psum[c] + pc
            acc = acc + _dot(pc.astype(BF16), vc_ref[0, 0, c * CMP_CHUNK:(c + 1) * CMP_CHUNK, :])
        o_ref[0, :, r * LANES:(r + 1) * LANES] = jnp.where(keep, acc * gates[:, r:r + 1], 0.0).astype(o_ref.dtype)
    imp = jnp.zeros((nsb, TILE), F32)
    per_sel = NSA_SEL_BLOCK // NSA_CMP_STRIDE
    for c in range(n_chunks):
        jb = lax.broadcasted_iota(jnp.int32, (nsb, CMP_CHUNK), 0)
        ci = lax.broadcasted_iota(jnp.int32, (nsb, CMP_CHUNK), 1) + c * CMP_CHUNK
        rel = ci - per_sel * jb
        over = (rel >= 1 - NSA_CMP_LEN // NSA_CMP_STRIDE) & (rel < per_sel) & (ci < nc - 1)
        ov = jnp.where(over, 1.0, 0.0).astype(BF16)
        hi, lo = _split_bf16(psum[c])
        imp = imp + _dot_nt(ov, hi) + _dot_nt(ov, lo)
    jb = lax.broadcasted_iota(jnp.int32, (nsb, TILE), 0)
    qpos = lax.broadcasted_iota(jnp.int32, (nsb, TILE), 1) + i * TILE
    cur = qpos // NSA_SEL_BLOCK
    forced = (jb == 0) | (jb == cur) | (jb == cur - 1)
    causal = jb * NSA_SEL_BLOCK <= qpos
    score = jnp.where(forced, FORCED_SCORE, jnp.where(causal, imp, NEG_INF))
    chosen = jnp.zeros((nsb, TILE), F32)
    for _ in range(n_top):
        mx = jnp.max(score, axis=0, keepdims=True)
        first = jnp.min(jnp.where(score == mx, jb, nsb), axis=0, keepdims=True)
        pick = jb == first
        chosen = jnp.where(pick, 1.0, chosen)
        score = jnp.where(pick, -jnp.inf, score)
    not_allowed = jnp.where((chosen > 0.0) & causal, 0.0, 1.0)
    sel_ref[0, 0] = not_allowed.T.astype(sel_ref.dtype)


def _nsa_sparse_kernel(q_ref, k_ref, v_ref, bias_ref, gate_ref, *rest, window):
    if window:
        (o_ref,) = rest
        nsel = None
    else:
        sel_ref, o_ref = rest
        nsel = sel_ref[0, 0]
    i = pl.program_id(2)
    q4 = q_ref[0]
    gates = gate_ref[0, 0, 0]
    q_heads = [q4[:, r * LANES:(r + 1) * LANES] for r in range(NSA_GROUP)]
    if window:
        n_tiles = NSA_WINDOW // TILE + 1
        lo = jnp.maximum(i - (n_tiles - 1), 0)
        res = _flash_chunks(q_heads, k_ref, v_ref, bias_ref, [None] * NSA_GROUP, lo, i + 1, i, 1, n_tiles)
    else:
        res = _flash_chunks(q_heads, k_ref, v_ref, bias_ref, [nsel] * NSA_GROUP, 0, i + 1, i,
                            TILE // NSA_SEL_BLOCK, MAIN_TILES)
    lane = lax.broadcasted_iota(jnp.int32, (TILE, LANES), 1)
    g_par = pl.program_id(1) % 2
    keep = (lane >= g_par * HEAD_DIM) & (lane < (g_par + 1) * HEAD_DIM)
    for r in range(NSA_GROUP):
        acc, l = res[r]
        o = acc / l * gates[:, r:r + 1]
        o_ref[0, :, r * LANES:(r + 1) * LANES] = jnp.where(keep, o, 0.0).astype(o_ref.dtype)


def _mixer_nsa(x, w_in, w_out, table_bias, pos_k, w1_k, w2_k, pos_v, w1_v, w2_v, bsz, seq):
    bias_main, bias_win, bias_cmp = table_bias
    aw = N_HEADS * HEAD_DIM
    kvw = NSA_KV_HEADS * HEAD_DIM
    G, R = NSA_KV_HEADS, NSA_GROUP
    nc = seq // NSA_CMP_STRIDE
    nsb = seq // NSA_SEL_BLOCK
    n_top = min(NSA_TOP_N, nsb)
    nq = seq // TILE
    assert nc % CMP_CHUNK == 0
    par = lambda h: (h // R) % 2
    wq = _pad_heads_cols(w_in[:, :aw] * ATTN_SCALE, par).astype(BF16)
    wkv = w_in[:, aw:aw + 6 * kvw].astype(BF16)
    wg = jnp.pad(w_in[:, aw + 6 * kvw:], ((0, 0), (0, LANES - 3 * N_HEADS))).astype(BF16)
    q = _matmul(x, wq, BF16).reshape(bsz, seq, N_HEADS * LANES)
    kv = _matmul(x, wkv, BF16).reshape(bsz, seq, 6 * kvw)
    gates = _matmul(x, wg, F32, act='sigmoid')[:, :3 * N_HEADS]
    gates = gates.reshape(bsz, seq, 3, G, R).transpose(2, 0, 3, 1, 4)

    def tokens(t):
        return t.reshape(bsz, seq, G, HEAD_DIM).transpose(0, 2, 1, 3).reshape(bsz, G, nc, NSA_CMP_STRIDE * HEAD_DIM)

    kc = _compress(tokens(kv[..., :kvw]), pos_k, w1_k, w2_k, bsz, seq)
    vc = _compress(tokens(kv[..., kvw:2 * kvw]), pos_v, w1_v, w2_v, bsz, seq)

    q_spec = pl.BlockSpec((1, TILE, R * LANES), lambda b, g, i: (b, i, g))
    o_spec = pl.BlockSpec((1, TILE, R * LANES), lambda b, g, i: (b, i, g))
    gate_spec = lambda br: pl.BlockSpec((1, 1, 1, TILE, R), lambda b, g, i: (br, b, g, i, 0))
    o_shape = jax.ShapeDtypeStruct((bsz, seq, N_HEADS * LANES), BF16)
    sem = ("parallel", "parallel", "arbitrary")

    o_cmp, nsel = pl.pallas_call(
        functools.partial(_nsa_cmp_kernel, n_top=n_top),
        out_shape=[o_shape, jax.ShapeDtypeStruct((bsz, G, seq, nsb), BF16)],
        grid=(bsz, G, nq),
        in_specs=[q_spec,
                  pl.BlockSpec((1, 1, nc, LANES), lambda b, g, i: (b, g, 0, 0)),
                  pl.BlockSpec((1, 1, nc, LANES), lambda b, g, i: (b, g, 0, 0)),
                  pl.BlockSpec((R, CMP_TILES, TILE, CMP_CHUNK), lambda b, g, i: (g, 0, 0, 0)),
                  gate_spec(0)],
        out_specs=[o_spec, pl.BlockSpec((1, 1, TILE, nsb), lambda b, g, i: (b, g, i, 0))],
        compiler_params=_params(sem),
        name="nsa_compressed_attn",
    )(q, kc, vc, bias_cmp, gates)

    def kv_spec(first_blk):
        return pl.BlockSpec((1, seq, LANES), lambda b, g, i: (b, 0, first_blk + g // 2))

    o_sel = pl.pallas_call(
        functools.partial(_nsa_sparse_kernel, window=False),
        out_shape=o_shape,
        grid=(bsz, G, nq),
        in_specs=[q_spec, kv_spec(4), kv_spec(6),
                  pl.BlockSpec((R, MAIN_TILES, TILE, TILE), lambda b, g, i: (g, 0, 0, 0)),
                  gate_spec(1),
                  pl.BlockSpec((1, 1, TILE, nsb), lambda b, g, i: (b, g, i, 0))],
        out_specs=o_spec,
        compiler_params=_params(sem),
        name="nsa_selected_attn",
    )(q, kv, kv, bias_main, gates, nsel)

    o_win = pl.pallas_call(
        functools.partial(_nsa_sparse_kernel, window=True),
        out_shape=o_shape,
        grid=(bsz, G, nq),
        in_specs=[q_spec, kv_spec(8), kv_spec(10),
                  pl.BlockSpec((R, NSA_WINDOW // TILE + 1, TILE, TILE), lambda b, g, i: (g, 0, 0, 0)),
                  gate_spec(2)],
        out_specs=o_spec,
        compiler_params=_params(sem),
        name="nsa_window_attn",
    )(q, kv, kv, bias_win, gates)

    m = bsz * seq
    parts = [t.reshape(m, N_HEADS * LANES) for t in (o_cmp, o_sel, o_win)]
    return parts, _pad_heads_rows(w_out, par).astype(BF16)


def _router_kernel(x_ref, w_ref, b_ref, o_ref):
    logits = lax.dot_general(w_ref[...], x_ref[...], (((1,), (1,)), ((), ())),
                             preferred_element_type=F32, precision=lax.Precision.HIGHEST)
    scores = 1.0 / (1.0 + jnp.exp(-logits))
    biased = scores + b_ref[...]
    rows = [biased[e:e + 1, :] for e in range(N_EXPERTS)]
    group_score = []
    for g in range(N_GROUPS):
        r = rows[g * EXPERTS_PER_GROUP:(g + 1) * EXPERTS_PER_GROUP]
        best = None
        for a in range(EXPERTS_PER_GROUP):
            for c in range(a + 1, EXPERTS_PER_GROUP):
                pair = r[a] + r[c]
                best = pair if best is None else jnp.maximum(best, pair)
        group_score.append(best)
    best_val = group_score[0]
    best_grp = jnp.zeros_like(best_val, dtype=jnp.int32)
    for g in range(1, N_GROUPS):
        better = group_score[g] > best_val
        best_val = jnp.where(better, group_score[g], best_val)
        best_grp = jnp.where(better, g, best_grp)
    picked = []
    for e in range(N_EXPERTS):
        g, a = divmod(e, EXPERTS_PER_GROUP)
        rank = jnp.zeros_like(best_grp)
        for c in range(EXPERTS_PER_GROUP):
            if c == a:
                continue
            other = rows[g * EXPERTS_PER_GROUP + c]
            ahead = (other > rows[e]) | ((other == rows[e]) & (c < a))
            rank = rank + jnp.where(ahead, 1, 0)
        picked.append((best_grp == g) & (rank < 2))
    raw = [jnp.where(picked[e], scores[e:e + 1, :], 0.0) for e in range(N_EXPERTS)]
    total = raw[0]
    for e in range(1, N_EXPERTS):
        total = total + raw[e]
    o_ref[...] = jnp.concatenate(raw, axis=0) / total


def _moe_kernel(x_ref, gate_ref, wg_ref, wu_ref, wd_ref, g_ref, b_ref, o_ref, xb_ref, acc_ref):
    e = pl.program_id(1)

    @pl.when(e == 0)
    def _():
        xb_ref[...] = x_ref[...].astype(BF16)
        acc_ref[...] = jnp.zeros_like(acc_ref)

    xb = xb_ref[...]
    lane = lax.broadcasted_iota(jnp.int32, gate_ref.shape, 1)
    gcol = jnp.sum(jnp.where(lane == e, gate_ref[...], 0.0), axis=-1, keepdims=True)
    a = _dot(xb, wg_ref[0])
    h = a / (1.0 + jnp.exp(-a)) * _dot(xb, wu_ref[0])
    acc_ref[...] += gcol * _dot(h.astype(BF16), wd_ref[0])

    @pl.when(e == N_EXPERTS - 1)
    def _():
        o_ref[...] = _layer_norm(DEEPNORM_ALPHA * x_ref[...] + acc_ref[...], g_ref[...], b_ref[...])


def _moe_ln(x, router_w, router_b, w_gate, w_up, w_down, g, b):
    m, d = x.shape
    tm = 1024
    gates_t = pl.pallas_call(
        _router_kernel,
        out_shape=jax.ShapeDtypeStruct((N_EXPERTS, m), F32),
        grid=(m // tm,),
        in_specs=[pl.BlockSpec((tm, d), lambda i: (i, 0)),
                  pl.BlockSpec((N_EXPERTS, d), lambda i: (0, 0)),
                  pl.BlockSpec((N_EXPERTS, 1), lambda i: (0, 0))],
        out_specs=pl.BlockSpec((N_EXPERTS, tm), lambda i: (0, i)),
        compiler_params=_params(("parallel",)),
        name="moe_router",
    )(x, router_w.T, router_b.reshape(N_EXPERTS, 1))
    gates = gates_t.T
    de = w_gate.shape[-1]
    return pl.pallas_call(
        _moe_kernel,
        out_shape=jax.ShapeDtypeStruct((m, d), F32),
        grid=(m // tm, N_EXPERTS),
        in_specs=[pl.BlockSpec((tm, d), lambda i, e: (i, 0)),
                  pl.BlockSpec((tm, N_EXPERTS), lambda i, e: (i, 0)),
                  pl.BlockSpec((1, d, de), lambda i, e: (e, 0, 0)),
                  pl.BlockSpec((1, d, de), lambda i, e: (e, 0, 0)),
                  pl.BlockSpec((1, de, d), lambda i, e: (e, 0, 0)),
                  pl.BlockSpec((1, d), lambda i, e: (0, 0)),
                  pl.BlockSpec((1, d), lambda i, e: (0, 0))],
        out_specs=pl.BlockSpec((tm, d), lambda i, e: (i, 0)),
        scratch_shapes=[pltpu.VMEM((tm, d), BF16), pltpu.VMEM((tm, d), F32)],
        compiler_params=_params(("parallel", "arbitrary")),
        name="moe_experts_ln",
    )(x, gates, w_gate.astype(BF16), w_up.astype(BF16), w_down.astype(BF16), g.reshape(1, d), b.reshape(1, d))


def kernel(x, rel_table, router_w, router_b, ln1_g, ln1_b, ln2_g, ln2_b, exp_w_gate, exp_w_up, exp_w_down, dil_w_in, dil_w_out, sb_w_in, sb_w_out, nsa_w_in, nsa_w_out, nsa_cmp_pos_k, nsa_cmp_w1_k, nsa_cmp_w2_k, nsa_cmp_pos_v, nsa_cmp_w1_v, nsa_cmp_w2_v, moba_w_in, moba_w_out):
    bsz, seq, d = x.shape
    assert d == D_MODEL and seq % (DILATED_PAIRS[-1][1] * DIL_BLOCK) == 0
    depth = ln1_g.shape[0]
    n_mixers = 4
    table = rel_table.astype(F32)
    span = DIL_BLOCK
    big = 1 << 30
    bias_dil = _bias_tiles(table, len(DILATED_PAIRS), DIL_BLOCK, 2 * DIL_BLOCK, DIL_BLOCK, 0, 1, 0, span,
                           [dl for _, dl in DILATED_PAIRS])
    bias_main = _bias_tiles(table, MAIN_TILES, TILE, TILE, 0, TILE, 1, 0, big, [1])
    bias_win = _bias_tiles(table, NSA_WINDOW // TILE + 1, TILE, TILE, 0, TILE, 1, 0, NSA_WINDOW - 1, [1])
    bias_cmp = _bias_tiles(table, CMP_TILES, TILE, CMP_CHUNK, -TILE - (NSA_CMP_LEN - 1), TILE, NSA_CMP_STRIDE,
                           0, big, [1])
    xf = x.reshape(bsz * seq, d)
    for layer in range(depth):
        kind, occ = layer % n_mixers, layer // n_mixers
        merge = None
        if kind == 0:
            parts, w_out = _mixer_dilated(xf, dil_w_in[occ], dil_w_out[occ], bias_dil, bsz, seq)
            merge = 'lse3'
        elif kind == 1:
            parts, w_out = _mixer_stick_breaking(xf, sb_w_in[occ], sb_w_out[occ], bsz, seq)
        elif kind == 2:
            parts, w_out = _mixer_nsa(xf, nsa_w_in[occ], nsa_w_out[occ], (bias_main, bias_win, bias_cmp),
                                      nsa_cmp_pos_k[occ], nsa_cmp_w1_k[occ], nsa_cmp_w2_k[occ],
                                      nsa_cmp_pos_v[occ], nsa_cmp_w1_v[occ], nsa_cmp_w2_v[occ], bsz, seq)
            merge = 'sum3'
        else:
            parts, w_out = _mixer_moba(xf, moba_w_in[occ], moba_w_out[occ], bias_main, bsz, seq)
        xf = _outproj_ln(parts, w_out, xf, ln1_g[layer], ln1_b[layer], merge=merge)
        xf = _moe_ln(xf, router_w, router_b, exp_w_gate[layer], exp_w_up[layer], exp_w_down[layer],
                     ln2_g[layer], ln2_b[layer])
    return xf.reshape(bsz, seq, d)
```

```python
import functools
import math

import numpy as np
import jax
import jax.numpy as jnp
from jax import lax
from jax.experimental import pallas as pl
from jax.experimental.pallas import tpu as pltpu

F32 = jnp.float32
BF16 = jnp.bfloat16

D_MODEL = 1024
HEAD_DIM = 64
N_HEADS = 16
LANES = 128
PAIRS = N_HEADS // 2
ATTN_SCALE = HEAD_DIM ** -0.5
REL_BUCKETS = 32
REL_MAX_DIST = 2048
DILATED_PAIRS = ((128, 1), (512, 4), (2048, 16))
DIL_BLOCK = 128
SB_CLIP = 60.0
LOG2_E = math.log2(math.e)
NSA_KV_HEADS = 4
NSA_GROUP = N_HEADS // NSA_KV_HEADS
NSA_CMP_LEN = 32
NSA_CMP_STRIDE = 16
NSA_SEL_BLOCK = 64
NSA_TOP_N = 16
NSA_WINDOW = 512
MOBA_BLOCK = 256
MOBA_TOP_K = 3
N_EXPERTS = 16
N_GROUPS = 4
EXPERTS_PER_GROUP = N_EXPERTS // N_GROUPS
DEPTH = 4
DEEPNORM_ALPHA = (2 * DEPTH) ** 0.25
LN_EPS = 1e-5
NEG_INF = -1e30
FORCED_SCORE = 1e9
TILE = 256
MAIN_TILES = 10
CMP_CHUNK = 128
CMP_TILES = 19
VMEM_LIMIT = 56 * 1024 * 1024


def _bucket_thresholds():
    n = np.arange(0, 2 * REL_MAX_DIST)
    exact = REL_BUCKETS // 2
    logf = np.log(np.maximum(n, 1).astype(np.float64) / exact) / math.log(REL_MAX_DIST / exact)
    large = np.minimum(exact + (logf * (REL_BUCKETS - exact)).astype(np.int64), REL_BUCKETS - 1)
    bucket = np.where(n < exact, n, large)
    return tuple(int(np.argmax(bucket >= k)) for k in range(1, REL_BUCKETS))


BUCKET_THRESHOLDS = _bucket_thresholds()


def _params(sem, vmem=VMEM_LIMIT):
    return pltpu.CompilerParams(dimension_semantics=sem, vmem_limit_bytes=vmem)


def _dot(a, b):
    return jnp.dot(a, b, preferred_element_type=F32)


def _dot_nt(a, b):
    return lax.dot_general(a, b, (((1,), (1,)), ((), ())), preferred_element_type=F32)


def _split_bf16(x):
    hi = x.astype(BF16)
    lo = (x - hi.astype(F32)).astype(BF16)
    return hi, lo


def _matmul_kernel(x_ref, w_ref, o_ref, *, act):
    y = _dot(x_ref[...].astype(BF16), w_ref[...])
    if act == 'sigmoid':
        y = 1.0 / (1.0 + jnp.exp(-y))
    o_ref[...] = y.astype(o_ref.dtype)


def _matmul(x, w, out_dtype, act=None):
    m, k = x.shape
    n = w.shape[1]
    tm = 512
    tn = 512 if n % 512 == 0 else 128
    return pl.pallas_call(
        functools.partial(_matmul_kernel, act=act),
        out_shape=jax.ShapeDtypeStruct((m, n), out_dtype),
        grid=(m // tm, n // tn),
        in_specs=[pl.BlockSpec((tm, k), lambda i, j: (i, 0)),
                  pl.BlockSpec((k, tn), lambda i, j: (0, j))],
        out_specs=pl.BlockSpec((tm, tn), lambda i, j: (i, j)),
        compiler_params=_params(("parallel", "arbitrary")),
        name="proj_matmul",
    )(x, w)


def _matmul_heads_kernel(x_ref, w_ref, o_ref):
    o_ref[0, 0] = _dot(x_ref[...].astype(BF16), w_ref[0]).astype(o_ref.dtype)


def _matmul_heads(x, w, bsz, seq):
    m, k = x.shape
    n = w.shape[0]
    tm = 512
    per_b = seq // tm
    return pl.pallas_call(
        _matmul_heads_kernel,
        out_shape=jax.ShapeDtypeStruct((bsz, n, seq, HEAD_DIM), BF16),
        grid=(m // tm, n),
        in_specs=[pl.BlockSpec((tm, k), lambda i, j: (i, 0)),
                  pl.BlockSpec((1, k, HEAD_DIM), lambda i, j: (j, 0, 0))],
        out_specs=pl.BlockSpec((1, 1, tm, HEAD_DIM), lambda i, j: (i // per_b, j, i % per_b, 0)),
        compiler_params=_params(("parallel", "arbitrary")),
        name="proj_heads",
    )(x, w)


def _layer_norm(z, g, b):
    mu = jnp.mean(z, axis=-1, keepdims=True)
    zc = z - mu
    var = jnp.mean(zc * zc, axis=-1, keepdims=True)
    return zc * lax.rsqrt(var + LN_EPS) * g + b


def _outproj_kernel(*refs, n_in, merge):
    ins = refs[:n_in]
    w_ref, x_ref, g_ref, b_ref, o_ref = refs[n_in:]
    if merge == 'lse3':
        o1, o2, o3, l1, l2, l3 = [r[...] for r in ins]
        mx = jnp.maximum(jnp.maximum(l1, l2), l3)
        e1, e2, e3 = jnp.exp(l1 - mx), jnp.exp(l2 - mx), jnp.exp(l3 - mx)
        a = (e1 * o1 + e2 * o2 + e3 * o3) / (e1 + e2 + e3)
    elif merge == 'sum3':
        a = ins[0][...].astype(F32) + ins[1][...].astype(F32) + ins[2][...].astype(F32)
    else:
        a = ins[0][...]
    y = _dot(a.astype(BF16), w_ref[...])
    o_ref[...] = _layer_norm(DEEPNORM_ALPHA * x_ref[...] + y, g_ref[...], b_ref[...])


def _outproj_ln(ins, w, x, g, b, merge=None):
    m, d = x.shape
    ka = ins[0].shape[1]
    tm = 256
    n_in = len(ins)
    return pl.pallas_call(
        functools.partial(_outproj_kernel, n_in=n_in, merge=merge),
        out_shape=jax.ShapeDtypeStruct((m, d), F32),
        grid=(m // tm,),
        in_specs=[pl.BlockSpec((tm, ka), lambda i: (i, 0)) for _ in ins]
        + [pl.BlockSpec((ka, d), lambda i: (0, 0)),
           pl.BlockSpec((tm, d), lambda i: (i, 0)),
           pl.BlockSpec((1, d), lambda i: (0, 0)),
           pl.BlockSpec((1, d), lambda i: (0, 0))],
        out_specs=pl.BlockSpec((tm, d), lambda i: (i, 0)),
        compiler_params=_params(("parallel",)),
        name="outproj_ln",
    )(*ins, w, x, g.reshape(1, d), b.reshape(1, d))


def _bias_kernel(tbl_ref, o_ref, *, rows, cols, off0, tstride, cmul, lo, hi, mults):
    t = pl.program_id(0)
    if len(set(mults)) == 1:
        mult = mults[0]
    else:
        mult = jnp.int32(mults[-1])
        for idx in range(len(mults) - 2, -1, -1):
            mult = jnp.where(t == idx, jnp.int32(mults[idx]), mult)
    base = off0 + tstride * t

    def strip(i, carry):
        r0 = pl.multiple_of(i * 8, 8)
        a = lax.broadcasted_iota(jnp.int32, (8, cols), 0) + r0
        c = lax.broadcasted_iota(jnp.int32, (8, cols), 1)
        steps = base + a - cmul * c
        valid = (steps >= lo) & (steps <= hi)
        dist = steps * mult
        for h in range(N_HEADS):
            acc = jnp.full((8, cols), tbl_ref[0, h], F32)
            for k, thr in enumerate(BUCKET_THRESHOLDS):
                acc = jnp.where(dist >= thr, tbl_ref[k + 1, h], acc)
            o_ref[h, 0, pl.ds(r0, 8), :] = jnp.where(valid, acc, NEG_INF)
        return carry

    lax.fori_loop(0, rows // 8, strip, 0)


def _bias_tiles(table, n_tiles, rows, cols, off0, tstride, cmul, lo, hi, mults):
    return pl.pallas_call(
        functools.partial(_bias_kernel, rows=rows, cols=cols, off0=off0, tstride=tstride, cmul=cmul,
                          lo=lo, hi=hi, mults=tuple(mults)),
        out_shape=jax.ShapeDtypeStruct((N_HEADS, n_tiles, rows, cols), F32),
        grid=(n_tiles,),
        in_specs=[pl.BlockSpec(memory_space=pltpu.SMEM)],
        out_specs=pl.BlockSpec((N_HEADS, 1, rows, cols), lambda t: (0, t, 0, 0)),
        compiler_params=_params(("arbitrary",)),
        name="bias_tiles",
    )(table)


def _dil_kernel(q_ref, kp_ref, kc_ref, vp_ref, vc_ref, bias_ref, o_ref, lse_ref, *, per_step):
    j = pl.program_id(3)
    k_all = jnp.concatenate([kp_ref[0], kc_ref[0]], axis=0)
    v_all = jnp.concatenate([vp_ref[0], vc_ref[0]], axis=0)
    colk = lax.broadcasted_iota(jnp.int32, (DIL_BLOCK, 2 * DIL_BLOCK), 1)
    first = jnp.where((colk < DIL_BLOCK) & (j == 0), NEG_INF, 0.0)
    low = lax.broadcasted_iota(jnp.int32, (DIL_BLOCK, LANES), 1) < HEAD_DIM
    for u in range(per_step):
        rows = slice(u * DIL_BLOCK, (u + 1) * DIL_BLOCK)
        q2 = q_ref[0, rows, :]
        k2 = k_all[u * DIL_BLOCK:(u + 2) * DIL_BLOCK]
        v2 = v_all[u * DIL_BLOCK:(u + 2) * DIL_BLOCK]
        outs, lses = [], []
        for h in range(2):
            s = _dot_nt(q2[:, h * LANES:(h + 1) * LANES], k2) + bias_ref[h, 0]
            if u == 0:
                s = s + first
            m = jnp.max(s, axis=-1, keepdims=True)
            p = jnp.exp(s - m)
            l = jnp.maximum(jnp.sum(p, axis=-1, keepdims=True), 1e-30)
            outs.append(_dot(p.astype(BF16), v2) / l)
            lses.append(m + jnp.log(l))
        o_ref[0, rows, :] = jnp.where(low, outs[0], outs[1])
        lse_ref[0, rows, :] = jnp.where(low, lses[0], lses[1])


def _dilated_group(proj, bias, g, dilation, bsz, seq):
    width = proj.shape[-1]
    length = seq // dilation
    nb = length // DIL_BLOCK
    per_step = math.gcd(nb, 4)
    rows = per_step * DIL_BLOCK
    wb = width // LANES
    pv = proj.reshape(bsz, length, dilation * width)
    q_blk = g * N_HEADS // 2
    k_blk = 3 * N_HEADS + g * 2 * PAIRS
    v_blk = k_blk + PAIRS

    def kv_specs(base):
        prev = pl.BlockSpec((1, DIL_BLOCK, LANES),
                            lambda hp, b, r, j: (b, jnp.maximum(per_step * j - 1, 0), r * wb + base + hp))
        cur = pl.BlockSpec((1, rows, LANES), lambda hp, b, r, j: (b, j, r * wb + base + hp))
        return [prev, cur]

    out_spec = pl.BlockSpec((1, rows, LANES), lambda hp, b, r, j: (b, j, r * PAIRS + hp))
    o, lse = pl.pallas_call(
        functools.partial(_dil_kernel, per_step=per_step),
        out_shape=[jax.ShapeDtypeStruct((bsz, length, dilation * D_MODEL), F32)] * 2,
        grid=(PAIRS, bsz, dilation, nb // per_step),
        in_specs=[pl.BlockSpec((1, rows, 2 * LANES), lambda hp, b, r, j: (b, j, r * (wb // 2) + q_blk + hp))]
        + kv_specs(k_blk) + kv_specs(v_blk)
        + [pl.BlockSpec((2, 1, DIL_BLOCK, 2 * DIL_BLOCK), lambda hp, b, r, j: (hp, g, 0, 0))],
        out_specs=[out_spec, out_spec],
        compiler_params=_params(("parallel", "parallel", "parallel", "arbitrary")),
        name="dilated_attn",
    )(pv, pv, pv, pv, pv, bias)
    return o.reshape(bsz * seq, D_MODEL), lse.reshape(bsz * seq, D_MODEL)


def _pad_heads_cols(w, parity_of_head):
    k, n = w.shape
    nh = n // HEAD_DIM
    wh = w.reshape(k, nh, HEAD_DIM)
    z = jnp.zeros_like(wh)
    par = jnp.asarray([parity_of_head(h) for h in range(nh)], jnp.int32)[None, :, None]
    lo = jnp.where(par == 0, wh, z)
    hi = jnp.where(par == 1, wh, z)
    return jnp.concatenate([lo, hi], axis=-1).reshape(k, nh * LANES)


def _pad_heads_rows(w, parity_of_head):
    return _pad_heads_cols(w.T, parity_of_head).T


def _mixer_dilated(x, w_in, w_out, bias_dil, bsz, seq):
    aw = N_HEADS * HEAD_DIM
    par = lambda h: h % 2
    qs, kvs = [], []
    for g in range(len(DILATED_PAIRS)):
        base = g * 3 * aw
        qs.append(_pad_heads_cols(w_in[:, base:base + aw] * ATTN_SCALE, par))
        kvs.append(w_in[:, base + aw:base + 3 * aw])
    w_all = jnp.concatenate(qs + kvs, axis=1).astype(BF16)
    proj = _matmul(x, w_all, BF16).reshape(bsz, seq, -1)
    parts = [_dilated_group(proj, bias_dil, g, d, bsz, seq) for g, (_, d) in enumerate(DILATED_PAIRS)]
    return [p[0] for p in parts] + [p[1] for p in parts], w_out.astype(BF16)


def _sb_kernel(q_ref, k_ref, v_ref, o_ref):
    i = pl.program_id(2)
    t = TILE
    row = lax.broadcasted_iota(jnp.int32, (t, t), 0)
    col = lax.broadcasted_iota(jnp.int32, (t, t), 1)
    suffix = jnp.where(row >= col, 1.0, 0.0).astype(BF16)
    strict = col < row
    q2 = q_ref[0]
    q_heads = [q2[:, h * LANES:(h + 1) * LANES] for h in range(2)]

    def block(qh, kj, masked):
        start = pl.multiple_of(kj * t, t)
        kb = k_ref[0, pl.ds(start, t), :]
        vb = v_ref[0, pl.ds(start, t), :]
        a = jnp.clip(_dot_nt(qh, kb), -SB_CLIP, SB_CLIP) * LOG2_E
        sp = jnp.log2(1.0 + jnp.exp2(a))
        if masked:
            sp = jnp.where(strict, sp, 0.0)
        hi, lo = _split_bf16(sp)
        rr = _dot(hi, suffix) + _dot(lo, suffix)
        att = jnp.exp2(a - rr)
        if masked:
            att = jnp.where(strict, att, 0.0)
        return _dot(att.astype(BF16), vb), rr[:, 0:1]

    def diagonal(kj, carry):
        return tuple(block(qh, kj, True) for qh in q_heads)

    zero = (jnp.zeros((t, LANES), F32), jnp.zeros((t, 1), F32))
    (acc0, r0), (acc1, r1) = lax.fori_loop(i, i + 1, diagonal, (zero, zero))

    def cond(carry):
        n, alive = carry[0], carry[1]
        return (n < i) & alive

    def step(carry):
        n = carry[0]
        new, tops = [], []
        for qh, (r_sum, w, acc) in zip(q_heads, carry[2:]):
            pv, r_blk = block(qh, i - 1 - n, False)
            r_new = r_sum + r_blk
            w_new = jnp.exp2(-r_new)
            new.append((r_new, w_new, acc + w * pv))
            tops.append(jnp.max(w_new, axis=0, keepdims=True))
        alive = jnp.maximum(tops[0], tops[1])[0, 0] > 0.0
        return (n + 1, alive) + tuple(new)

    init = (jnp.int32(0), jnp.bool_(True), (r0, jnp.exp2(-r0), acc0), (r1, jnp.exp2(-r1), acc1))
    res = lax.while_loop(cond, step, init)
    lane = lax.broadcasted_iota(jnp.int32, (t, LANES), 1)
    o_ref[0] = jnp.where(lane < HEAD_DIM, res[2][2], res[3][2]).astype(o_ref.dtype)


def _mixer_stick_breaking(x, w_in, w_out, bsz, seq):
    aw = N_HEADS * HEAD_DIM
    wq = _pad_heads_cols(w_in[:, :aw] * ATTN_SCALE, lambda h: h % 2)
    w_all = jnp.concatenate([wq, w_in[:, aw:]], axis=1).astype(BF16)
    proj = _matmul(x, w_all, BF16).reshape(bsz, seq, -1)
    kb = N_HEADS
    o = pl.pallas_call(
        _sb_kernel,
        out_shape=jax.ShapeDtypeStruct((bsz, seq, D_MODEL), BF16),
        grid=(bsz, PAIRS, seq // TILE),
        in_specs=[pl.BlockSpec((1, TILE, 2 * LANES), lambda b, hp, i: (b, i, hp)),
                  pl.BlockSpec((1, seq, LANES), lambda b, hp, i: (b, 0, kb + hp)),
                  pl.BlockSpec((1, seq, LANES), lambda b, hp, i: (b, 0, kb + PAIRS + hp))],
        out_specs=pl.BlockSpec((1, TILE, LANES), lambda b, hp, i: (b, i, hp)),
        compiler_params=_params(("parallel", "parallel", "arbitrary")),
        name="stick_breaking_attn",
    )(proj, proj, proj)
    return [o.reshape(bsz * seq, D_MODEL)], w_out.astype(BF16)


def _flash_chunks(q_heads, k_ref, v_ref, bias_ref, nsel_of_head, lo, hi, i, blocks_per_chunk, n_bias_tiles,
                  pipelined=False):
    nh = len(q_heads)
    distinct = []
    for ns in nsel_of_head:
        if ns is not None and all(ns is not d for d in distinct):
            distinct.append(ns)

    def logits(ch):
        start = pl.multiple_of(ch * TILE, TILE)
        kb = k_ref[0, pl.ds(start, TILE), :]
        tile = jnp.minimum(i - ch, n_bias_tiles - 1)
        adds = []
        for ns in distinct:
            nblk = ns.shape[1]
            rj = lax.broadcasted_iota(jnp.int32, (nblk, TILE), 0)
            ck = lax.broadcasted_iota(jnp.int32, (nblk, TILE), 1)
            blk_of_key = ch * blocks_per_chunk + ck // (TILE // blocks_per_chunk)
            expand = jnp.where(rj == blk_of_key, NEG_INF, 0.0).astype(BF16)
            adds.append(_dot(ns, expand))
        out = []
        for h in range(nh):
            s = _dot_nt(q_heads[h], kb) + bias_ref[h, tile]
            ns = nsel_of_head[h]
            if ns is not None:
                s = s + adds[[d is ns for d in distinct].index(True)]
            out.append(s)
        return tuple(out)

    def update(ch, s_heads, state):
        start = pl.multiple_of(ch * TILE, TILE)
        vb = v_ref[0, pl.ds(start, TILE), :]
        new = []
        for h in range(nh):
            m, l, acc = state[h]
            s = s_heads[h]
            m_new = jnp.maximum(m, jnp.max(s, axis=-1, keepdims=True))
            alpha = jnp.exp(m - m_new)
            p = jnp.exp(s - m_new)
            l = alpha * l + jnp.sum(p, axis=-1, keepdims=True)
            acc = alpha * acc + _dot(p.astype(BF16), vb)
            new.append((m_new, l, acc))
        return tuple(new)

    state0 = tuple((jnp.full((TILE, 1), NEG_INF, F32), jnp.zeros((TILE, 1), F32), jnp.zeros((TILE, LANES), F32))
                   for _ in range(nh))
    if not pipelined:
        res = lax.fori_loop(lo, hi, lambda ch, state: update(ch, logits(ch), state), state0)
        return [(acc, l) for (_, l, acc) in res]

    def body(ch, carry):
        s_cur, state = carry
        s_next = logits(jnp.minimum(ch + 1, hi - 1))
        return s_next, update(ch, s_cur, state)

    zeros = tuple(jnp.zeros((TILE, TILE), F32) for _ in range(nh))
    s0 = lax.fori_loop(lo, lo + 1, lambda ch, _: logits(ch), zeros)
    _, res = lax.fori_loop(lo, hi, body, (s0, state0))
    return [(acc, l) for (_, l, acc) in res]


def _kmean_kernel(k_ref, o_ref):
    rows = k_ref.shape[1]
    kf = k_ref[0].astype(F32).reshape(rows // MOBA_BLOCK, MOBA_BLOCK, k_ref.shape[2])
    o_ref[0] = jnp.sum(kf, axis=1) * (1.0 / MOBA_BLOCK)


def _moba_kernel(q_ref, k_ref, v_ref, km_ref, bias_ref, o_ref):
    i = pl.program_id(2)
    nblk = km_ref.shape[1]
    km_hi, km_lo = _split_bf16(km_ref[0])
    q2 = q_ref[0]
    blk = lax.broadcasted_iota(jnp.int32, (TILE, nblk), 1)
    past = blk < i
    q_heads, nsels = [], []
    for h in range(2):
        qh = q2[:, h * LANES:(h + 1) * LANES]
        gate = jnp.where(past, _dot_nt(qh, km_hi) + _dot_nt(qh, km_lo), NEG_INF)
        allowed = blk == i
        for _ in range(MOBA_TOP_K):
            mx = jnp.max(gate, axis=-1, keepdims=True)
            first = jnp.min(jnp.where(gate == mx, blk, nblk), axis=-1, keepdims=True)
            pick = blk == first
            allowed = allowed | (pick & past)
            gate = jnp.where(pick, -jnp.inf, gate)
        q_heads.append(qh)
        nsels.append(jnp.where(allowed, 0.0, 1.0).astype(BF16))
    res = _flash_chunks(q_heads, k_ref, v_ref, bias_ref, nsels, 0, i + 1, i, 1, MAIN_TILES, pipelined=True)
    lane = lax.broadcasted_iota(jnp.int32, (TILE, LANES), 1)
    o0 = res[0][0] / res[0][1]
    o1 = res[1][0] / res[1][1]
    o_ref[0] = jnp.where(lane < HEAD_DIM, o0, o1).astype(o_ref.dtype)


def _mixer_moba(x, w_in, w_out, bias_main, bsz, seq):
    aw = N_HEADS * HEAD_DIM
    nblk = seq // MOBA_BLOCK
    assert nblk - 1 >= MOBA_TOP_K
    wq = _pad_heads_cols(w_in[:, :aw] * ATTN_SCALE, lambda h: h % 2)
    w_all = jnp.concatenate([wq, w_in[:, aw:]], axis=1).astype(BF16)
    proj = _matmul(x, w_all, BF16).reshape(bsz, seq, -1)
    kb = N_HEADS
    rows = 8 * MOBA_BLOCK
    kmean = pl.pallas_call(
        _kmean_kernel,
        out_shape=jax.ShapeDtypeStruct((bsz, nblk, D_MODEL), F32),
        grid=(bsz, seq // rows),
        in_specs=[pl.BlockSpec((1, rows, D_MODEL), lambda b, i: (b, i, kb * LANES // D_MODEL))],
        out_specs=pl.BlockSpec((1, 8, D_MODEL), lambda b, i: (b, i, 0)),
        compiler_params=_params(("parallel", "parallel")),
        name="moba_kmean",
    )(proj)
    o = pl.pallas_call(
        _moba_kernel,
        out_shape=jax.ShapeDtypeStruct((bsz, seq, D_MODEL), BF16),
        grid=(bsz, PAIRS, seq // TILE),
        in_specs=[pl.BlockSpec((1, TILE, 2 * LANES), lambda b, hp, i: (b, i, hp)),
                  pl.BlockSpec((1, seq, LANES), lambda b, hp, i: (b, 0, kb + hp)),
                  pl.BlockSpec((1, seq, LANES), lambda b, hp, i: (b, 0, kb + PAIRS + hp)),
                  pl.BlockSpec((1, nblk, LANES), lambda b, hp, i: (b, 0, hp)),
                  pl.BlockSpec((2, MAIN_TILES, TILE, TILE), lambda b, hp, i: (hp, 0, 0, 0))],
        out_specs=pl.BlockSpec((1, TILE, LANES), lambda b, hp, i: (b, i, hp)),
        compiler_params=_params(("parallel", "parallel", "arbitrary")),
        name="moba_attn",
    )(proj, proj, proj, kmean, bias_main)
    return [o.reshape(bsz * seq, D_MODEL)], w_out.astype(BF16)


def _gelu_tanh(x):
    return 0.5 * x * (1.0 + jnp.tanh(math.sqrt(2.0 / math.pi) * (x + 0.044715 * (x * x * x))))


def _compress_kernel(a_ref, pos_ref, w1_ref, w2_ref, o_ref):
    nc = a_ref.shape[2]
    half = NSA_CMP_STRIDE * HEAD_DIM
    a = a_ref[0, 0].astype(F32)
    top = _dot((a + pos_ref[0:1, :]).astype(BF16), w1_ref[0:half, :])
    bot = _dot((a + pos_ref[1:2, :]).astype(BF16), w1_ref[half:2 * half, :])
    hid = top + pltpu.roll(bot, nc - 1, 0)
    out = _dot(_gelu_tanh(hid).astype(BF16), w2_ref[0])
    rowi = lax.broadcasted_iota(jnp.int32, out.shape, 0)
    o_ref[0, 0] = jnp.where(rowi < nc - 1, out, 0.0).astype(o_ref.dtype)


def _compress(tok, first, pos, w1, w2, bsz, seq):
    nc = seq // NSA_CMP_STRIDE
    half = NSA_CMP_STRIDE * HEAD_DIM
    pos2 = pos.reshape(2, half)
    z = jnp.zeros_like(w2)
    w2p = jnp.stack([jnp.concatenate([w2, z], axis=1), jnp.concatenate([z, w2], axis=1)]).astype(BF16)
    return pl.pallas_call(
        _compress_kernel,
        out_shape=jax.ShapeDtypeStruct((bsz, NSA_KV_HEADS, nc, LANES), BF16),
        grid=(bsz, NSA_KV_HEADS),
        in_specs=[pl.BlockSpec((1, 1, nc, half), lambda b, g: (b, first + g, 0, 0)),
                  pl.BlockSpec((2, half), lambda b, g: (0, 0)),
                  pl.BlockSpec((2 * half, w1.shape[1]), lambda b, g: (0, 0)),
                  pl.BlockSpec((1, w2.shape[0], LANES), lambda b, g: (g % 2, 0, 0))],
        out_specs=pl.BlockSpec((1, 1, nc, LANES), lambda b, g: (b, g, 0, 0)),
        compiler_params=_params(("parallel", "parallel")),
        name="nsa_compress",
    )(tok, pos2, w1.astype(BF16), w2p)


def _gate_columns(gate_ref, branch):
    tile = gate_ref[...]
    lane = lax.broadcasted_iota(jnp.int32, tile.shape, 1)
    base = branch * N_HEADS + pl.program_id(1) * NSA_GROUP
    return [jnp.sum(jnp.where(lane == base + r, tile, 0.0), axis=-1, keepdims=True) for r in range(NSA_GROUP)]


def _nsa_cmp_kernel(q_ref, kc_ref, vc_ref, bias_ref, gate_ref, o_ref, sel_ref, *, n_top):
    i = pl.program_id(2)
    nc = kc_ref.shape[2]
    n_chunks = nc // CMP_CHUNK
    nsb = sel_ref.shape[3]
    q4 = q_ref[0]
    gates = _gate_columns(gate_ref, 0)
    lane = lax.broadcasted_iota(jnp.int32, (TILE, LANES), 1)
    g_par = pl.program_id(1) % 2
    keep = (lane >= g_par * HEAD_DIM) & (lane < (g_par + 1) * HEAD_DIM)
    psum = [jnp.zeros((TILE, CMP_CHUNK), F32) for _ in range(n_chunks)]
    for r in range(NSA_GROUP):
        qh = q4[:, r * LANES:(r + 1) * LANES]
        ss = []
        for c in range(n_chunks):
            tile = jnp.clip(i - (CMP_CHUNK * NSA_CMP_STRIDE // TILE) * c, -1, CMP_TILES - 2) + 1
            ss.append(_dot_nt(qh, kc_ref[0, 0, c * CMP_CHUNK:(c + 1) * CMP_CHUNK, :]) + bias_ref[r, tile])
        m = ss[0].max(axis=-1, keepdims=True)
        for c in range(1, n_chunks):
            m = jnp.maximum(m, ss[c].max(axis=-1, keepdims=True))
        ps = [jnp.where(s > 0.5 * NEG_INF, jnp.exp(s - m), 0.0) for s in ss]
        l = ps[0].sum(axis=-1, keepdims=True)
        for c in range(1, n_chunks):
            l = l + ps[c].sum(axis=-1, keepdims=True)
        inv = 1.0 / jnp.maximum(l, 1e-30)
        acc = jnp.zeros((TILE, LANES), F32)
        for c in range(n_chunks):
            pc = ps[c] * inv
            psum[c] = psum[c] + pc
            acc = acc + _dot(pc.astype(BF16), vc_ref[0, 0, c * CMP_CHUNK:(c + 1) * CMP_CHUNK, :])
        o_ref[0, :, r * LANES:(r + 1) * LANES] = jnp.where(keep, acc * gates[r], 0.0).astype(o_ref.dtype)
    imp = jnp.zeros((nsb, TILE), F32)
    per_sel = NSA_SEL_BLOCK // NSA_CMP_STRIDE
    for c in range(n_chunks):
        jb = lax.broadcasted_iota(jnp.int32, (nsb, CMP_CHUNK), 0)
        ci = lax.broadcasted_iota(jnp.int32, (nsb, CMP_CHUNK), 1) + c * CMP_CHUNK
        rel = ci - per_sel * jb
        over = (rel >= 1 - NSA_CMP_LEN // NSA_CMP_STRIDE) & (rel < per_sel) & (ci < nc - 1)
        ov = jnp.where(over, 1.0, 0.0).astype(BF16)
        hi, lo = _split_bf16(psum[c])
        imp = imp + _dot_nt(ov, hi) + _dot_nt(ov, lo)
    jb = lax.broadcasted_iota(jnp.int32, (nsb, TILE), 0)
    qpos = lax.broadcasted_iota(jnp.int32, (nsb, TILE), 1) + i * TILE
    cur = qpos // NSA_SEL_BLOCK
    forced = (jb == 0) | (jb == cur) | (jb == cur - 1)
    causal = jb * NSA_SEL_BLOCK <= qpos
    score = jnp.where(forced, FORCED_SCORE, jnp.where(causal, imp, NEG_INF))
    chosen = jnp.zeros((nsb, TILE), F32)
    for _ in range(n_top):
        mx = jnp.max(score, axis=0, keepdims=True)
        first = jnp.min(jnp.where(score == mx, jb, nsb), axis=0, keepdims=True)
        pick = jb == first
        chosen = jnp.where(pick, 1.0, chosen)
        score = jnp.where(pick, -jnp.inf, score)
    not_allowed = jnp.where((chosen > 0.0) & causal, 0.0, 1.0)
    sel_ref[0, 0] = not_allowed.T.astype(sel_ref.dtype)


def _nsa_sparse_kernel(q_ref, k_ref, v_ref, bias_ref, gate_ref, *rest, window):
    if window:
        (o_ref,) = rest
        nsel = None
    else:
        sel_ref, o_ref = rest
        nsel = sel_ref[0, 0]
    i = pl.program_id(2)
    q4 = q_ref[0]
    gates = _gate_columns(gate_ref, 2 if window else 1)
    q_heads = [q4[:, r * LANES:(r + 1) * LANES] for r in range(NSA_GROUP)]
    if window:
        n_tiles = NSA_WINDOW // TILE + 1
        lo = jnp.maximum(i - (n_tiles - 1), 0)
        res = _flash_chunks(q_heads, k_ref, v_ref, bias_ref, [None] * NSA_GROUP, lo, i + 1, i, 1, n_tiles)
    else:
        res = _flash_chunks(q_heads, k_ref, v_ref, bias_ref, [nsel] * NSA_GROUP, 0, i + 1, i,
                            TILE // NSA_SEL_BLOCK, MAIN_TILES)
    lane = lax.broadcasted_iota(jnp.int32, (TILE, LANES), 1)
    g_par = pl.program_id(1) % 2
    keep = (lane >= g_par * HEAD_DIM) & (lane < (g_par + 1) * HEAD_DIM)
    for r in range(NSA_GROUP):
        acc, l = res[r]
        o = acc / l * gates[r]
        o_ref[0, :, r * LANES:(r + 1) * LANES] = jnp.where(keep, o, 0.0).astype(o_ref.dtype)


def _mixer_nsa(x, w_in, w_out, table_bias, pos_k, w1_k, w2_k, pos_v, w1_v, w2_v, bsz, seq):
    bias_main, bias_win, bias_cmp = table_bias
    aw = N_HEADS * HEAD_DIM
    kvw = NSA_KV_HEADS * HEAD_DIM
    G, R = NSA_KV_HEADS, NSA_GROUP
    nc = seq // NSA_CMP_STRIDE
    nsb = seq // NSA_SEL_BLOCK
    n_top = min(NSA_TOP_N, nsb)
    nq = seq // TILE
    assert nc % CMP_CHUNK == 0
    par = lambda h: (h // R) % 2
    wq = _pad_heads_cols(w_in[:, :aw] * ATTN_SCALE, par).astype(BF16)
    w_cmp = w_in[:, aw:aw + 2 * kvw].astype(BF16)
    w_cmp = w_cmp.reshape(D_MODEL, 2 * G, HEAD_DIM).transpose(1, 0, 2)
    wkv = w_in[:, aw + 2 * kvw:aw + 6 * kvw].astype(BF16)
    wg = jnp.pad(w_in[:, aw + 6 * kvw:], ((0, 0), (0, LANES - 3 * N_HEADS))).astype(BF16)
    q = _matmul(x, wq, BF16).reshape(bsz, seq, N_HEADS * LANES)
    kv = _matmul(x, wkv, BF16).reshape(bsz, seq, 4 * kvw)
    gates = _matmul(x, wg, F32, act='sigmoid')
    tok = _matmul_heads(x, w_cmp, bsz, seq).reshape(bsz, 2 * G, nc, NSA_CMP_STRIDE * HEAD_DIM)
    kc = _compress(tok, 0, pos_k, w1_k, w2_k, bsz, seq)
    vc = _compress(tok, G, pos_v, w1_v, w2_v, bsz, seq)

    q_spec = pl.BlockSpec((1, TILE, R * LANES), lambda b, g, i: (b, i, g))
    o_spec = pl.BlockSpec((1, TILE, R * LANES), lambda b, g, i: (b, i, g))
    gate_spec = pl.BlockSpec((TILE, LANES), lambda b, g, i: (b * nq + i, 0))
    o_shape = jax.ShapeDtypeStruct((bsz, seq, N_HEADS * LANES), BF16)
    sem = ("parallel", "parallel", "arbitrary")

    o_cmp, nsel = pl.pallas_call(
        functools.partial(_nsa_cmp_kernel, n_top=n_top),
        out_shape=[o_shape, jax.ShapeDtypeStruct((bsz, G, seq, nsb), BF16)],
        grid=(bsz, G, nq),
        in_specs=[q_spec,
                  pl.BlockSpec((1, 1, nc, LANES), lambda b, g, i: (b, g, 0, 0)),
                  pl.BlockSpec((1, 1, nc, LANES), lambda b, g, i: (b, g, 0, 0)),
                  pl.BlockSpec((R, CMP_TILES, TILE, CMP_CHUNK), lambda b, g, i: (g, 0, 0, 0)),
                  gate_spec],
        out_specs=[o_spec, pl.BlockSpec((1, 1, TILE, nsb), lambda b, g, i: (b, g, i, 0))],
        compiler_params=_params(sem),
        name="nsa_compressed_attn",
    )(q, kc, vc, bias_cmp, gates)

    def kv_spec(first_blk):
        return pl.BlockSpec((1, seq, LANES), lambda b, g, i: (b, 0, first_blk + g // 2))

    o_sel = pl.pallas_call(
        functools.partial(_nsa_sparse_kernel, window=False),
        out_shape=o_shape,
        grid=(bsz, G, nq),
        in_specs=[q_spec, kv_spec(0), kv_spec(2),
                  pl.BlockSpec((R, MAIN_TILES, TILE, TILE), lambda b, g, i: (g, 0, 0, 0)),
                  gate_spec,
                  pl.BlockSpec((1, 1, TILE, nsb), lambda b, g, i: (b, g, i, 0))],
        out_specs=o_spec,
        compiler_params=_params(sem),
        name="nsa_selected_attn",
    )(q, kv, kv, bias_main, gates, nsel)

    o_win = pl.pallas_call(
        functools.partial(_nsa_sparse_kernel, window=True),
        out_shape=o_shape,
        grid=(bsz, G, nq),
        in_specs=[q_spec, kv_spec(4), kv_spec(6),
                  pl.BlockSpec((R, NSA_WINDOW // TILE + 1, TILE, TILE), lambda b, g, i: (g, 0, 0, 0)),
                  gate_spec],
        out_specs=o_spec,
        compiler_params=_params(sem),
        name="nsa_window_attn",
    )(q, kv, kv, bias_win, gates)

    m = bsz * seq
    parts = [t.reshape(m, N_HEADS * LANES) for t in (o_cmp, o_sel, o_win)]
    return parts, _pad_heads_rows(w_out, par).astype(BF16)


def _router_kernel(x_ref, w_ref, b_ref, o_ref):
    logits = lax.dot_general(w_ref[...], x_ref[...], (((1,), (1,)), ((), ())),
                             preferred_element_type=F32, precision=lax.Precision.HIGHEST)
    scores = 1.0 / (1.0 + jnp.exp(-logits))
    biased = scores + b_ref[...]
    rows = [biased[e:e + 1, :] for e in range(N_EXPERTS)]
    group_score = []
    for g in range(N_GROUPS):
        r = rows[g * EXPERTS_PER_GROUP:(g + 1) * EXPERTS_PER_GROUP]
        best = None
        for a in range(EXPERTS_PER_GROUP):
            for c in range(a + 1, EXPERTS_PER_GROUP):
                pair = r[a] + r[c]
                best = pair if best is None else jnp.maximum(best, pair)
        group_score.append(best)
    best_val = group_score[0]
    best_grp = jnp.zeros_like(best_val, dtype=jnp.int32)
    for g in range(1, N_GROUPS):
        better = group_score[g] > best_val
        best_val = jnp.where(better, group_score[g], best_val)
        best_grp = jnp.where(better, g, best_grp)
    picked = []
    for e in range(N_EXPERTS):
        g, a = divmod(e, EXPERTS_PER_GROUP)
        rank = jnp.zeros_like(best_grp)
        for c in range(EXPERTS_PER_GROUP):
            if c == a:
                continue
            other = rows[g * EXPERTS_PER_GROUP + c]
            ahead = (other > rows[e]) | ((other == rows[e]) & (c < a))
            rank = rank + jnp.where(ahead, 1, 0)
        picked.append((best_grp == g) & (rank < 2))
    raw = [jnp.where(picked[e], scores[e:e + 1, :], 0.0) for e in range(N_EXPERTS)]
    total = raw[0]
    for e in range(1, N_EXPERTS):
        total = total + raw[e]
    o_ref[...] = (jnp.concatenate(raw, axis=0) / total).T


def _moe_kernel(x_ref, gate_ref, wg_ref, wu_ref, wd_ref, g_ref, b_ref, o_ref, xb_ref, acc_ref):
    e = pl.program_id(1)

    @pl.when(e == 0)
    def _():
        xb_ref[...] = x_ref[...].astype(BF16)
        acc_ref[...] = jnp.zeros_like(acc_ref)

    xb = xb_ref[...]
    lane = lax.broadcasted_iota(jnp.int32, gate_ref.shape, 1)
    gcol = jnp.sum(jnp.where(lane == e, gate_ref[...], 0.0), axis=-1, keepdims=True)
    a = _dot(xb, wg_ref[0])
    h = a / (1.0 + jnp.exp(-a)) * _dot(xb, wu_ref[0])
    acc_ref[...] += gcol * _dot(h.astype(BF16), wd_ref[0])

    @pl.when(e == N_EXPERTS - 1)
    def _():
        o_ref[...] = _layer_norm(DEEPNORM_ALPHA * x_ref[...] + acc_ref[...], g_ref[...], b_ref[...])


def _moe_ln(x, router_w, router_b, w_gate, w_up, w_down, g, b):
    m, d = x.shape
    tm = 1024
    gates = pl.pallas_call(
        _router_kernel,
        out_shape=jax.ShapeDtypeStruct((m, N_EXPERTS), F32),
        grid=(m // tm,),
        in_specs=[pl.BlockSpec((tm, d), lambda i: (i, 0)),
                  pl.BlockSpec((N_EXPERTS, d), lambda i: (0, 0)),
                  pl.BlockSpec((N_EXPERTS, 1), lambda i: (0, 0))],
        out_specs=pl.BlockSpec((tm, N_EXPERTS), lambda i: (i, 0)),
        compiler_params=_params(("parallel",)),
        name="moe_router",
    )(x, router_w.T, router_b.reshape(N_EXPERTS, 1))
    de = w_gate.shape[-1]
    return pl.pallas_call(
        _moe_kernel,
        out_shape=jax.ShapeDtypeStruct((m, d), F32),
        grid=(m // tm, N_EXPERTS),
        in_specs=[pl.BlockSpec((tm, d), lambda i, e: (i, 0)),
                  pl.BlockSpec((tm, N_EXPERTS), lambda i, e: (i, 0)),
                  pl.BlockSpec((1, d, de), lambda i, e: (e, 0, 0)),
                  pl.BlockSpec((1, d, de), lambda i, e: (e, 0, 0)),
                  pl.BlockSpec((1, de, d), lambda i, e: (e, 0, 0)),
                  pl.BlockSpec((1, d), lambda i, e: (0, 0)),
                  pl.BlockSpec((1, d), lambda i, e: (0, 0))],
        out_specs=pl.BlockSpec((tm, d), lambda i, e: (i, 0)),
        scratch_shapes=[pltpu.VMEM((tm, d), BF16), pltpu.VMEM((tm, d), F32)],
        compiler_params=_params(("parallel", "arbitrary")),
        name="moe_experts_ln",
    )(x, gates, w_gate.astype(BF16), w_up.astype(BF16), w_down.astype(BF16), g.reshape(1, d), b.reshape(1, d))


def kernel(x, rel_table, router_w, router_b, ln1_g, ln1_b, ln2_g, ln2_b, exp_w_gate, exp_w_up, exp_w_down, dil_w_in, dil_w_out, sb_w_in, sb_w_out, nsa_w_in, nsa_w_out, nsa_cmp_pos_k, nsa_cmp_w1_k, nsa_cmp_w2_k, nsa_cmp_pos_v, nsa_cmp_w1_v, nsa_cmp_w2_v, moba_w_in, moba_w_out):
    bsz, seq, d = x.shape
    assert d == D_MODEL and seq % (DILATED_PAIRS[-1][1] * DIL_BLOCK) == 0
    depth = ln1_g.shape[0]
    n_mixers = 4
    table = rel_table.astype(F32)
    span = DIL_BLOCK
    big = 1 << 30
    bias_dil = _bias_tiles(table, len(DILATED_PAIRS), DIL_BLOCK, 2 * DIL_BLOCK, DIL_BLOCK, 0, 1, 0, span,
                           [dl for _, dl in DILATED_PAIRS])
    bias_main = _bias_tiles(table, MAIN_TILES, TILE, TILE, 0, TILE, 1, 0, big, [1])
    bias_win = _bias_tiles(table, NSA_WINDOW // TILE + 1, TILE, TILE, 0, TILE, 1, 0, NSA_WINDOW - 1, [1])
    bias_cmp = _bias_tiles(table, CMP_TILES, TILE, CMP_CHUNK, -TILE - (NSA_CMP_LEN - 1), TILE, NSA_CMP_STRIDE,
                           0, big, [1])
    xf = x.reshape(bsz * seq, d)
    for layer in range(depth):
        kind, occ = layer % n_mixers, layer // n_mixers
        merge = None
        if kind == 0:
            parts, w_out = _mixer_dilated(xf, dil_w_in[occ], dil_w_out[occ], bias_dil, bsz, seq)
            merge = 'lse3'
        elif kind == 1:
            parts, w_out = _mixer_stick_breaking(xf, sb_w_in[occ], sb_w_out[occ], bsz, seq)
        elif kind == 2:
            parts, w_out = _mixer_nsa(xf, nsa_w_in[occ], nsa_w_out[occ], (bias_main, bias_win, bias_cmp),
                                      nsa_cmp_pos_k[occ], nsa_cmp_w1_k[occ], nsa_cmp_w2_k[occ],
                                      nsa_cmp_pos_v[occ], nsa_cmp_w1_v[occ], nsa_cmp_w2_v[occ], bsz, seq)
            merge = 'sum3'
        else:
            parts, w_out = _mixer_moba(xf, moba_w_in[occ], moba_w_out[occ], bias_main, bsz, seq)
        xf = _outproj_ln(parts, w_out, xf, ln1_g[layer], ln1_b[layer], merge=merge)
        xf = _moe_ln(xf, router_w, router_b, exp_w_gate[layer], exp_w_up[layer], exp_w_down[layer],
                     ln2_g[layer], ln2_b[layer])
    return xf.reshape(bsz, seq, d)
```

```python
import functools
import math

import numpy as np
import jax
import jax.numpy as jnp
from jax import lax
from jax.experimental import pallas as pl
from jax.experimental.pallas import tpu as pltpu

F32 = jnp.float32
BF16 = jnp.bfloat16

D_MODEL = 1024
HEAD_DIM = 64
N_HEADS = 16
LANES = 128
PAIRS = N_HEADS // 2
ATTN_SCALE = HEAD_DIM ** -0.5
REL_BUCKETS = 32
REL_MAX_DIST = 2048
DILATED_PAIRS = ((128, 1), (512, 4), (2048, 16))
DIL_BLOCK = 128
SB_CLIP = 60.0
LOG2_E = math.log2(math.e)
NSA_KV_HEADS = 4
NSA_GROUP = N_HEADS // NSA_KV_HEADS
NSA_CMP_LEN = 32
NSA_CMP_STRIDE = 16
NSA_SEL_BLOCK = 64
NSA_TOP_N = 16
NSA_WINDOW = 512
MOBA_BLOCK = 256
MOBA_TOP_K = 3
N_EXPERTS = 16
N_GROUPS = 4
EXPERTS_PER_GROUP = N_EXPERTS // N_GROUPS
DEPTH = 4
DEEPNORM_ALPHA = (2 * DEPTH) ** 0.25
LN_EPS = 1e-5
NEG_INF = -1e30
FORCED_SCORE = 1e9
TILE = 256
BIASED_TILES = 7
MOBA_FAR_WIDTH = 2
NSA_FAR_WIDTH = 4
MAIN_TILES = BIASED_TILES + max(MOBA_FAR_WIDTH, NSA_FAR_WIDTH) - 1
CMP_CHUNK = 128
CMP_TILES = 19
VMEM_LIMIT = 56 * 1024 * 1024


def _bucket_thresholds():
    n = np.arange(0, 2 * REL_MAX_DIST)
    exact = REL_BUCKETS // 2
    logf = np.log(np.maximum(n, 1).astype(np.float64) / exact) / math.log(REL_MAX_DIST / exact)
    large = np.minimum(exact + (logf * (REL_BUCKETS - exact)).astype(np.int64), REL_BUCKETS - 1)
    bucket = np.where(n < exact, n, large)
    return tuple(int(np.argmax(bucket >= k)) for k in range(1, REL_BUCKETS))


BUCKET_THRESHOLDS = _bucket_thresholds()


def _params(sem, vmem=VMEM_LIMIT):
    return pltpu.CompilerParams(dimension_semantics=sem, vmem_limit_bytes=vmem)


def _dot(a, b):
    return jnp.dot(a, b, preferred_element_type=F32)


def _dot_nt(a, b):
    return lax.dot_general(a, b, (((1,), (1,)), ((), ())), preferred_element_type=F32)


def _split_bf16(x):
    hi = x.astype(BF16)
    lo = (x - hi.astype(F32)).astype(BF16)
    return hi, lo


def _matmul_kernel(x_ref, w_ref, o_ref, *, act):
    y = _dot(x_ref[...].astype(BF16), w_ref[...])
    if act == 'sigmoid':
        y = 1.0 / (1.0 + jnp.exp(-y))
    o_ref[...] = y.astype(o_ref.dtype)


def _matmul(x, w, out_dtype, act=None):
    m, k = x.shape
    n = w.shape[1]
    tm = 512
    tn = 512 if n % 512 == 0 else 128
    return pl.pallas_call(
        functools.partial(_matmul_kernel, act=act),
        out_shape=jax.ShapeDtypeStruct((m, n), out_dtype),
        grid=(m // tm, n // tn),
        in_specs=[pl.BlockSpec((tm, k), lambda i, j: (i, 0)),
                  pl.BlockSpec((k, tn), lambda i, j: (0, j))],
        out_specs=pl.BlockSpec((tm, tn), lambda i, j: (i, j)),
        compiler_params=_params(("parallel", "arbitrary")),
        name="proj_matmul",
    )(x, w)


def _matmul_t_kernel(x_ref, w_ref, o_ref):
    y = _dot_nt(w_ref[...], x_ref[...].astype(BF16))
    row = lax.broadcasted_iota(jnp.int32, y.shape, 0)
    o_ref[...] = jnp.where(row % LANES == HEAD_DIM, 1.0, y).astype(o_ref.dtype)


def _matmul_t(x, wt):
    m, k = x.shape
    n = wt.shape[0]
    tm = 512
    tn = 512 if n % 512 == 0 else 256
    return pl.pallas_call(
        _matmul_t_kernel,
        out_shape=jax.ShapeDtypeStruct((n, m), BF16),
        grid=(m // tm, n // tn),
        in_specs=[pl.BlockSpec((tm, k), lambda i, j: (i, 0)),
                  pl.BlockSpec((tn, k), lambda i, j: (j, 0))],
        out_specs=pl.BlockSpec((tn, tm), lambda i, j: (j, i)),
        compiler_params=_params(("parallel", "arbitrary")),
        name="proj_matmul_t",
    )(x, wt)


def _matmul_heads_kernel(x_ref, w_ref, o_ref):
    o_ref[0, 0] = _dot(x_ref[...].astype(BF16), w_ref[0]).astype(o_ref.dtype)


def _matmul_heads(x, w, bsz, seq):
    m, k = x.shape
    n = w.shape[0]
    tm = 512
    per_b = seq // tm
    return pl.pallas_call(
        _matmul_heads_kernel,
        out_shape=jax.ShapeDtypeStruct((bsz, n, seq, HEAD_DIM), BF16),
        grid=(m // tm, n),
        in_specs=[pl.BlockSpec((tm, k), lambda i, j: (i, 0)),
                  pl.BlockSpec((1, k, HEAD_DIM), lambda i, j: (j, 0, 0))],
        out_specs=pl.BlockSpec((1, 1, tm, HEAD_DIM), lambda i, j: (i // per_b, j, i % per_b, 0)),
        compiler_params=_params(("parallel", "arbitrary")),
        name="proj_heads",
    )(x, w)


def _layer_norm(z, g, b):
    mu = jnp.mean(z, axis=-1, keepdims=True)
    zc = z - mu
    var = jnp.mean(zc * zc, axis=-1, keepdims=True)
    return zc * lax.rsqrt(var + LN_EPS) * g + b


def _outproj_kernel(*refs, n_in, merge):
    ins = refs[:n_in]
    w_ref, x_ref, g_ref, b_ref, o_ref = refs[n_in:]
    if merge == 'lse3':
        o1, o2, o3, l1, l2, l3 = [r[...] for r in ins]
        mx = jnp.maximum(jnp.maximum(l1, l2), l3)
        e1, e2, e3 = jnp.exp(l1 - mx), jnp.exp(l2 - mx), jnp.exp(l3 - mx)
        a = (e1 * o1 + e2 * o2 + e3 * o3) / (e1 + e2 + e3)
    elif merge == 'sum3':
        a = ins[0][...].astype(F32) + ins[1][...].astype(F32) + ins[2][...].astype(F32)
    else:
        a = ins[0][...]
    y = _dot(a.astype(BF16), w_ref[...])
    o_ref[...] = _layer_norm(DEEPNORM_ALPHA * x_ref[...] + y, g_ref[...], b_ref[...])


def _outproj_ln(ins, w, x, g, b, merge=None):
    m, d = x.shape
    ka = ins[0].shape[1]
    tm = 256
    n_in = len(ins)
    return pl.pallas_call(
        functools.partial(_outproj_kernel, n_in=n_in, merge=merge),
        out_shape=jax.ShapeDtypeStruct((m, d), F32),
        grid=(m // tm,),
        in_specs=[pl.BlockSpec((tm, ka), lambda i: (i, 0)) for _ in ins]
        + [pl.BlockSpec((ka, d), lambda i: (0, 0)),
           pl.BlockSpec((tm, d), lambda i: (i, 0)),
           pl.BlockSpec((1, d), lambda i: (0, 0)),
           pl.BlockSpec((1, d), lambda i: (0, 0))],
        out_specs=pl.BlockSpec((tm, d), lambda i: (i, 0)),
        compiler_params=_params(("parallel",)),
        name="outproj_ln",
    )(*ins, w, x, g.reshape(1, d), b.reshape(1, d))


def _bias_kernel(tbl_ref, o_ref, *, rows, cols, off0, tstride, cmul, lo, hi, mults, shift, transposed, scale):
    t = pl.program_id(0)
    if len(set(mults)) == 1:
        mult = mults[0]
    else:
        mult = jnp.int32(mults[-1])
        for idx in range(len(mults) - 2, -1, -1):
            mult = jnp.where(t == idx, jnp.int32(mults[idx]), mult)
    base = off0 + tstride * t

    def strip(i, carry):
        r0 = pl.multiple_of(i * 8, 8)
        a = lax.broadcasted_iota(jnp.int32, (8, cols), 0) + r0
        c = lax.broadcasted_iota(jnp.int32, (8, cols), 1)
        steps = base + c - cmul * a if transposed else base + a - cmul * c
        valid = (steps >= lo) & (steps <= hi)
        dist = steps * mult
        for h in range(N_HEADS):
            acc = jnp.full((8, cols), tbl_ref[0, h], F32)
            for k, thr in enumerate(BUCKET_THRESHOLDS):
                acc = jnp.where(dist >= thr, tbl_ref[k + 1, h], acc)
            if shift:
                acc = acc - tbl_ref[REL_BUCKETS - 1, h]
            if scale != 1.0:
                acc = acc * scale
            o_ref[h, 0, pl.ds(r0, 8), :] = jnp.where(valid, acc, NEG_INF)
        return carry

    lax.fori_loop(0, rows // 8, strip, 0)


def _bias_tiles(table, n_tiles, rows, cols, off0, tstride, cmul, lo, hi, mults, shift=False, transposed=False,
                scale=1.0):
    return pl.pallas_call(
        functools.partial(_bias_kernel, rows=rows, cols=cols, off0=off0, tstride=tstride, cmul=cmul,
                          lo=lo, hi=hi, mults=tuple(mults), shift=shift, transposed=transposed, scale=scale),
        out_shape=jax.ShapeDtypeStruct((N_HEADS, n_tiles, rows, cols), F32),
        grid=(n_tiles,),
        in_specs=[pl.BlockSpec(memory_space=pltpu.SMEM)],
        out_specs=pl.BlockSpec((N_HEADS, 1, rows, cols), lambda t: (0, t, 0, 0)),
        compiler_params=_params(("arbitrary",)),
        name="bias_tiles",
    )(table)


def _dil_kernel(q_ref, kp_ref, kc_ref, vp_ref, vc_ref, bias_ref, o_ref, lse_ref, *, per_step):
    j = pl.program_id(3)
    k_all = jnp.concatenate([kp_ref[0], kc_ref[0]], axis=0)
    v_all = jnp.concatenate([vp_ref[0], vc_ref[0]], axis=0)
    colk = lax.broadcasted_iota(jnp.int32, (DIL_BLOCK, 2 * DIL_BLOCK), 1)
    first = jnp.where((colk < DIL_BLOCK) & (j == 0), NEG_INF, 0.0)
    low = lax.broadcasted_iota(jnp.int32, (DIL_BLOCK, LANES), 1) < HEAD_DIM
    for u in range(per_step):
        rows = slice(u * DIL_BLOCK, (u + 1) * DIL_BLOCK)
        q2 = q_ref[0, rows, :]
        k2 = k_all[u * DIL_BLOCK:(u + 2) * DIL_BLOCK]
        v2 = v_all[u * DIL_BLOCK:(u + 2) * DIL_BLOCK]
        outs, lses = [], []
        for h in range(2):
            s = _dot_nt(q2[:, h * LANES:(h + 1) * LANES], k2) + bias_ref[h, 0]
            if u == 0:
                s = s + first
            m = jnp.max(s, axis=-1, keepdims=True)
            p = jnp.exp(s - m)
            l = jnp.maximum(jnp.sum(p, axis=-1, keepdims=True), 1e-30)
            outs.append(_dot(p.astype(BF16), v2) / l)
            lses.append(m + jnp.log(l))
        o_ref[0, rows, :] = jnp.where(low, outs[0], outs[1])
        lse_ref[0, rows, :] = jnp.where(low, lses[0], lses[1])


def _dilated_group(proj, bias, g, dilation, bsz, seq):
    width = proj.shape[-1]
    length = seq // dilation
    nb = length // DIL_BLOCK
    per_step = math.gcd(nb, 4)
    rows = per_step * DIL_BLOCK
    wb = width // LANES
    pv = proj.reshape(bsz, length, dilation * width)
    q_blk = g * N_HEADS // 2
    k_blk = 3 * N_HEADS + g * 2 * PAIRS
    v_blk = k_blk + PAIRS

    def kv_specs(base):
        prev = pl.BlockSpec((1, DIL_BLOCK, LANES),
                            lambda hp, b, r, j: (b, jnp.maximum(per_step * j - 1, 0), r * wb + base + hp))
        cur = pl.BlockSpec((1, rows, LANES), lambda hp, b, r, j: (b, j, r * wb + base + hp))
        return [prev, cur]

    out_spec = pl.BlockSpec((1, rows, LANES), lambda hp, b, r, j: (b, j, r * PAIRS + hp))
    o, lse = pl.pallas_call(
        functools.partial(_dil_kernel, per_step=per_step),
        out_shape=[jax.ShapeDtypeStruct((bsz, length, dilation * D_MODEL), F32)] * 2,
        grid=(PAIRS, bsz, dilation, nb // per_step),
        in_specs=[pl.BlockSpec((1, rows, 2 * LANES), lambda hp, b, r, j: (b, j, r * (wb // 2) + q_blk + hp))]
        + kv_specs(k_blk) + kv_specs(v_blk)
        + [pl.BlockSpec((2, 1, DIL_BLOCK, 2 * DIL_BLOCK), lambda hp, b, r, j: (hp, g, 0, 0))],
        out_specs=[out_spec, out_spec],
        compiler_params=_params(("parallel", "parallel", "parallel", "arbitrary")),
        name="dilated_attn",
    )(pv, pv, pv, pv, pv, bias)
    return o.reshape(bsz * seq, D_MODEL), lse.reshape(bsz * seq, D_MODEL)


def _pad_heads_cols(w, parity_of_head):
    k, n = w.shape
    nh = n // HEAD_DIM
    wh = w.reshape(k, nh, HEAD_DIM)
    z = jnp.zeros_like(wh)
    par = jnp.asarray([parity_of_head(h) for h in range(nh)], jnp.int32)[None, :, None]
    lo = jnp.where(par == 0, wh, z)
    hi = jnp.where(par == 1, wh, z)
    return jnp.concatenate([lo, hi], axis=-1).reshape(k, nh * LANES)


def _pad_heads_rows(w, parity_of_head):
    return _pad_heads_cols(w.T, parity_of_head).T


def _mixer_dilated(x, w_in, w_out, bias_dil, bsz, seq):
    aw = N_HEADS * HEAD_DIM
    par = lambda h: h % 2
    qs, kvs = [], []
    for g in range(len(DILATED_PAIRS)):
        base = g * 3 * aw
        qs.append(_pad_heads_cols(w_in[:, base:base + aw] * ATTN_SCALE, par))
        kvs.append(w_in[:, base + aw:base + 3 * aw])
    w_all = jnp.concatenate(qs + kvs, axis=1).astype(BF16)
    proj = _matmul(x, w_all, BF16).reshape(bsz, seq, -1)
    parts = [_dilated_group(proj, bias_dil, g, d, bsz, seq) for g, (_, d) in enumerate(DILATED_PAIRS)]
    return [p[0] for p in parts] + [p[1] for p in parts], w_out.astype(BF16)


def _sb_kernel(q_ref, k_ref, v_ref, o_ref):
    i = pl.program_id(2)
    t = TILE
    row = lax.broadcasted_iota(jnp.int32, (t, t), 0)
    col = lax.broadcasted_iota(jnp.int32, (t, t), 1)
    suffix = jnp.where(row >= col, 1.0, 0.0).astype(BF16)
    strict = col < row
    q2 = q_ref[0]
    q_heads = [q2[:, h * LANES:(h + 1) * LANES] for h in range(2)]

    def block(qh, kj, masked):
        start = pl.multiple_of(kj * t, t)
        kb = k_ref[0, pl.ds(start, t), :]
        vb = v_ref[0, pl.ds(start, t), :]
        a = jnp.clip(_dot_nt(qh, kb), -SB_CLIP, SB_CLIP) * LOG2_E
        sp = jnp.log2(1.0 + jnp.exp2(a))
        if masked:
            sp = jnp.where(strict, sp, 0.0)
        hi, lo = _split_bf16(sp)
        rr = _dot(hi, suffix) + _dot(lo, suffix)
        att = jnp.exp2(a - rr)
        if masked:
            att = jnp.where(strict, att, 0.0)
        return _dot(att.astype(BF16), vb), rr[:, 0:1]

    def diagonal(kj, carry):
        return tuple(block(qh, kj, True) for qh in q_heads)

    zero = (jnp.zeros((t, LANES), F32), jnp.zeros((t, 1), F32))
    (acc0, r0), (acc1, r1) = lax.fori_loop(i, i + 1, diagonal, (zero, zero))

    def cond(carry):
        n, alive = carry[0], carry[1]
        return (n < i) & alive

    def step(carry):
        n = carry[0]
        new, tops = [], []
        for qh, (r_sum, w, acc) in zip(q_heads, carry[2:]):
            pv, r_blk = block(qh, i - 1 - n, False)
            r_new = r_sum + r_blk
            w_new = jnp.exp2(-r_new)
            new.append((r_new, w_new, acc + w * pv))
            tops.append(jnp.max(w_new, axis=0, keepdims=True))
        alive = jnp.maximum(tops[0], tops[1])[0, 0] > 0.0
        return (n + 1, alive) + tuple(new)

    init = (jnp.int32(0), jnp.bool_(True), (r0, jnp.exp2(-r0), acc0), (r1, jnp.exp2(-r1), acc1))
    res = lax.while_loop(cond, step, init)
    lane = lax.broadcasted_iota(jnp.int32, (t, LANES), 1)
    o_ref[0] = jnp.where(lane < HEAD_DIM, res[2][2], res[3][2]).astype(o_ref.dtype)


def _mixer_stick_breaking(x, w_in, w_out, bsz, seq):
    aw = N_HEADS * HEAD_DIM
    wq = _pad_heads_cols(w_in[:, :aw] * ATTN_SCALE, lambda h: h % 2)
    w_all = jnp.concatenate([wq, w_in[:, aw:]], axis=1).astype(BF16)
    proj = _matmul(x, w_all, BF16).reshape(bsz, seq, -1)
    kb = N_HEADS
    o = pl.pallas_call(
        _sb_kernel,
        out_shape=jax.ShapeDtypeStruct((bsz, seq, D_MODEL), BF16),
        grid=(bsz, PAIRS, seq // TILE),
        in_specs=[pl.BlockSpec((1, TILE, 2 * LANES), lambda b, hp, i: (b, i, hp)),
                  pl.BlockSpec((1, seq, LANES), lambda b, hp, i: (b, 0, kb + hp)),
                  pl.BlockSpec((1, seq, LANES), lambda b, hp, i: (b, 0, kb + PAIRS + hp))],
        out_specs=pl.BlockSpec((1, TILE, LANES), lambda b, hp, i: (b, i, hp)),
        compiler_params=_params(("parallel", "parallel", "arbitrary")),
        name="stick_breaking_attn",
    )(proj, proj, proj)
    return [o.reshape(bsz * seq, D_MODEL)], w_out.astype(BF16)


def _flash_chunks(q_heads, k_ref, v_ref, v_rows, bias_ref, nsel_of_head, lo, hi, i, blocks_per_chunk, n_tiles,
                  fold_mask=False, pipelined=False, far_width=1):
    nh = len(q_heads)
    distinct = []
    if not fold_mask:
        for ns in nsel_of_head:
            if ns is not None and all(ns is not d for d in distinct):
                distinct.append(ns)

    def logits(ch, near, width):
        rows = width * TILE
        kb = k_ref[0, pl.ds(pl.multiple_of(ch * TILE, TILE), rows), :]
        adds = []
        if fold_mask:
            key = lax.broadcasted_iota(jnp.int32, (rows, LANES), 0)
            blk = lax.broadcasted_iota(jnp.int32, (rows, LANES), 1)
            blk_of_key = ch * blocks_per_chunk + key // (TILE // blocks_per_chunk)
            kb = jnp.concatenate([kb, jnp.where(blk == blk_of_key, NEG_INF, 0.0).astype(BF16)], axis=1)
        else:
            for ns in distinct:
                nblk = ns.shape[0]
                kk = lax.broadcasted_iota(jnp.int32, (rows, nblk), 0)
                bj = lax.broadcasted_iota(jnp.int32, (rows, nblk), 1)
                blk_of_key = ch * blocks_per_chunk + kk // (TILE // blocks_per_chunk)
                expand = jnp.where(bj == blk_of_key, NEG_INF, 0.0).astype(BF16)
                adds.append(_dot(expand, ns))
        out = []
        for h in range(nh):
            s = _dot_nt(kb, q_heads[h])
            if near:
                s = s + bias_ref[h, i - ch]
            ns = nsel_of_head[h]
            if ns is not None and not fold_mask:
                s = s + adds[[d is ns for d in distinct].index(True)]
            out.append(s)
        return tuple(out)

    def update(ch, s_heads, state, width):
        start = pl.multiple_of(ch * TILE, TILE)
        vts = {}
        new = []
        for h in range(nh):
            r0 = v_rows[h]
            if r0 not in vts:
                vts[r0] = v_ref[r0:r0 + LANES, pl.ds(start, width * TILE)]
            m, acc = state[h]
            s = s_heads[h]
            m_new = jnp.maximum(m, jnp.max(s, axis=0, keepdims=True))
            alpha = jnp.exp2(m - m_new)
            p = jnp.exp2(s - m_new)
            acc = alpha * acc + _dot(vts[r0], p.astype(BF16))
            new.append((m_new, acc))
        return tuple(new)

    def run(ch0, steps, near, width, state):
        if not pipelined:
            return lax.fori_loop(
                0, steps, lambda t, st: update(ch0 + t * width, logits(ch0 + t * width, near, width), st, width), state)

        def body(t, carry):
            s_cur, st = carry
            s_next = logits(ch0 + jnp.minimum(t + 1, steps - 1) * width, near, width)
            return s_next, update(ch0 + t * width, s_cur, st, width)

        zeros = tuple(jnp.zeros((width * TILE, TILE), F32) for _ in range(nh))
        first = jnp.minimum(ch0, i)
        s0 = lax.fori_loop(first, first + 1, lambda c, _: logits(c, near, width), zeros)
        return lax.fori_loop(0, steps, body, (s0, state))[1]

    state = tuple((jnp.full((1, TILE), NEG_INF, F32), jnp.zeros((LANES, TILE), F32)) for _ in range(nh))
    n_near = n_tiles - (far_width - 1)
    far_steps = jnp.maximum(hi - n_near - lo, 0) // far_width
    split = lo + far_steps * far_width
    state = run(lo, far_steps, False, far_width, state)
    state = run(split, hi - split, True, 1, state)
    outs = []
    for _, acc in state:
        o = (acc / acc[HEAD_DIM:HEAD_DIM + 1, :]).T
        lane = lax.broadcasted_iota(jnp.int32, o.shape, 1)
        outs.append(jnp.where(lane < HEAD_DIM, o, 0.0))
    return outs


def _kmean_kernel(k_ref, o_ref):
    rows = k_ref.shape[1]
    kf = k_ref[0].astype(F32).reshape(rows // MOBA_BLOCK, MOBA_BLOCK, k_ref.shape[2])
    o_ref[0] = jnp.sum(kf, axis=1) * (1.0 / MOBA_BLOCK)


def _moba_kernel(q_ref, k_ref, v_ref, km_ref, bias_ref, o_ref):
    i = pl.program_id(2)
    nblk = km_ref.shape[1]
    km_hi, km_lo = _split_bf16(km_ref[0])
    q2 = q_ref[0]
    blk = lax.broadcasted_iota(jnp.int32, (nblk, TILE), 0)
    past = blk < i
    q_heads = []
    for h in range(2):
        qh = q2[:, h * LANES:(h + 1) * LANES]
        gate = jnp.where(past, _dot_nt(km_hi, qh) + _dot_nt(km_lo, qh), NEG_INF)
        allowed = blk == i
        for _ in range(MOBA_TOP_K):
            mx = jnp.max(gate, axis=0, keepdims=True)
            first = jnp.min(jnp.where(gate == mx, blk, nblk), axis=0, keepdims=True)
            pick = blk == first
            allowed = allowed | (pick & past)
            gate = jnp.where(pick, -jnp.inf, gate)
        nsel = jnp.where(allowed, 0.0, 1.0).T.astype(BF16)
        q_heads.append(jnp.concatenate([qh, nsel], axis=1))
    outs = _flash_chunks(q_heads, k_ref, v_ref, [0, LANES], bias_ref, [None, None], 0, i + 1, i, 1,
                         BIASED_TILES + MOBA_FAR_WIDTH - 1, fold_mask=True, pipelined=True, far_width=MOBA_FAR_WIDTH)
    for h in range(2):
        o_ref[0, :, h * LANES:(h + 1) * LANES] = outs[h].astype(o_ref.dtype)


def _mixer_moba(x, w_in, w_out, bias_main, bsz, seq):
    aw = N_HEADS * HEAD_DIM
    nblk = seq // MOBA_BLOCK
    assert nblk - 1 >= MOBA_TOP_K
    wq = _pad_heads_cols(w_in[:, :aw] * (ATTN_SCALE * LOG2_E), lambda h: h % 2)
    w_all = jnp.concatenate([wq, w_in[:, aw:2 * aw]], axis=1).astype(BF16)
    proj = _matmul(x, w_all, BF16).reshape(bsz, seq, -1)
    v_t = _matmul_t(x, _pad_heads_cols(w_in[:, 2 * aw:], lambda h: 0).T.astype(BF16))
    kb = N_HEADS
    rows = 8 * MOBA_BLOCK
    kmean = pl.pallas_call(
        _kmean_kernel,
        out_shape=jax.ShapeDtypeStruct((bsz, nblk, D_MODEL), F32),
        grid=(bsz, seq // rows),
        in_specs=[pl.BlockSpec((1, rows, D_MODEL), lambda b, i: (b, i, kb * LANES // D_MODEL))],
        out_specs=pl.BlockSpec((1, 8, D_MODEL), lambda b, i: (b, i, 0)),
        compiler_params=_params(("parallel", "parallel")),
        name="moba_kmean",
    )(proj)
    assert nblk <= LANES
    kmean = jnp.pad(kmean, ((0, 0), (0, LANES - nblk), (0, 0)))
    o = pl.pallas_call(
        _moba_kernel,
        out_shape=jax.ShapeDtypeStruct((bsz, seq, N_HEADS * LANES), BF16),
        grid=(bsz, PAIRS, seq // TILE),
        in_specs=[pl.BlockSpec((1, TILE, 2 * LANES), lambda b, hp, i: (b, i, hp)),
                  pl.BlockSpec((1, seq, LANES), lambda b, hp, i: (b, 0, kb + hp)),
                  pl.BlockSpec((2 * LANES, seq), lambda b, hp, i: (hp, b)),
                  pl.BlockSpec((1, LANES, LANES), lambda b, hp, i: (b, 0, hp)),
                  pl.BlockSpec((2, MAIN_TILES, TILE, TILE), lambda b, hp, i: (hp, 0, 0, 0))],
        out_specs=pl.BlockSpec((1, TILE, 2 * LANES), lambda b, hp, i: (b, i, hp)),
        compiler_params=_params(("parallel", "parallel", "arbitrary")),
        name="moba_attn",
    )(proj, proj, v_t, kmean, bias_main)
    return [o.reshape(bsz * seq, N_HEADS * LANES)], _pad_heads_rows(w_out, lambda h: 0).astype(BF16)


def _gelu_tanh(x):
    return 0.5 * x * (1.0 + jnp.tanh(math.sqrt(2.0 / math.pi) * (x + 0.044715 * (x * x * x))))


def _compress_kernel(a_ref, pos_ref, w1_ref, w2_ref, o_ref):
    nc = a_ref.shape[2]
    half = NSA_CMP_STRIDE * HEAD_DIM
    a = a_ref[0, 0].astype(F32)
    top = _dot((a + pos_ref[0:1, :]).astype(BF16), w1_ref[0:half, :])
    bot = _dot((a + pos_ref[1:2, :]).astype(BF16), w1_ref[half:2 * half, :])
    hid = top + pltpu.roll(bot, nc - 1, 0)
    out = _dot(_gelu_tanh(hid).astype(BF16), w2_ref[0])
    rowi = lax.broadcasted_iota(jnp.int32, out.shape, 0)
    o_ref[0, 0] = jnp.where(rowi < nc - 1, out, 0.0).astype(o_ref.dtype)


def _compress(tok, first, pos, w1, w2, bsz, seq, by_parity):
    nc = seq // NSA_CMP_STRIDE
    half = NSA_CMP_STRIDE * HEAD_DIM
    pos2 = pos.reshape(2, half)
    z = jnp.zeros_like(w2)
    low = jnp.concatenate([w2, z], axis=1)
    w2p = jnp.stack([low, jnp.concatenate([z, w2], axis=1) if by_parity else low]).astype(BF16)
    return pl.pallas_call(
        _compress_kernel,
        out_shape=jax.ShapeDtypeStruct((bsz, NSA_KV_HEADS, nc, LANES), BF16),
        grid=(bsz, NSA_KV_HEADS),
        in_specs=[pl.BlockSpec((1, 1, nc, half), lambda b, g: (b, first + g, 0, 0)),
                  pl.BlockSpec((2, half), lambda b, g: (0, 0)),
                  pl.BlockSpec((2 * half, w1.shape[1]), lambda b, g: (0, 0)),
                  pl.BlockSpec((1, w2.shape[0], LANES), lambda b, g: (g % 2, 0, 0))],
        out_specs=pl.BlockSpec((1, 1, nc, LANES), lambda b, g: (b, g, 0, 0)),
        compiler_params=_params(("parallel", "parallel")),
        name="nsa_compress",
    )(tok, pos2, w1.astype(BF16), w2p)


def _gate_columns(gate_ref, branch):
    tile = gate_ref[...]
    lane = lax.broadcasted_iota(jnp.int32, tile.shape, 1)
    base = branch * N_HEADS + pl.program_id(1) * NSA_GROUP
    return [jnp.sum(jnp.where(lane == base + r, tile, 0.0), axis=-1, keepdims=True) for r in range(NSA_GROUP)]


def _nsa_cmp_kernel(q_ref, kc_ref, vc_ref, bias_ref, gate_ref, o_ref, sel_ref, *, n_top):
    i = pl.program_id(2)
    nc = kc_ref.shape[2]
    n_chunks = nc // CMP_CHUNK
    nsb = sel_ref.shape[2]
    q4 = q_ref[0]
    gates = _gate_columns(gate_ref, 0)
    keep = lax.broadcasted_iota(jnp.int32, (TILE, LANES), 1) < HEAD_DIM
    psum = [jnp.zeros((TILE, CMP_CHUNK), F32) for _ in range(n_chunks)]
    for r in range(NSA_GROUP):
        qh = q4[:, r * LANES:(r + 1) * LANES]
        ss = []
        for c in range(n_chunks):
            tile = jnp.clip(i - (CMP_CHUNK * NSA_CMP_STRIDE // TILE) * c, -1, CMP_TILES - 2) + 1
            ss.append(_dot_nt(qh, kc_ref[0, 0, c * CMP_CHUNK:(c + 1) * CMP_CHUNK, :]) + bias_ref[r, tile])
        m = ss[0].max(axis=-1, keepdims=True)
        for c in range(1, n_chunks):
            m = jnp.maximum(m, ss[c].max(axis=-1, keepdims=True))
        ps = [jnp.where(s > 0.5 * NEG_INF, jnp.exp2(s - m), 0.0) for s in ss]
        l = ps[0].sum(axis=-1, keepdims=True)
        for c in range(1, n_chunks):
            l = l + ps[c].sum(axis=-1, keepdims=True)
        inv = 1.0 / jnp.maximum(l, 1e-30)
        acc = jnp.zeros((TILE, LANES), F32)
        for c in range(n_chunks):
            pc = ps[c] * inv
            psum[c] = psum[c] + pc
            acc = acc + _dot(pc.astype(BF16), vc_ref[0, 0, c * CMP_CHUNK:(c + 1) * CMP_CHUNK, :])
        o_ref[0, :, r * LANES:(r + 1) * LANES] = jnp.where(keep, acc * gates[r], 0.0).astype(o_ref.dtype)
    imp = jnp.zeros((nsb, TILE), F32)
    per_sel = NSA_SEL_BLOCK // NSA_CMP_STRIDE
    for c in range(n_chunks):
        jb = lax.broadcasted_iota(jnp.int32, (nsb, CMP_CHUNK), 0)
        ci = lax.broadcasted_iota(jnp.int32, (nsb, CMP_CHUNK), 1) + c * CMP_CHUNK
        rel = ci - per_sel * jb
        over = (rel >= 1 - NSA_CMP_LEN // NSA_CMP_STRIDE) & (rel < per_sel) & (ci < nc - 1)
        ov = jnp.where(over, 1.0, 0.0).astype(BF16)
        hi, lo = _split_bf16(psum[c])
        imp = imp + _dot_nt(ov, hi) + _dot_nt(ov, lo)
    jb = lax.broadcasted_iota(jnp.int32, (nsb, TILE), 0)
    qpos = lax.broadcasted_iota(jnp.int32, (nsb, TILE), 1) + i * TILE
    cur = qpos // NSA_SEL_BLOCK
    forced = (jb == 0) | (jb == cur) | (jb == cur - 1)
    causal = jb * NSA_SEL_BLOCK <= qpos
    score = jnp.where(forced, FORCED_SCORE, jnp.where(causal, imp, NEG_INF))
    chosen = jnp.zeros((nsb, TILE), F32)
    for _ in range(n_top):
        mx = jnp.max(score, axis=0, keepdims=True)
        first = jnp.min(jnp.where(score == mx, jb, nsb), axis=0, keepdims=True)
        pick = jb == first
        chosen = jnp.where(pick, 1.0, chosen)
        score = jnp.where(pick, -jnp.inf, score)
    not_allowed = jnp.where((chosen > 0.0) & causal, 0.0, 1.0)
    sel_ref[0, 0] = not_allowed.astype(sel_ref.dtype)


def _nsa_sparse_kernel(q_ref, k_ref, v_ref, bias_ref, gate_ref, *rest, window):
    if window:
        (o_ref,) = rest
        nsel = None
    else:
        sel_ref, o_ref = rest
        nsel = sel_ref[0, 0]
    i = pl.program_id(2)
    q4 = q_ref[0]
    gates = _gate_columns(gate_ref, 2 if window else 1)
    q_heads = [q4[:, r * LANES:(r + 1) * LANES] for r in range(NSA_GROUP)]
    if window:
        n_tiles = NSA_WINDOW // TILE + 1
        lo = jnp.maximum(i - (n_tiles - 1), 0)
        outs = _flash_chunks(q_heads, k_ref, v_ref, [0] * NSA_GROUP, bias_ref, [None] * NSA_GROUP, lo, i + 1, i,
                             1, n_tiles)
    else:
        outs = _flash_chunks(q_heads, k_ref, v_ref, [0] * NSA_GROUP, bias_ref, [nsel] * NSA_GROUP, 0, i + 1, i,
                             TILE // NSA_SEL_BLOCK, BIASED_TILES + NSA_FAR_WIDTH - 1, far_width=NSA_FAR_WIDTH)
    for r in range(NSA_GROUP):
        o_ref[0, :, r * LANES:(r + 1) * LANES] = (outs[r] * gates[r]).astype(o_ref.dtype)


def _mixer_nsa(x, w_in, w_out, table_bias, pos_k, w1_k, w2_k, pos_v, w1_v, w2_v, bsz, seq):
    bias_main, bias_win, bias_cmp = table_bias
    aw = N_HEADS * HEAD_DIM
    kvw = NSA_KV_HEADS * HEAD_DIM
    G, R = NSA_KV_HEADS, NSA_GROUP
    nc = seq // NSA_CMP_STRIDE
    nsb = seq // NSA_SEL_BLOCK
    n_top = min(NSA_TOP_N, nsb)
    nq = seq // TILE
    assert nc % CMP_CHUNK == 0
    par = lambda h: (h // R) % 2
    wq = _pad_heads_cols(w_in[:, :aw] * (ATTN_SCALE * LOG2_E), par).astype(BF16)
    w_cmp = w_in[:, aw:aw + 2 * kvw].astype(BF16)
    w_cmp = w_cmp.reshape(D_MODEL, 2 * G, HEAD_DIM).transpose(1, 0, 2)
    c0 = aw + 2 * kvw
    wk = jnp.concatenate([w_in[:, c0:c0 + kvw], w_in[:, c0 + 2 * kvw:c0 + 3 * kvw]], axis=1).astype(BF16)
    wv = jnp.concatenate([w_in[:, c0 + kvw:c0 + 2 * kvw], w_in[:, c0 + 3 * kvw:c0 + 4 * kvw]], axis=1)
    wv_t = _pad_heads_cols(wv, lambda h: 0).T.astype(BF16)
    wg = jnp.pad(w_in[:, aw + 6 * kvw:], ((0, 0), (0, LANES - 3 * N_HEADS))).astype(BF16)
    q = _matmul(x, wq, BF16).reshape(bsz, seq, N_HEADS * LANES)
    kk = _matmul(x, wk, BF16).reshape(bsz, seq, 2 * kvw)
    v_t = _matmul_t(x, wv_t)
    gates = _matmul(x, wg, F32, act='sigmoid')
    tok = _matmul_heads(x, w_cmp, bsz, seq).reshape(bsz, 2 * G, nc, NSA_CMP_STRIDE * HEAD_DIM)
    kc = _compress(tok, 0, pos_k, w1_k, w2_k, bsz, seq, True)
    vc = _compress(tok, G, pos_v, w1_v, w2_v, bsz, seq, False)

    q_spec = pl.BlockSpec((1, TILE, R * LANES), lambda b, g, i: (b, i, g))
    o_spec = pl.BlockSpec((1, TILE, R * LANES), lambda b, g, i: (b, i, g))
    gate_spec = pl.BlockSpec((TILE, LANES), lambda b, g, i: (b * nq + i, 0))
    o_shape = jax.ShapeDtypeStruct((bsz, seq, N_HEADS * LANES), BF16)
    sem = ("parallel", "parallel", "arbitrary")

    o_cmp, nsel = pl.pallas_call(
        functools.partial(_nsa_cmp_kernel, n_top=n_top),
        out_shape=[o_shape, jax.ShapeDtypeStruct((bsz, G, nsb, seq), BF16)],
        grid=(bsz, G, nq),
        in_specs=[q_spec,
                  pl.BlockSpec((1, 1, nc, LANES), lambda b, g, i: (b, g, 0, 0)),
                  pl.BlockSpec((1, 1, nc, LANES), lambda b, g, i: (b, g, 0, 0)),
                  pl.BlockSpec((R, CMP_TILES, TILE, CMP_CHUNK), lambda b, g, i: (g, 0, 0, 0)),
                  gate_spec],
        out_specs=[o_spec, pl.BlockSpec((1, 1, nsb, TILE), lambda b, g, i: (b, g, 0, i))],
        compiler_params=_params(sem),
        name="nsa_compressed_attn",
    )(q, kc, vc, bias_cmp, gates)

    def k_spec(first_blk):
        return pl.BlockSpec((1, seq, LANES), lambda b, g, i: (b, 0, first_blk + g // 2))

    def vt_spec(first_blk):
        return pl.BlockSpec((LANES, seq), lambda b, g, i: (first_blk + g, b))

    o_sel = pl.pallas_call(
        functools.partial(_nsa_sparse_kernel, window=False),
        out_shape=o_shape,
        grid=(bsz, G, nq),
        in_specs=[q_spec, k_spec(0), vt_spec(0),
                  pl.BlockSpec((R, MAIN_TILES, TILE, TILE), lambda b, g, i: (g, 0, 0, 0)),
                  gate_spec,
                  pl.BlockSpec((1, 1, nsb, TILE), lambda b, g, i: (b, g, 0, i))],
        out_specs=o_spec,
        compiler_params=_params(sem),
        name="nsa_selected_attn",
    )(q, kk, v_t, bias_main, gates, nsel)

    o_win = pl.pallas_call(
        functools.partial(_nsa_sparse_kernel, window=True),
        out_shape=o_shape,
        grid=(bsz, G, nq),
        in_specs=[q_spec, k_spec(2), vt_spec(G),
                  pl.BlockSpec((R, NSA_WINDOW // TILE + 1, TILE, TILE), lambda b, g, i: (g, 0, 0, 0)),
                  gate_spec],
        out_specs=o_spec,
        compiler_params=_params(sem),
        name="nsa_window_attn",
    )(q, kk, v_t, bias_win, gates)

    m = bsz * seq
    parts = [t.reshape(m, N_HEADS * LANES) for t in (o_cmp, o_sel, o_win)]
    return parts, _pad_heads_rows(w_out, lambda h: 0).astype(BF16)


def _router_kernel(x_ref, w_ref, b_ref, o_ref):
    logits = lax.dot_general(w_ref[...], x_ref[...], (((1,), (1,)), ((), ())),
                             preferred_element_type=F32, precision=lax.Precision.HIGHEST)
    scores = 1.0 / (1.0 + jnp.exp(-logits))
    biased = scores + b_ref[...]
    rows = [biased[e:e + 1, :] for e in range(N_EXPERTS)]
    group_score = []
    for g in range(N_GROUPS):
        r = rows[g * EXPERTS_PER_GROUP:(g + 1) * EXPERTS_PER_GROUP]
        best = None
        for a in range(EXPERTS_PER_GROUP):
            for c in range(a + 1, EXPERTS_PER_GROUP):
                pair = r[a] + r[c]
                best = pair if best is None else jnp.maximum(best, pair)
        group_score.append(best)
    best_val = group_score[0]
    best_grp = jnp.zeros_like(best_val, dtype=jnp.int32)
    for g in range(1, N_GROUPS):
        better = group_score[g] > best_val
        best_val = jnp.where(better, group_score[g], best_val)
        best_grp = jnp.where(better, g, best_grp)
    picked = []
    for e in range(N_EXPERTS):
        g, a = divmod(e, EXPERTS_PER_GROUP)
        rank = jnp.zeros_like(best_grp)
        for c in range(EXPERTS_PER_GROUP):
            if c == a:
                continue
            other = rows[g * EXPERTS_PER_GROUP + c]
            ahead = (other > rows[e]) | ((other == rows[e]) & (c < a))
            rank = rank + jnp.where(ahead, 1, 0)
        picked.append((best_grp == g) & (rank < 2))
    raw = [jnp.where(picked[e], scores[e:e + 1, :], 0.0) for e in range(N_EXPERTS)]
    total = raw[0]
    for e in range(1, N_EXPERTS):
        total = total + raw[e]
    o_ref[...] = (jnp.concatenate(raw, axis=0) / total).T


def _moe_kernel(x_ref, gate_ref, wg_ref, wu_ref, wd_ref, g_ref, b_ref, o_ref, xb_ref, acc_ref):
    e = pl.program_id(1)

    @pl.when(e == 0)
    def _():
        xb_ref[...] = x_ref[...].astype(BF16)
        acc_ref[...] = jnp.zeros_like(acc_ref)

    xb = xb_ref[...]
    lane = lax.broadcasted_iota(jnp.int32, gate_ref.shape, 1)
    gcol = jnp.sum(jnp.where(lane == e, gate_ref[...], 0.0), axis=-1, keepdims=True)
    a = _dot(xb, wg_ref[0])
    h = a / (1.0 + jnp.exp(-a)) * _dot(xb, wu_ref[0])
    acc_ref[...] += gcol * _dot(h.astype(BF16), wd_ref[0])

    @pl.when(e == N_EXPERTS - 1)
    def _():
        o_ref[...] = _layer_norm(DEEPNORM_ALPHA * x_ref[...] + acc_ref[...], g_ref[...], b_ref[...])


def _moe_ln(x, router_w, router_b, w_gate, w_up, w_down, g, b):
    m, d = x.shape
    tm = 1024
    gates = pl.pallas_call(
        _router_kernel,
        out_shape=jax.ShapeDtypeStruct((m, N_EXPERTS), F32),
        grid=(m // tm,),
        in_specs=[pl.BlockSpec((tm, d), lambda i: (i, 0)),
                  pl.BlockSpec((N_EXPERTS, d), lambda i: (0, 0)),
                  pl.BlockSpec((N_EXPERTS, 1), lambda i: (0, 0))],
        out_specs=pl.BlockSpec((tm, N_EXPERTS), lambda i: (i, 0)),
        compiler_params=_params(("parallel",)),
        name="moe_router",
    )(x, router_w.T, router_b.reshape(N_EXPERTS, 1))
    de = w_gate.shape[-1]
    return pl.pallas_call(
        _moe_kernel,
        out_shape=jax.ShapeDtypeStruct((m, d), F32),
        grid=(m // tm, N_EXPERTS),
        in_specs=[pl.BlockSpec((tm, d), lambda i, e: (i, 0)),
                  pl.BlockSpec((tm, N_EXPERTS), lambda i, e: (i, 0)),
                  pl.BlockSpec((1, d, de), lambda i, e: (e, 0, 0)),
                  pl.BlockSpec((1, d, de), lambda i, e: (e, 0, 0)),
                  pl.BlockSpec((1, de, d), lambda i, e: (e, 0, 0)),
                  pl.BlockSpec((1, d), lambda i, e: (0, 0)),
                  pl.BlockSpec((1, d), lambda i, e: (0, 0))],
        out_specs=pl.BlockSpec((tm, d), lambda i, e: (i, 0)),
        scratch_shapes=[pltpu.VMEM((tm, d), BF16), pltpu.VMEM((tm, d), F32)],
        compiler_params=_params(("parallel", "arbitrary")),
        name="moe_experts_ln",
    )(x, gates, w_gate.astype(BF16), w_up.astype(BF16), w_down.astype(BF16), g.reshape(1, d), b.reshape(1, d))


def kernel(x, rel_table, router_w, router_b, ln1_g, ln1_b, ln2_g, ln2_b, exp_w_gate, exp_w_up, exp_w_down, dil_w_in, dil_w_out, sb_w_in, sb_w_out, nsa_w_in, nsa_w_out, nsa_cmp_pos_k, nsa_cmp_w1_k, nsa_cmp_w2_k, nsa_cmp_pos_v, nsa_cmp_w1_v, nsa_cmp_w2_v, moba_w_in, moba_w_out):
    bsz, seq, d = x.shape
    assert d == D_MODEL and seq % (DILATED_PAIRS[-1][1] * DIL_BLOCK) == 0
    depth = ln1_g.shape[0]
    n_mixers = 4
    table = rel_table.astype(F32)
    span = DIL_BLOCK
    big = 1 << 30
    bias_dil = _bias_tiles(table, len(DILATED_PAIRS), DIL_BLOCK, 2 * DIL_BLOCK, DIL_BLOCK, 0, 1, 0, span,
                           [dl for _, dl in DILATED_PAIRS])
    assert BIASED_TILES * TILE - (TILE - 1) >= BUCKET_THRESHOLDS[-1]
    bias_main = _bias_tiles(table, MAIN_TILES, TILE, TILE, 0, TILE, 1, 0, big, [1], shift=True, transposed=True,
                            scale=LOG2_E)
    bias_win = _bias_tiles(table, NSA_WINDOW // TILE + 1, TILE, TILE, 0, TILE, 1, 0, NSA_WINDOW - 1, [1],
                           transposed=True, scale=LOG2_E)
    bias_cmp = _bias_tiles(table, CMP_TILES, TILE, CMP_CHUNK, -TILE - (NSA_CMP_LEN - 1), TILE, NSA_CMP_STRIDE,
                           0, big, [1], scale=LOG2_E)
    xf = x.reshape(bsz * seq, d)
    for layer in range(depth):
        kind, occ = layer % n_mixers, layer // n_mixers
        merge = None
        if kind == 0:
            parts, w_out = _mixer_dilated(xf, dil_w_in[occ], dil_w_out[occ], bias_dil, bsz, seq)
            merge = 'lse3'
        elif kind == 1:
            parts, w_out = _mixer_stick_breaking(xf, sb_w_in[occ], sb_w_out[occ], bsz, seq)
        elif kind == 2:
            parts, w_out = _mixer_nsa(xf, nsa_w_in[occ], nsa_w_out[occ], (bias_main, bias_win, bias_cmp),
                                      nsa_cmp_pos_k[occ], nsa_cmp_w1_k[occ], nsa_cmp_w2_k[occ],
                                      nsa_cmp_pos_v[occ], nsa_cmp_w1_v[occ], nsa_cmp_w2_v[occ], bsz, seq)
            merge = 'sum3'
        else:
            parts, w_out = _mixer_moba(xf, moba_w_in[occ], moba_w_out[occ], bias_main, bsz, seq)
        xf = _outproj_ln(parts, w_out, xf, ln1_g[layer], ln1_b[layer], merge=merge)
        xf = _moe_ln(xf, router_w, router_b, exp_w_gate[layer], exp_w_up[layer], exp_w_down[layer],
                     ln2_g[layer], ln2_b[layer])
    return xf.reshape(bsz, seq, d)
```

```python
import functools
import math

import numpy as np
import jax
import jax.numpy as jnp
from jax import lax
from jax.experimental import pallas as pl
from jax.experimental.pallas import tpu as pltpu

F32 = jnp.float32
BF16 = jnp.bfloat16

D_MODEL = 1024
HEAD_DIM = 64
N_HEADS = 16
LANES = 128
PAIRS = N_HEADS // 2
ATTN_SCALE = HEAD_DIM ** -0.5
REL_BUCKETS = 32
REL_MAX_DIST = 2048
DILATED_PAIRS = ((128, 1), (512, 4), (2048, 16))
DIL_BLOCK = 128
SB_CLIP = 60.0
LOG2_E = math.log2(math.e)
NSA_KV_HEADS = 4
NSA_GROUP = N_HEADS // NSA_KV_HEADS
NSA_CMP_LEN = 32
NSA_CMP_STRIDE = 16
NSA_SEL_BLOCK = 64
NSA_TOP_N = 16
NSA_WINDOW = 512
MOBA_BLOCK = 256
MOBA_TOP_K = 3
N_EXPERTS = 16
N_GROUPS = 4
EXPERTS_PER_GROUP = N_EXPERTS // N_GROUPS
DEPTH = 4
DEEPNORM_ALPHA = (2 * DEPTH) ** 0.25
LN_EPS = 1e-5
NEG_INF = -1e30
FORCED_SCORE = 1e9
TILE = 256
BIASED_TILES = 7
MOBA_FAR_WIDTH = 2
NSA_FAR_WIDTH = 4
MAIN_TILES = BIASED_TILES + max(MOBA_FAR_WIDTH, NSA_FAR_WIDTH) - 1
CMP_CHUNK = 128
CMP_TILES = 19
VMEM_LIMIT = 56 * 1024 * 1024


def _bucket_thresholds():
    n = np.arange(0, 2 * REL_MAX_DIST)
    exact = REL_BUCKETS // 2
    logf = np.log(np.maximum(n, 1).astype(np.float64) / exact) / math.log(REL_MAX_DIST / exact)
    large = np.minimum(exact + (logf * (REL_BUCKETS - exact)).astype(np.int64), REL_BUCKETS - 1)
    bucket = np.where(n < exact, n, large)
    return tuple(int(np.argmax(bucket >= k)) for k in range(1, REL_BUCKETS))


BUCKET_THRESHOLDS = _bucket_thresholds()


def _params(sem, vmem=VMEM_LIMIT):
    return pltpu.CompilerParams(dimension_semantics=sem, vmem_limit_bytes=vmem)


def _dot(a, b):
    return jnp.dot(a, b, preferred_element_type=F32)


def _dot_nt(a, b):
    return lax.dot_general(a, b, (((1,), (1,)), ((), ())), preferred_element_type=F32)


def _split_bf16(x):
    hi = x.astype(BF16)
    lo = (x - hi.astype(F32)).astype(BF16)
    return hi, lo


def _matmul_kernel(x_ref, w_ref, o_ref, *, act):
    y = _dot(x_ref[...].astype(BF16), w_ref[...])
    if act == 'sigmoid':
        y = 1.0 / (1.0 + jnp.exp(-y))
    o_ref[...] = y.astype(o_ref.dtype)


def _matmul(x, w, out_dtype, act=None):
    m, k = x.shape
    n = w.shape[1]
    tm = 1024
    tn = next(t for t in (1024, 512, 128) if n % t == 0)
    return pl.pallas_call(
        functools.partial(_matmul_kernel, act=act),
        out_shape=jax.ShapeDtypeStruct((m, n), out_dtype),
        grid=(m // tm, n // tn),
        in_specs=[pl.BlockSpec((tm, k), lambda i, j: (i, 0)),
                  pl.BlockSpec((k, tn), lambda i, j: (0, j))],
        out_specs=pl.BlockSpec((tm, tn), lambda i, j: (i, j)),
        compiler_params=_params(("parallel", "arbitrary")),
        name="proj_matmul",
    )(x, w)


def _matmul_t_kernel(x_ref, w_ref, o_ref):
    y = _dot_nt(w_ref[...], x_ref[...].astype(BF16))
    row = lax.broadcasted_iota(jnp.int32, y.shape, 0)
    o_ref[...] = jnp.where(row % LANES == HEAD_DIM, 1.0, y).astype(o_ref.dtype)


def _matmul_t(x, wt):
    m, k = x.shape
    n = wt.shape[0]
    tm = 512
    tn = 512 if n % 512 == 0 else 256
    return pl.pallas_call(
        _matmul_t_kernel,
        out_shape=jax.ShapeDtypeStruct((n, m), BF16),
        grid=(m // tm, n // tn),
        in_specs=[pl.BlockSpec((tm, k), lambda i, j: (i, 0)),
                  pl.BlockSpec((tn, k), lambda i, j: (j, 0))],
        out_specs=pl.BlockSpec((tn, tm), lambda i, j: (j, i)),
        compiler_params=_params(("parallel", "arbitrary")),
        name="proj_matmul_t",
    )(x, wt)


def _matmul_heads_kernel(x_ref, w_ref, o_ref):
    o_ref[0, 0] = _dot(x_ref[...].astype(BF16), w_ref[0]).astype(o_ref.dtype)


def _matmul_heads(x, w, bsz, seq):
    m, k = x.shape
    n = w.shape[0]
    tm = 512
    per_b = seq // tm
    return pl.pallas_call(
        _matmul_heads_kernel,
        out_shape=jax.ShapeDtypeStruct((bsz, n, seq, HEAD_DIM), BF16),
        grid=(m // tm, n),
        in_specs=[pl.BlockSpec((tm, k), lambda i, j: (i, 0)),
                  pl.BlockSpec((1, k, HEAD_DIM), lambda i, j: (j, 0, 0))],
        out_specs=pl.BlockSpec((1, 1, tm, HEAD_DIM), lambda i, j: (i // per_b, j, i % per_b, 0)),
        compiler_params=_params(("parallel", "arbitrary")),
        name="proj_heads",
    )(x, w)


def _layer_norm(z, g, b):
    mu = jnp.mean(z, axis=-1, keepdims=True)
    zc = z - mu
    var = jnp.mean(zc * zc, axis=-1, keepdims=True)
    return zc * lax.rsqrt(var + LN_EPS) * g + b


def _outproj_kernel(*refs, n_in, merge):
    ins = refs[:n_in]
    w_ref, x_ref, g_ref, b_ref, o_ref = refs[n_in:]
    if merge == 'lse3':
        o1, o2, o3, l1, l2, l3 = [r[...] for r in ins]
        mx = jnp.maximum(jnp.maximum(l1, l2), l3)
        e1, e2, e3 = jnp.exp(l1 - mx), jnp.exp(l2 - mx), jnp.exp(l3 - mx)
        a = (e1 * o1 + e2 * o2 + e3 * o3) / (e1 + e2 + e3)
    elif merge == 'sum3':
        a = ins[0][...].astype(F32) + ins[1][...].astype(F32) + ins[2][...].astype(F32)
    else:
        a = ins[0][...]
    y = _dot(a.astype(BF16), w_ref[...])
    o_ref[...] = _layer_norm(DEEPNORM_ALPHA * x_ref[...] + y, g_ref[...], b_ref[...])


def _outproj_ln(ins, w, x, g, b, merge=None):
    m, d = x.shape
    ka = ins[0].shape[1]
    tm = 256
    n_in = len(ins)
    return pl.pallas_call(
        functools.partial(_outproj_kernel, n_in=n_in, merge=merge),
        out_shape=jax.ShapeDtypeStruct((m, d), F32),
        grid=(m // tm,),
        in_specs=[pl.BlockSpec((tm, ka), lambda i: (i, 0)) for _ in ins]
        + [pl.BlockSpec((ka, d), lambda i: (0, 0)),
           pl.BlockSpec((tm, d), lambda i: (i, 0)),
           pl.BlockSpec((1, d), lambda i: (0, 0)),
           pl.BlockSpec((1, d), lambda i: (0, 0))],
        out_specs=pl.BlockSpec((tm, d), lambda i: (i, 0)),
        compiler_params=_params(("parallel",)),
        name="outproj_ln",
    )(*ins, w, x, g.reshape(1, d), b.reshape(1, d))


def _bias_kernel(tbl_ref, o_ref, *, rows, cols, off0, tstride, cmul, lo, hi, mults, shift, transposed, scale):
    t = pl.program_id(0)
    if len(set(mults)) == 1:
        mult = mults[0]
    else:
        mult = jnp.int32(mults[-1])
        for idx in range(len(mults) - 2, -1, -1):
            mult = jnp.where(t == idx, jnp.int32(mults[idx]), mult)
    base = off0 + tstride * t

    def strip(i, carry):
        r0 = pl.multiple_of(i * 8, 8)
        a = lax.broadcasted_iota(jnp.int32, (8, cols), 0) + r0
        c = lax.broadcasted_iota(jnp.int32, (8, cols), 1)
        steps = base + c - cmul * a if transposed else base + a - cmul * c
        valid = (steps >= lo) & (steps <= hi)
        dist = steps * mult
        for h in range(N_HEADS):
            acc = jnp.full((8, cols), tbl_ref[0, h], F32)
            for k, thr in enumerate(BUCKET_THRESHOLDS):
                acc = jnp.where(dist >= thr, tbl_ref[k + 1, h], acc)
            if shift:
                acc = acc - tbl_ref[REL_BUCKETS - 1, h]
            if scale != 1.0:
                acc = acc * scale
            o_ref[h, 0, pl.ds(r0, 8), :] = jnp.where(valid, acc, NEG_INF)
        return carry

    lax.fori_loop(0, rows // 8, strip, 0)


def _bias_tiles(table, n_tiles, rows, cols, off0, tstride, cmul, lo, hi, mults, shift=False, transposed=False,
                scale=1.0):
    return pl.pallas_call(
        functools.partial(_bias_kernel, rows=rows, cols=cols, off0=off0, tstride=tstride, cmul=cmul,
                          lo=lo, hi=hi, mults=tuple(mults), shift=shift, transposed=transposed, scale=scale),
        out_shape=jax.ShapeDtypeStruct((N_HEADS, n_tiles, rows, cols), F32),
        grid=(n_tiles,),
        in_specs=[pl.BlockSpec(memory_space=pltpu.SMEM)],
        out_specs=pl.BlockSpec((N_HEADS, 1, rows, cols), lambda t: (0, t, 0, 0)),
        compiler_params=_params(("arbitrary",)),
        name="bias_tiles",
    )(table)


def _dil_kernel(q_ref, kp_ref, kc_ref, vp_ref, vc_ref, bias_ref, o_ref, lse_ref,
                q_scr, kp_scr, kc_scr, vp_scr, vc_scr, *, dilation, per_step):
    j = pl.program_id(2)
    d = dilation
    n_cur = per_step * DIL_BLOCK

    def rows_of(ref, r, n, lead=()):
        if d == 1:
            return ref[lead + (slice(0, n), slice(None))]
        return ref[lead + (pl.ds(r, n, stride=d), slice(None))]

    if d > 1:
        for h in range(2):
            q_scr[h] = q_ref[0, :, h * LANES:(h + 1) * LANES].astype(F32)
        kp_scr[...] = kp_ref[0].astype(F32)
        kc_scr[...] = kc_ref[0].astype(F32)
        vp_scr[...] = vp_ref[0].astype(F32)
        vc_scr[...] = vc_ref[0].astype(F32)
    colk = lax.broadcasted_iota(jnp.int32, (DIL_BLOCK, 2 * DIL_BLOCK), 1)
    first = jnp.where((colk < DIL_BLOCK) & (j == 0), NEG_INF, 0.0)
    low = lax.broadcasted_iota(jnp.int32, (DIL_BLOCK, LANES), 1) < HEAD_DIM
    for r in range(d):
        if d == 1:
            q_heads = [q_ref[0, :, h * LANES:(h + 1) * LANES] for h in range(2)]
            k_all = jnp.concatenate([kp_ref[0], kc_ref[0]], axis=0)
            v_all = jnp.concatenate([vp_ref[0], vc_ref[0]], axis=0)
        else:
            q_heads = [rows_of(q_scr, r, n_cur, (h,)).astype(BF16) for h in range(2)]
            k_all = jnp.concatenate([rows_of(kp_scr, r, DIL_BLOCK), rows_of(kc_scr, r, n_cur)], axis=0).astype(BF16)
            v_all = jnp.concatenate([rows_of(vp_scr, r, DIL_BLOCK), rows_of(vc_scr, r, n_cur)], axis=0).astype(BF16)
        for u in range(per_step):
            k2 = k_all[u * DIL_BLOCK:(u + 2) * DIL_BLOCK]
            v2 = v_all[u * DIL_BLOCK:(u + 2) * DIL_BLOCK]
            outs, lses = [], []
            for h in range(2):
                s = _dot_nt(q_heads[h][u * DIL_BLOCK:(u + 1) * DIL_BLOCK], k2) + bias_ref[h, 0]
                if u == 0:
                    s = s + first
                m = jnp.max(s, axis=-1, keepdims=True)
                p = jnp.exp(s - m)
                l = jnp.maximum(jnp.sum(p, axis=-1, keepdims=True), 1e-30)
                outs.append(_dot(p.astype(BF16), v2) / l)
                lses.append(m + jnp.log(l))
            if d == 1:
                dst = (0, slice(u * DIL_BLOCK, (u + 1) * DIL_BLOCK), slice(None))
            else:
                dst = (0, pl.ds(u * DIL_BLOCK * d + r, DIL_BLOCK, stride=d), slice(None))
            o_ref[dst] = jnp.where(low, outs[0], outs[1])
            lse_ref[dst] = jnp.where(low, lses[0], lses[1])


def _dilated_group(proj, bias, g, dilation, bsz, seq):
    prev_rows = DIL_BLOCK * dilation
    step_rows = math.gcd(seq, 16 * DIL_BLOCK)
    per_step = step_rows // prev_rows
    assert per_step >= 1 and seq % step_rows == 0
    q_blk = g * N_HEADS // 2
    k_blk = 3 * N_HEADS + g * 2 * PAIRS
    v_blk = k_blk + PAIRS

    def kv_specs(base):
        prev = pl.BlockSpec((1, prev_rows, LANES), lambda hp, b, j: (b, jnp.maximum(per_step * j - 1, 0), base + hp))
        cur = pl.BlockSpec((1, step_rows, LANES), lambda hp, b, j: (b, j, base + hp))
        return [prev, cur]

    out_spec = pl.BlockSpec((1, step_rows, LANES), lambda hp, b, j: (b, j, hp))
    staged = 8 if dilation == 1 else None
    o, lse = pl.pallas_call(
        functools.partial(_dil_kernel, dilation=dilation, per_step=per_step),
        out_shape=[jax.ShapeDtypeStruct((bsz, seq, D_MODEL), F32)] * 2,
        grid=(PAIRS, bsz, seq // step_rows),
        in_specs=[pl.BlockSpec((1, step_rows, 2 * LANES), lambda hp, b, j: (b, j, q_blk + hp))]
        + kv_specs(k_blk) + kv_specs(v_blk)
        + [pl.BlockSpec((2, 1, DIL_BLOCK, 2 * DIL_BLOCK), lambda hp, b, j: (hp, g, 0, 0))],
        out_specs=[out_spec, out_spec],
        scratch_shapes=[pltpu.VMEM((2, staged or step_rows, LANES), F32),
                        pltpu.VMEM((staged or prev_rows, LANES), F32), pltpu.VMEM((staged or step_rows, LANES), F32),
                        pltpu.VMEM((staged or prev_rows, LANES), F32), pltpu.VMEM((staged or step_rows, LANES), F32)],
        compiler_params=_params(("parallel", "parallel", "arbitrary")),
        name="dilated_attn",
    )(proj, proj, proj, proj, proj, bias)
    return o.reshape(bsz * seq, D_MODEL), lse.reshape(bsz * seq, D_MODEL)


def _pad_heads_cols(w, parity_of_head):
    k, n = w.shape
    nh = n // HEAD_DIM
    wh = w.reshape(k, nh, HEAD_DIM)
    z = jnp.zeros_like(wh)
    par = jnp.asarray([parity_of_head(h) for h in range(nh)], jnp.int32)[None, :, None]
    lo = jnp.where(par == 0, wh, z)
    hi = jnp.where(par == 1, wh, z)
    return jnp.concatenate([lo, hi], axis=-1).reshape(k, nh * LANES)


def _pad_heads_rows(w, parity_of_head):
    return _pad_heads_cols(w.T, parity_of_head).T


def _mixer_dilated(x, w_in, w_out, bias_dil, bsz, seq):
    aw = N_HEADS * HEAD_DIM
    par = lambda h: h % 2
    qs, kvs = [], []
    for g in range(len(DILATED_PAIRS)):
        base = g * 3 * aw
        qs.append(_pad_heads_cols(w_in[:, base:base + aw] * ATTN_SCALE, par))
        kvs.append(w_in[:, base + aw:base + 3 * aw])
    w_all = jnp.concatenate(qs + kvs, axis=1).astype(BF16)
    proj = _matmul(x, w_all, BF16).reshape(bsz, seq, -1)
    parts = [_dilated_group(proj, bias_dil, g, d, bsz, seq) for g, (_, d) in enumerate(DILATED_PAIRS)]
    return [p[0] for p in parts] + [p[1] for p in parts], w_out.astype(BF16)


def _sb_kernel(q_ref, k_ref, v_ref, o_ref):
    i = pl.program_id(2)
    t = TILE
    row = lax.broadcasted_iota(jnp.int32, (t, t), 0)
    col = lax.broadcasted_iota(jnp.int32, (t, t), 1)
    suffix = jnp.where(row >= col, 1.0, 0.0).astype(BF16)
    strict = col < row
    q2 = q_ref[0]
    q_heads = [q2[:, h * LANES:(h + 1) * LANES] for h in range(2)]

    def block(qh, kj, masked):
        start = pl.multiple_of(kj * t, t)
        kb = k_ref[0, pl.ds(start, t), :]
        vb = v_ref[0, pl.ds(start, t), :]
        a = jnp.clip(_dot_nt(qh, kb), -SB_CLIP, SB_CLIP) * LOG2_E
        sp = jnp.log2(1.0 + jnp.exp2(a))
        if masked:
            sp = jnp.where(strict, sp, 0.0)
        hi, lo = _split_bf16(sp)
        rr = _dot(hi, suffix) + _dot(lo, suffix)
        att = jnp.exp2(a - rr)
        if masked:
            att = jnp.where(strict, att, 0.0)
        return _dot(att.astype(BF16), vb), rr[:, 0:1]

    def diagonal(kj, carry):
        return tuple(block(qh, kj, True) for qh in q_heads)

    zero = (jnp.zeros((t, LANES), F32), jnp.zeros((t, 1), F32))
    (acc0, r0), (acc1, r1) = lax.fori_loop(i, i + 1, diagonal, (zero, zero))

    def cond(carry):
        n, alive = carry[0], carry[1]
        return (n < i) & alive

    def step(carry):
        n = carry[0]
        new, tops = [], []
        for qh, (r_sum, w, acc) in zip(q_heads, carry[2:]):
            pv, r_blk = block(qh, i - 1 - n, False)
            r_new = r_sum + r_blk
            w_new = jnp.exp2(-r_new)
            new.append((r_new, w_new, acc + w * pv))
            tops.append(jnp.max(w_new, axis=0, keepdims=True))
        alive = jnp.maximum(tops[0], tops[1])[0, 0] > 0.0
        return (n + 1, alive) + tuple(new)

    init = (jnp.int32(0), jnp.bool_(True), (r0, jnp.exp2(-r0), acc0), (r1, jnp.exp2(-r1), acc1))
    res = lax.while_loop(cond, step, init)
    lane = lax.broadcasted_iota(jnp.int32, (t, LANES), 1)
    o_ref[0] = jnp.where(lane < HEAD_DIM, res[2][2], res[3][2]).astype(o_ref.dtype)


def _mixer_stick_breaking(x, w_in, w_out, bsz, seq):
    aw = N_HEADS * HEAD_DIM
    wq = _pad_heads_cols(w_in[:, :aw] * ATTN_SCALE, lambda h: h % 2)
    w_all = jnp.concatenate([wq, w_in[:, aw:]], axis=1).astype(BF16)
    proj = _matmul(x, w_all, BF16).reshape(bsz, seq, -1)
    kb = N_HEADS
    o = pl.pallas_call(
        _sb_kernel,
        out_shape=jax.ShapeDtypeStruct((bsz, seq, D_MODEL), BF16),
        grid=(bsz, PAIRS, seq // TILE),
        in_specs=[pl.BlockSpec((1, TILE, 2 * LANES), lambda b, hp, i: (b, i, hp)),
                  pl.BlockSpec((1, seq, LANES), lambda b, hp, i: (b, 0, kb + hp)),
                  pl.BlockSpec((1, seq, LANES), lambda b, hp, i: (b, 0, kb + PAIRS + hp))],
        out_specs=pl.BlockSpec((1, TILE, LANES), lambda b, hp, i: (b, i, hp)),
        compiler_params=_params(("parallel", "parallel", "arbitrary")),
        name="stick_breaking_attn",
    )(proj, proj, proj)
    return [o.reshape(bsz * seq, D_MODEL)], w_out.astype(BF16)


def _flash_chunks(q_heads, k_ref, v_ref, v_rows, bias_ref, nsel_of_head, lo, hi, i, blocks_per_chunk, n_tiles,
                  fold_mask=False, pipelined=False, far_width=1):
    nh = len(q_heads)
    distinct = []
    if not fold_mask:
        for ns in nsel_of_head:
            if ns is not None and all(ns is not d for d in distinct):
                distinct.append(ns)

    def logits(ch, near, width):
        rows = width * TILE
        kb = k_ref[0, pl.ds(pl.multiple_of(ch * TILE, TILE), rows), :]
        adds = []
        if fold_mask:
            key = lax.broadcasted_iota(jnp.int32, (rows, LANES), 0)
            blk = lax.broadcasted_iota(jnp.int32, (rows, LANES), 1)
            blk_of_key = ch * blocks_per_chunk + key // (TILE // blocks_per_chunk)
            kb = jnp.concatenate([kb, jnp.where(blk == blk_of_key, NEG_INF, 0.0).astype(BF16)], axis=1)
        else:
            for ns in distinct:
                nblk = ns.shape[0]
                kk = lax.broadcasted_iota(jnp.int32, (rows, nblk), 0)
                bj = lax.broadcasted_iota(jnp.int32, (rows, nblk), 1)
                blk_of_key = ch * blocks_per_chunk + kk // (TILE // blocks_per_chunk)
                expand = jnp.where(bj == blk_of_key, NEG_INF, 0.0).astype(BF16)
                adds.append(_dot(expand, ns))
        out = []
        for h in range(nh):
            s = _dot_nt(kb, q_heads[h])
            if near:
                s = s + bias_ref[h, i - ch]
            ns = nsel_of_head[h]
            if ns is not None and not fold_mask:
                s = s + adds[[d is ns for d in distinct].index(True)]
            out.append(s)
        return tuple(out)

    def update(ch, s_heads, state, width):
        start = pl.multiple_of(ch * TILE, TILE)
        vts = {}
        new = []
        for h in range(nh):
            r0 = v_rows[h]
            if r0 not in vts:
                vts[r0] = v_ref[r0:r0 + LANES, pl.ds(start, width * TILE)]
            m, acc = state[h]
            s = s_heads[h]
            m_new = jnp.maximum(m, jnp.max(s, axis=0, keepdims=True))
            alpha = jnp.exp2(m - m_new)
            p = jnp.exp2(s - m_new)
            acc = alpha * acc + _dot(vts[r0], p.astype(BF16))
            new.append((m_new, acc))
        return tuple(new)

    def run(ch0, steps, near, width, state):
        if not pipelined:
            return lax.fori_loop(
                0, steps, lambda t, st: update(ch0 + t * width, logits(ch0 + t * width, near, width), st, width), state)

        def body(t, carry):
            s_cur, st = carry
            s_next = logits(ch0 + jnp.minimum(t + 1, steps - 1) * width, near, width)
            return s_next, update(ch0 + t * width, s_cur, st, width)

        zeros = tuple(jnp.zeros((width * TILE, TILE), F32) for _ in range(nh))
        first = jnp.minimum(ch0, i)
        s0 = lax.fori_loop(first, first + 1, lambda c, _: logits(c, near, width), zeros)
        return lax.fori_loop(0, steps, body, (s0, state))[1]

    state = tuple((jnp.full((1, TILE), NEG_INF, F32), jnp.zeros((LANES, TILE), F32)) for _ in range(nh))
    n_near = n_tiles - (far_width - 1)
    far_steps = jnp.maximum(hi - n_near - lo, 0) // far_width
    split = lo + far_steps * far_width
    state = run(lo, far_steps, False, far_width, state)
    state = run(split, hi - split, True, 1, state)
    outs = []
    for _, acc in state:
        o = (acc / acc[HEAD_DIM:HEAD_DIM + 1, :]).T
        lane = lax.broadcasted_iota(jnp.int32, o.shape, 1)
        outs.append(jnp.where(lane < HEAD_DIM, o, 0.0))
    return outs


def _kmean_kernel(k_ref, o_ref):
    rows = k_ref.shape[1]
    kf = k_ref[0].astype(F32).reshape(rows // MOBA_BLOCK, MOBA_BLOCK, k_ref.shape[2])
    o_ref[0] = jnp.sum(kf, axis=1) * (1.0 / MOBA_BLOCK)


def _moba_kernel(q_ref, k_ref, v_ref, km_ref, bias_ref, o_ref):
    i = pl.program_id(2)
    nblk = km_ref.shape[1]
    km_hi, km_lo = _split_bf16(km_ref[0])
    q2 = q_ref[0]
    blk = lax.broadcasted_iota(jnp.int32, (nblk, TILE), 0)
    past = blk < i
    q_heads = []
    for h in range(2):
        qh = q2[:, h * LANES:(h + 1) * LANES]
        gate = jnp.where(past, _dot_nt(km_hi, qh) + _dot_nt(km_lo, qh), NEG_INF)
        allowed = blk == i
        for _ in range(MOBA_TOP_K):
            mx = jnp.max(gate, axis=0, keepdims=True)
            first = jnp.min(jnp.where(gate == mx, blk, nblk), axis=0, keepdims=True)
            pick = blk == first
            allowed = allowed | (pick & past)
            gate = jnp.where(pick, -jnp.inf, gate)
        nsel = jnp.where(allowed, 0.0, 1.0).T.astype(BF16)
        q_heads.append(jnp.concatenate([qh, nsel], axis=1))
    outs = _flash_chunks(q_heads, k_ref, v_ref, [0, LANES], bias_ref, [None, None], 0, i + 1, i, 1,
                         BIASED_TILES + MOBA_FAR_WIDTH - 1, fold_mask=True, pipelined=True, far_width=MOBA_FAR_WIDTH)
    for h in range(2):
        o_ref[0, :, h * LANES:(h + 1) * LANES] = outs[h].astype(o_ref.dtype)


def _mixer_moba(x, w_in, w_out, bias_main, bsz, seq):
    aw = N_HEADS * HEAD_DIM
    nblk = seq // MOBA_BLOCK
    assert nblk - 1 >= MOBA_TOP_K
    wq = _pad_heads_cols(w_in[:, :aw] * (ATTN_SCALE * LOG2_E), lambda h: h % 2)
    w_all = jnp.concatenate([wq, w_in[:, aw:2 * aw]], axis=1).astype(BF16)
    proj = _matmul(x, w_all, BF16).reshape(bsz, seq, -1)
    v_t = _matmul_t(x, _pad_heads_cols(w_in[:, 2 * aw:], lambda h: 0).T.astype(BF16))
    kb = N_HEADS
    rows = 8 * MOBA_BLOCK
    kmean = pl.pallas_call(
        _kmean_kernel,
        out_shape=jax.ShapeDtypeStruct((bsz, nblk, D_MODEL), F32),
        grid=(bsz, seq // rows),
        in_specs=[pl.BlockSpec((1, rows, D_MODEL), lambda b, i: (b, i, kb * LANES // D_MODEL))],
        out_specs=pl.BlockSpec((1, 8, D_MODEL), lambda b, i: (b, i, 0)),
        compiler_params=_params(("parallel", "parallel")),
        name="moba_kmean",
    )(proj)
    assert nblk <= LANES
    kmean = jnp.pad(kmean, ((0, 0), (0, LANES - nblk), (0, 0)))
    o = pl.pallas_call(
        _moba_kernel,
        out_shape=jax.ShapeDtypeStruct((bsz, seq, N_HEADS * LANES), BF16),
        grid=(bsz, PAIRS, seq // TILE),
        in_specs=[pl.BlockSpec((1, TILE, 2 * LANES), lambda b, hp, i: (b, i, hp)),
                  pl.BlockSpec((1, seq, LANES), lambda b, hp, i: (b, 0, kb + hp)),
                  pl.BlockSpec((2 * LANES, seq), lambda b, hp, i: (hp, b)),
                  pl.BlockSpec((1, LANES, LANES), lambda b, hp, i: (b, 0, hp)),
                  pl.BlockSpec((2, MAIN_TILES, TILE, TILE), lambda b, hp, i: (hp, 0, 0, 0))],
        out_specs=pl.BlockSpec((1, TILE, 2 * LANES), lambda b, hp, i: (b, i, hp)),
        compiler_params=_params(("parallel", "parallel", "arbitrary")),
        name="moba_attn",
    )(proj, proj, v_t, kmean, bias_main)
    return [o.reshape(bsz * seq, N_HEADS * LANES)], _pad_heads_rows(w_out, lambda h: 0).astype(BF16)


def _gelu_tanh(x):
    return 0.5 * x * (1.0 + jnp.tanh(math.sqrt(2.0 / math.pi) * (x + 0.044715 * (x * x * x))))


def _compress_kernel(a_ref, pos_ref, w1_ref, w2_ref, o_ref):
    nc = a_ref.shape[2]
    half = NSA_CMP_STRIDE * HEAD_DIM
    a = a_ref[0, 0].astype(F32)
    top = _dot((a + pos_ref[0:1, :]).astype(BF16), w1_ref[0:half, :])
    bot = _dot((a + pos_ref[1:2, :]).astype(BF16), w1_ref[half:2 * half, :])
    hid = top + pltpu.roll(bot, nc - 1, 0)
    out = _dot(_gelu_tanh(hid).astype(BF16), w2_ref[0])
    rowi = lax.broadcasted_iota(jnp.int32, out.shape, 0)
    o_ref[0, 0] = jnp.where(rowi < nc - 1, out, 0.0).astype(o_ref.dtype)


def _compress(tok, first, pos, w1, w2, bsz, seq, by_parity):
    nc = seq // NSA_CMP_STRIDE
    half = NSA_CMP_STRIDE * HEAD_DIM
    pos2 = pos.reshape(2, half)
    z = jnp.zeros_like(w2)
    low = jnp.concatenate([w2, z], axis=1)
    w2p = jnp.stack([low, jnp.concatenate([z, w2], axis=1) if by_parity else low]).astype(BF16)
    return pl.pallas_call(
        _compress_kernel,
        out_shape=jax.ShapeDtypeStruct((bsz, NSA_KV_HEADS, nc, LANES), BF16),
        grid=(bsz, NSA_KV_HEADS),
        in_specs=[pl.BlockSpec((1, 1, nc, half), lambda b, g: (b, first + g, 0, 0)),
                  pl.BlockSpec((2, half), lambda b, g: (0, 0)),
                  pl.BlockSpec((2 * half, w1.shape[1]), lambda b, g: (0, 0)),
                  pl.BlockSpec((1, w2.shape[0], LANES), lambda b, g: (g % 2, 0, 0))],
        out_specs=pl.BlockSpec((1, 1, nc, LANES), lambda b, g: (b, g, 0, 0)),
        compiler_params=_params(("parallel", "parallel")),
        name="nsa_compress",
    )(tok, pos2, w1.astype(BF16), w2p)


def _gate_columns(gate_ref, branch):
    tile = gate_ref[...]
    lane = lax.broadcasted_iota(jnp.int32, tile.shape, 1)
    base = branch * N_HEADS + pl.program_id(1) * NSA_GROUP
    return [jnp.sum(jnp.where(lane == base + r, tile, 0.0), axis=-1, keepdims=True) for r in range(NSA_GROUP)]


def _nsa_cmp_kernel(q_ref, kc_ref, vc_ref, bias_ref, gate_ref, o_ref, sel_ref, *, n_top):
    i = pl.program_id(2)
    nc = kc_ref.shape[2]
    n_chunks = nc // CMP_CHUNK
    nsb = sel_ref.shape[2]
    q4 = q_ref[0]
    gates = _gate_columns(gate_ref, 0)
    keep = lax.broadcasted_iota(jnp.int32, (TILE, LANES), 1) < HEAD_DIM
    psum = [jnp.zeros((TILE, CMP_CHUNK), F32) for _ in range(n_chunks)]
    for r in range(NSA_GROUP):
        qh = q4[:, r * LANES:(r + 1) * LANES]
        ss = []
        for c in range(n_chunks):
            tile = jnp.clip(i - (CMP_CHUNK * NSA_CMP_STRIDE // TILE) * c, -1, CMP_TILES - 2) + 1
            ss.append(_dot_nt(qh, kc_ref[0, 0, c * CMP_CHUNK:(c + 1) * CMP_CHUNK, :]) + bias_ref[r, tile])
        m = ss[0].max(axis=-1, keepdims=True)
        for c in range(1, n_chunks):
            m = jnp.maximum(m, ss[c].max(axis=-1, keepdims=True))
        ps = [jnp.where(s > 0.5 * NEG_INF, jnp.exp2(s - m), 0.0) for s in ss]
        l = ps[0].sum(axis=-1, keepdims=True)
        for c in range(1, n_chunks):
            l = l + ps[c].sum(axis=-1, keepdims=True)
        inv = 1.0 / jnp.maximum(l, 1e-30)
        acc = jnp.zeros((TILE, LANES), F32)
        for c in range(n_chunks):
            pc = ps[c] * inv
            psum[c] = psum[c] + pc
            acc = acc + _dot(pc.astype(BF16), vc_ref[0, 0, c * CMP_CHUNK:(c + 1) * CMP_CHUNK, :])
        o_ref[0, :, r * LANES:(r + 1) * LANES] = jnp.where(keep, acc * gates[r], 0.0).astype(o_ref.dtype)
    imp = jnp.zeros((nsb, TILE), F32)
    per_sel = NSA_SEL_BLOCK // NSA_CMP_STRIDE
    for c in range(n_chunks):
        jb = lax.broadcasted_iota(jnp.int32, (nsb, CMP_CHUNK), 0)
        ci = lax.broadcasted_iota(jnp.int32, (nsb, CMP_CHUNK), 1) + c * CMP_CHUNK
        rel = ci - per_sel * jb
        over = (rel >= 1 - NSA_CMP_LEN // NSA_CMP_STRIDE) & (rel < per_sel) & (ci < nc - 1)
        ov = jnp.where(over, 1.0, 0.0).astype(BF16)
        hi, lo = _split_bf16(psum[c])
        imp = imp + _dot_nt(ov, hi) + _dot_nt(ov, lo)
    jb = lax.broadcasted_iota(jnp.int32, (nsb, TILE), 0)
    qpos = lax.broadcasted_iota(jnp.int32, (nsb, TILE), 1) + i * TILE
    cur = qpos // NSA_SEL_BLOCK
    forced = (jb == 0) | (jb == cur) | (jb == cur - 1)
    causal = jb * NSA_SEL_BLOCK <= qpos
    score = jnp.where(forced, FORCED_SCORE, jnp.where(causal, imp, NEG_INF))
    chosen = jnp.zeros((nsb, TILE), F32)
    for _ in range(n_top):
        mx = jnp.max(score, axis=0, keepdims=True)
        first = jnp.min(jnp.where(score == mx, jb, nsb), axis=0, keepdims=True)
        pick = jb == first
        chosen = jnp.where(pick, 1.0, chosen)
        score = jnp.where(pick, -jnp.inf, score)
    not_allowed = jnp.where((chosen > 0.0) & causal, 0.0, 1.0)
    sel_ref[0, 0] = not_allowed.astype(sel_ref.dtype)


def _nsa_sparse_kernel(q_ref, k_ref, v_ref, bias_ref, gate_ref, *rest, window):
    if window:
        (o_ref,) = rest
        nsel = None
    else:
        sel_ref, o_ref = rest
        nsel = sel_ref[0, 0]
    i = pl.program_id(2)
    q4 = q_ref[0]
    gates = _gate_columns(gate_ref, 2 if window else 1)
    q_heads = [q4[:, r * LANES:(r + 1) * LANES] for r in range(NSA_GROUP)]
    if window:
        n_tiles = NSA_WINDOW // TILE + 1
        lo = jnp.maximum(i - (n_tiles - 1), 0)
        outs = _flash_chunks(q_heads, k_ref, v_ref, [0] * NSA_GROUP, bias_ref, [None] * NSA_GROUP, lo, i + 1, i,
                             1, n_tiles)
    else:
        outs = _flash_chunks(q_heads, k_ref, v_ref, [0] * NSA_GROUP, bias_ref, [nsel] * NSA_GROUP, 0, i + 1, i,
                             TILE // NSA_SEL_BLOCK, BIASED_TILES + NSA_FAR_WIDTH - 1, far_width=NSA_FAR_WIDTH)
    for r in range(NSA_GROUP):
        o_ref[0, :, r * LANES:(r + 1) * LANES] = (outs[r] * gates[r]).astype(o_ref.dtype)


def _mixer_nsa(x, w_in, w_out, table_bias, pos_k, w1_k, w2_k, pos_v, w1_v, w2_v, bsz, seq):
    bias_main, bias_win, bias_cmp = table_bias
    aw = N_HEADS * HEAD_DIM
    kvw = NSA_KV_HEADS * HEAD_DIM
    G, R = NSA_KV_HEADS, NSA_GROUP
    nc = seq // NSA_CMP_STRIDE
    nsb = seq // NSA_SEL_BLOCK
    n_top = min(NSA_TOP_N, nsb)
    nq = seq // TILE
    assert nc % CMP_CHUNK == 0
    par = lambda h: (h // R) % 2
    wq = _pad_heads_cols(w_in[:, :aw] * (ATTN_SCALE * LOG2_E), par).astype(BF16)
    w_cmp = w_in[:, aw:aw + 2 * kvw].astype(BF16)
    w_cmp = w_cmp.reshape(D_MODEL, 2 * G, HEAD_DIM).transpose(1, 0, 2)
    c0 = aw + 2 * kvw
    wk = jnp.concatenate([w_in[:, c0:c0 + kvw], w_in[:, c0 + 2 * kvw:c0 + 3 * kvw]], axis=1).astype(BF16)
    wv = jnp.concatenate([w_in[:, c0 + kvw:c0 + 2 * kvw], w_in[:, c0 + 3 * kvw:c0 + 4 * kvw]], axis=1)
    wv_t = _pad_heads_cols(wv, lambda h: 0).T.astype(BF16)
    wg = jnp.pad(w_in[:, aw + 6 * kvw:], ((0, 0), (0, LANES - 3 * N_HEADS))).astype(BF16)
    q = _matmul(x, wq, BF16).reshape(bsz, seq, N_HEADS * LANES)
    kk = _matmul(x, wk, BF16).reshape(bsz, seq, 2 * kvw)
    v_t = _matmul_t(x, wv_t)
    gates = _matmul(x, wg, F32, act='sigmoid')
    tok = _matmul_heads(x, w_cmp, bsz, seq).reshape(bsz, 2 * G, nc, NSA_CMP_STRIDE * HEAD_DIM)
    kc = _compress(tok, 0, pos_k, w1_k, w2_k, bsz, seq, True)
    vc = _compress(tok, G, pos_v, w1_v, w2_v, bsz, seq, False)

    q_spec = pl.BlockSpec((1, TILE, R * LANES), lambda b, g, i: (b, i, g))
    o_spec = pl.BlockSpec((1, TILE, R * LANES), lambda b, g, i: (b, i, g))
    gate_spec = pl.BlockSpec((TILE, LANES), lambda b, g, i: (b * nq + i, 0))
    o_shape = jax.ShapeDtypeStruct((bsz, seq, N_HEADS * LANES), BF16)
    sem = ("parallel", "parallel", "arbitrary")

    o_cmp, nsel = pl.pallas_call(
        functools.partial(_nsa_cmp_kernel, n_top=n_top),
        out_shape=[o_shape, jax.ShapeDtypeStruct((bsz, G, nsb, seq), BF16)],
        grid=(bsz, G, nq),
        in_specs=[q_spec,
                  pl.BlockSpec((1, 1, nc, LANES), lambda b, g, i: (b, g, 0, 0)),
                  pl.BlockSpec((1, 1, nc, LANES), lambda b, g, i: (b, g, 0, 0)),
                  pl.BlockSpec((R, CMP_TILES, TILE, CMP_CHUNK), lambda b, g, i: (g, 0, 0, 0)),
                  gate_spec],
        out_specs=[o_spec, pl.BlockSpec((1, 1, nsb, TILE), lambda b, g, i: (b, g, 0, i))],
        compiler_params=_params(sem),
        name="nsa_compressed_attn",
    )(q, kc, vc, bias_cmp, gates)

    def k_spec(first_blk):
        return pl.BlockSpec((1, seq, LANES), lambda b, g, i: (b, 0, first_blk + g // 2))

    def vt_spec(first_blk):
        return pl.BlockSpec((LANES, seq), lambda b, g, i: (first_blk + g, b))

    o_sel = pl.pallas_call(
        functools.partial(_nsa_sparse_kernel, window=False),
        out_shape=o_shape,
        grid=(bsz, G, nq),
        in_specs=[q_spec, k_spec(0), vt_spec(0),
                  pl.BlockSpec((R, MAIN_TILES, TILE, TILE), lambda b, g, i: (g, 0, 0, 0)),
                  gate_spec,
                  pl.BlockSpec((1, 1, nsb, TILE), lambda b, g, i: (b, g, 0, i))],
        out_specs=o_spec,
        compiler_params=_params(sem),
        name="nsa_selected_attn",
    )(q, kk, v_t, bias_main, gates, nsel)

    o_win = pl.pallas_call(
        functools.partial(_nsa_sparse_kernel, window=True),
        out_shape=o_shape,
        grid=(bsz, G, nq),
        in_specs=[q_spec, k_spec(2), vt_spec(G),
                  pl.BlockSpec((R, NSA_WINDOW // TILE + 1, TILE, TILE), lambda b, g, i: (g, 0, 0, 0)),
                  gate_spec],
        out_specs=o_spec,
        compiler_params=_params(sem),
        name="nsa_window_attn",
    )(q, kk, v_t, bias_win, gates)

    m = bsz * seq
    parts = [t.reshape(m, N_HEADS * LANES) for t in (o_cmp, o_sel, o_win)]
    return parts, _pad_heads_rows(w_out, lambda h: 0).astype(BF16)


def _router_kernel(x_ref, w_ref, b_ref, o_ref):
    logits = lax.dot_general(w_ref[...], x_ref[...], (((1,), (1,)), ((), ())),
                             preferred_element_type=F32, precision=lax.Precision.HIGHEST)
    scores = 1.0 / (1.0 + jnp.exp(-logits))
    biased = scores + b_ref[...]
    rows = [biased[e:e + 1, :] for e in range(N_EXPERTS)]
    group_score = []
    for g in range(N_GROUPS):
        r = rows[g * EXPERTS_PER_GROUP:(g + 1) * EXPERTS_PER_GROUP]
        best = None
        for a in range(EXPERTS_PER_GROUP):
            for c in range(a + 1, EXPERTS_PER_GROUP):
                pair = r[a] + r[c]
                best = pair if best is None else jnp.maximum(best, pair)
        group_score.append(best)
    best_val = group_score[0]
    best_grp = jnp.zeros_like(best_val, dtype=jnp.int32)
    for g in range(1, N_GROUPS):
        better = group_score[g] > best_val
        best_val = jnp.where(better, group_score[g], best_val)
        best_grp = jnp.where(better, g, best_grp)
    picked = []
    for e in range(N_EXPERTS):
        g, a = divmod(e, EXPERTS_PER_GROUP)
        rank = jnp.zeros_like(best_grp)
        for c in range(EXPERTS_PER_GROUP):
            if c == a:
                continue
            other = rows[g * EXPERTS_PER_GROUP + c]
            ahead = (other > rows[e]) | ((other == rows[e]) & (c < a))
            rank = rank + jnp.where(ahead, 1, 0)
        picked.append((best_grp == g) & (rank < 2))
    raw = [jnp.where(picked[e], scores[e:e + 1, :], 0.0) for e in range(N_EXPERTS)]
    total = raw[0]
    for e in range(1, N_EXPERTS):
        total = total + raw[e]
    o_ref[...] = (jnp.concatenate(raw, axis=0) / total).T


def _moe_kernel(x_ref, gate_ref, wg_ref, wu_ref, wd_ref, g_ref, b_ref, o_ref, xb_ref, acc_ref):
    e = pl.program_id(1)

    @pl.when(e == 0)
    def _():
        xb_ref[...] = x_ref[...].astype(BF16)
        acc_ref[...] = jnp.zeros_like(acc_ref)

    xb = xb_ref[...]
    lane = lax.broadcasted_iota(jnp.int32, gate_ref.shape, 1)
    gcol = jnp.sum(jnp.where(lane == e, gate_ref[...], 0.0), axis=-1, keepdims=True)
    a = _dot(xb, wg_ref[0])
    h = a / (1.0 + jnp.exp(-a)) * _dot(xb, wu_ref[0])
    acc_ref[...] += gcol * _dot(h.astype(BF16), wd_ref[0])

    @pl.when(e == N_EXPERTS - 1)
    def _():
        o_ref[...] = _layer_norm(DEEPNORM_ALPHA * x_ref[...] + acc_ref[...], g_ref[...], b_ref[...])


def _moe_ln(x, router_w, router_b, w_gate, w_up, w_down, g, b):
    m, d = x.shape
    tm = 1024
    gates = pl.pallas_call(
        _router_kernel,
        out_shape=jax.ShapeDtypeStruct((m, N_EXPERTS), F32),
        grid=(m // tm,),
        in_specs=[pl.BlockSpec((tm, d), lambda i: (i, 0)),
                  pl.BlockSpec((N_EXPERTS, d), lambda i: (0, 0)),
                  pl.BlockSpec((N_EXPERTS, 1), lambda i: (0, 0))],
        out_specs=pl.BlockSpec((tm, N_EXPERTS), lambda i: (i, 0)),
        compiler_params=_params(("parallel",)),
        name="moe_router",
    )(x, router_w.T, router_b.reshape(N_EXPERTS, 1))
    de = w_gate.shape[-1]
    return pl.pallas_call(
        _moe_kernel,
        out_shape=jax.ShapeDtypeStruct((m, d), F32),
        grid=(m // tm, N_EXPERTS),
        in_specs=[pl.BlockSpec((tm, d), lambda i, e: (i, 0)),
                  pl.BlockSpec((tm, N_EXPERTS), lambda i, e: (i, 0)),
                  pl.BlockSpec((1, d, de), lambda i, e: (e, 0, 0)),
                  pl.BlockSpec((1, d, de), lambda i, e: (e, 0, 0)),
                  pl.BlockSpec((1, de, d), lambda i, e: (e, 0, 0)),
                  pl.BlockSpec((1, d), lambda i, e: (0, 0)),
                  pl.BlockSpec((1, d), lambda i, e: (0, 0))],
        out_specs=pl.BlockSpec((tm, d), lambda i, e: (i, 0)),
        scratch_shapes=[pltpu.VMEM((tm, d), BF16), pltpu.VMEM((tm, d), F32)],
        compiler_params=_params(("parallel", "arbitrary")),
        name="moe_experts_ln",
    )(x, gates, w_gate.astype(BF16), w_up.astype(BF16), w_down.astype(BF16), g.reshape(1, d), b.reshape(1, d))


def kernel(x, rel_table, router_w, router_b, ln1_g, ln1_b, ln2_g, ln2_b, exp_w_gate, exp_w_up, exp_w_down, dil_w_in, dil_w_out, sb_w_in, sb_w_out, nsa_w_in, nsa_w_out, nsa_cmp_pos_k, nsa_cmp_w1_k, nsa_cmp_w2_k, nsa_cmp_pos_v, nsa_cmp_w1_v, nsa_cmp_w2_v, moba_w_in, moba_w_out):
    bsz, seq, d = x.shape
    assert d == D_MODEL and seq % (DILATED_PAIRS[-1][1] * DIL_BLOCK) == 0
    depth = ln1_g.shape[0]
    n_mixers = 4
    table = rel_table.astype(F32)
    span = DIL_BLOCK
    big = 1 << 30
    bias_dil = _bias_tiles(table, len(DILATED_PAIRS), DIL_BLOCK, 2 * DIL_BLOCK, DIL_BLOCK, 0, 1, 0, span,
                           [dl for _, dl in DILATED_PAIRS])
    assert BIASED_TILES * TILE - (TILE - 1) >= BUCKET_THRESHOLDS[-1]
    bias_main = _bias_tiles(table, MAIN_TILES, TILE, TILE, 0, TILE, 1, 0, big, [1], shift=True, transposed=True,
                            scale=LOG2_E)
    bias_win = _bias_tiles(table, NSA_WINDOW // TILE + 1, TILE, TILE, 0, TILE, 1, 0, NSA_WINDOW - 1, [1],
                           transposed=True, scale=LOG2_E)
    bias_cmp = _bias_tiles(table, CMP_TILES, TILE, CMP_CHUNK, -TILE - (NSA_CMP_LEN - 1), TILE, NSA_CMP_STRIDE,
                           0, big, [1], scale=LOG2_E)
    xf = x.reshape(bsz * seq, d)
    for layer in range(depth):
        kind, occ = layer % n_mixers, layer // n_mixers
        merge = None
        if kind == 0:
            parts, w_out = _mixer_dilated(xf, dil_w_in[occ], dil_w_out[occ], bias_dil, bsz, seq)
            merge = 'lse3'
        elif kind == 1:
            parts, w_out = _mixer_stick_breaking(xf, sb_w_in[occ], sb_w_out[occ], bsz, seq)
        elif kind == 2:
            parts, w_out = _mixer_nsa(xf, nsa_w_in[occ], nsa_w_out[occ], (bias_main, bias_win, bias_cmp),
                                      nsa_cmp_pos_k[occ], nsa_cmp_w1_k[occ], nsa_cmp_w2_k[occ],
                                      nsa_cmp_pos_v[occ], nsa_cmp_w1_v[occ], nsa_cmp_w2_v[occ], bsz, seq)
            merge = 'sum3'
        else:
            parts, w_out = _mixer_moba(xf, moba_w_in[occ], moba_w_out[occ], bias_main, bsz, seq)
        xf = _outproj_ln(parts, w_out, xf, ln1_g[layer], ln1_b[layer], merge=merge)
        xf = _moe_ln(xf, router_w, router_b, exp_w_gate[layer], exp_w_up[layer], exp_w_down[layer],
                     ln2_g[layer], ln2_b[layer])
    return xf.reshape(bsz, seq, d)
```

```python
import functools
import math

import numpy as np
import jax
import jax.numpy as jnp
from jax import lax
from jax.experimental import pallas as pl
from jax.experimental.pallas import tpu as pltpu

F32 = jnp.float32
BF16 = jnp.bfloat16

D_MODEL = 1024
HEAD_DIM = 64
N_HEADS = 16
LANES = 128
PAIRS = N_HEADS // 2
ATTN_SCALE = HEAD_DIM ** -0.5
REL_BUCKETS = 32
REL_MAX_DIST = 2048
DILATED_PAIRS = ((128, 1), (512, 4), (2048, 16))
DIL_BLOCK = 128
SB_CLIP = 60.0
SB_TILE = 256
LOG2_E = math.log2(math.e)
NSA_KV_HEADS = 4
NSA_GROUP = N_HEADS // NSA_KV_HEADS
NSA_CMP_LEN = 32
NSA_CMP_STRIDE = 16
NSA_SEL_BLOCK = 64
NSA_TOP_N = 16
NSA_WINDOW = 512
MOBA_BLOCK = 256
MOBA_TOP_K = 3
N_EXPERTS = 16
N_GROUPS = 4
EXPERTS_PER_GROUP = N_EXPERTS // N_GROUPS
DEPTH = 4
DEEPNORM_ALPHA = (2 * DEPTH) ** 0.25
LN_EPS = 1e-5
NEG_INF = -1e30
FORCED_SCORE = 1e9
TILE = 256
BIASED_TILES = 7
MOBA_FAR_WIDTH = 2
NSA_FAR_WIDTH = 4
MAIN_TILES = BIASED_TILES + max(MOBA_FAR_WIDTH, NSA_FAR_WIDTH) - 1
CMP_CHUNK = 128
CMP_TILES = 19
VMEM_LIMIT = 56 * 1024 * 1024


def _bucket_thresholds():
    n = np.arange(0, 2 * REL_MAX_DIST)
    exact = REL_BUCKETS // 2
    logf = np.log(np.maximum(n, 1).astype(np.float64) / exact) / math.log(REL_MAX_DIST / exact)
    large = np.minimum(exact + (logf * (REL_BUCKETS - exact)).astype(np.int64), REL_BUCKETS - 1)
    bucket = np.where(n < exact, n, large)
    return tuple(int(np.argmax(bucket >= k)) for k in range(1, REL_BUCKETS))


BUCKET_THRESHOLDS = _bucket_thresholds()


def _params(sem, vmem=VMEM_LIMIT):
    return pltpu.CompilerParams(dimension_semantics=sem, vmem_limit_bytes=vmem)


def _dot(a, b):
    return jnp.dot(a, b, preferred_element_type=F32)


def _dot_nt(a, b):
    return lax.dot_general(a, b, (((1,), (1,)), ((), ())), preferred_element_type=F32)


def _split_bf16(x):
    hi = x.astype(BF16)
    lo = (x - hi.astype(F32)).astype(BF16)
    return hi, lo


def _matmul_kernel(x_ref, w_ref, o_ref, *, act):
    y = _dot(x_ref[...].astype(BF16), w_ref[...])
    if act == 'sigmoid':
        y = 1.0 / (1.0 + jnp.exp(-y))
    o_ref[...] = y.astype(o_ref.dtype)


def _matmul(x, w, out_dtype, act=None):
    m, k = x.shape
    n = w.shape[1]
    tm = 1024
    tn = next(t for t in (1024, 512, 128) if n % t == 0)
    return pl.pallas_call(
        functools.partial(_matmul_kernel, act=act),
        out_shape=jax.ShapeDtypeStruct((m, n), out_dtype),
        grid=(m // tm, n // tn),
        in_specs=[pl.BlockSpec((tm, k), lambda i, j: (i, 0)),
                  pl.BlockSpec((k, tn), lambda i, j: (0, j))],
        out_specs=pl.BlockSpec((tm, tn), lambda i, j: (i, j)),
        compiler_params=_params(("parallel", "arbitrary")),
        name="proj_matmul",
    )(x, w)


def _matmul_t_kernel(x_ref, w_ref, o_ref):
    y = _dot_nt(w_ref[...], x_ref[...].astype(BF16))
    row = lax.broadcasted_iota(jnp.int32, y.shape, 0)
    o_ref[...] = jnp.where(row % LANES == HEAD_DIM, 1.0, y).astype(o_ref.dtype)


def _matmul_t(x, wt):
    m, k = x.shape
    n = wt.shape[0]
    tm = 512
    tn = 512 if n % 512 == 0 else 256
    return pl.pallas_call(
        _matmul_t_kernel,
        out_shape=jax.ShapeDtypeStruct((n, m), BF16),
        grid=(m // tm, n // tn),
        in_specs=[pl.BlockSpec((tm, k), lambda i, j: (i, 0)),
                  pl.BlockSpec((tn, k), lambda i, j: (j, 0))],
        out_specs=pl.BlockSpec((tn, tm), lambda i, j: (j, i)),
        compiler_params=_params(("parallel", "arbitrary")),
        name="proj_matmul_t",
    )(x, wt)


def _matmul_heads_kernel(x_ref, w_ref, o_ref):
    y = _dot(x_ref[...].astype(BF16), w_ref[...]).astype(o_ref.dtype)
    for j in range(o_ref.shape[1]):
        o_ref[0, j] = y[:, j * HEAD_DIM:(j + 1) * HEAD_DIM]


def _matmul_heads(x, w, bsz, seq):
    m, k = x.shape
    n = w.shape[1] // HEAD_DIM
    tm = 512
    per_b = seq // tm
    return pl.pallas_call(
        _matmul_heads_kernel,
        out_shape=jax.ShapeDtypeStruct((bsz, n, seq, HEAD_DIM), BF16),
        grid=(m // tm,),
        in_specs=[pl.BlockSpec((tm, k), lambda i: (i, 0)),
                  pl.BlockSpec((k, n * HEAD_DIM), lambda i: (0, 0))],
        out_specs=pl.BlockSpec((1, n, tm, HEAD_DIM), lambda i: (i // per_b, 0, i % per_b, 0)),
        compiler_params=_params(("parallel",)),
        name="proj_heads",
    )(x, w)


def _layer_norm(z, g, b):
    mu = jnp.mean(z, axis=-1, keepdims=True)
    zc = z - mu
    var = jnp.mean(zc * zc, axis=-1, keepdims=True)
    return zc * lax.rsqrt(var + LN_EPS) * g + b


def _outproj_kernel(*refs, n_in, merge):
    ins = refs[:n_in]
    w_ref, x_ref, g_ref, b_ref, o_ref = refs[n_in:]
    if merge == 'lse3':
        o1, o2, o3, l1, l2, l3 = [r[...] for r in ins]
        mx = jnp.maximum(jnp.maximum(l1, l2), l3)
        e1, e2, e3 = jnp.exp(l1 - mx), jnp.exp(l2 - mx), jnp.exp(l3 - mx)
        a = (e1 * o1 + e2 * o2 + e3 * o3) / (e1 + e2 + e3)
    elif merge == 'sum3':
        a = ins[0][...].astype(F32) + ins[1][...].astype(F32) + ins[2][...].astype(F32)
    else:
        a = ins[0][...]
    y = _dot(a.astype(BF16), w_ref[...])
    o_ref[...] = _layer_norm(DEEPNORM_ALPHA * x_ref[...] + y, g_ref[...], b_ref[...])


def _outproj_ln(ins, w, x, g, b, merge=None):
    m, d = x.shape
    ka = ins[0].shape[1]
    tm = 256
    n_in = len(ins)
    return pl.pallas_call(
        functools.partial(_outproj_kernel, n_in=n_in, merge=merge),
        out_shape=jax.ShapeDtypeStruct((m, d), F32),
        grid=(m // tm,),
        in_specs=[pl.BlockSpec((tm, ka), lambda i: (i, 0)) for _ in ins]
        + [pl.BlockSpec((ka, d), lambda i: (0, 0)),
           pl.BlockSpec((tm, d), lambda i: (i, 0)),
           pl.BlockSpec((1, d), lambda i: (0, 0)),
           pl.BlockSpec((1, d), lambda i: (0, 0))],
        out_specs=pl.BlockSpec((tm, d), lambda i: (i, 0)),
        compiler_params=_params(("parallel",)),
        name="outproj_ln",
    )(*ins, w, x, g.reshape(1, d), b.reshape(1, d))


def _bias_kernel(tbl_ref, o_ref, *, rows, cols, off0, tstride, cmul, lo, hi, mults, shift, transposed, scale):
    t = pl.program_id(0)
    if len(set(mults)) == 1:
        mult = mults[0]
    else:
        mult = jnp.int32(mults[-1])
        for idx in range(len(mults) - 2, -1, -1):
            mult = jnp.where(t == idx, jnp.int32(mults[idx]), mult)
    base = off0 + tstride * t

    def strip(i, carry):
        r0 = pl.multiple_of(i * 8, 8)
        a = lax.broadcasted_iota(jnp.int32, (8, cols), 0) + r0
        c = lax.broadcasted_iota(jnp.int32, (8, cols), 1)
        steps = base + c - cmul * a if transposed else base + a - cmul * c
        valid = (steps >= lo) & (steps <= hi)
        dist = steps * mult
        for h in range(N_HEADS):
            acc = jnp.full((8, cols), tbl_ref[0, h], F32)
            for k, thr in enumerate(BUCKET_THRESHOLDS):
                acc = jnp.where(dist >= thr, tbl_ref[k + 1, h], acc)
            if shift:
                acc = acc - tbl_ref[REL_BUCKETS - 1, h]
            if scale != 1.0:
                acc = acc * scale
            o_ref[h, 0, pl.ds(r0, 8), :] = jnp.where(valid, acc, NEG_INF)
        return carry

    lax.fori_loop(0, rows // 8, strip, 0)


def _bias_tiles(table, n_tiles, rows, cols, off0, tstride, cmul, lo, hi, mults, shift=False, transposed=False,
                scale=1.0):
    return pl.pallas_call(
        functools.partial(_bias_kernel, rows=rows, cols=cols, off0=off0, tstride=tstride, cmul=cmul,
                          lo=lo, hi=hi, mults=tuple(mults), shift=shift, transposed=transposed, scale=scale),
        out_shape=jax.ShapeDtypeStruct((N_HEADS, n_tiles, rows, cols), F32),
        grid=(n_tiles,),
        in_specs=[pl.BlockSpec(memory_space=pltpu.SMEM)],
        out_specs=pl.BlockSpec((N_HEADS, 1, rows, cols), lambda t: (0, t, 0, 0)),
        compiler_params=_params(("arbitrary",)),
        name="bias_tiles",
    )(table)


def _dil_kernel(q_ref, kp_ref, kc_ref, vp_ref, vc_ref, bias_ref, o_ref, lse_ref,
                q_scr, kp_scr, kc_scr, vp_scr, vc_scr, *, dilation, per_step):
    j = pl.program_id(2)
    d = dilation
    n_cur = per_step * DIL_BLOCK

    def rows_of(ref, r, n, lead=()):
        if d == 1:
            return ref[lead + (slice(0, n), slice(None))]
        return ref[lead + (pl.ds(r, n, stride=d), slice(None))]

    if d > 1:
        for h in range(2):
            q_scr[h] = q_ref[0, :, h * LANES:(h + 1) * LANES].astype(F32)
        kp_scr[...] = kp_ref[0].astype(F32)
        kc_scr[...] = kc_ref[0].astype(F32)
        vp_scr[...] = vp_ref[0].astype(F32)
        vc_scr[...] = vc_ref[0].astype(F32)
    colk = lax.broadcasted_iota(jnp.int32, (DIL_BLOCK, 2 * DIL_BLOCK), 1)
    first = jnp.where((colk < DIL_BLOCK) & (j == 0), NEG_INF, 0.0)
    low = lax.broadcasted_iota(jnp.int32, (DIL_BLOCK, LANES), 1) < HEAD_DIM
    for r in range(d):
        if d == 1:
            q_heads = [q_ref[0, :, h * LANES:(h + 1) * LANES] for h in range(2)]
            k_all = jnp.concatenate([kp_ref[0], kc_ref[0]], axis=0)
            v_all = jnp.concatenate([vp_ref[0], vc_ref[0]], axis=0)
        else:
            q_heads = [rows_of(q_scr, r, n_cur, (h,)).astype(BF16) for h in range(2)]
            k_all = jnp.concatenate([rows_of(kp_scr, r, DIL_BLOCK), rows_of(kc_scr, r, n_cur)], axis=0).astype(BF16)
            v_all = jnp.concatenate([rows_of(vp_scr, r, DIL_BLOCK), rows_of(vc_scr, r, n_cur)], axis=0).astype(BF16)
        for u in range(per_step):
            k2 = k_all[u * DIL_BLOCK:(u + 2) * DIL_BLOCK]
            v2 = v_all[u * DIL_BLOCK:(u + 2) * DIL_BLOCK]
            outs, lses = [], []
            for h in range(2):
                s = _dot_nt(q_heads[h][u * DIL_BLOCK:(u + 1) * DIL_BLOCK], k2) + bias_ref[h, 0]
                if u == 0:
                    s = s + first
                m = jnp.max(s, axis=-1, keepdims=True)
                p = jnp.exp(s - m)
                l = jnp.maximum(jnp.sum(p, axis=-1, keepdims=True), 1e-30)
                outs.append(_dot(p.astype(BF16), v2) / l)
                lses.append(m + jnp.log(l))
            if d == 1:
                dst = (0, slice(u * DIL_BLOCK, (u + 1) * DIL_BLOCK), slice(None))
            else:
                dst = (0, pl.ds(u * DIL_BLOCK * d + r, DIL_BLOCK, stride=d), slice(None))
            o_ref[dst] = jnp.where(low, outs[0], outs[1])
            lse_ref[dst] = jnp.where(low, lses[0], lses[1])


def _dilated_group(proj, bias, g, dilation, bsz, seq):
    prev_rows = DIL_BLOCK * dilation
    step_rows = math.gcd(seq, 16 * DIL_BLOCK)
    per_step = step_rows // prev_rows
    assert per_step >= 1 and seq % step_rows == 0
    q_blk = g * N_HEADS // 2
    k_blk = 3 * N_HEADS + g * 2 * PAIRS
    v_blk = k_blk + PAIRS

    def kv_specs(base):
        prev = pl.BlockSpec((1, prev_rows, LANES), lambda hp, b, j: (b, jnp.maximum(per_step * j - 1, 0), base + hp))
        cur = pl.BlockSpec((1, step_rows, LANES), lambda hp, b, j: (b, j, base + hp))
        return [prev, cur]

    out_spec = pl.BlockSpec((1, step_rows, LANES), lambda hp, b, j: (b, j, hp))
    staged = 8 if dilation == 1 else None
    o, lse = pl.pallas_call(
        functools.partial(_dil_kernel, dilation=dilation, per_step=per_step),
        out_shape=[jax.ShapeDtypeStruct((bsz, seq, D_MODEL), F32)] * 2,
        grid=(PAIRS, bsz, seq // step_rows),
        in_specs=[pl.BlockSpec((1, step_rows, 2 * LANES), lambda hp, b, j: (b, j, q_blk + hp))]
        + kv_specs(k_blk) + kv_specs(v_blk)
        + [pl.BlockSpec((2, 1, DIL_BLOCK, 2 * DIL_BLOCK), lambda hp, b, j: (hp, g, 0, 0))],
        out_specs=[out_spec, out_spec],
        scratch_shapes=[pltpu.VMEM((2, staged or step_rows, LANES), F32),
                        pltpu.VMEM((staged or prev_rows, LANES), F32), pltpu.VMEM((staged or step_rows, LANES), F32),
                        pltpu.VMEM((staged or prev_rows, LANES), F32), pltpu.VMEM((staged or step_rows, LANES), F32)],
        compiler_params=_params(("parallel", "parallel", "arbitrary")),
        name="dilated_attn",
    )(proj, proj, proj, proj, proj, bias)
    return o.reshape(bsz * seq, D_MODEL), lse.reshape(bsz * seq, D_MODEL)


def _pad_heads_cols(w, parity_of_head):
    k, n = w.shape
    nh = n // HEAD_DIM
    wh = w.reshape(k, nh, HEAD_DIM)
    z = jnp.zeros_like(wh)
    par = jnp.asarray([parity_of_head(h) for h in range(nh)], jnp.int32)[None, :, None]
    lo = jnp.where(par == 0, wh, z)
    hi = jnp.where(par == 1, wh, z)
    return jnp.concatenate([lo, hi], axis=-1).reshape(k, nh * LANES)


def _pad_heads_rows(w, parity_of_head):
    return _pad_heads_cols(w.T, parity_of_head).T


def _mixer_dilated(x, w_in, w_out, bias_dil, bsz, seq):
    aw = N_HEADS * HEAD_DIM
    par = lambda h: h % 2
    qs, kvs = [], []
    for g in range(len(DILATED_PAIRS)):
        base = g * 3 * aw
        qs.append(_pad_heads_cols(w_in[:, base:base + aw] * ATTN_SCALE, par))
        kvs.append(w_in[:, base + aw:base + 3 * aw])
    w_all = jnp.concatenate(qs + kvs, axis=1).astype(BF16)
    proj = _matmul(x, w_all, BF16).reshape(bsz, seq, -1)
    parts = [_dilated_group(proj, bias_dil, g, d, bsz, seq) for g, (_, d) in enumerate(DILATED_PAIRS)]
    return [p[0] for p in parts] + [p[1] for p in parts], w_out.astype(BF16)


def _sb_kernel(q_ref, k_ref, v_ref, o_ref):
    i = pl.program_id(2)
    t = SB_TILE
    row = lax.broadcasted_iota(jnp.int32, (t, t), 0)
    col = lax.broadcasted_iota(jnp.int32, (t, t), 1)
    suffix = jnp.where(row >= col, 1.0, 0.0).astype(BF16)
    strict = col < row
    q2 = q_ref[0]
    q_heads = [q2[:, h * LANES:(h + 1) * LANES] for h in range(2)]

    def block(qh, kj, masked):
        start = pl.multiple_of(kj * t, t)
        kb = k_ref[0, pl.ds(start, t), :]
        vb = v_ref[0, pl.ds(start, t), :]
        a = jnp.clip(_dot_nt(qh, kb), -SB_CLIP, SB_CLIP) * LOG2_E
        sp = jnp.log2(1.0 + jnp.exp2(a))
        if masked:
            sp = jnp.where(strict, sp, 0.0)
        hi, lo = _split_bf16(sp)
        rr = _dot(hi, suffix) + _dot(lo, suffix)
        att = jnp.exp2(a - rr)
        if masked:
            att = jnp.where(strict, att, 0.0)
        return _dot(att.astype(BF16), vb), rr[:, 0:1]

    def first_two(kj, carry):
        has_prev = jnp.where(kj > 0, 1.0, 0.0)
        out = []
        for qh in q_heads:
            acc_d, r_d = block(qh, kj, True)
            pv, r_p = block(qh, jnp.maximum(kj - 1, 0), False)
            out.append((acc_d + (has_prev * jnp.exp2(-r_d)) * pv, r_d + has_prev * r_p))
        return tuple(out)

    zero = (jnp.zeros((t, LANES), F32), jnp.zeros((t, 1), F32))
    (acc0, r0), (acc1, r1) = lax.fori_loop(i, i + 1, first_two, (zero, zero))

    def cond(carry):
        n, alive = carry[0], carry[1]
        return (n < i - 1) & alive

    def step(carry):
        n = carry[0]
        new, tops = [], []
        for qh, (r_sum, w, acc) in zip(q_heads, carry[2:]):
            pv, r_blk = block(qh, i - 2 - n, False)
            r_new = r_sum + r_blk
            w_new = jnp.exp2(-r_new)
            new.append((r_new, w_new, acc + w * pv))
            tops.append(jnp.max(w_new, axis=0, keepdims=True))
        alive = jnp.maximum(tops[0], tops[1])[0, 0] > 0.0
        return (n + 1, alive) + tuple(new)

    w0, w1 = jnp.exp2(-r0), jnp.exp2(-r1)
    alive0 = jnp.maximum(jnp.max(w0, axis=0, keepdims=True), jnp.max(w1, axis=0, keepdims=True))[0, 0] > 0.0
    init = (jnp.int32(0), alive0, (r0, w0, acc0), (r1, w1, acc1))
    res = lax.while_loop(cond, step, init)
    lane = lax.broadcasted_iota(jnp.int32, (t, LANES), 1)
    o_ref[0] = jnp.where(lane < HEAD_DIM, res[2][2], res[3][2]).astype(o_ref.dtype)


def _mixer_stick_breaking(x, w_in, w_out, bsz, seq):
    aw = N_HEADS * HEAD_DIM
    wq = _pad_heads_cols(w_in[:, :aw] * ATTN_SCALE, lambda h: h % 2)
    w_all = jnp.concatenate([wq, w_in[:, aw:]], axis=1).astype(BF16)
    proj = _matmul(x, w_all, BF16).reshape(bsz, seq, -1)
    kb = N_HEADS
    o = pl.pallas_call(
        _sb_kernel,
        out_shape=jax.ShapeDtypeStruct((bsz, seq, D_MODEL), BF16),
        grid=(bsz, PAIRS, seq // SB_TILE),
        in_specs=[pl.BlockSpec((1, SB_TILE, 2 * LANES), lambda b, hp, i: (b, i, hp)),
                  pl.BlockSpec((1, seq, LANES), lambda b, hp, i: (b, 0, kb + hp)),
                  pl.BlockSpec((1, seq, LANES), lambda b, hp, i: (b, 0, kb + PAIRS + hp))],
        out_specs=pl.BlockSpec((1, SB_TILE, LANES), lambda b, hp, i: (b, i, hp)),
        compiler_params=_params(("parallel", "parallel", "arbitrary")),
        name="stick_breaking_attn",
    )(proj, proj, proj)
    return [o.reshape(bsz * seq, D_MODEL)], w_out.astype(BF16)


def _flash_chunks(q_heads, k_ref, v_ref, v_rows, bias_ref, nsel_of_head, lo, hi, i, blocks_per_chunk, n_tiles,
                  fold_mask=False, pipelined=False, far_width=1, near_width=1):
    nh = len(q_heads)
    distinct = []
    if not fold_mask:
        for ns in nsel_of_head:
            if ns is not None and all(ns is not d for d in distinct):
                distinct.append(ns)

    def logits(ch, near, width):
        rows = width * TILE
        kb = k_ref[0, pl.ds(pl.multiple_of(ch * TILE, TILE), rows), :]
        adds = []
        if fold_mask:
            key = lax.broadcasted_iota(jnp.int32, (rows, LANES), 0)
            blk = lax.broadcasted_iota(jnp.int32, (rows, LANES), 1)
            blk_of_key = ch * blocks_per_chunk + key // (TILE // blocks_per_chunk)
            kb = jnp.concatenate([kb, jnp.where(blk == blk_of_key, NEG_INF, 0.0).astype(BF16)], axis=1)
        else:
            for ns in distinct:
                nblk = ns.shape[0]
                kk = lax.broadcasted_iota(jnp.int32, (rows, nblk), 0)
                bj = lax.broadcasted_iota(jnp.int32, (rows, nblk), 1)
                blk_of_key = ch * blocks_per_chunk + kk // (TILE // blocks_per_chunk)
                expand = jnp.where(bj == blk_of_key, NEG_INF, 0.0).astype(BF16)
                adds.append(_dot(expand, ns))
        out = []
        for h in range(nh):
            s = _dot_nt(kb, q_heads[h])
            if near:
                tiles = [bias_ref[h, jnp.maximum(i - ch - w, 0)] for w in range(width)]
                s = s + (tiles[0] if width == 1 else jnp.concatenate(tiles, axis=0))
            ns = nsel_of_head[h]
            if ns is not None and not fold_mask:
                s = s + adds[[d is ns for d in distinct].index(True)]
            out.append(s)
        return tuple(out)

    def update(ch, s_heads, state, width):
        start = pl.multiple_of(ch * TILE, TILE)
        vts = {}
        new = []
        for h in range(nh):
            r0 = v_rows[h]
            if r0 not in vts:
                vts[r0] = v_ref[r0:r0 + LANES, pl.ds(start, width * TILE)]
            m, acc = state[h]
            s = s_heads[h]
            m_new = jnp.maximum(m, jnp.max(s, axis=0, keepdims=True))
            alpha = jnp.exp2(m - m_new)
            p = jnp.exp2(s - m_new)
            acc = alpha * acc + _dot(vts[r0], p.astype(BF16))
            new.append((m_new, acc))
        return tuple(new)

    def run(ch0, steps, near, width, state, pipelined=pipelined):
        if not pipelined:
            return lax.fori_loop(
                0, steps, lambda t, st: update(ch0 + t * width, logits(ch0 + t * width, near, width), st, width), state)

        def body(t, carry):
            s_cur, st = carry
            s_next = logits(ch0 + jnp.minimum(t + 1, steps - 1) * width, near, width)
            return s_next, update(ch0 + t * width, s_cur, st, width)

        zeros = tuple(jnp.zeros((width * TILE, TILE), F32) for _ in range(nh))
        first = jnp.minimum(ch0, i)
        s0 = lax.fori_loop(first, first + 1, lambda c, _: logits(c, near, width), zeros)
        return lax.fori_loop(0, steps, body, (s0, state))[1]

    state = tuple((jnp.full((1, TILE), NEG_INF, F32), jnp.zeros((LANES, TILE), F32)) for _ in range(nh))
    n_near = n_tiles - (far_width - 1)
    far_steps = jnp.maximum(hi - n_near - lo, 0) // far_width
    split = lo + far_steps * far_width
    state = run(lo, far_steps, False, far_width, state)
    near_pairs = (hi - split) // near_width if near_width > 1 else 0
    if near_width > 1:
        state = run(split, near_pairs, True, near_width, state)
    single = split + near_pairs * near_width
    state = run(single, hi - single, True, 1, state, pipelined=pipelined and near_width == 1)
    outs = []
    for _, acc in state:
        o = (acc / acc[HEAD_DIM:HEAD_DIM + 1, :]).T
        lane = lax.broadcasted_iota(jnp.int32, o.shape, 1)
        outs.append(jnp.where(lane < HEAD_DIM, o, 0.0))
    return outs


def _kmean_kernel(k_ref, o_ref):
    rows = k_ref.shape[1]
    kf = k_ref[0].astype(F32).reshape(rows // MOBA_BLOCK, MOBA_BLOCK, k_ref.shape[2])
    o_ref[0] = jnp.sum(kf, axis=1) * (1.0 / MOBA_BLOCK)


def _moba_kernel(q_ref, k_ref, v_ref, km_ref, bias_ref, o_ref):
    i = pl.program_id(2)
    nblk = km_ref.shape[1]
    km_hi, km_lo = _split_bf16(km_ref[0])
    q2 = q_ref[0]
    blk = lax.broadcasted_iota(jnp.int32, (nblk, TILE), 0)
    past = blk < i
    q_heads = []
    for h in range(2):
        qh = q2[:, h * LANES:(h + 1) * LANES]
        gate = jnp.where(past, _dot_nt(km_hi, qh) + _dot_nt(km_lo, qh), NEG_INF)
        allowed = blk == i
        for _ in range(MOBA_TOP_K):
            mx = jnp.max(gate, axis=0, keepdims=True)
            first = jnp.min(jnp.where(gate == mx, blk, nblk), axis=0, keepdims=True)
            pick = blk == first
            allowed = allowed | (pick & past)
            gate = jnp.where(pick, -jnp.inf, gate)
        nsel = jnp.where(allowed, 0.0, 1.0).T.astype(BF16)
        q_heads.append(jnp.concatenate([qh, nsel], axis=1))
    outs = _flash_chunks(q_heads, k_ref, v_ref, [0, LANES], bias_ref, [None, None], 0, i + 1, i, 1,
                         BIASED_TILES + MOBA_FAR_WIDTH - 1, fold_mask=True, pipelined=True, far_width=MOBA_FAR_WIDTH,
                         near_width=2)
    for h in range(2):
        o_ref[0, :, h * LANES:(h + 1) * LANES] = outs[h].astype(o_ref.dtype)


def _mixer_moba(x, w_in, w_out, bias_main, bsz, seq):
    aw = N_HEADS * HEAD_DIM
    nblk = seq // MOBA_BLOCK
    assert nblk - 1 >= MOBA_TOP_K
    wq = _pad_heads_cols(w_in[:, :aw] * (ATTN_SCALE * LOG2_E), lambda h: h % 2)
    w_all = jnp.concatenate([wq, w_in[:, aw:2 * aw]], axis=1).astype(BF16)
    proj = _matmul(x, w_all, BF16).reshape(bsz, seq, -1)
    v_t = _matmul_t(x, _pad_heads_cols(w_in[:, 2 * aw:], lambda h: 0).T.astype(BF16))
    kb = N_HEADS
    rows = 8 * MOBA_BLOCK
    kmean = pl.pallas_call(
        _kmean_kernel,
        out_shape=jax.ShapeDtypeStruct((bsz, nblk, D_MODEL), F32),
        grid=(bsz, seq // rows),
        in_specs=[pl.BlockSpec((1, rows, D_MODEL), lambda b, i: (b, i, kb * LANES // D_MODEL))],
        out_specs=pl.BlockSpec((1, 8, D_MODEL), lambda b, i: (b, i, 0)),
        compiler_params=_params(("parallel", "parallel")),
        name="moba_kmean",
    )(proj)
    assert nblk <= LANES
    kmean = jnp.pad(kmean, ((0, 0), (0, LANES - nblk), (0, 0)))
    o = pl.pallas_call(
        _moba_kernel,
        out_shape=jax.ShapeDtypeStruct((bsz, seq, N_HEADS * LANES), BF16),
        grid=(bsz, PAIRS, seq // TILE),
        in_specs=[pl.BlockSpec((1, TILE, 2 * LANES), lambda b, hp, i: (b, i, hp)),
                  pl.BlockSpec((1, seq, LANES), lambda b, hp, i: (b, 0, kb + hp)),
                  pl.BlockSpec((2 * LANES, seq), lambda b, hp, i: (hp, b)),
                  pl.BlockSpec((1, LANES, LANES), lambda b, hp, i: (b, 0, hp)),
                  pl.BlockSpec((2, MAIN_TILES, TILE, TILE), lambda b, hp, i: (hp, 0, 0, 0))],
        out_specs=pl.BlockSpec((1, TILE, 2 * LANES), lambda b, hp, i: (b, i, hp)),
        compiler_params=_params(("parallel", "parallel", "arbitrary")),
        name="moba_attn",
    )(proj, proj, v_t, kmean, bias_main)
    return [o.reshape(bsz * seq, N_HEADS * LANES)], _pad_heads_rows(w_out, lambda h: 0).astype(BF16)


def _gelu_tanh(x):
    return 0.5 * x * (1.0 + jnp.tanh(math.sqrt(2.0 / math.pi) * (x + 0.044715 * (x * x * x))))


def _compress_kernel(a_ref, pos_ref, w1_ref, w2_ref, o_ref):
    nc = a_ref.shape[2]
    half = NSA_CMP_STRIDE * HEAD_DIM
    a = a_ref[0, 0].astype(F32)
    top = _dot((a + pos_ref[0:1, :]).astype(BF16), w1_ref[0:half, :])
    bot = _dot((a + pos_ref[1:2, :]).astype(BF16), w1_ref[half:2 * half, :])
    hid = top + pltpu.roll(bot, nc - 1, 0)
    out = _dot(_gelu_tanh(hid).astype(BF16), w2_ref[0])
    rowi = lax.broadcasted_iota(jnp.int32, out.shape, 0)
    o_ref[0, 0] = jnp.where(rowi < nc - 1, out, 0.0).astype(o_ref.dtype)


def _compress(tok, first, pos, w1, w2, bsz, seq, by_parity):
    nc = seq // NSA_CMP_STRIDE
    half = NSA_CMP_STRIDE * HEAD_DIM
    pos2 = pos.reshape(2, half)
    z = jnp.zeros_like(w2)
    low = jnp.concatenate([w2, z], axis=1)
    w2p = jnp.stack([low, jnp.concatenate([z, w2], axis=1) if by_parity else low]).astype(BF16)
    return pl.pallas_call(
        _compress_kernel,
        out_shape=jax.ShapeDtypeStruct((bsz, NSA_KV_HEADS, nc, LANES), BF16),
        grid=(bsz, NSA_KV_HEADS),
        in_specs=[pl.BlockSpec((1, 1, nc, half), lambda b, g: (b, first + g, 0, 0)),
                  pl.BlockSpec((2, half), lambda b, g: (0, 0)),
                  pl.BlockSpec((2 * half, w1.shape[1]), lambda b, g: (0, 0)),
                  pl.BlockSpec((1, w2.shape[0], LANES), lambda b, g: (g % 2, 0, 0))],
        out_specs=pl.BlockSpec((1, 1, nc, LANES), lambda b, g: (b, g, 0, 0)),
        compiler_params=_params(("parallel", "parallel")),
        name="nsa_compress",
    )(tok, pos2, w1.astype(BF16), w2p)


def _gate_columns(gate_ref, branch):
    tile = gate_ref[...]
    lane = lax.broadcasted_iota(jnp.int32, tile.shape, 1)
    base = branch * N_HEADS + pl.program_id(1) * NSA_GROUP
    return [jnp.sum(jnp.where(lane == base + r, tile, 0.0), axis=-1, keepdims=True) for r in range(NSA_GROUP)]


def _nsa_cmp_kernel(q_ref, kc_ref, vc_ref, bias_ref, gate_ref, o_ref, sel_ref, *, n_top):
    i = pl.program_id(2)
    nc = kc_ref.shape[2]
    n_chunks = nc // CMP_CHUNK
    nsb = sel_ref.shape[2]
    q4 = q_ref[0]
    gates = _gate_columns(gate_ref, 0)
    keep = lax.broadcasted_iota(jnp.int32, (TILE, LANES), 1) < HEAD_DIM
    psum = [jnp.zeros((TILE, CMP_CHUNK), F32) for _ in range(n_chunks)]
    for r in range(NSA_GROUP):
        qh = q4[:, r * LANES:(r + 1) * LANES]
        ss = []
        for c in range(n_chunks):
            tile = jnp.clip(i - (CMP_CHUNK * NSA_CMP_STRIDE // TILE) * c, -1, CMP_TILES - 2) + 1
            ss.append(_dot_nt(qh, kc_ref[0, 0, c * CMP_CHUNK:(c + 1) * CMP_CHUNK, :]) + bias_ref[r, tile])
        m = ss[0].max(axis=-1, keepdims=True)
        for c in range(1, n_chunks):
            m = jnp.maximum(m, ss[c].max(axis=-1, keepdims=True))
        ps = [jnp.where(s > 0.5 * NEG_INF, jnp.exp2(s - m), 0.0) for s in ss]
        l = ps[0].sum(axis=-1, keepdims=True)
        for c in range(1, n_chunks):
            l = l + ps[c].sum(axis=-1, keepdims=True)
        inv = 1.0 / jnp.maximum(l, 1e-30)
        acc = jnp.zeros((TILE, LANES), F32)
        for c in range(n_chunks):
            pc = ps[c] * inv
            psum[c] = psum[c] + pc
            acc = acc + _dot(pc.astype(BF16), vc_ref[0, 0, c * CMP_CHUNK:(c + 1) * CMP_CHUNK, :])
        o_ref[0, :, r * LANES:(r + 1) * LANES] = jnp.where(keep, acc * gates[r], 0.0).astype(o_ref.dtype)
    imp = jnp.zeros((nsb, TILE), F32)
    per_sel = NSA_SEL_BLOCK // NSA_CMP_STRIDE
    for c in range(n_chunks):
        jb = lax.broadcasted_iota(jnp.int32, (nsb, CMP_CHUNK), 0)
        ci = lax.broadcasted_iota(jnp.int32, (nsb, CMP_CHUNK), 1) + c * CMP_CHUNK
        rel = ci - per_sel * jb
        over = (rel >= 1 - NSA_CMP_LEN // NSA_CMP_STRIDE) & (rel < per_sel) & (ci < nc - 1)
        ov = jnp.where(over, 1.0, 0.0).astype(BF16)
        hi, lo = _split_bf16(psum[c])
        imp = imp + _dot_nt(ov, hi) + _dot_nt(ov, lo)
    jb = lax.broadcasted_iota(jnp.int32, (nsb, TILE), 0)
    qpos = lax.broadcasted_iota(jnp.int32, (nsb, TILE), 1) + i * TILE
    cur = qpos // NSA_SEL_BLOCK
    forced = (jb == 0) | (jb == cur) | (jb == cur - 1)
    causal = jb * NSA_SEL_BLOCK <= qpos
    score = jnp.where(forced, FORCED_SCORE, jnp.where(causal, imp, NEG_INF))
    chosen = jnp.zeros((nsb, TILE), F32)
    for _ in range(n_top):
        mx = jnp.max(score, axis=0, keepdims=True)
        first = jnp.min(jnp.where(score == mx, jb, nsb), axis=0, keepdims=True)
        pick = jb == first
        chosen = jnp.where(pick, 1.0, chosen)
        score = jnp.where(pick, -jnp.inf, score)
    not_allowed = jnp.where((chosen > 0.0) & causal, 0.0, 1.0)
    sel_ref[0, 0] = not_allowed.astype(sel_ref.dtype)


def _nsa_sparse_kernel(q_ref, k_ref, v_ref, bias_ref, gate_ref, *rest, window):
    if window:
        (o_ref,) = rest
        nsel = None
    else:
        sel_ref, o_ref = rest
        nsel = sel_ref[0, 0]
    i = pl.program_id(2)
    q4 = q_ref[0]
    gates = _gate_columns(gate_ref, 2 if window else 1)
    q_heads = [q4[:, r * LANES:(r + 1) * LANES] for r in range(NSA_GROUP)]
    if window:
        n_tiles = NSA_WINDOW // TILE + 1
        lo = jnp.maximum(i - (n_tiles - 1), 0)
        outs = _flash_chunks(q_heads, k_ref, v_ref, [0] * NSA_GROUP, bias_ref, [None] * NSA_GROUP, lo, i + 1, i,
                             1, n_tiles)
    else:
        outs = _flash_chunks(q_heads, k_ref, v_ref, [0] * NSA_GROUP, bias_ref, [nsel] * NSA_GROUP, 0, i + 1, i,
                             TILE // NSA_SEL_BLOCK, BIASED_TILES + NSA_FAR_WIDTH - 1, far_width=NSA_FAR_WIDTH,
                             near_width=2)
    for r in range(NSA_GROUP):
        o_ref[0, :, r * LANES:(r + 1) * LANES] = (outs[r] * gates[r]).astype(o_ref.dtype)


def _mixer_nsa(x, w_in, w_out, table_bias, pos_k, w1_k, w2_k, pos_v, w1_v, w2_v, bsz, seq):
    bias_main, bias_win, bias_cmp = table_bias
    aw = N_HEADS * HEAD_DIM
    kvw = NSA_KV_HEADS * HEAD_DIM
    G, R = NSA_KV_HEADS, NSA_GROUP
    nc = seq // NSA_CMP_STRIDE
    nsb = seq // NSA_SEL_BLOCK
    n_top = min(NSA_TOP_N, nsb)
    nq = seq // TILE
    assert nc % CMP_CHUNK == 0
    par = lambda h: (h // R) % 2
    wq = _pad_heads_cols(w_in[:, :aw] * (ATTN_SCALE * LOG2_E), par).astype(BF16)
    w_cmp = w_in[:, aw:aw + 2 * kvw].astype(BF16)
    c0 = aw + 2 * kvw
    wk = jnp.concatenate([w_in[:, c0:c0 + kvw], w_in[:, c0 + 2 * kvw:c0 + 3 * kvw]], axis=1).astype(BF16)
    wv = jnp.concatenate([w_in[:, c0 + kvw:c0 + 2 * kvw], w_in[:, c0 + 3 * kvw:c0 + 4 * kvw]], axis=1)
    wv_t = _pad_heads_cols(wv, lambda h: 0).T.astype(BF16)
    wg = jnp.pad(w_in[:, aw + 6 * kvw:], ((0, 0), (0, LANES - 3 * N_HEADS))).astype(BF16)
    q = _matmul(x, wq, BF16).reshape(bsz, seq, N_HEADS * LANES)
    kk = _matmul(x, wk, BF16).reshape(bsz, seq, 2 * kvw)
    v_t = _matmul_t(x, wv_t)
    gates = _matmul(x, wg, F32, act='sigmoid')
    tok = _matmul_heads(x, w_cmp, bsz, seq).reshape(bsz, 2 * G, nc, NSA_CMP_STRIDE * HEAD_DIM)
    kc = _compress(tok, 0, pos_k, w1_k, w2_k, bsz, seq, True)
    vc = _compress(tok, G, pos_v, w1_v, w2_v, bsz, seq, False)

    q_spec = pl.BlockSpec((1, TILE, R * LANES), lambda b, g, i: (b, i, g))
    o_spec = pl.BlockSpec((1, TILE, R * LANES), lambda b, g, i: (b, i, g))
    gate_spec = pl.BlockSpec((TILE, LANES), lambda b, g, i: (b * nq + i, 0))
    o_shape = jax.ShapeDtypeStruct((bsz, seq, N_HEADS * LANES), BF16)
    sem = ("parallel", "parallel", "arbitrary")

    o_cmp, nsel = pl.pallas_call(
        functools.partial(_nsa_cmp_kernel, n_top=n_top),
        out_shape=[o_shape, jax.ShapeDtypeStruct((bsz, G, nsb, seq), BF16)],
        grid=(bsz, G, nq),
        in_specs=[q_spec,
                  pl.BlockSpec((1, 1, nc, LANES), lambda b, g, i: (b, g, 0, 0)),
                  pl.BlockSpec((1, 1, nc, LANES), lambda b, g, i: (b, g, 0, 0)),
                  pl.BlockSpec((R, CMP_TILES, TILE, CMP_CHUNK), lambda b, g, i: (g, 0, 0, 0)),
                  gate_spec],
        out_specs=[o_spec, pl.BlockSpec((1, 1, nsb, TILE), lambda b, g, i: (b, g, 0, i))],
        compiler_params=_params(sem),
        name="nsa_compressed_attn",
    )(q, kc, vc, bias_cmp, gates)

    def k_spec(first_blk):
        return pl.BlockSpec((1, seq, LANES), lambda b, g, i: (b, 0, first_blk + g // 2))

    def vt_spec(first_blk):
        return pl.BlockSpec((LANES, seq), lambda b, g, i: (first_blk + g, b))

    o_sel = pl.pallas_call(
        functools.partial(_nsa_sparse_kernel, window=False),
        out_shape=o_shape,
        grid=(bsz, G, nq),
        in_specs=[q_spec, k_spec(0), vt_spec(0),
                  pl.BlockSpec((R, MAIN_TILES, TILE, TILE), lambda b, g, i: (g, 0, 0, 0)),
                  gate_spec,
                  pl.BlockSpec((1, 1, nsb, TILE), lambda b, g, i: (b, g, 0, i))],
        out_specs=o_spec,
        compiler_params=_params(sem),
        name="nsa_selected_attn",
    )(q, kk, v_t, bias_main, gates, nsel)

    o_win = pl.pallas_call(
        functools.partial(_nsa_sparse_kernel, window=True),
        out_shape=o_shape,
        grid=(bsz, G, nq),
        in_specs=[q_spec, k_spec(2), vt_spec(G),
                  pl.BlockSpec((R, NSA_WINDOW // TILE + 1, TILE, TILE), lambda b, g, i: (g, 0, 0, 0)),
                  gate_spec],
        out_specs=o_spec,
        compiler_params=_params(sem),
        name="nsa_window_attn",
    )(q, kk, v_t, bias_win, gates)

    m = bsz * seq
    parts = [t.reshape(m, N_HEADS * LANES) for t in (o_cmp, o_sel, o_win)]
    return parts, _pad_heads_rows(w_out, lambda h: 0).astype(BF16)


def _router_kernel(x_ref, w_ref, b_ref, o_ref):
    logits = lax.dot_general(w_ref[...], x_ref[...], (((1,), (1,)), ((), ())),
                             preferred_element_type=F32, precision=lax.Precision.HIGHEST)
    scores = 1.0 / (1.0 + jnp.exp(-logits))
    biased = scores + b_ref[...]
    rows = [biased[e:e + 1, :] for e in range(N_EXPERTS)]
    group_score = []
    for g in range(N_GROUPS):
        r = rows[g * EXPERTS_PER_GROUP:(g + 1) * EXPERTS_PER_GROUP]
        best = None
        for a in range(EXPERTS_PER_GROUP):
            for c in range(a + 1, EXPERTS_PER_GROUP):
                pair = r[a] + r[c]
                best = pair if best is None else jnp.maximum(best, pair)
        group_score.append(best)
    best_val = group_score[0]
    best_grp = jnp.zeros_like(best_val, dtype=jnp.int32)
    for g in range(1, N_GROUPS):
        better = group_score[g] > best_val
        best_val = jnp.where(better, group_score[g], best_val)
        best_grp = jnp.where(better, g, best_grp)
    picked = []
    for e in range(N_EXPERTS):
        g, a = divmod(e, EXPERTS_PER_GROUP)
        rank = jnp.zeros_like(best_grp)
        for c in range(EXPERTS_PER_GROUP):
            if c == a:
                continue
            other = rows[g * EXPERTS_PER_GROUP + c]
            ahead = (other > rows[e]) | ((other == rows[e]) & (c < a))
            rank = rank + jnp.where(ahead, 1, 0)
        picked.append((best_grp == g) & (rank < 2))
    raw = [jnp.where(picked[e], scores[e:e + 1, :], 0.0) for e in range(N_EXPERTS)]
    total = raw[0]
    for e in range(1, N_EXPERTS):
        total = total + raw[e]
    o_ref[...] = (jnp.concatenate(raw, axis=0) / total).T


def _moe_kernel(x_ref, gate_ref, wg_ref, wu_ref, wd_ref, g_ref, b_ref, o_ref, xb_ref, acc_ref):
    e = pl.program_id(1)

    @pl.when(e == 0)
    def _():
        xb_ref[...] = x_ref[...].astype(BF16)
        acc_ref[...] = jnp.zeros_like(acc_ref)

    xb = xb_ref[...]
    lane = lax.broadcasted_iota(jnp.int32, gate_ref.shape, 1)
    gcol = jnp.sum(jnp.where(lane == e, gate_ref[...], 0.0), axis=-1, keepdims=True)
    a = _dot(xb, wg_ref[0])
    h = a / (1.0 + jnp.exp(-a)) * _dot(xb, wu_ref[0])
    acc_ref[...] += gcol * _dot(h.astype(BF16), wd_ref[0])

    @pl.when(e == N_EXPERTS - 1)
    def _():
        o_ref[...] = _layer_norm(DEEPNORM_ALPHA * x_ref[...] + acc_ref[...], g_ref[...], b_ref[...])


def _moe_ln(x, router_w, router_b, w_gate, w_up, w_down, g, b):
    m, d = x.shape
    tm = 1024
    gates = pl.pallas_call(
        _router_kernel,
        out_shape=jax.ShapeDtypeStruct((m, N_EXPERTS), F32),
        grid=(m // tm,),
        in_specs=[pl.BlockSpec((tm, d), lambda i: (i, 0)),
                  pl.BlockSpec((N_EXPERTS, d), lambda i: (0, 0)),
                  pl.BlockSpec((N_EXPERTS, 1), lambda i: (0, 0))],
        out_specs=pl.BlockSpec((tm, N_EXPERTS), lambda i: (i, 0)),
        compiler_params=_params(("parallel",)),
        name="moe_router",
    )(x, router_w.T, router_b.reshape(N_EXPERTS, 1))
    de = w_gate.shape[-1]
    return pl.pallas_call(
        _moe_kernel,
        out_shape=jax.ShapeDtypeStruct((m, d), F32),
        grid=(m // tm, N_EXPERTS),
        in_specs=[pl.BlockSpec((tm, d), lambda i, e: (i, 0)),
                  pl.BlockSpec((tm, N_EXPERTS), lambda i, e: (i, 0)),
                  pl.BlockSpec((1, d, de), lambda i, e: (e, 0, 0)),
                  pl.BlockSpec((1, d, de), lambda i, e: (e, 0, 0)),
                  pl.BlockSpec((1, de, d), lambda i, e: (e, 0, 0)),
                  pl.BlockSpec((1, d), lambda i, e: (0, 0)),
                  pl.BlockSpec((1, d), lambda i, e: (0, 0))],
        out_specs=pl.BlockSpec((tm, d), lambda i, e: (i, 0)),
        scratch_shapes=[pltpu.VMEM((tm, d), BF16), pltpu.VMEM((tm, d), F32)],
        compiler_params=_params(("parallel", "arbitrary")),
        name="moe_experts_ln",
    )(x, gates, w_gate.astype(BF16), w_up.astype(BF16), w_down.astype(BF16), g.reshape(1, d), b.reshape(1, d))


def kernel(x, rel_table, router_w, router_b, ln1_g, ln1_b, ln2_g, ln2_b, exp_w_gate, exp_w_up, exp_w_down, dil_w_in, dil_w_out, sb_w_in, sb_w_out, nsa_w_in, nsa_w_out, nsa_cmp_pos_k, nsa_cmp_w1_k, nsa_cmp_w2_k, nsa_cmp_pos_v, nsa_cmp_w1_v, nsa_cmp_w2_v, moba_w_in, moba_w_out):
    bsz, seq, d = x.shape
    assert d == D_MODEL and seq % (DILATED_PAIRS[-1][1] * DIL_BLOCK) == 0
    depth = ln1_g.shape[0]
    n_mixers = 4
    table = rel_table.astype(F32)
    span = DIL_BLOCK
    big = 1 << 30
    bias_dil = _bias_tiles(table, len(DILATED_PAIRS), DIL_BLOCK, 2 * DIL_BLOCK, DIL_BLOCK, 0, 1, 0, span,
                           [dl for _, dl in DILATED_PAIRS])
    assert BIASED_TILES * TILE - (TILE - 1) >= BUCKET_THRESHOLDS[-1]
    bias_main = _bias_tiles(table, MAIN_TILES, TILE, TILE, 0, TILE, 1, 0, big, [1], shift=True, transposed=True,
                            scale=LOG2_E)
    bias_win = _bias_tiles(table, NSA_WINDOW // TILE + 1, TILE, TILE, 0, TILE, 1, 0, NSA_WINDOW - 1, [1],
                           transposed=True, scale=LOG2_E)
    bias_cmp = _bias_tiles(table, CMP_TILES, TILE, CMP_CHUNK, -TILE - (NSA_CMP_LEN - 1), TILE, NSA_CMP_STRIDE,
                           0, big, [1], scale=LOG2_E)
    xf = x.reshape(bsz * seq, d)
    for layer in range(depth):
        kind, occ = layer % n_mixers, layer // n_mixers
        merge = None
        if kind == 0:
            parts, w_out = _mixer_dilated(xf, dil_w_in[occ], dil_w_out[occ], bias_dil, bsz, seq)
            merge = 'lse3'
        elif kind == 1:
            parts, w_out = _mixer_stick_breaking(xf, sb_w_in[occ], sb_w_out[occ], bsz, seq)
        elif kind == 2:
            parts, w_out = _mixer_nsa(xf, nsa_w_in[occ], nsa_w_out[occ], (bias_main, bias_win, bias_cmp),
                                      nsa_cmp_pos_k[occ], nsa_cmp_w1_k[occ], nsa_cmp_w2_k[occ],
                                      nsa_cmp_pos_v[occ], nsa_cmp_w1_v[occ], nsa_cmp_w2_v[occ], bsz, seq)
            merge = 'sum3'
        else:
            parts, w_out = _mixer_moba(xf, moba_w_in[occ], moba_w_out[occ], bias_main, bsz, seq)
        xf = _outproj_ln(parts, w_out, xf, ln1_g[layer], ln1_b[layer], merge=merge)
        xf = _moe_ln(xf, router_w, router_b, exp_w_gate[layer], exp_w_up[layer], exp_w_down[layer],
                     ln2_g[layer], ln2_b[layer])
    return xf.reshape(bsz, seq, d)
```

```python
import functools
import math

import numpy as np
import jax
import jax.numpy as jnp
from jax import lax
from jax.experimental import pallas as pl
from jax.experimental.pallas import tpu as pltpu

F32 = jnp.float32
BF16 = jnp.bfloat16

D_MODEL = 1024
HEAD_DIM = 64
N_HEADS = 16
LANES = 128
PAIRS = N_HEADS // 2
ATTN_SCALE = HEAD_DIM ** -0.5
REL_BUCKETS = 32
REL_MAX_DIST = 2048
DILATED_PAIRS = ((128, 1), (512, 4), (2048, 16))
DIL_BLOCK = 128
SB_CLIP = 60.0
SB_TILE = 256
LOG2_E = math.log2(math.e)
NSA_KV_HEADS = 4
NSA_GROUP = N_HEADS // NSA_KV_HEADS
NSA_CMP_LEN = 32
NSA_CMP_STRIDE = 16
NSA_SEL_BLOCK = 64
NSA_TOP_N = 16
NSA_WINDOW = 512
MOBA_BLOCK = 256
MOBA_TOP_K = 3
N_EXPERTS = 16
N_GROUPS = 4
EXPERTS_PER_GROUP = N_EXPERTS // N_GROUPS
DEPTH = 4
DEEPNORM_ALPHA = (2 * DEPTH) ** 0.25
LN_EPS = 1e-5
NEG_INF = -1e30
FORCED_SCORE = 1e9
TILE = 256
BIASED_TILES = 7
MOBA_FAR_WIDTH = 4
MOBA_HEADS = 4
NSA_FAR_WIDTH = 4
MAIN_TILES = BIASED_TILES + max(MOBA_FAR_WIDTH, NSA_FAR_WIDTH) - 1
CMP_CHUNK = 128
CMP_TILES = 19
VMEM_LIMIT = 56 * 1024 * 1024


def _bucket_thresholds():
    n = np.arange(0, 2 * REL_MAX_DIST)
    exact = REL_BUCKETS // 2
    logf = np.log(np.maximum(n, 1).astype(np.float64) / exact) / math.log(REL_MAX_DIST / exact)
    large = np.minimum(exact + (logf * (REL_BUCKETS - exact)).astype(np.int64), REL_BUCKETS - 1)
    bucket = np.where(n < exact, n, large)
    return tuple(int(np.argmax(bucket >= k)) for k in range(1, REL_BUCKETS))


BUCKET_THRESHOLDS = _bucket_thresholds()


def _params(sem, vmem=VMEM_LIMIT):
    return pltpu.CompilerParams(dimension_semantics=sem, vmem_limit_bytes=vmem)


def _dot(a, b):
    return jnp.dot(a, b, preferred_element_type=F32)


def _dot_nt(a, b):
    return lax.dot_general(a, b, (((1,), (1,)), ((), ())), preferred_element_type=F32)


def _split_bf16(x):
    hi = x.astype(BF16)
    lo = (x - hi.astype(F32)).astype(BF16)
    return hi, lo


def _matmul_kernel(x_ref, w_ref, o_ref, *, act):
    y = _dot(x_ref[...].astype(BF16), w_ref[...])
    if act == 'sigmoid':
        y = 1.0 / (1.0 + jnp.exp(-y))
    o_ref[...] = y.astype(o_ref.dtype)


def _matmul(x, w, out_dtype, act=None):
    m, k = x.shape
    n = w.shape[1]
    tm = 1024
    tn = next(t for t in (1024, 512, 128) if n % t == 0)
    return pl.pallas_call(
        functools.partial(_matmul_kernel, act=act),
        out_shape=jax.ShapeDtypeStruct((m, n), out_dtype),
        grid=(m // tm, n // tn),
        in_specs=[pl.BlockSpec((tm, k), lambda i, j: (i, 0)),
                  pl.BlockSpec((k, tn), lambda i, j: (0, j))],
        out_specs=pl.BlockSpec((tm, tn), lambda i, j: (i, j)),
        compiler_params=_params(("parallel", "arbitrary")),
        name="proj_matmul",
    )(x, w)


def _matmul_t_kernel(x_ref, w_ref, o_ref):
    y = _dot_nt(w_ref[...], x_ref[...].astype(BF16))
    row = lax.broadcasted_iota(jnp.int32, y.shape, 0)
    o_ref[...] = jnp.where(row % LANES == HEAD_DIM, 1.0, y).astype(o_ref.dtype)


def _matmul_t(x, wt):
    m, k = x.shape
    n = wt.shape[0]
    tm = 512
    tn = 512 if n % 512 == 0 else 256
    return pl.pallas_call(
        _matmul_t_kernel,
        out_shape=jax.ShapeDtypeStruct((n, m), BF16),
        grid=(m // tm, n // tn),
        in_specs=[pl.BlockSpec((tm, k), lambda i, j: (i, 0)),
                  pl.BlockSpec((tn, k), lambda i, j: (j, 0))],
        out_specs=pl.BlockSpec((tn, tm), lambda i, j: (j, i)),
        compiler_params=_params(("parallel", "arbitrary")),
        name="proj_matmul_t",
    )(x, wt)


def _matmul_heads_kernel(x_ref, w_ref, o_ref):
    y = _dot(x_ref[...].astype(BF16), w_ref[...]).astype(o_ref.dtype)
    for j in range(o_ref.shape[1]):
        o_ref[0, j] = y[:, j * HEAD_DIM:(j + 1) * HEAD_DIM]


def _matmul_heads(x, w, bsz, seq):
    m, k = x.shape
    n = w.shape[1] // HEAD_DIM
    tm = 512
    per_b = seq // tm
    return pl.pallas_call(
        _matmul_heads_kernel,
        out_shape=jax.ShapeDtypeStruct((bsz, n, seq, HEAD_DIM), BF16),
        grid=(m // tm,),
        in_specs=[pl.BlockSpec((tm, k), lambda i: (i, 0)),
                  pl.BlockSpec((k, n * HEAD_DIM), lambda i: (0, 0))],
        out_specs=pl.BlockSpec((1, n, tm, HEAD_DIM), lambda i: (i // per_b, 0, i % per_b, 0)),
        compiler_params=_params(("parallel",)),
        name="proj_heads",
    )(x, w)


def _layer_norm(z, g, b):
    mu = jnp.mean(z, axis=-1, keepdims=True)
    zc = z - mu
    var = jnp.mean(zc * zc, axis=-1, keepdims=True)
    return zc * lax.rsqrt(var + LN_EPS) * g + b


def _outproj_kernel(*refs, n_in, merge):
    ins = refs[:n_in]
    w_ref, x_ref, g_ref, b_ref, o_ref = refs[n_in:]
    if merge == 'lse3':
        o1, o2, o3, l1, l2, l3 = [r[...] for r in ins]
        mx = jnp.maximum(jnp.maximum(l1, l2), l3)
        e1, e2, e3 = jnp.exp(l1 - mx), jnp.exp(l2 - mx), jnp.exp(l3 - mx)
        a = (e1 * o1 + e2 * o2 + e3 * o3) / (e1 + e2 + e3)
    elif merge == 'sum3':
        a = ins[0][...].astype(F32) + ins[1][...].astype(F32) + ins[2][...].astype(F32)
    else:
        a = ins[0][...]
    y = _dot(a.astype(BF16), w_ref[...])
    o_ref[...] = _layer_norm(DEEPNORM_ALPHA * x_ref[...] + y, g_ref[...], b_ref[...])


def _outproj_ln(ins, w, x, g, b, merge=None):
    m, d = x.shape
    ka = ins[0].shape[1]
    tm = 256
    n_in = len(ins)
    return pl.pallas_call(
        functools.partial(_outproj_kernel, n_in=n_in, merge=merge),
        out_shape=jax.ShapeDtypeStruct((m, d), F32),
        grid=(m // tm,),
        in_specs=[pl.BlockSpec((tm, ka), lambda i: (i, 0)) for _ in ins]
        + [pl.BlockSpec((ka, d), lambda i: (0, 0)),
           pl.BlockSpec((tm, d), lambda i: (i, 0)),
           pl.BlockSpec((1, d), lambda i: (0, 0)),
           pl.BlockSpec((1, d), lambda i: (0, 0))],
        out_specs=pl.BlockSpec((tm, d), lambda i: (i, 0)),
        compiler_params=_params(("parallel",)),
        name="outproj_ln",
    )(*ins, w, x, g.reshape(1, d), b.reshape(1, d))


def _bias_kernel(tbl_ref, o_ref, *, rows, cols, off0, tstride, cmul, lo, hi, mults, shift, transposed, scale):
    t = pl.program_id(0)
    if len(set(mults)) == 1:
        mult = mults[0]
    else:
        mult = jnp.int32(mults[-1])
        for idx in range(len(mults) - 2, -1, -1):
            mult = jnp.where(t == idx, jnp.int32(mults[idx]), mult)
    base = off0 + tstride * t

    def strip(i, carry):
        r0 = pl.multiple_of(i * 8, 8)
        a = lax.broadcasted_iota(jnp.int32, (8, cols), 0) + r0
        c = lax.broadcasted_iota(jnp.int32, (8, cols), 1)
        steps = base + c - cmul * a if transposed else base + a - cmul * c
        valid = (steps >= lo) & (steps <= hi)
        dist = steps * mult
        for h in range(N_HEADS):
            acc = jnp.full((8, cols), tbl_ref[0, h], F32)
            for k, thr in enumerate(BUCKET_THRESHOLDS):
                acc = jnp.where(dist >= thr, tbl_ref[k + 1, h], acc)
            if shift:
                acc = acc - tbl_ref[REL_BUCKETS - 1, h]
            if scale != 1.0:
                acc = acc * scale
            o_ref[h, 0, pl.ds(r0, 8), :] = jnp.where(valid, acc, NEG_INF)
        return carry

    lax.fori_loop(0, rows // 8, strip, 0)


def _bias_tiles(table, n_tiles, rows, cols, off0, tstride, cmul, lo, hi, mults, shift=False, transposed=False,
                scale=1.0):
    return pl.pallas_call(
        functools.partial(_bias_kernel, rows=rows, cols=cols, off0=off0, tstride=tstride, cmul=cmul,
                          lo=lo, hi=hi, mults=tuple(mults), shift=shift, transposed=transposed, scale=scale),
        out_shape=jax.ShapeDtypeStruct((N_HEADS, n_tiles, rows, cols), F32),
        grid=(n_tiles,),
        in_specs=[pl.BlockSpec(memory_space=pltpu.SMEM)],
        out_specs=pl.BlockSpec((N_HEADS, 1, rows, cols), lambda t: (0, t, 0, 0)),
        compiler_params=_params(("arbitrary",)),
        name="bias_tiles",
    )(table)


def _dil_kernel(q_ref, kp_ref, kc_ref, vp_ref, vc_ref, bias_ref, o_ref, lse_ref,
                q_scr, kp_scr, kc_scr, vp_scr, vc_scr, *, dilation, per_step):
    j = pl.program_id(2)
    d = dilation
    n_cur = per_step * DIL_BLOCK

    def rows_of(ref, r, n, lead=()):
        if d == 1:
            return ref[lead + (slice(0, n), slice(None))]
        return ref[lead + (pl.ds(r, n, stride=d), slice(None))]

    if d > 1:
        for h in range(2):
            q_scr[h] = q_ref[0, :, h * LANES:(h + 1) * LANES].astype(F32)
        kp_scr[...] = kp_ref[0].astype(F32)
        kc_scr[...] = kc_ref[0].astype(F32)
        vp_scr[...] = vp_ref[0].astype(F32)
        vc_scr[...] = vc_ref[0].astype(F32)
    colk = lax.broadcasted_iota(jnp.int32, (DIL_BLOCK, 2 * DIL_BLOCK), 1)
    first = jnp.where((colk < DIL_BLOCK) & (j == 0), NEG_INF, 0.0)
    low = lax.broadcasted_iota(jnp.int32, (DIL_BLOCK, LANES), 1) < HEAD_DIM
    for r in range(d):
        if d == 1:
            q_heads = [q_ref[0, :, h * LANES:(h + 1) * LANES] for h in range(2)]
            k_all = jnp.concatenate([kp_ref[0], kc_ref[0]], axis=0)
            v_all = jnp.concatenate([vp_ref[0], vc_ref[0]], axis=0)
        else:
            q_heads = [rows_of(q_scr, r, n_cur, (h,)).astype(BF16) for h in range(2)]
            k_all = jnp.concatenate([rows_of(kp_scr, r, DIL_BLOCK), rows_of(kc_scr, r, n_cur)], axis=0).astype(BF16)
            v_all = jnp.concatenate([rows_of(vp_scr, r, DIL_BLOCK), rows_of(vc_scr, r, n_cur)], axis=0).astype(BF16)
        for u in range(per_step):
            k2 = k_all[u * DIL_BLOCK:(u + 2) * DIL_BLOCK]
            v2 = v_all[u * DIL_BLOCK:(u + 2) * DIL_BLOCK]
            outs, lses = [], []
            for h in range(2):
                s = _dot_nt(q_heads[h][u * DIL_BLOCK:(u + 1) * DIL_BLOCK], k2) + bias_ref[h, 0]
                if u == 0:
                    s = s + first
                m = jnp.max(s, axis=-1, keepdims=True)
                p = jnp.exp(s - m)
                l = jnp.maximum(jnp.sum(p, axis=-1, keepdims=True), 1e-30)
                outs.append(_dot(p.astype(BF16), v2) / l)
                lses.append(m + jnp.log(l))
            if d == 1:
                dst = (0, slice(u * DIL_BLOCK, (u + 1) * DIL_BLOCK), slice(None))
            else:
                dst = (0, pl.ds(u * DIL_BLOCK * d + r, DIL_BLOCK, stride=d), slice(None))
            o_ref[dst] = jnp.where(low, outs[0], outs[1])
            lse_ref[dst] = jnp.where(low, lses[0], lses[1])


def _dilated_group(proj, bias, g, dilation, bsz, seq):
    prev_rows = DIL_BLOCK * dilation
    step_rows = math.gcd(seq, 16 * DIL_BLOCK)
    per_step = step_rows // prev_rows
    assert per_step >= 1 and seq % step_rows == 0
    q_blk = g * N_HEADS // 2
    k_blk = 3 * N_HEADS + g * 2 * PAIRS
    v_blk = k_blk + PAIRS

    def kv_specs(base):
        prev = pl.BlockSpec((1, prev_rows, LANES), lambda hp, b, j: (b, jnp.maximum(per_step * j - 1, 0), base + hp))
        cur = pl.BlockSpec((1, step_rows, LANES), lambda hp, b, j: (b, j, base + hp))
        return [prev, cur]

    out_spec = pl.BlockSpec((1, step_rows, LANES), lambda hp, b, j: (b, j, hp))
    staged = 8 if dilation == 1 else None
    o, lse = pl.pallas_call(
        functools.partial(_dil_kernel, dilation=dilation, per_step=per_step),
        out_shape=[jax.ShapeDtypeStruct((bsz, seq, D_MODEL), F32)] * 2,
        grid=(PAIRS, bsz, seq // step_rows),
        in_specs=[pl.BlockSpec((1, step_rows, 2 * LANES), lambda hp, b, j: (b, j, q_blk + hp))]
        + kv_specs(k_blk) + kv_specs(v_blk)
        + [pl.BlockSpec((2, 1, DIL_BLOCK, 2 * DIL_BLOCK), lambda hp, b, j: (hp, g, 0, 0))],
        out_specs=[out_spec, out_spec],
        scratch_shapes=[pltpu.VMEM((2, staged or step_rows, LANES), F32),
                        pltpu.VMEM((staged or prev_rows, LANES), F32), pltpu.VMEM((staged or step_rows, LANES), F32),
                        pltpu.VMEM((staged or prev_rows, LANES), F32), pltpu.VMEM((staged or step_rows, LANES), F32)],
        compiler_params=_params(("parallel", "parallel", "arbitrary")),
        name="dilated_attn",
    )(proj, proj, proj, proj, proj, bias)
    return o.reshape(bsz * seq, D_MODEL), lse.reshape(bsz * seq, D_MODEL)


def _pad_heads_cols(w, parity_of_head):
    k, n = w.shape
    nh = n // HEAD_DIM
    wh = w.reshape(k, nh, HEAD_DIM)
    z = jnp.zeros_like(wh)
    par = jnp.asarray([parity_of_head(h) for h in range(nh)], jnp.int32)[None, :, None]
    lo = jnp.where(par == 0, wh, z)
    hi = jnp.where(par == 1, wh, z)
    return jnp.concatenate([lo, hi], axis=-1).reshape(k, nh * LANES)


def _pad_heads_rows(w, parity_of_head):
    return _pad_heads_cols(w.T, parity_of_head).T


def _mixer_dilated(x, w_in, w_out, bias_dil, bsz, seq):
    aw = N_HEADS * HEAD_DIM
    par = lambda h: h % 2
    qs, kvs = [], []
    for g in range(len(DILATED_PAIRS)):
        base = g * 3 * aw
        qs.append(_pad_heads_cols(w_in[:, base:base + aw] * ATTN_SCALE, par))
        kvs.append(w_in[:, base + aw:base + 3 * aw])
    w_all = jnp.concatenate(qs + kvs, axis=1).astype(BF16)
    proj = _matmul(x, w_all, BF16).reshape(bsz, seq, -1)
    parts = [_dilated_group(proj, bias_dil, g, d, bsz, seq) for g, (_, d) in enumerate(DILATED_PAIRS)]
    return [p[0] for p in parts] + [p[1] for p in parts], w_out.astype(BF16)


def _sb_kernel(q_ref, k_ref, v_ref, o_ref):
    i = pl.program_id(2)
    t = SB_TILE
    row = lax.broadcasted_iota(jnp.int32, (t, t), 0)
    col = lax.broadcasted_iota(jnp.int32, (t, t), 1)
    suffix = jnp.where(row >= col, 1.0, 0.0).astype(BF16)
    strict = col < row
    q2 = q_ref[0]
    q_heads = [q2[:, h * LANES:(h + 1) * LANES] for h in range(2)]

    def block(qh, kj, masked):
        start = pl.multiple_of(kj * t, t)
        kb = k_ref[0, pl.ds(start, t), :]
        vb = v_ref[0, pl.ds(start, t), :]
        a = jnp.clip(_dot_nt(qh, kb), -SB_CLIP, SB_CLIP) * LOG2_E
        sp = jnp.log2(1.0 + jnp.exp2(a))
        if masked:
            sp = jnp.where(strict, sp, 0.0)
        hi, lo = _split_bf16(sp)
        rr = _dot(hi, suffix) + _dot(lo, suffix)
        att = jnp.exp2(a - rr)
        if masked:
            att = jnp.where(strict, att, 0.0)
        return _dot(att.astype(BF16), vb), rr[:, 0:1]

    def first_two(kj, carry):
        has_prev = jnp.where(kj > 0, 1.0, 0.0)
        out = []
        for qh in q_heads:
            acc_d, r_d = block(qh, kj, True)
            pv, r_p = block(qh, jnp.maximum(kj - 1, 0), False)
            out.append((acc_d + (has_prev * jnp.exp2(-r_d)) * pv, r_d + has_prev * r_p))
        return tuple(out)

    zero = (jnp.zeros((t, LANES), F32), jnp.zeros((t, 1), F32))
    (acc0, r0), (acc1, r1) = lax.fori_loop(i, i + 1, first_two, (zero, zero))

    def cond(carry):
        n, alive = carry[0], carry[1]
        return (n < i - 1) & alive

    def step(carry):
        n = carry[0]
        new, tops = [], []
        for qh, (r_sum, w, acc) in zip(q_heads, carry[2:]):
            pv, r_blk = block(qh, i - 2 - n, False)
            r_new = r_sum + r_blk
            w_new = jnp.exp2(-r_new)
            new.append((r_new, w_new, acc + w * pv))
            tops.append(jnp.max(w_new, axis=0, keepdims=True))
        alive = jnp.maximum(tops[0], tops[1])[0, 0] > 0.0
        return (n + 1, alive) + tuple(new)

    w0, w1 = jnp.exp2(-r0), jnp.exp2(-r1)
    alive0 = jnp.maximum(jnp.max(w0, axis=0, keepdims=True), jnp.max(w1, axis=0, keepdims=True))[0, 0] > 0.0
    init = (jnp.int32(0), alive0, (r0, w0, acc0), (r1, w1, acc1))
    res = lax.while_loop(cond, step, init)
    lane = lax.broadcasted_iota(jnp.int32, (t, LANES), 1)
    o_ref[0] = jnp.where(lane < HEAD_DIM, res[2][2], res[3][2]).astype(o_ref.dtype)


def _mixer_stick_breaking(x, w_in, w_out, bsz, seq):
    aw = N_HEADS * HEAD_DIM
    wq = _pad_heads_cols(w_in[:, :aw] * ATTN_SCALE, lambda h: h % 2)
    w_all = jnp.concatenate([wq, w_in[:, aw:]], axis=1).astype(BF16)
    proj = _matmul(x, w_all, BF16).reshape(bsz, seq, -1)
    kb = N_HEADS
    o = pl.pallas_call(
        _sb_kernel,
        out_shape=jax.ShapeDtypeStruct((bsz, seq, D_MODEL), BF16),
        grid=(bsz, PAIRS, seq // SB_TILE),
        in_specs=[pl.BlockSpec((1, SB_TILE, 2 * LANES), lambda b, hp, i: (b, i, hp)),
                  pl.BlockSpec((1, seq, LANES), lambda b, hp, i: (b, 0, kb + hp)),
                  pl.BlockSpec((1, seq, LANES), lambda b, hp, i: (b, 0, kb + PAIRS + hp))],
        out_specs=pl.BlockSpec((1, SB_TILE, LANES), lambda b, hp, i: (b, i, hp)),
        compiler_params=_params(("parallel", "parallel", "arbitrary")),
        name="stick_breaking_attn",
    )(proj, proj, proj)
    return [o.reshape(bsz * seq, D_MODEL)], w_out.astype(BF16)


def _flash_chunks(q_heads, k_ref, k_cols, v_ref, v_rows, bias_ref, nsel_of_head, lo, hi, i, blocks_per_chunk, n_tiles,
                  fold_mask=False, pipelined=False, far_width=1, near_width=1):
    nh = len(q_heads)
    distinct = []
    if not fold_mask:
        for ns in nsel_of_head:
            if ns is not None and all(ns is not d for d in distinct):
                distinct.append(ns)

    def logits(ch, near, width):
        rows = width * TILE
        start = pl.multiple_of(ch * TILE, TILE)
        kbs = {c: k_ref[0, pl.ds(start, rows), c:c + LANES] for c in sorted(set(k_cols))}
        adds = []
        if fold_mask:
            key = lax.broadcasted_iota(jnp.int32, (rows, LANES), 0)
            blk = lax.broadcasted_iota(jnp.int32, (rows, LANES), 1)
            blk_of_key = ch * blocks_per_chunk + key // (TILE // blocks_per_chunk)
            ind = jnp.where(blk == blk_of_key, NEG_INF, 0.0).astype(BF16)
            kbs = {c: jnp.concatenate([kb, ind], axis=1) for c, kb in kbs.items()}
        else:
            for ns in distinct:
                nblk = ns.shape[0]
                kk = lax.broadcasted_iota(jnp.int32, (rows, nblk), 0)
                bj = lax.broadcasted_iota(jnp.int32, (rows, nblk), 1)
                blk_of_key = ch * blocks_per_chunk + kk // (TILE // blocks_per_chunk)
                expand = jnp.where(bj == blk_of_key, NEG_INF, 0.0).astype(BF16)
                adds.append(_dot(expand, ns))
        out = []
        for h in range(nh):
            s = _dot_nt(kbs[k_cols[h]], q_heads[h])
            if near:
                tiles = [bias_ref[h, jnp.maximum(i - ch - w, 0)] for w in range(width)]
                s = s + (tiles[0] if width == 1 else jnp.concatenate(tiles, axis=0))
            ns = nsel_of_head[h]
            if ns is not None and not fold_mask:
                s = s + adds[[d is ns for d in distinct].index(True)]
            out.append(s)
        return tuple(out)

    def update(ch, s_heads, state, width):
        start = pl.multiple_of(ch * TILE, TILE)
        vts = {}
        new = []
        for h in range(nh):
            r0 = v_rows[h]
            if r0 not in vts:
                vts[r0] = v_ref[r0:r0 + LANES, pl.ds(start, width * TILE)]
            m, acc = state[h]
            s = s_heads[h]
            m_new = jnp.maximum(m, jnp.max(s, axis=0, keepdims=True))
            alpha = jnp.exp2(m - m_new)
            p = jnp.exp2(s - m_new)
            acc = alpha * acc + _dot(vts[r0], p.astype(BF16))
            new.append((m_new, acc))
        return tuple(new)

    def run(ch0, steps, near, width, state, pipelined=pipelined):
        if not pipelined:
            return lax.fori_loop(
                0, steps, lambda t, st: update(ch0 + t * width, logits(ch0 + t * width, near, width), st, width), state)

        def body(t, carry):
            s_cur, st = carry
            s_next = logits(ch0 + jnp.minimum(t + 1, steps - 1) * width, near, width)
            return s_next, update(ch0 + t * width, s_cur, st, width)

        zeros = tuple(jnp.zeros((width * TILE, TILE), F32) for _ in range(nh))
        first = jnp.minimum(ch0, i)
        s0 = lax.fori_loop(first, first + 1, lambda c, _: logits(c, near, width), zeros)
        return lax.fori_loop(0, steps, body, (s0, state))[1]

    state = tuple((jnp.full((1, TILE), NEG_INF, F32), jnp.zeros((LANES, TILE), F32)) for _ in range(nh))
    n_near = n_tiles - (far_width - 1)
    far_steps = jnp.maximum(hi - n_near - lo, 0) // far_width
    split = lo + far_steps * far_width
    state = run(lo, far_steps, False, far_width, state)
    near_pairs = (hi - split) // near_width if near_width > 1 else 0
    if near_width > 1:
        state = run(split, near_pairs, True, near_width, state)
    single = split + near_pairs * near_width
    state = run(single, hi - single, True, 1, state, pipelined=pipelined and near_width == 1)
    outs = []
    for _, acc in state:
        o = (acc / acc[HEAD_DIM:HEAD_DIM + 1, :]).T
        lane = lax.broadcasted_iota(jnp.int32, o.shape, 1)
        outs.append(jnp.where(lane < HEAD_DIM, o, 0.0))
    return outs


def _kmean_kernel(k_ref, o_ref):
    rows = k_ref.shape[1]
    kf = k_ref[0].astype(F32).reshape(rows // MOBA_BLOCK, MOBA_BLOCK, k_ref.shape[2])
    o_ref[0] = jnp.sum(kf, axis=1) * (1.0 / MOBA_BLOCK)


def _moba_kernel(q_ref, k_ref, v_ref, km_ref, bias_ref, o_ref):
    i = pl.program_id(2)
    nblk = km_ref.shape[1]
    km_hi, km_lo = _split_bf16(km_ref[0])
    qs = q_ref[0]
    blk = lax.broadcasted_iota(jnp.int32, (nblk, TILE), 0)
    past = blk < i
    q_heads = []
    for h in range(MOBA_HEADS):
        qh = qs[:, h * LANES:(h + 1) * LANES]
        pair = slice((h // 2) * LANES, (h // 2 + 1) * LANES)
        gate = jnp.where(past, _dot_nt(km_hi[:, pair], qh) + _dot_nt(km_lo[:, pair], qh), NEG_INF)
        allowed = blk == i
        for _ in range(MOBA_TOP_K):
            mx = jnp.max(gate, axis=0, keepdims=True)
            first = jnp.min(jnp.where(gate == mx, blk, nblk), axis=0, keepdims=True)
            pick = blk == first
            allowed = allowed | (pick & past)
            gate = jnp.where(pick, -jnp.inf, gate)
        nsel = jnp.where(allowed, 0.0, 1.0).T.astype(BF16)
        q_heads.append(jnp.concatenate([qh, nsel], axis=1))
    outs = _flash_chunks(q_heads, k_ref, [(h // 2) * LANES for h in range(MOBA_HEADS)],
                         v_ref, [h * LANES for h in range(MOBA_HEADS)], bias_ref, [None] * MOBA_HEADS, 0, i + 1, i, 1,
                         BIASED_TILES + MOBA_FAR_WIDTH - 1, fold_mask=True, pipelined=False, far_width=MOBA_FAR_WIDTH,
                         near_width=2)
    for h in range(MOBA_HEADS):
        o_ref[0, :, h * LANES:(h + 1) * LANES] = outs[h].astype(o_ref.dtype)


def _mixer_moba(x, w_in, w_out, bias_main, bsz, seq):
    aw = N_HEADS * HEAD_DIM
    nblk = seq // MOBA_BLOCK
    assert nblk - 1 >= MOBA_TOP_K
    wq = _pad_heads_cols(w_in[:, :aw] * (ATTN_SCALE * LOG2_E), lambda h: h % 2)
    w_all = jnp.concatenate([wq, w_in[:, aw:2 * aw]], axis=1).astype(BF16)
    proj = _matmul(x, w_all, BF16).reshape(bsz, seq, -1)
    v_t = _matmul_t(x, _pad_heads_cols(w_in[:, 2 * aw:], lambda h: 0).T.astype(BF16))
    kb = N_HEADS
    rows = 8 * MOBA_BLOCK
    kmean = pl.pallas_call(
        _kmean_kernel,
        out_shape=jax.ShapeDtypeStruct((bsz, nblk, D_MODEL), F32),
        grid=(bsz, seq // rows),
        in_specs=[pl.BlockSpec((1, rows, D_MODEL), lambda b, i: (b, i, kb * LANES // D_MODEL))],
        out_specs=pl.BlockSpec((1, 8, D_MODEL), lambda b, i: (b, i, 0)),
        compiler_params=_params(("parallel", "parallel")),
        name="moba_kmean",
    )(proj)
    assert nblk <= LANES
    kmean = jnp.pad(kmean, ((0, 0), (0, LANES - nblk), (0, 0)))
    nh = MOBA_HEADS
    kw = nh // 2 * LANES
    once = pl.Buffered(1)
    o = pl.pallas_call(
        _moba_kernel,
        out_shape=jax.ShapeDtypeStruct((bsz, seq, N_HEADS * LANES), BF16),
        grid=(bsz, N_HEADS // nh, seq // TILE),
        in_specs=[pl.BlockSpec((1, TILE, nh * LANES), lambda b, hq, i: (b, i, hq)),
                  pl.BlockSpec((1, seq, kw), lambda b, hq, i: (b, 0, kb * LANES // kw + hq), pipeline_mode=once),
                  pl.BlockSpec((nh * LANES, seq), lambda b, hq, i: (hq, b), pipeline_mode=once),
                  pl.BlockSpec((1, LANES, kw), lambda b, hq, i: (b, 0, hq)),
                  pl.BlockSpec((nh, MAIN_TILES, TILE, TILE), lambda b, hq, i: (hq, 0, 0, 0), pipeline_mode=once)],
        out_specs=pl.BlockSpec((1, TILE, nh * LANES), lambda b, hq, i: (b, i, hq)),
        compiler_params=_params(("parallel", "parallel", "arbitrary")),
        name="moba_attn",
    )(proj, proj, v_t, kmean, bias_main)
    return [o.reshape(bsz * seq, N_HEADS * LANES)], _pad_heads_rows(w_out, lambda h: 0).astype(BF16)


def _gelu_tanh(x):
    return 0.5 * x * (1.0 + jnp.tanh(math.sqrt(2.0 / math.pi) * (x + 0.044715 * (x * x * x))))


def _compress_kernel(a_ref, pos_ref, w1_ref, w2_ref, o_ref):
    nc = a_ref.shape[2]
    half = NSA_CMP_STRIDE * HEAD_DIM
    a = a_ref[0, 0].astype(F32)
    top = _dot((a + pos_ref[0:1, :]).astype(BF16), w1_ref[0:half, :])
    bot = _dot((a + pos_ref[1:2, :]).astype(BF16), w1_ref[half:2 * half, :])
    hid = top + pltpu.roll(bot, nc - 1, 0)
    out = _dot(_gelu_tanh(hid).astype(BF16), w2_ref[0])
    rowi = lax.broadcasted_iota(jnp.int32, out.shape, 0)
    o_ref[0, 0] = jnp.where(rowi < nc - 1, out, 0.0).astype(o_ref.dtype)


def _compress(tok, first, pos, w1, w2, bsz, seq, by_parity):
    nc = seq // NSA_CMP_STRIDE
    half = NSA_CMP_STRIDE * HEAD_DIM
    pos2 = pos.reshape(2, half)
    z = jnp.zeros_like(w2)
    low = jnp.concatenate([w2, z], axis=1)
    w2p = jnp.stack([low, jnp.concatenate([z, w2], axis=1) if by_parity else low]).astype(BF16)
    return pl.pallas_call(
        _compress_kernel,
        out_shape=jax.ShapeDtypeStruct((bsz, NSA_KV_HEADS, nc, LANES), BF16),
        grid=(bsz, NSA_KV_HEADS),
        in_specs=[pl.BlockSpec((1, 1, nc, half), lambda b, g: (b, first + g, 0, 0)),
                  pl.BlockSpec((2, half), lambda b, g: (0, 0)),
                  pl.BlockSpec((2 * half, w1.shape[1]), lambda b, g: (0, 0)),
                  pl.BlockSpec((1, w2.shape[0], LANES), lambda b, g: (g % 2, 0, 0))],
        out_specs=pl.BlockSpec((1, 1, nc, LANES), lambda b, g: (b, g, 0, 0)),
        compiler_params=_params(("parallel", "parallel")),
        name="nsa_compress",
    )(tok, pos2, w1.astype(BF16), w2p)


def _gate_columns(gate_ref, branch):
    tile = gate_ref[...]
    lane = lax.broadcasted_iota(jnp.int32, tile.shape, 1)
    base = branch * N_HEADS + pl.program_id(1) * NSA_GROUP
    return [jnp.sum(jnp.where(lane == base + r, tile, 0.0), axis=-1, keepdims=True) for r in range(NSA_GROUP)]


def _nsa_cmp_kernel(q_ref, kc_ref, vc_ref, bias_ref, gate_ref, o_ref, sel_ref, *, n_top):
    i = pl.program_id(2)
    nc = kc_ref.shape[2]
    n_chunks = nc // CMP_CHUNK
    nsb = sel_ref.shape[2]
    q4 = q_ref[0]
    gates = _gate_columns(gate_ref, 0)
    keep = lax.broadcasted_iota(jnp.int32, (TILE, LANES), 1) < HEAD_DIM
    psum = [jnp.zeros((TILE, CMP_CHUNK), F32) for _ in range(n_chunks)]
    for r in range(NSA_GROUP):
        qh = q4[:, r * LANES:(r + 1) * LANES]
        ss = []
        for c in range(n_chunks):
            tile = jnp.clip(i - (CMP_CHUNK * NSA_CMP_STRIDE // TILE) * c, -1, CMP_TILES - 2) + 1
            ss.append(_dot_nt(qh, kc_ref[0, 0, c * CMP_CHUNK:(c + 1) * CMP_CHUNK, :]) + bias_ref[r, tile])
        m = ss[0].max(axis=-1, keepdims=True)
        for c in range(1, n_chunks):
            m = jnp.maximum(m, ss[c].max(axis=-1, keepdims=True))
        ps = [jnp.where(s > 0.5 * NEG_INF, jnp.exp2(s - m), 0.0) for s in ss]
        l = ps[0].sum(axis=-1, keepdims=True)
        for c in range(1, n_chunks):
            l = l + ps[c].sum(axis=-1, keepdims=True)
        inv = 1.0 / jnp.maximum(l, 1e-30)
        acc = jnp.zeros((TILE, LANES), F32)
        for c in range(n_chunks):
            pc = ps[c] * inv
            psum[c] = psum[c] + pc
            acc = acc + _dot(pc.astype(BF16), vc_ref[0, 0, c * CMP_CHUNK:(c + 1) * CMP_CHUNK, :])
        o_ref[0, :, r * LANES:(r + 1) * LANES] = jnp.where(keep, acc * gates[r], 0.0).astype(o_ref.dtype)
    imp = jnp.zeros((nsb, TILE), F32)
    per_sel = NSA_SEL_BLOCK // NSA_CMP_STRIDE
    for c in range(n_chunks):
        jb = lax.broadcasted_iota(jnp.int32, (nsb, CMP_CHUNK), 0)
        ci = lax.broadcasted_iota(jnp.int32, (nsb, CMP_CHUNK), 1) + c * CMP_CHUNK
        rel = ci - per_sel * jb
        over = (rel >= 1 - NSA_CMP_LEN // NSA_CMP_STRIDE) & (rel < per_sel) & (ci < nc - 1)
        ov = jnp.where(over, 1.0, 0.0).astype(BF16)
        hi, lo = _split_bf16(psum[c])
        imp = imp + _dot_nt(ov, hi) + _dot_nt(ov, lo)
    jb = lax.broadcasted_iota(jnp.int32, (nsb, TILE), 0)
    qpos = lax.broadcasted_iota(jnp.int32, (nsb, TILE), 1) + i * TILE
    cur = qpos // NSA_SEL_BLOCK
    forced = (jb == 0) | (jb == cur) | (jb == cur - 1)
    causal = jb * NSA_SEL_BLOCK <= qpos
    score = jnp.where(forced, FORCED_SCORE, jnp.where(causal, imp, NEG_INF))
    chosen = jnp.zeros((nsb, TILE), F32)
    for _ in range(n_top):
        mx = jnp.max(score, axis=0, keepdims=True)
        first = jnp.min(jnp.where(score == mx, jb, nsb), axis=0, keepdims=True)
        pick = jb == first
        chosen = jnp.where(pick, 1.0, chosen)
        score = jnp.where(pick, -jnp.inf, score)
    not_allowed = jnp.where((chosen > 0.0) & causal, 0.0, 1.0)
    sel_ref[0, 0] = not_allowed.astype(sel_ref.dtype)


def _nsa_sparse_kernel(q_ref, k_ref, v_ref, bias_ref, gate_ref, *rest, window):
    if window:
        (o_ref,) = rest
        nsel = None
    else:
        sel_ref, o_ref = rest
        nsel = sel_ref[0, 0]
    i = pl.program_id(2)
    q4 = q_ref[0]
    gates = _gate_columns(gate_ref, 2 if window else 1)
    q_heads = [q4[:, r * LANES:(r + 1) * LANES] for r in range(NSA_GROUP)]
    if window:
        n_tiles = NSA_WINDOW // TILE + 1
        lo = jnp.maximum(i - (n_tiles - 1), 0)
        outs = _flash_chunks(q_heads, k_ref, [0] * NSA_GROUP, v_ref, [0] * NSA_GROUP, bias_ref, [None] * NSA_GROUP,
                             lo, i + 1, i, 1, n_tiles, near_width=2)
    else:
        outs = _flash_chunks(q_heads, k_ref, [0] * NSA_GROUP, v_ref, [0] * NSA_GROUP, bias_ref, [nsel] * NSA_GROUP,
                             0, i + 1, i,
                             TILE // NSA_SEL_BLOCK, BIASED_TILES + NSA_FAR_WIDTH - 1, far_width=NSA_FAR_WIDTH,
                             near_width=2)
    for r in range(NSA_GROUP):
        o_ref[0, :, r * LANES:(r + 1) * LANES] = (outs[r] * gates[r]).astype(o_ref.dtype)


def _mixer_nsa(x, w_in, w_out, table_bias, pos_k, w1_k, w2_k, pos_v, w1_v, w2_v, bsz, seq):
    bias_main, bias_win, bias_cmp = table_bias
    aw = N_HEADS * HEAD_DIM
    kvw = NSA_KV_HEADS * HEAD_DIM
    G, R = NSA_KV_HEADS, NSA_GROUP
    nc = seq // NSA_CMP_STRIDE
    nsb = seq // NSA_SEL_BLOCK
    n_top = min(NSA_TOP_N, nsb)
    nq = seq // TILE
    assert nc % CMP_CHUNK == 0
    par = lambda h: (h // R) % 2
    wq = _pad_heads_cols(w_in[:, :aw] * (ATTN_SCALE * LOG2_E), par).astype(BF16)
    w_cmp = w_in[:, aw:aw + 2 * kvw].astype(BF16)
    c0 = aw + 2 * kvw
    wk = jnp.concatenate([w_in[:, c0:c0 + kvw], w_in[:, c0 + 2 * kvw:c0 + 3 * kvw]], axis=1).astype(BF16)
    wv = jnp.concatenate([w_in[:, c0 + kvw:c0 + 2 * kvw], w_in[:, c0 + 3 * kvw:c0 + 4 * kvw]], axis=1)
    wv_t = _pad_heads_cols(wv, lambda h: 0).T.astype(BF16)
    wg = jnp.pad(w_in[:, aw + 6 * kvw:], ((0, 0), (0, LANES - 3 * N_HEADS))).astype(BF16)
    q = _matmul(x, wq, BF16).reshape(bsz, seq, N_HEADS * LANES)
    kk = _matmul(x, wk, BF16).reshape(bsz, seq, 2 * kvw)
    v_t = _matmul_t(x, wv_t)
    gates = _matmul(x, wg, F32, act='sigmoid')
    tok = _matmul_heads(x, w_cmp, bsz, seq).reshape(bsz, 2 * G, nc, NSA_CMP_STRIDE * HEAD_DIM)
    kc = _compress(tok, 0, pos_k, w1_k, w2_k, bsz, seq, True)
    vc = _compress(tok, G, pos_v, w1_v, w2_v, bsz, seq, False)

    q_spec = pl.BlockSpec((1, TILE, R * LANES), lambda b, g, i: (b, i, g))
    o_spec = pl.BlockSpec((1, TILE, R * LANES), lambda b, g, i: (b, i, g))
    gate_spec = pl.BlockSpec((TILE, LANES), lambda b, g, i: (b * nq + i, 0))
    o_shape = jax.ShapeDtypeStruct((bsz, seq, N_HEADS * LANES), BF16)
    sem = ("parallel", "parallel", "arbitrary")

    o_cmp, nsel = pl.pallas_call(
        functools.partial(_nsa_cmp_kernel, n_top=n_top),
        out_shape=[o_shape, jax.ShapeDtypeStruct((bsz, G, nsb, seq), BF16)],
        grid=(bsz, G, nq),
        in_specs=[q_spec,
                  pl.BlockSpec((1, 1, nc, LANES), lambda b, g, i: (b, g, 0, 0)),
                  pl.BlockSpec((1, 1, nc, LANES), lambda b, g, i: (b, g, 0, 0)),
                  pl.BlockSpec((R, CMP_TILES, TILE, CMP_CHUNK), lambda b, g, i: (g, 0, 0, 0)),
                  gate_spec],
        out_specs=[o_spec, pl.BlockSpec((1, 1, nsb, TILE), lambda b, g, i: (b, g, 0, i))],
        compiler_params=_params(sem),
        name="nsa_compressed_attn",
    )(q, kc, vc, bias_cmp, gates)

    def k_spec(first_blk):
        return pl.BlockSpec((1, seq, LANES), lambda b, g, i: (b, 0, first_blk + g // 2))

    def vt_spec(first_blk):
        return pl.BlockSpec((LANES, seq), lambda b, g, i: (first_blk + g, b))

    o_sel = pl.pallas_call(
        functools.partial(_nsa_sparse_kernel, window=False),
        out_shape=o_shape,
        grid=(bsz, G, nq),
        in_specs=[q_spec, k_spec(0), vt_spec(0),
                  pl.BlockSpec((R, MAIN_TILES, TILE, TILE), lambda b, g, i: (g, 0, 0, 0)),
                  gate_spec,
                  pl.BlockSpec((1, 1, nsb, TILE), lambda b, g, i: (b, g, 0, i))],
        out_specs=o_spec,
        compiler_params=_params(sem),
        name="nsa_selected_attn",
    )(q, kk, v_t, bias_main, gates, nsel)

    o_win = pl.pallas_call(
        functools.partial(_nsa_sparse_kernel, window=True),
        out_shape=o_shape,
        grid=(bsz, G, nq),
        in_specs=[q_spec, k_spec(2), vt_spec(G),
                  pl.BlockSpec((R, NSA_WINDOW // TILE + 1, TILE, TILE), lambda b, g, i: (g, 0, 0, 0)),
                  gate_spec],
        out_specs=o_spec,
        compiler_params=_params(sem),
        name="nsa_window_attn",
    )(q, kk, v_t, bias_win, gates)

    m = bsz * seq
    parts = [t.reshape(m, N_HEADS * LANES) for t in (o_cmp, o_sel, o_win)]
    return parts, _pad_heads_rows(w_out, lambda h: 0).astype(BF16)


def _router_kernel(x_ref, w_ref, b_ref, o_ref):
    logits = lax.dot_general(w_ref[...], x_ref[...], (((1,), (1,)), ((), ())),
                             preferred_element_type=F32, precision=lax.Precision.HIGHEST)
    scores = 1.0 / (1.0 + jnp.exp(-logits))
    biased = scores + b_ref[...]
    rows = [biased[e:e + 1, :] for e in range(N_EXPERTS)]
    group_score = []
    for g in range(N_GROUPS):
        r = rows[g * EXPERTS_PER_GROUP:(g + 1) * EXPERTS_PER_GROUP]
        best = None
        for a in range(EXPERTS_PER_GROUP):
            for c in range(a + 1, EXPERTS_PER_GROUP):
                pair = r[a] + r[c]
                best = pair if best is None else jnp.maximum(best, pair)
        group_score.append(best)
    best_val = group_score[0]
    best_grp = jnp.zeros_like(best_val, dtype=jnp.int32)
    for g in range(1, N_GROUPS):
        better = group_score[g] > best_val
        best_val = jnp.where(better, group_score[g], best_val)
        best_grp = jnp.where(better, g, best_grp)
    picked = []
    for e in range(N_EXPERTS):
        g, a = divmod(e, EXPERTS_PER_GROUP)
        rank = jnp.zeros_like(best_grp)
        for c in range(EXPERTS_PER_GROUP):
            if c == a:
                continue
            other = rows[g * EXPERTS_PER_GROUP + c]
            ahead = (other > rows[e]) | ((other == rows[e]) & (c < a))
            rank = rank + jnp.where(ahead, 1, 0)
        picked.append((best_grp == g) & (rank < 2))
    raw = [jnp.where(picked[e], scores[e:e + 1, :], 0.0) for e in range(N_EXPERTS)]
    total = raw[0]
    for e in range(1, N_EXPERTS):
        total = total + raw[e]
    o_ref[...] = (jnp.concatenate(raw, axis=0) / total).T


def _moe_kernel(x_ref, gate_ref, wg_ref, wu_ref, wd_ref, g_ref, b_ref, o_ref, xb_ref, acc_ref):
    e = pl.program_id(1)

    @pl.when(e == 0)
    def _():
        xb_ref[...] = x_ref[...].astype(BF16)
        acc_ref[...] = jnp.zeros_like(acc_ref)

    xb = xb_ref[...]
    lane = lax.broadcasted_iota(jnp.int32, gate_ref.shape, 1)
    gcol = jnp.sum(jnp.where(lane == e, gate_ref[...], 0.0), axis=-1, keepdims=True)
    a = _dot(xb, wg_ref[0])
    h = a / (1.0 + jnp.exp(-a)) * _dot(xb, wu_ref[0])
    acc_ref[...] += gcol * _dot(h.astype(BF16), wd_ref[0])

    @pl.when(e == N_EXPERTS - 1)
    def _():
        o_ref[...] = _layer_norm(DEEPNORM_ALPHA * x_ref[...] + acc_ref[...], g_ref[...], b_ref[...])


def _moe_ln(x, router_w, router_b, w_gate, w_up, w_down, g, b):
    m, d = x.shape
    tm = 1024
    gates = pl.pallas_call(
        _router_kernel,
        out_shape=jax.ShapeDtypeStruct((m, N_EXPERTS), F32),
        grid=(m // tm,),
        in_specs=[pl.BlockSpec((tm, d), lambda i: (i, 0)),
                  pl.BlockSpec((N_EXPERTS, d), lambda i: (0, 0)),
                  pl.BlockSpec((N_EXPERTS, 1), lambda i: (0, 0))],
        out_specs=pl.BlockSpec((tm, N_EXPERTS), lambda i: (i, 0)),
        compiler_params=_params(("parallel",)),
        name="moe_router",
    )(x, router_w.T, router_b.reshape(N_EXPERTS, 1))
    de = w_gate.shape[-1]
    return pl.pallas_call(
        _moe_kernel,
        out_shape=jax.ShapeDtypeStruct((m, d), F32),
        grid=(m // tm, N_EXPERTS),
        in_specs=[pl.BlockSpec((tm, d), lambda i, e: (i, 0)),
                  pl.BlockSpec((tm, N_EXPERTS), lambda i, e: (i, 0)),
                  pl.BlockSpec((1, d, de), lambda i, e: (e, 0, 0)),
                  pl.BlockSpec((1, d, de), lambda i, e: (e, 0, 0)),
                  pl.BlockSpec((1, de, d), lambda i, e: (e, 0, 0)),
                  pl.BlockSpec((1, d), lambda i, e: (0, 0)),
                  pl.BlockSpec((1, d), lambda i, e: (0, 0))],
        out_specs=pl.BlockSpec((tm, d), lambda i, e: (i, 0)),
        scratch_shapes=[pltpu.VMEM((tm, d), BF16), pltpu.VMEM((tm, d), F32)],
        compiler_params=_params(("parallel", "arbitrary")),
        name="moe_experts_ln",
    )(x, gates, w_gate.astype(BF16), w_up.astype(BF16), w_down.astype(BF16), g.reshape(1, d), b.reshape(1, d))


def kernel(x, rel_table, router_w, router_b, ln1_g, ln1_b, ln2_g, ln2_b, exp_w_gate, exp_w_up, exp_w_down, dil_w_in, dil_w_out, sb_w_in, sb_w_out, nsa_w_in, nsa_w_out, nsa_cmp_pos_k, nsa_cmp_w1_k, nsa_cmp_w2_k, nsa_cmp_pos_v, nsa_cmp_w1_v, nsa_cmp_w2_v, moba_w_in, moba_w_out):
    bsz, seq, d = x.shape
    assert d == D_MODEL and seq % (DILATED_PAIRS[-1][1] * DIL_BLOCK) == 0
    depth = ln1_g.shape[0]
    n_mixers = 4
    table = rel_table.astype(F32)
    span = DIL_BLOCK
    big = 1 << 30
    bias_dil = _bias_tiles(table, len(DILATED_PAIRS), DIL_BLOCK, 2 * DIL_BLOCK, DIL_BLOCK, 0, 1, 0, span,
                           [dl for _, dl in DILATED_PAIRS])
    assert BIASED_TILES * TILE - (TILE - 1) >= BUCKET_THRESHOLDS[-1]
    bias_main = _bias_tiles(table, MAIN_TILES, TILE, TILE, 0, TILE, 1, 0, big, [1], shift=True, transposed=True,
                            scale=LOG2_E)
    bias_win = _bias_tiles(table, NSA_WINDOW // TILE + 1, TILE, TILE, 0, TILE, 1, 0, NSA_WINDOW - 1, [1],
                           transposed=True, scale=LOG2_E)
    bias_cmp = _bias_tiles(table, CMP_TILES, TILE, CMP_CHUNK, -TILE - (NSA_CMP_LEN - 1), TILE, NSA_CMP_STRIDE,
                           0, big, [1], scale=LOG2_E)
    xf = x.reshape(bsz * seq, d)
    for layer in range(depth):
        kind, occ = layer % n_mixers, layer // n_mixers
        merge = None
        if kind == 0:
            parts, w_out = _mixer_dilated(xf, dil_w_in[occ], dil_w_out[occ], bias_dil, bsz, seq)
            merge = 'lse3'
        elif kind == 1:
            parts, w_out = _mixer_stick_breaking(xf, sb_w_in[occ], sb_w_out[occ], bsz, seq)
        elif kind == 2:
            parts, w_out = _mixer_nsa(xf, nsa_w_in[occ], nsa_w_out[occ], (bias_main, bias_win, bias_cmp),
                                      nsa_cmp_pos_k[occ], nsa_cmp_w1_k[occ], nsa_cmp_w2_k[occ],
                                      nsa_cmp_pos_v[occ], nsa_cmp_w1_v[occ], nsa_cmp_w2_v[occ], bsz, seq)
            merge = 'sum3'
        else:
            parts, w_out = _mixer_moba(xf, moba_w_in[occ], moba_w_out[occ], bias_main, bsz, seq)
        xf = _outproj_ln(parts, w_out, xf, ln1_g[layer], ln1_b[layer], merge=merge)
        xf = _moe_ln(xf, router_w, router_b, exp_w_gate[layer], exp_w_up[layer], exp_w_down[layer],
                     ln2_g[layer], ln2_b[layer])
    return xf.reshape(bsz, seq, d)
```

```python
import functools
import math

import numpy as np
import jax
import jax.numpy as jnp
from jax import lax
from jax.experimental import pallas as pl
from jax.experimental.pallas import tpu as pltpu

F32 = jnp.float32
BF16 = jnp.bfloat16

D_MODEL = 1024
HEAD_DIM = 64
N_HEADS = 16
LANES = 128
PAIRS = N_HEADS // 2
ATTN_SCALE = HEAD_DIM ** -0.5
REL_BUCKETS = 32
REL_MAX_DIST = 2048
DILATED_PAIRS = ((128, 1), (512, 4), (2048, 16))
DIL_BLOCK = 128
SB_CLIP = 60.0
SB_TILE = 256
LOG2_E = math.log2(math.e)
NSA_KV_HEADS = 4
NSA_GROUP = N_HEADS // NSA_KV_HEADS
NSA_CMP_LEN = 32
NSA_CMP_STRIDE = 16
NSA_SEL_BLOCK = 64
NSA_TOP_N = 16
NSA_WINDOW = 512
MOBA_BLOCK = 256
MOBA_TOP_K = 3
N_EXPERTS = 16
N_GROUPS = 4
EXPERTS_PER_GROUP = N_EXPERTS // N_GROUPS
DEPTH = 4
DEEPNORM_ALPHA = (2 * DEPTH) ** 0.25
LN_EPS = 1e-5
NEG_INF = -1e30
FORCED_SCORE = 1e9
TILE = 256
BIASED_TILES = 7
MOBA_FAR_WIDTH = 4
MOBA_HEADS = 4
NSA_FAR_WIDTH = 4
MAIN_TILES = BIASED_TILES + max(MOBA_FAR_WIDTH, NSA_FAR_WIDTH) - 1
CMP_CHUNK = 128
CMP_TILES = 19
VMEM_LIMIT = 56 * 1024 * 1024


def _bucket_thresholds():
    n = np.arange(0, 2 * REL_MAX_DIST)
    exact = REL_BUCKETS // 2
    logf = np.log(np.maximum(n, 1).astype(np.float64) / exact) / math.log(REL_MAX_DIST / exact)
    large = np.minimum(exact + (logf * (REL_BUCKETS - exact)).astype(np.int64), REL_BUCKETS - 1)
    bucket = np.where(n < exact, n, large)
    return tuple(int(np.argmax(bucket >= k)) for k in range(1, REL_BUCKETS))


BUCKET_THRESHOLDS = _bucket_thresholds()


def _params(sem, vmem=VMEM_LIMIT):
    return pltpu.CompilerParams(dimension_semantics=sem, vmem_limit_bytes=vmem)


def _dot(a, b):
    return jnp.dot(a, b, preferred_element_type=F32)


def _dot_nt(a, b):
    return lax.dot_general(a, b, (((1,), (1,)), ((), ())), preferred_element_type=F32)


def _split_bf16(x):
    hi = x.astype(BF16)
    lo = (x - hi.astype(F32)).astype(BF16)
    return hi, lo


def _matmul_kernel(x_ref, w_ref, o_ref, *, act):
    y = _dot(x_ref[...].astype(BF16), w_ref[...])
    if act == 'sigmoid':
        y = 1.0 / (1.0 + jnp.exp(-y))
    o_ref[...] = y.astype(o_ref.dtype)


def _matmul(x, w, out_dtype, act=None):
    m, k = x.shape
    n = w.shape[1]
    tm = 1024
    tn = next(t for t in (1024, 512, 128) if n % t == 0)
    return pl.pallas_call(
        functools.partial(_matmul_kernel, act=act),
        out_shape=jax.ShapeDtypeStruct((m, n), out_dtype),
        grid=(m // tm, n // tn),
        in_specs=[pl.BlockSpec((tm, k), lambda i, j: (i, 0)),
                  pl.BlockSpec((k, tn), lambda i, j: (0, j))],
        out_specs=pl.BlockSpec((tm, tn), lambda i, j: (i, j)),
        compiler_params=_params(("parallel", "arbitrary")),
        name="proj_matmul",
    )(x, w)


def _matmul_t_kernel(x_ref, w_ref, o_ref):
    y = _dot_nt(w_ref[...], x_ref[...].astype(BF16))
    row = lax.broadcasted_iota(jnp.int32, y.shape, 0)
    o_ref[...] = jnp.where(row % LANES == HEAD_DIM, 1.0, y).astype(o_ref.dtype)


def _matmul_t(x, wt):
    m, k = x.shape
    n = wt.shape[0]
    tm = 512
    tn = 512 if n % 512 == 0 else 256
    return pl.pallas_call(
        _matmul_t_kernel,
        out_shape=jax.ShapeDtypeStruct((n, m), BF16),
        grid=(m // tm, n // tn),
        in_specs=[pl.BlockSpec((tm, k), lambda i, j: (i, 0)),
                  pl.BlockSpec((tn, k), lambda i, j: (j, 0))],
        out_specs=pl.BlockSpec((tn, tm), lambda i, j: (j, i)),
        compiler_params=_params(("parallel", "arbitrary")),
        name="proj_matmul_t",
    )(x, wt)


def _matmul_heads_kernel(x_ref, w_ref, o_ref):
    y = _dot(x_ref[...].astype(BF16), w_ref[...]).astype(o_ref.dtype)
    for j in range(o_ref.shape[1]):
        o_ref[0, j] = y[:, j * HEAD_DIM:(j + 1) * HEAD_DIM]


def _matmul_heads(x, w, bsz, seq):
    m, k = x.shape
    n = w.shape[1] // HEAD_DIM
    tm = 512
    per_b = seq // tm
    return pl.pallas_call(
        _matmul_heads_kernel,
        out_shape=jax.ShapeDtypeStruct((bsz, n, seq, HEAD_DIM), BF16),
        grid=(m // tm,),
        in_specs=[pl.BlockSpec((tm, k), lambda i: (i, 0)),
                  pl.BlockSpec((k, n * HEAD_DIM), lambda i: (0, 0))],
        out_specs=pl.BlockSpec((1, n, tm, HEAD_DIM), lambda i: (i // per_b, 0, i % per_b, 0)),
        compiler_params=_params(("parallel",)),
        name="proj_heads",
    )(x, w)


def _layer_norm(z, g, b):
    mu = jnp.mean(z, axis=-1, keepdims=True)
    zc = z - mu
    var = jnp.mean(zc * zc, axis=-1, keepdims=True)
    return zc * lax.rsqrt(var + LN_EPS) * g + b


def _outproj_kernel(*refs, n_in, merge):
    ins = refs[:n_in]
    w_ref, x_ref, g_ref, b_ref, o_ref = refs[n_in:]
    if merge == 'lse3':
        o1, o2, o3, l1, l2, l3 = [r[...] for r in ins]
        mx = jnp.maximum(jnp.maximum(l1, l2), l3)
        e1, e2, e3 = jnp.exp(l1 - mx), jnp.exp(l2 - mx), jnp.exp(l3 - mx)
        a = (e1 * o1 + e2 * o2 + e3 * o3) / (e1 + e2 + e3)
    elif merge == 'sum3':
        a = ins[0][...].astype(F32) + ins[1][...].astype(F32) + ins[2][...].astype(F32)
    else:
        a = ins[0][...]
    y = _dot(a.astype(BF16), w_ref[...])
    o_ref[...] = _layer_norm(DEEPNORM_ALPHA * x_ref[...] + y, g_ref[...], b_ref[...])


def _outproj_ln(ins, w, x, g, b, merge=None):
    m, d = x.shape
    ka = ins[0].shape[1]
    tm = 256
    n_in = len(ins)
    return pl.pallas_call(
        functools.partial(_outproj_kernel, n_in=n_in, merge=merge),
        out_shape=jax.ShapeDtypeStruct((m, d), F32),
        grid=(m // tm,),
        in_specs=[pl.BlockSpec((tm, ka), lambda i: (i, 0)) for _ in ins]
        + [pl.BlockSpec((ka, d), lambda i: (0, 0)),
           pl.BlockSpec((tm, d), lambda i: (i, 0)),
           pl.BlockSpec((1, d), lambda i: (0, 0)),
           pl.BlockSpec((1, d), lambda i: (0, 0))],
        out_specs=pl.BlockSpec((tm, d), lambda i: (i, 0)),
        compiler_params=_params(("parallel",)),
        name="outproj_ln",
    )(*ins, w, x, g.reshape(1, d), b.reshape(1, d))


def _bias_kernel(tbl_ref, o_ref, *, rows, cols, off0, tstride, cmul, lo, hi, mults, shift, transposed, scale):
    t = pl.program_id(0)
    if len(set(mults)) == 1:
        mult = mults[0]
    else:
        mult = jnp.int32(mults[-1])
        for idx in range(len(mults) - 2, -1, -1):
            mult = jnp.where(t == idx, jnp.int32(mults[idx]), mult)
    base = off0 + tstride * t

    def strip(i, carry):
        r0 = pl.multiple_of(i * 8, 8)
        a = lax.broadcasted_iota(jnp.int32, (8, cols), 0) + r0
        c = lax.broadcasted_iota(jnp.int32, (8, cols), 1)
        steps = base + c - cmul * a if transposed else base + a - cmul * c
        valid = (steps >= lo) & (steps <= hi)
        dist = steps * mult
        for h in range(N_HEADS):
            acc = jnp.full((8, cols), tbl_ref[0, h], F32)
            for k, thr in enumerate(BUCKET_THRESHOLDS):
                acc = jnp.where(dist >= thr, tbl_ref[k + 1, h], acc)
            if shift:
                acc = acc - tbl_ref[REL_BUCKETS - 1, h]
            if scale != 1.0:
                acc = acc * scale
            o_ref[h, 0, pl.ds(r0, 8), :] = jnp.where(valid, acc, NEG_INF)
        return carry

    lax.fori_loop(0, rows // 8, strip, 0)


def _bias_tiles(table, n_tiles, rows, cols, off0, tstride, cmul, lo, hi, mults, shift=False, transposed=False,
                scale=1.0):
    return pl.pallas_call(
        functools.partial(_bias_kernel, rows=rows, cols=cols, off0=off0, tstride=tstride, cmul=cmul,
                          lo=lo, hi=hi, mults=tuple(mults), shift=shift, transposed=transposed, scale=scale),
        out_shape=jax.ShapeDtypeStruct((N_HEADS, n_tiles, rows, cols), F32),
        grid=(n_tiles,),
        in_specs=[pl.BlockSpec(memory_space=pltpu.SMEM)],
        out_specs=pl.BlockSpec((N_HEADS, 1, rows, cols), lambda t: (0, t, 0, 0)),
        compiler_params=_params(("arbitrary",)),
        name="bias_tiles",
    )(table)


def _dil_kernel(q_ref, kp_ref, kc_ref, vp_ref, vc_ref, bias_ref, o_ref, lse_ref,
                q_scr, kp_scr, kc_scr, vp_scr, vc_scr, *, dilation, per_step):
    j = pl.program_id(2)
    d = dilation
    n_cur = per_step * DIL_BLOCK

    def rows_of(ref, r, n, lead=()):
        if d == 1:
            return ref[lead + (slice(0, n), slice(None))]
        return ref[lead + (pl.ds(r, n, stride=d), slice(None))]

    if d > 1:
        for h in range(2):
            q_scr[h] = q_ref[0, :, h * LANES:(h + 1) * LANES].astype(F32)
        kp_scr[...] = kp_ref[0].astype(F32)
        kc_scr[...] = kc_ref[0].astype(F32)
        vp_scr[...] = vp_ref[0].astype(F32)
        vc_scr[...] = vc_ref[0].astype(F32)
    colk = lax.broadcasted_iota(jnp.int32, (DIL_BLOCK, 2 * DIL_BLOCK), 1)
    first = jnp.where((colk < DIL_BLOCK) & (j == 0), NEG_INF, 0.0)
    low = lax.broadcasted_iota(jnp.int32, (DIL_BLOCK, LANES), 1) < HEAD_DIM
    for r in range(d):
        if d == 1:
            q_heads = [q_ref[0, :, h * LANES:(h + 1) * LANES] for h in range(2)]
            k_all = jnp.concatenate([kp_ref[0], kc_ref[0]], axis=0)
            v_all = jnp.concatenate([vp_ref[0], vc_ref[0]], axis=0)
        else:
            q_heads = [rows_of(q_scr, r, n_cur, (h,)).astype(BF16) for h in range(2)]
            k_all = jnp.concatenate([rows_of(kp_scr, r, DIL_BLOCK), rows_of(kc_scr, r, n_cur)], axis=0).astype(BF16)
            v_all = jnp.concatenate([rows_of(vp_scr, r, DIL_BLOCK), rows_of(vc_scr, r, n_cur)], axis=0).astype(BF16)
        for u in range(per_step):
            k2 = k_all[u * DIL_BLOCK:(u + 2) * DIL_BLOCK]
            v2 = v_all[u * DIL_BLOCK:(u + 2) * DIL_BLOCK]
            outs, lses = [], []
            for h in range(2):
                s = _dot_nt(q_heads[h][u * DIL_BLOCK:(u + 1) * DIL_BLOCK], k2) + bias_ref[h, 0]
                if u == 0:
                    s = s + first
                m = jnp.max(s, axis=-1, keepdims=True)
                p = jnp.exp(s - m)
                l = jnp.maximum(jnp.sum(p, axis=-1, keepdims=True), 1e-30)
                outs.append(_dot(p.astype(BF16), v2) / l)
                lses.append(m + jnp.log(l))
            if d == 1:
                dst = (0, slice(u * DIL_BLOCK, (u + 1) * DIL_BLOCK), slice(None))
            else:
                dst = (0, pl.ds(u * DIL_BLOCK * d + r, DIL_BLOCK, stride=d), slice(None))
            o_ref[dst] = jnp.where(low, outs[0], outs[1])
            lse_ref[dst] = jnp.where(low, lses[0], lses[1])


def _dilated_group(proj, bias, g, dilation, bsz, seq):
    prev_rows = DIL_BLOCK * dilation
    step_rows = math.gcd(seq, 16 * DIL_BLOCK)
    per_step = step_rows // prev_rows
    assert per_step >= 1 and seq % step_rows == 0
    q_blk = g * N_HEADS // 2
    k_blk = 3 * N_HEADS + g * 2 * PAIRS
    v_blk = k_blk + PAIRS

    def kv_specs(base):
        prev = pl.BlockSpec((1, prev_rows, LANES), lambda hp, b, j: (b, jnp.maximum(per_step * j - 1, 0), base + hp))
        cur = pl.BlockSpec((1, step_rows, LANES), lambda hp, b, j: (b, j, base + hp))
        return [prev, cur]

    out_spec = pl.BlockSpec((1, step_rows, LANES), lambda hp, b, j: (b, j, hp))
    staged = 8 if dilation == 1 else None
    o, lse = pl.pallas_call(
        functools.partial(_dil_kernel, dilation=dilation, per_step=per_step),
        out_shape=[jax.ShapeDtypeStruct((bsz, seq, D_MODEL), F32)] * 2,
        grid=(PAIRS, bsz, seq // step_rows),
        in_specs=[pl.BlockSpec((1, step_rows, 2 * LANES), lambda hp, b, j: (b, j, q_blk + hp))]
        + kv_specs(k_blk) + kv_specs(v_blk)
        + [pl.BlockSpec((2, 1, DIL_BLOCK, 2 * DIL_BLOCK), lambda hp, b, j: (hp, g, 0, 0))],
        out_specs=[out_spec, out_spec],
        scratch_shapes=[pltpu.VMEM((2, staged or step_rows, LANES), F32),
                        pltpu.VMEM((staged or prev_rows, LANES), F32), pltpu.VMEM((staged or step_rows, LANES), F32),
                        pltpu.VMEM((staged or prev_rows, LANES), F32), pltpu.VMEM((staged or step_rows, LANES), F32)],
        compiler_params=_params(("parallel", "parallel", "arbitrary")),
        name="dilated_attn",
    )(proj, proj, proj, proj, proj, bias)
    return o.reshape(bsz * seq, D_MODEL), lse.reshape(bsz * seq, D_MODEL)


def _pad_heads_cols(w, parity_of_head):
    k, n = w.shape
    nh = n // HEAD_DIM
    wh = w.reshape(k, nh, HEAD_DIM)
    z = jnp.zeros_like(wh)
    par = jnp.asarray([parity_of_head(h) for h in range(nh)], jnp.int32)[None, :, None]
    lo = jnp.where(par == 0, wh, z)
    hi = jnp.where(par == 1, wh, z)
    return jnp.concatenate([lo, hi], axis=-1).reshape(k, nh * LANES)


def _pad_heads_rows(w, parity_of_head):
    return _pad_heads_cols(w.T, parity_of_head).T


def _mixer_dilated(x, w_in, w_out, bias_dil, bsz, seq):
    aw = N_HEADS * HEAD_DIM
    par = lambda h: h % 2
    qs, kvs = [], []
    for g in range(len(DILATED_PAIRS)):
        base = g * 3 * aw
        qs.append(_pad_heads_cols(w_in[:, base:base + aw] * ATTN_SCALE, par))
        kvs.append(w_in[:, base + aw:base + 3 * aw])
    w_all = jnp.concatenate(qs + kvs, axis=1).astype(BF16)
    proj = _matmul(x, w_all, BF16).reshape(bsz, seq, -1)
    parts = [_dilated_group(proj, bias_dil, g, d, bsz, seq) for g, (_, d) in enumerate(DILATED_PAIRS)]
    return [p[0] for p in parts] + [p[1] for p in parts], w_out.astype(BF16)


def _sb_kernel(q_ref, k_ref, v_ref, o_ref):
    i = pl.program_id(2)
    t = SB_TILE
    row = lax.broadcasted_iota(jnp.int32, (t, t), 0)
    col = lax.broadcasted_iota(jnp.int32, (t, t), 1)
    suffix = jnp.where(row >= col, 1.0, 0.0).astype(BF16)
    strict = col < row
    q2 = q_ref[0]
    q_heads = [q2[:, h * LANES:(h + 1) * LANES] for h in range(2)]

    def block(qh, kj, masked):
        start = pl.multiple_of(kj * t, t)
        kb = k_ref[0, pl.ds(start, t), :]
        vb = v_ref[0, pl.ds(start, t), :]
        a = jnp.clip(_dot_nt(qh, kb), -SB_CLIP, SB_CLIP) * LOG2_E
        sp = jnp.log2(1.0 + jnp.exp2(a))
        if masked:
            sp = jnp.where(strict, sp, 0.0)
        hi, lo = _split_bf16(sp)
        rr = _dot(hi, suffix) + _dot(lo, suffix)
        att = jnp.exp2(a - rr)
        if masked:
            att = jnp.where(strict, att, 0.0)
        return _dot(att.astype(BF16), vb), rr[:, 0:1]

    def first_two(kj, carry):
        has_prev = jnp.where(kj > 0, 1.0, 0.0)
        out = []
        for qh in q_heads:
            acc_d, r_d = block(qh, kj, True)
            pv, r_p = block(qh, jnp.maximum(kj - 1, 0), False)
            out.append((acc_d + (has_prev * jnp.exp2(-r_d)) * pv, r_d + has_prev * r_p))
        return tuple(out)

    zero = (jnp.zeros((t, LANES), F32), jnp.zeros((t, 1), F32))
    (acc0, r0), (acc1, r1) = lax.fori_loop(i, i + 1, first_two, (zero, zero))

    def cond(carry):
        n, alive = carry[0], carry[1]
        return (n < i - 1) & alive

    def step(carry):
        n = carry[0]
        new, tops = [], []
        for qh, (r_sum, w, acc) in zip(q_heads, carry[2:]):
            pv, r_blk = block(qh, i - 2 - n, False)
            r_new = r_sum + r_blk
            w_new = jnp.exp2(-r_new)
            new.append((r_new, w_new, acc + w * pv))
            tops.append(jnp.max(w_new, axis=0, keepdims=True))
        alive = jnp.maximum(tops[0], tops[1])[0, 0] > 0.0
        return (n + 1, alive) + tuple(new)

    w0, w1 = jnp.exp2(-r0), jnp.exp2(-r1)
    alive0 = jnp.maximum(jnp.max(w0, axis=0, keepdims=True), jnp.max(w1, axis=0, keepdims=True))[0, 0] > 0.0
    init = (jnp.int32(0), alive0, (r0, w0, acc0), (r1, w1, acc1))
    res = lax.while_loop(cond, step, init)
    lane = lax.broadcasted_iota(jnp.int32, (t, LANES), 1)
    o_ref[0] = jnp.where(lane < HEAD_DIM, res[2][2], res[3][2]).astype(o_ref.dtype)


def _mixer_stick_breaking(x, w_in, w_out, bsz, seq):
    aw = N_HEADS * HEAD_DIM
    wq = _pad_heads_cols(w_in[:, :aw] * ATTN_SCALE, lambda h: h % 2)
    w_all = jnp.concatenate([wq, w_in[:, aw:]], axis=1).astype(BF16)
    proj = _matmul(x, w_all, BF16).reshape(bsz, seq, -1)
    kb = N_HEADS
    o = pl.pallas_call(
        _sb_kernel,
        out_shape=jax.ShapeDtypeStruct((bsz, seq, D_MODEL), BF16),
        grid=(bsz, PAIRS, seq // SB_TILE),
        in_specs=[pl.BlockSpec((1, SB_TILE, 2 * LANES), lambda b, hp, i: (b, i, hp)),
                  pl.BlockSpec((1, seq, LANES), lambda b, hp, i: (b, 0, kb + hp)),
                  pl.BlockSpec((1, seq, LANES), lambda b, hp, i: (b, 0, kb + PAIRS + hp))],
        out_specs=pl.BlockSpec((1, SB_TILE, LANES), lambda b, hp, i: (b, i, hp)),
        compiler_params=_params(("parallel", "parallel", "arbitrary")),
        name="stick_breaking_attn",
    )(proj, proj, proj)
    return [o.reshape(bsz * seq, D_MODEL)], w_out.astype(BF16)


def _flash_chunks(q_heads, k_ref, k_cols, v_ref, v_rows, bias_ref, mask_rows, lo, hi, i, blocks_per_chunk, n_tiles,
                  fold_mask=False, pipelined=False, far_width=1, near_width=1):
    nh = len(q_heads)

    def logits(ch, near, width):
        rows = width * TILE
        start = pl.multiple_of(ch * TILE, TILE)
        kbs = {c: k_ref[0, pl.ds(start, rows), c:c + LANES] for c in sorted(set(k_cols))}
        add = None
        if fold_mask:
            key = lax.broadcasted_iota(jnp.int32, (rows, LANES), 0)
            blk = lax.broadcasted_iota(jnp.int32, (rows, LANES), 1)
            blk_of_key = ch * blocks_per_chunk + key // (TILE // blocks_per_chunk)
            ind = jnp.where(blk == blk_of_key, NEG_INF, 0.0).astype(BF16)
            kbs = {c: jnp.concatenate([kb, ind], axis=1) for c, kb in kbs.items()}
        elif mask_rows is not None:
            nb = width * blocks_per_chunk
            per = TILE // blocks_per_chunk
            blk_rows = mask_rows(ch * blocks_per_chunk, nb)
            add = jnp.concatenate([jnp.broadcast_to(blk_rows[r:r + 1, :], (per, TILE)) for r in range(nb)], axis=0)
        out = []
        for h in range(nh):
            s = _dot_nt(kbs[k_cols[h]], q_heads[h])
            if near:
                tiles = [bias_ref[h, jnp.maximum(i - ch - w, 0)] for w in range(width)]
                s = s + (tiles[0] if width == 1 else jnp.concatenate(tiles, axis=0))
            if add is not None:
                s = s + add
            out.append(s)
        return tuple(out)

    def update(ch, s_heads, state, width):
        start = pl.multiple_of(ch * TILE, TILE)
        vts = {}
        new = []
        for h in range(nh):
            r0 = v_rows[h]
            if r0 not in vts:
                vts[r0] = v_ref[r0:r0 + LANES, pl.ds(start, width * TILE)]
            m, acc = state[h]
            s = s_heads[h]
            m_new = jnp.maximum(m, jnp.max(s, axis=0, keepdims=True))
            alpha = jnp.exp2(m - m_new)
            p = jnp.exp2(s - m_new)
            acc = alpha * acc + _dot(vts[r0], p.astype(BF16))
            new.append((m_new, acc))
        return tuple(new)

    def run(ch0, steps, near, width, state, pipelined=pipelined):
        if not pipelined:
            return lax.fori_loop(
                0, steps, lambda t, st: update(ch0 + t * width, logits(ch0 + t * width, near, width), st, width), state)

        def body(t, carry):
            s_cur, st = carry
            s_next = logits(ch0 + jnp.minimum(t + 1, steps - 1) * width, near, width)
            return s_next, update(ch0 + t * width, s_cur, st, width)

        zeros = tuple(jnp.zeros((width * TILE, TILE), F32) for _ in range(nh))
        first = jnp.minimum(ch0, i)
        s0 = lax.fori_loop(first, first + 1, lambda c, _: logits(c, near, width), zeros)
        return lax.fori_loop(0, steps, body, (s0, state))[1]

    state = tuple((jnp.full((1, TILE), NEG_INF, F32), jnp.zeros((LANES, TILE), F32)) for _ in range(nh))
    n_near = n_tiles - (far_width - 1)
    far_steps = jnp.maximum(hi - n_near - lo, 0) // far_width
    split = lo + far_steps * far_width
    state = run(lo, far_steps, False, far_width, state)
    near_pairs = (hi - split) // near_width if near_width > 1 else 0
    if near_width > 1:
        state = run(split, near_pairs, True, near_width, state)
    single = split + near_pairs * near_width
    state = run(single, hi - single, True, 1, state, pipelined=pipelined and near_width == 1)
    outs = []
    for _, acc in state:
        o = (acc / acc[HEAD_DIM:HEAD_DIM + 1, :]).T
        lane = lax.broadcasted_iota(jnp.int32, o.shape, 1)
        outs.append(jnp.where(lane < HEAD_DIM, o, 0.0))
    return outs


def _kmean_kernel(k_ref, o_ref):
    rows = k_ref.shape[1]
    kf = k_ref[0].astype(F32).reshape(rows // MOBA_BLOCK, MOBA_BLOCK, k_ref.shape[2])
    o_ref[0] = jnp.sum(kf, axis=1) * (1.0 / MOBA_BLOCK)


def _moba_kernel(q_ref, k_ref, v_ref, km_ref, bias_ref, o_ref):
    i = pl.program_id(2)
    nblk = km_ref.shape[1]
    km_hi, km_lo = _split_bf16(km_ref[0])
    qs = q_ref[0]
    blk = lax.broadcasted_iota(jnp.int32, (nblk, TILE), 0)
    past = blk < i
    q_heads = []
    for h in range(MOBA_HEADS):
        qh = qs[:, h * LANES:(h + 1) * LANES]
        pair = slice((h // 2) * LANES, (h // 2 + 1) * LANES)
        gate = jnp.where(past, _dot_nt(km_hi[:, pair], qh) + _dot_nt(km_lo[:, pair], qh), NEG_INF)
        allowed = blk == i
        for _ in range(MOBA_TOP_K):
            mx = jnp.max(gate, axis=0, keepdims=True)
            first = jnp.min(jnp.where(gate == mx, blk, nblk), axis=0, keepdims=True)
            pick = blk == first
            allowed = allowed | (pick & past)
            gate = jnp.where(pick, -jnp.inf, gate)
        nsel = jnp.where(allowed, 0.0, 1.0).T.astype(BF16)
        q_heads.append(jnp.concatenate([qh, nsel], axis=1))
    outs = _flash_chunks(q_heads, k_ref, [(h // 2) * LANES for h in range(MOBA_HEADS)],
                         v_ref, [h * LANES for h in range(MOBA_HEADS)], bias_ref, None, 0, i + 1, i, 1,
                         BIASED_TILES + MOBA_FAR_WIDTH - 1, fold_mask=True, pipelined=False, far_width=MOBA_FAR_WIDTH,
                         near_width=2)
    for h in range(MOBA_HEADS):
        o_ref[0, :, h * LANES:(h + 1) * LANES] = outs[h].astype(o_ref.dtype)


def _mixer_moba(x, w_in, w_out, bias_main, bsz, seq):
    aw = N_HEADS * HEAD_DIM
    nblk = seq // MOBA_BLOCK
    assert nblk - 1 >= MOBA_TOP_K
    wq = _pad_heads_cols(w_in[:, :aw] * (ATTN_SCALE * LOG2_E), lambda h: h % 2)
    w_all = jnp.concatenate([wq, w_in[:, aw:2 * aw]], axis=1).astype(BF16)
    proj = _matmul(x, w_all, BF16).reshape(bsz, seq, -1)
    v_t = _matmul_t(x, _pad_heads_cols(w_in[:, 2 * aw:], lambda h: 0).T.astype(BF16))
    kb = N_HEADS
    rows = 8 * MOBA_BLOCK
    kmean = pl.pallas_call(
        _kmean_kernel,
        out_shape=jax.ShapeDtypeStruct((bsz, nblk, D_MODEL), F32),
        grid=(bsz, seq // rows),
        in_specs=[pl.BlockSpec((1, rows, D_MODEL), lambda b, i: (b, i, kb * LANES // D_MODEL))],
        out_specs=pl.BlockSpec((1, 8, D_MODEL), lambda b, i: (b, i, 0)),
        compiler_params=_params(("parallel", "parallel")),
        name="moba_kmean",
    )(proj)
    assert nblk <= LANES
    kmean = jnp.pad(kmean, ((0, 0), (0, LANES - nblk), (0, 0)))
    nh = MOBA_HEADS
    kw = nh // 2 * LANES
    once = pl.Buffered(1)
    o = pl.pallas_call(
        _moba_kernel,
        out_shape=jax.ShapeDtypeStruct((bsz, seq, N_HEADS * LANES), BF16),
        grid=(bsz, N_HEADS // nh, seq // TILE),
        in_specs=[pl.BlockSpec((1, TILE, nh * LANES), lambda b, hq, i: (b, i, hq)),
                  pl.BlockSpec((1, seq, kw), lambda b, hq, i: (b, 0, kb * LANES // kw + hq), pipeline_mode=once),
                  pl.BlockSpec((nh * LANES, seq), lambda b, hq, i: (hq, b), pipeline_mode=once),
                  pl.BlockSpec((1, LANES, kw), lambda b, hq, i: (b, 0, hq)),
                  pl.BlockSpec((nh, MAIN_TILES, TILE, TILE), lambda b, hq, i: (hq, 0, 0, 0), pipeline_mode=once)],
        out_specs=pl.BlockSpec((1, TILE, nh * LANES), lambda b, hq, i: (b, i, hq)),
        compiler_params=_params(("parallel", "parallel", "arbitrary")),
        name="moba_attn",
    )(proj, proj, v_t, kmean, bias_main)
    return [o.reshape(bsz * seq, N_HEADS * LANES)], _pad_heads_rows(w_out, lambda h: 0).astype(BF16)


def _gelu_tanh(x):
    return 0.5 * x * (1.0 + jnp.tanh(math.sqrt(2.0 / math.pi) * (x + 0.044715 * (x * x * x))))


def _compress_kernel(a_ref, pos_ref, w1_ref, w2_ref, o_ref, *, transpose_out):
    nc = a_ref.shape[2]
    half = NSA_CMP_STRIDE * HEAD_DIM
    a = a_ref[0, 0].astype(F32)
    top = _dot((a + pos_ref[0:1, :]).astype(BF16), w1_ref[0:half, :])
    bot = _dot((a + pos_ref[1:2, :]).astype(BF16), w1_ref[half:2 * half, :])
    hid = top + pltpu.roll(bot, nc - 1, 0)
    out = _dot(_gelu_tanh(hid).astype(BF16), w2_ref[0])
    rowi = lax.broadcasted_iota(jnp.int32, out.shape, 0)
    out = jnp.where(rowi < nc - 1, out, 0.0)
    o_ref[0, 0] = (out.T if transpose_out else out).astype(o_ref.dtype)


def _compress(tok, first, pos, w1, w2, bsz, seq, by_parity):
    nc = seq // NSA_CMP_STRIDE
    half = NSA_CMP_STRIDE * HEAD_DIM
    pos2 = pos.reshape(2, half)
    z = jnp.zeros_like(w2)
    low = jnp.concatenate([w2, z], axis=1)
    w2p = jnp.stack([low, jnp.concatenate([z, w2], axis=1) if by_parity else low]).astype(BF16)
    out_dims = (nc, LANES) if by_parity else (LANES, nc)
    return pl.pallas_call(
        functools.partial(_compress_kernel, transpose_out=not by_parity),
        out_shape=jax.ShapeDtypeStruct((bsz, NSA_KV_HEADS) + out_dims, BF16),
        grid=(bsz, NSA_KV_HEADS),
        in_specs=[pl.BlockSpec((1, 1, nc, half), lambda b, g: (b, first + g, 0, 0)),
                  pl.BlockSpec((2, half), lambda b, g: (0, 0)),
                  pl.BlockSpec((2 * half, w1.shape[1]), lambda b, g: (0, 0)),
                  pl.BlockSpec((1, w2.shape[0], LANES), lambda b, g: (g % 2, 0, 0))],
        out_specs=pl.BlockSpec((1, 1) + out_dims, lambda b, g: (b, g, 0, 0)),
        compiler_params=_params(("parallel", "parallel")),
        name="nsa_compress",
    )(tok, pos2, w1.astype(BF16), w2p)


def _gate_columns(gate_ref, branch):
    tile = gate_ref[...]
    lane = lax.broadcasted_iota(jnp.int32, tile.shape, 1)
    base = branch * N_HEADS + pl.program_id(1) * NSA_GROUP
    return [jnp.sum(jnp.where(lane == base + r, tile, 0.0), axis=-1, keepdims=True) for r in range(NSA_GROUP)]


def _nsa_cmp_kernel(q_ref, kc_ref, vc_ref, bias_ref, gate_ref, o_ref, sel_ref, *, n_top):
    i = pl.program_id(2)
    nc = kc_ref.shape[2]
    n_chunks = nc // CMP_CHUNK
    nsb = sel_ref.shape[2]
    q4 = q_ref[0]
    gates = _gate_columns(gate_ref, 0)
    keep = lax.broadcasted_iota(jnp.int32, (TILE, LANES), 1) < HEAD_DIM
    psum = [jnp.zeros((CMP_CHUNK, TILE), F32) for _ in range(n_chunks)]
    for r in range(NSA_GROUP):
        qh = q4[:, r * LANES:(r + 1) * LANES]
        ss = []
        for c in range(n_chunks):
            tile = jnp.clip(i - (CMP_CHUNK * NSA_CMP_STRIDE // TILE) * c, -1, CMP_TILES - 2) + 1
            ss.append(_dot_nt(kc_ref[0, 0, c * CMP_CHUNK:(c + 1) * CMP_CHUNK, :], qh) + bias_ref[r, tile])
        m = ss[0].max(axis=0, keepdims=True)
        for c in range(1, n_chunks):
            m = jnp.maximum(m, ss[c].max(axis=0, keepdims=True))
        ps = [jnp.exp2(s - m) for s in ss]
        l = ps[0].sum(axis=0, keepdims=True)
        for c in range(1, n_chunks):
            l = l + ps[c].sum(axis=0, keepdims=True)
        inv = jnp.where(m > 0.5 * NEG_INF, 1.0 / jnp.maximum(l, 1e-30), 0.0)
        acc = jnp.zeros((LANES, TILE), F32)
        for c in range(n_chunks):
            pc = ps[c] * inv
            psum[c] = psum[c] + pc
            acc = acc + _dot(vc_ref[0, 0, :, c * CMP_CHUNK:(c + 1) * CMP_CHUNK], pc.astype(BF16))
        o_ref[0, :, r * LANES:(r + 1) * LANES] = jnp.where(keep, acc.T * gates[r], 0.0).astype(o_ref.dtype)
    imp = jnp.zeros((nsb, TILE), F32)
    per_sel = NSA_SEL_BLOCK // NSA_CMP_STRIDE
    for c in range(n_chunks):
        jb = lax.broadcasted_iota(jnp.int32, (nsb, CMP_CHUNK), 0)
        ci = lax.broadcasted_iota(jnp.int32, (nsb, CMP_CHUNK), 1) + c * CMP_CHUNK
        rel = ci - per_sel * jb
        over = (rel >= 1 - NSA_CMP_LEN // NSA_CMP_STRIDE) & (rel < per_sel) & (ci < nc - 1)
        ov = jnp.where(over, 1.0, 0.0).astype(BF16)
        hi, lo = _split_bf16(psum[c])
        imp = imp + _dot(ov, hi) + _dot(ov, lo)
    jb = lax.broadcasted_iota(jnp.int32, (nsb, TILE), 0)
    qpos = lax.broadcasted_iota(jnp.int32, (nsb, TILE), 1) + i * TILE
    cur = qpos // NSA_SEL_BLOCK
    forced = (jb == 0) | (jb == cur) | (jb == cur - 1)
    causal = jb * NSA_SEL_BLOCK <= qpos
    score = jnp.where(forced, FORCED_SCORE, jnp.where(causal, imp, NEG_INF))
    chosen = jnp.zeros((nsb, TILE), F32)
    for _ in range(n_top):
        mx = jnp.max(score, axis=0, keepdims=True)
        first = jnp.min(jnp.where(score == mx, jb, nsb), axis=0, keepdims=True)
        pick = jb == first
        chosen = jnp.where(pick, 1.0, chosen)
        score = jnp.where(pick, -jnp.inf, score)
    sel_ref[0, 0] = jnp.where((chosen > 0.0) & causal, 0.0, NEG_INF)


def _nsa_sparse_kernel(q_ref, k_ref, v_ref, bias_ref, gate_ref, *rest, window):
    if window:
        (o_ref,) = rest
        mask_rows = None
    else:
        sel_ref, o_ref = rest

        def mask_rows(b0, n):
            if n % 8 == 0:
                return sel_ref[0, 0, pl.ds(pl.multiple_of(b0, 8), n), :]
            assert n == 4
            rows = sel_ref[0, 0, pl.ds(pl.multiple_of(b0 // 8 * 8, 8), 8), :]
            return jnp.where(b0 % 8 == 0, rows[0:4], rows[4:8])

    i = pl.program_id(2)
    q4 = q_ref[0]
    gates = _gate_columns(gate_ref, 2 if window else 1)
    q_heads = [q4[:, r * LANES:(r + 1) * LANES] for r in range(NSA_GROUP)]
    if window:
        n_tiles = NSA_WINDOW // TILE + 1
        lo = jnp.maximum(i - (n_tiles - 1), 0)
        outs = _flash_chunks(q_heads, k_ref, [0] * NSA_GROUP, v_ref, [0] * NSA_GROUP, bias_ref, None,
                             lo, i + 1, i, 1, n_tiles, near_width=2)
    else:
        outs = _flash_chunks(q_heads, k_ref, [0] * NSA_GROUP, v_ref, [0] * NSA_GROUP, bias_ref, mask_rows,
                             0, i + 1, i,
                             TILE // NSA_SEL_BLOCK, BIASED_TILES + NSA_FAR_WIDTH - 1, far_width=NSA_FAR_WIDTH,
                             near_width=2)
    for r in range(NSA_GROUP):
        o_ref[0, :, r * LANES:(r + 1) * LANES] = (outs[r] * gates[r]).astype(o_ref.dtype)


def _mixer_nsa(x, w_in, w_out, table_bias, pos_k, w1_k, w2_k, pos_v, w1_v, w2_v, bsz, seq):
    bias_main, bias_win, bias_cmp = table_bias
    aw = N_HEADS * HEAD_DIM
    kvw = NSA_KV_HEADS * HEAD_DIM
    G, R = NSA_KV_HEADS, NSA_GROUP
    nc = seq // NSA_CMP_STRIDE
    nsb = seq // NSA_SEL_BLOCK
    n_top = min(NSA_TOP_N, nsb)
    nq = seq // TILE
    assert nc % CMP_CHUNK == 0
    par = lambda h: (h // R) % 2
    wq = _pad_heads_cols(w_in[:, :aw] * (ATTN_SCALE * LOG2_E), par).astype(BF16)
    w_cmp = w_in[:, aw:aw + 2 * kvw].astype(BF16)
    c0 = aw + 2 * kvw
    wk = jnp.concatenate([w_in[:, c0:c0 + kvw], w_in[:, c0 + 2 * kvw:c0 + 3 * kvw]], axis=1).astype(BF16)
    wv = jnp.concatenate([w_in[:, c0 + kvw:c0 + 2 * kvw], w_in[:, c0 + 3 * kvw:c0 + 4 * kvw]], axis=1)
    wv_t = _pad_heads_cols(wv, lambda h: 0).T.astype(BF16)
    wg = jnp.pad(w_in[:, aw + 6 * kvw:], ((0, 0), (0, LANES - 3 * N_HEADS))).astype(BF16)
    q = _matmul(x, wq, BF16).reshape(bsz, seq, N_HEADS * LANES)
    kk = _matmul(x, wk, BF16).reshape(bsz, seq, 2 * kvw)
    v_t = _matmul_t(x, wv_t)
    gates = _matmul(x, wg, F32, act='sigmoid')
    tok = _matmul_heads(x, w_cmp, bsz, seq).reshape(bsz, 2 * G, nc, NSA_CMP_STRIDE * HEAD_DIM)
    kc = _compress(tok, 0, pos_k, w1_k, w2_k, bsz, seq, True)
    vc = _compress(tok, G, pos_v, w1_v, w2_v, bsz, seq, False)

    q_spec = pl.BlockSpec((1, TILE, R * LANES), lambda b, g, i: (b, i, g))
    o_spec = pl.BlockSpec((1, TILE, R * LANES), lambda b, g, i: (b, i, g))
    gate_spec = pl.BlockSpec((TILE, LANES), lambda b, g, i: (b * nq + i, 0))
    o_shape = jax.ShapeDtypeStruct((bsz, seq, N_HEADS * LANES), BF16)
    sem = ("parallel", "parallel", "arbitrary")

    o_cmp, nsel = pl.pallas_call(
        functools.partial(_nsa_cmp_kernel, n_top=n_top),
        out_shape=[o_shape, jax.ShapeDtypeStruct((bsz, G, nsb, seq), F32)],
        grid=(bsz, G, nq),
        in_specs=[q_spec,
                  pl.BlockSpec((1, 1, nc, LANES), lambda b, g, i: (b, g, 0, 0)),
                  pl.BlockSpec((1, 1, LANES, nc), lambda b, g, i: (b, g, 0, 0)),
                  pl.BlockSpec((R, CMP_TILES, CMP_CHUNK, TILE), lambda b, g, i: (g, 0, 0, 0)),
                  gate_spec],
        out_specs=[o_spec, pl.BlockSpec((1, 1, nsb, TILE), lambda b, g, i: (b, g, 0, i))],
        compiler_params=_params(sem),
        name="nsa_compressed_attn",
    )(q, kc, vc, bias_cmp, gates)

    def k_spec(first_blk):
        return pl.BlockSpec((1, seq, LANES), lambda b, g, i: (b, 0, first_blk + g // 2))

    def vt_spec(first_blk):
        return pl.BlockSpec((LANES, seq), lambda b, g, i: (first_blk + g, b))

    o_sel = pl.pallas_call(
        functools.partial(_nsa_sparse_kernel, window=False),
        out_shape=o_shape,
        grid=(bsz, G, nq),
        in_specs=[q_spec, k_spec(0), vt_spec(0),
                  pl.BlockSpec((R, MAIN_TILES, TILE, TILE), lambda b, g, i: (g, 0, 0, 0)),
                  gate_spec,
                  pl.BlockSpec((1, 1, nsb, TILE), lambda b, g, i: (b, g, 0, i))],
        out_specs=o_spec,
        compiler_params=_params(sem),
        name="nsa_selected_attn",
    )(q, kk, v_t, bias_main, gates, nsel)

    o_win = pl.pallas_call(
        functools.partial(_nsa_sparse_kernel, window=True),
        out_shape=o_shape,
        grid=(bsz, G, nq),
        in_specs=[q_spec, k_spec(2), vt_spec(G),
                  pl.BlockSpec((R, NSA_WINDOW // TILE + 1, TILE, TILE), lambda b, g, i: (g, 0, 0, 0)),
                  gate_spec],
        out_specs=o_spec,
        compiler_params=_params(sem),
        name="nsa_window_attn",
    )(q, kk, v_t, bias_win, gates)

    m = bsz * seq
    parts = [t.reshape(m, N_HEADS * LANES) for t in (o_cmp, o_sel, o_win)]
    return parts, _pad_heads_rows(w_out, lambda h: 0).astype(BF16)


def _router_kernel(x_ref, w_ref, b_ref, o_ref):
    logits = lax.dot_general(w_ref[...], x_ref[...], (((1,), (1,)), ((), ())),
                             preferred_element_type=F32, precision=lax.Precision.HIGHEST)
    scores = 1.0 / (1.0 + jnp.exp(-logits))
    biased = scores + b_ref[...]
    rows = [biased[e:e + 1, :] for e in range(N_EXPERTS)]
    group_score = []
    for g in range(N_GROUPS):
        r = rows[g * EXPERTS_PER_GROUP:(g + 1) * EXPERTS_PER_GROUP]
        best = None
        for a in range(EXPERTS_PER_GROUP):
            for c in range(a + 1, EXPERTS_PER_GROUP):
                pair = r[a] + r[c]
                best = pair if best is None else jnp.maximum(best, pair)
        group_score.append(best)
    best_val = group_score[0]
    best_grp = jnp.zeros_like(best_val, dtype=jnp.int32)
    for g in range(1, N_GROUPS):
        better = group_score[g] > best_val
        best_val = jnp.where(better, group_score[g], best_val)
        best_grp = jnp.where(better, g, best_grp)
    picked = []
    for e in range(N_EXPERTS):
        g, a = divmod(e, EXPERTS_PER_GROUP)
        rank = jnp.zeros_like(best_grp)
        for c in range(EXPERTS_PER_GROUP):
            if c == a:
                continue
            other = rows[g * EXPERTS_PER_GROUP + c]
            ahead = (other > rows[e]) | ((other == rows[e]) & (c < a))
            rank = rank + jnp.where(ahead, 1, 0)
        picked.append((best_grp == g) & (rank < 2))
    raw = [jnp.where(picked[e], scores[e:e + 1, :], 0.0) for e in range(N_EXPERTS)]
    total = raw[0]
    for e in range(1, N_EXPERTS):
        total = total + raw[e]
    o_ref[...] = (jnp.concatenate(raw, axis=0) / total).T


def _moe_kernel(x_ref, gate_ref, wg_ref, wu_ref, wd_ref, g_ref, b_ref, o_ref, xb_ref, acc_ref):
    e = pl.program_id(1)

    @pl.when(e == 0)
    def _():
        xb_ref[...] = x_ref[...].astype(BF16)
        acc_ref[...] = jnp.zeros_like(acc_ref)

    xb = xb_ref[...]
    lane = lax.broadcasted_iota(jnp.int32, gate_ref.shape, 1)
    gcol = jnp.sum(jnp.where(lane == e, gate_ref[...], 0.0), axis=-1, keepdims=True)
    a = _dot(xb, wg_ref[0])
    h = a / (1.0 + jnp.exp(-a)) * _dot(xb, wu_ref[0])
    acc_ref[...] += gcol * _dot(h.astype(BF16), wd_ref[0])

    @pl.when(e == N_EXPERTS - 1)
    def _():
        o_ref[...] = _layer_norm(DEEPNORM_ALPHA * x_ref[...] + acc_ref[...], g_ref[...], b_ref[...])


def _moe_ln(x, router_w, router_b, w_gate, w_up, w_down, g, b):
    m, d = x.shape
    tm = 1024
    gates = pl.pallas_call(
        _router_kernel,
        out_shape=jax.ShapeDtypeStruct((m, N_EXPERTS), F32),
        grid=(m // tm,),
        in_specs=[pl.BlockSpec((tm, d), lambda i: (i, 0)),
                  pl.BlockSpec((N_EXPERTS, d), lambda i: (0, 0)),
                  pl.BlockSpec((N_EXPERTS, 1), lambda i: (0, 0))],
        out_specs=pl.BlockSpec((tm, N_EXPERTS), lambda i: (i, 0)),
        compiler_params=_params(("parallel",)),
        name="moe_router",
    )(x, router_w.T, router_b.reshape(N_EXPERTS, 1))
    de = w_gate.shape[-1]
    return pl.pallas_call(
        _moe_kernel,
        out_shape=jax.ShapeDtypeStruct((m, d), F32),
        grid=(m // tm, N_EXPERTS),
        in_specs=[pl.BlockSpec((tm, d), lambda i, e: (i, 0)),
                  pl.BlockSpec((tm, N_EXPERTS), lambda i, e: (i, 0)),
                  pl.BlockSpec((1, d, de), lambda i, e: (e, 0, 0)),
                  pl.BlockSpec((1, d, de), lambda i, e: (e, 0, 0)),
                  pl.BlockSpec((1, de, d), lambda i, e: (e, 0, 0)),
                  pl.BlockSpec((1, d), lambda i, e: (0, 0)),
                  pl.BlockSpec((1, d), lambda i, e: (0, 0))],
        out_specs=pl.BlockSpec((tm, d), lambda i, e: (i, 0)),
        scratch_shapes=[pltpu.VMEM((tm, d), BF16), pltpu.VMEM((tm, d), F32)],
        compiler_params=_params(("parallel", "arbitrary")),
        name="moe_experts_ln",
    )(x, gates, w_gate.astype(BF16), w_up.astype(BF16), w_down.astype(BF16), g.reshape(1, d), b.reshape(1, d))


def kernel(x, rel_table, router_w, router_b, ln1_g, ln1_b, ln2_g, ln2_b, exp_w_gate, exp_w_up, exp_w_down, dil_w_in, dil_w_out, sb_w_in, sb_w_out, nsa_w_in, nsa_w_out, nsa_cmp_pos_k, nsa_cmp_w1_k, nsa_cmp_w2_k, nsa_cmp_pos_v, nsa_cmp_w1_v, nsa_cmp_w2_v, moba_w_in, moba_w_out):
    bsz, seq, d = x.shape
    assert d == D_MODEL and seq % (DILATED_PAIRS[-1][1] * DIL_BLOCK) == 0
    depth = ln1_g.shape[0]
    n_mixers = 4
    table = rel_table.astype(F32)
    span = DIL_BLOCK
    big = 1 << 30
    bias_dil = _bias_tiles(table, len(DILATED_PAIRS), DIL_BLOCK, 2 * DIL_BLOCK, DIL_BLOCK, 0, 1, 0, span,
                           [dl for _, dl in DILATED_PAIRS])
    assert BIASED_TILES * TILE - (TILE - 1) >= BUCKET_THRESHOLDS[-1]
    bias_main = _bias_tiles(table, MAIN_TILES, TILE, TILE, 0, TILE, 1, 0, big, [1], shift=True, transposed=True,
                            scale=LOG2_E)
    bias_win = _bias_tiles(table, NSA_WINDOW // TILE + 1, TILE, TILE, 0, TILE, 1, 0, NSA_WINDOW - 1, [1],
                           transposed=True, scale=LOG2_E)
    bias_cmp = _bias_tiles(table, CMP_TILES, CMP_CHUNK, TILE, -TILE - (NSA_CMP_LEN - 1), TILE, NSA_CMP_STRIDE,
                           0, big, [1], transposed=True, scale=LOG2_E)
    xf = x.reshape(bsz * seq, d)
    for layer in range(depth):
        kind, occ = layer % n_mixers, layer // n_mixers
        merge = None
        if kind == 0:
            parts, w_out = _mixer_dilated(xf, dil_w_in[occ], dil_w_out[occ], bias_dil, bsz, seq)
            merge = 'lse3'
        elif kind == 1:
            parts, w_out = _mixer_stick_breaking(xf, sb_w_in[occ], sb_w_out[occ], bsz, seq)
        elif kind == 2:
            parts, w_out = _mixer_nsa(xf, nsa_w_in[occ], nsa_w_out[occ], (bias_main, bias_win, bias_cmp),
                                      nsa_cmp_pos_k[occ], nsa_cmp_w1_k[occ], nsa_cmp_w2_k[occ],
                                      nsa_cmp_pos_v[occ], nsa_cmp_w1_v[occ], nsa_cmp_w2_v[occ], bsz, seq)
            merge = 'sum3'
        else:
            parts, w_out = _mixer_moba(xf, moba_w_in[occ], moba_w_out[occ], bias_main, bsz, seq)
        xf = _outproj_ln(parts, w_out, xf, ln1_g[layer], ln1_b[layer], merge=merge)
        xf = _moe_ln(xf, router_w, router_b, exp_w_gate[layer], exp_w_up[layer], exp_w_down[layer],
                     ln2_g[layer], ln2_b[layer])
    return xf.reshape(bsz, seq, d)
```

```python
import functools
import math

import numpy as np
import jax
import jax.numpy as jnp
from jax import lax
from jax.experimental import pallas as pl
from jax.experimental.pallas import tpu as pltpu

F32 = jnp.float32
BF16 = jnp.bfloat16

D_MODEL = 1024
HEAD_DIM = 64
N_HEADS = 16
LANES = 128
PAIRS = N_HEADS // 2
ATTN_SCALE = HEAD_DIM ** -0.5
REL_BUCKETS = 32
REL_MAX_DIST = 2048
DILATED_PAIRS = ((128, 1), (512, 4), (2048, 16))
DIL_BLOCK = 128
SB_CLIP = 60.0
SB_TILE = 256
LOG2_E = math.log2(math.e)
NSA_KV_HEADS = 4
NSA_GROUP = N_HEADS // NSA_KV_HEADS
NSA_CMP_LEN = 32
NSA_CMP_STRIDE = 16
NSA_SEL_BLOCK = 64
NSA_TOP_N = 16
NSA_WINDOW = 512
MOBA_BLOCK = 256
MOBA_TOP_K = 3
N_EXPERTS = 16
N_GROUPS = 4
EXPERTS_PER_GROUP = N_EXPERTS // N_GROUPS
DEPTH = 4
DEEPNORM_ALPHA = (2 * DEPTH) ** 0.25
LN_EPS = 1e-5
NEG_INF = -1e30
FORCED_SCORE = 1e9
TILE = 256
BIASED_TILES = 7
MOBA_FAR_WIDTH = 8
MOBA_HEADS = 4
NSA_FAR_WIDTH = 4
MAIN_TILES = BIASED_TILES + max(MOBA_FAR_WIDTH, NSA_FAR_WIDTH) - 1
CMP_CHUNK = 128
CMP_TILES = 19
VMEM_LIMIT = 56 * 1024 * 1024


def _bucket_thresholds():
    n = np.arange(0, 2 * REL_MAX_DIST)
    exact = REL_BUCKETS // 2
    logf = np.log(np.maximum(n, 1).astype(np.float64) / exact) / math.log(REL_MAX_DIST / exact)
    large = np.minimum(exact + (logf * (REL_BUCKETS - exact)).astype(np.int64), REL_BUCKETS - 1)
    bucket = np.where(n < exact, n, large)
    return tuple(int(np.argmax(bucket >= k)) for k in range(1, REL_BUCKETS))


BUCKET_THRESHOLDS = _bucket_thresholds()


def _params(sem, vmem=VMEM_LIMIT):
    return pltpu.CompilerParams(dimension_semantics=sem, vmem_limit_bytes=vmem)


def _dot(a, b):
    return jnp.dot(a, b, preferred_element_type=F32)


def _dot_nt(a, b):
    return lax.dot_general(a, b, (((1,), (1,)), ((), ())), preferred_element_type=F32)


def _split_bf16(x):
    hi = x.astype(BF16)
    lo = (x - hi.astype(F32)).astype(BF16)
    return hi, lo


def _matmul_kernel(x_ref, w_ref, o_ref, *, act):
    y = _dot(x_ref[...].astype(BF16), w_ref[...])
    if act == 'sigmoid':
        y = 1.0 / (1.0 + jnp.exp(-y))
    o_ref[...] = y.astype(o_ref.dtype)


def _matmul(x, w, out_dtype, act=None):
    m, k = x.shape
    n = w.shape[1]
    tm = 1024
    tn = next(t for t in (1024, 512, 128) if n % t == 0)
    return pl.pallas_call(
        functools.partial(_matmul_kernel, act=act),
        out_shape=jax.ShapeDtypeStruct((m, n), out_dtype),
        grid=(m // tm, n // tn),
        in_specs=[pl.BlockSpec((tm, k), lambda i, j: (i, 0)),
                  pl.BlockSpec((k, tn), lambda i, j: (0, j))],
        out_specs=pl.BlockSpec((tm, tn), lambda i, j: (i, j)),
        compiler_params=_params(("parallel", "arbitrary")),
        name="proj_matmul",
    )(x, w)


def _matmul_t_kernel(x_ref, w_ref, o_ref):
    y = _dot_nt(w_ref[...], x_ref[...].astype(BF16))
    row = lax.broadcasted_iota(jnp.int32, y.shape, 0)
    o_ref[...] = jnp.where(row % LANES == HEAD_DIM, 1.0, y).astype(o_ref.dtype)


def _matmul_t(x, wt):
    m, k = x.shape
    n = wt.shape[0]
    tm = 512
    tn = 512 if n % 512 == 0 else 256
    return pl.pallas_call(
        _matmul_t_kernel,
        out_shape=jax.ShapeDtypeStruct((n, m), BF16),
        grid=(m // tm, n // tn),
        in_specs=[pl.BlockSpec((tm, k), lambda i, j: (i, 0)),
                  pl.BlockSpec((tn, k), lambda i, j: (j, 0))],
        out_specs=pl.BlockSpec((tn, tm), lambda i, j: (j, i)),
        compiler_params=_params(("parallel", "arbitrary")),
        name="proj_matmul_t",
    )(x, wt)


def _matmul_heads_kernel(x_ref, w_ref, o_ref):
    y = _dot(x_ref[...].astype(BF16), w_ref[...]).astype(o_ref.dtype)
    for j in range(o_ref.shape[1]):
        o_ref[0, j] = y[:, j * HEAD_DIM:(j + 1) * HEAD_DIM]


def _matmul_heads(x, w, bsz, seq):
    m, k = x.shape
    n = w.shape[1] // HEAD_DIM
    tm = 512
    per_b = seq // tm
    return pl.pallas_call(
        _matmul_heads_kernel,
        out_shape=jax.ShapeDtypeStruct((bsz, n, seq, HEAD_DIM), BF16),
        grid=(m // tm,),
        in_specs=[pl.BlockSpec((tm, k), lambda i: (i, 0)),
                  pl.BlockSpec((k, n * HEAD_DIM), lambda i: (0, 0))],
        out_specs=pl.BlockSpec((1, n, tm, HEAD_DIM), lambda i: (i // per_b, 0, i % per_b, 0)),
        compiler_params=_params(("parallel",)),
        name="proj_heads",
    )(x, w)


def _layer_norm(z, g, b):
    mu = jnp.mean(z, axis=-1, keepdims=True)
    zc = z - mu
    var = jnp.mean(zc * zc, axis=-1, keepdims=True)
    return zc * lax.rsqrt(var + LN_EPS) * g + b


def _outproj_kernel(*refs, n_in, merge):
    ins = refs[:n_in]
    w_ref, x_ref, g_ref, b_ref, o_ref = refs[n_in:]
    if merge == 'lse3':
        o1, o2, o3, l1, l2, l3 = [r[...] for r in ins]
        mx = jnp.maximum(jnp.maximum(l1, l2), l3)
        e1, e2, e3 = jnp.exp(l1 - mx), jnp.exp(l2 - mx), jnp.exp(l3 - mx)
        a = (e1 * o1 + e2 * o2 + e3 * o3) / (e1 + e2 + e3)
    elif merge == 'sum3':
        a = ins[0][...].astype(F32) + ins[1][...].astype(F32) + ins[2][...].astype(F32)
    else:
        a = ins[0][...]
    y = _dot(a.astype(BF16), w_ref[...])
    o_ref[...] = _layer_norm(DEEPNORM_ALPHA * x_ref[...] + y, g_ref[...], b_ref[...])


def _outproj_ln(ins, w, x, g, b, merge=None):
    m, d = x.shape
    ka = ins[0].shape[1]
    tm = 256
    n_in = len(ins)
    return pl.pallas_call(
        functools.partial(_outproj_kernel, n_in=n_in, merge=merge),
        out_shape=jax.ShapeDtypeStruct((m, d), F32),
        grid=(m // tm,),
        in_specs=[pl.BlockSpec((tm, ka), lambda i: (i, 0)) for _ in ins]
        + [pl.BlockSpec((ka, d), lambda i: (0, 0)),
           pl.BlockSpec((tm, d), lambda i: (i, 0)),
           pl.BlockSpec((1, d), lambda i: (0, 0)),
           pl.BlockSpec((1, d), lambda i: (0, 0))],
        out_specs=pl.BlockSpec((tm, d), lambda i: (i, 0)),
        compiler_params=_params(("parallel",)),
        name="outproj_ln",
    )(*ins, w, x, g.reshape(1, d), b.reshape(1, d))


def _bias_kernel(tbl_ref, o_ref, *, rows, cols, off0, tstride, cmul, lo, hi, mults, shift, transposed, scale):
    t = pl.program_id(0)
    if len(set(mults)) == 1:
        mult = mults[0]
    else:
        mult = jnp.int32(mults[-1])
        for idx in range(len(mults) - 2, -1, -1):
            mult = jnp.where(t == idx, jnp.int32(mults[idx]), mult)
    base = off0 + tstride * t

    def strip(i, carry):
        r0 = pl.multiple_of(i * 8, 8)
        a = lax.broadcasted_iota(jnp.int32, (8, cols), 0) + r0
        c = lax.broadcasted_iota(jnp.int32, (8, cols), 1)
        steps = base + c - cmul * a if transposed else base + a - cmul * c
        valid = (steps >= lo) & (steps <= hi)
        dist = steps * mult
        for h in range(N_HEADS):
            acc = jnp.full((8, cols), tbl_ref[0, h], F32)
            for k, thr in enumerate(BUCKET_THRESHOLDS):
                acc = jnp.where(dist >= thr, tbl_ref[k + 1, h], acc)
            if shift:
                acc = acc - tbl_ref[REL_BUCKETS - 1, h]
            if scale != 1.0:
                acc = acc * scale
            o_ref[h, 0, pl.ds(r0, 8), :] = jnp.where(valid, acc, NEG_INF)
        return carry

    lax.fori_loop(0, rows // 8, strip, 0)


def _bias_tiles(table, n_tiles, rows, cols, off0, tstride, cmul, lo, hi, mults, shift=False, transposed=False,
                scale=1.0):
    return pl.pallas_call(
        functools.partial(_bias_kernel, rows=rows, cols=cols, off0=off0, tstride=tstride, cmul=cmul,
                          lo=lo, hi=hi, mults=tuple(mults), shift=shift, transposed=transposed, scale=scale),
        out_shape=jax.ShapeDtypeStruct((N_HEADS, n_tiles, rows, cols), F32),
        grid=(n_tiles,),
        in_specs=[pl.BlockSpec(memory_space=pltpu.SMEM)],
        out_specs=pl.BlockSpec((N_HEADS, 1, rows, cols), lambda t: (0, t, 0, 0)),
        compiler_params=_params(("arbitrary",)),
        name="bias_tiles",
    )(table)


def _dil_kernel(q_ref, kp_ref, kc_ref, vp_ref, vc_ref, bias_ref, o_ref, lse_ref,
                q_scr, kp_scr, kc_scr, vp_scr, vc_scr, *, dilation, per_step):
    j = pl.program_id(2)
    d = dilation
    n_cur = per_step * DIL_BLOCK

    def rows_of(ref, r, n, lead=()):
        if d == 1:
            return ref[lead + (slice(0, n), slice(None))]
        return ref[lead + (pl.ds(r, n, stride=d), slice(None))]

    if d > 1:
        for h in range(2):
            q_scr[h] = q_ref[0, :, h * LANES:(h + 1) * LANES].astype(F32)
        kp_scr[...] = kp_ref[0].astype(F32)
        kc_scr[...] = kc_ref[0].astype(F32)
        vp_scr[...] = vp_ref[0].astype(F32)
        vc_scr[...] = vc_ref[0].astype(F32)
    colk = lax.broadcasted_iota(jnp.int32, (DIL_BLOCK, 2 * DIL_BLOCK), 1)
    first = jnp.where((colk < DIL_BLOCK) & (j == 0), NEG_INF, 0.0)
    low = lax.broadcasted_iota(jnp.int32, (DIL_BLOCK, LANES), 1) < HEAD_DIM
    for r in range(d):
        if d == 1:
            q_heads = [q_ref[0, :, h * LANES:(h + 1) * LANES] for h in range(2)]
            k_all = jnp.concatenate([kp_ref[0], kc_ref[0]], axis=0)
            v_all = jnp.concatenate([vp_ref[0], vc_ref[0]], axis=0)
        else:
            q_heads = [rows_of(q_scr, r, n_cur, (h,)).astype(BF16) for h in range(2)]
            k_all = jnp.concatenate([rows_of(kp_scr, r, DIL_BLOCK), rows_of(kc_scr, r, n_cur)], axis=0).astype(BF16)
            v_all = jnp.concatenate([rows_of(vp_scr, r, DIL_BLOCK), rows_of(vc_scr, r, n_cur)], axis=0).astype(BF16)
        for u in range(per_step):
            k2 = k_all[u * DIL_BLOCK:(u + 2) * DIL_BLOCK]
            v2 = v_all[u * DIL_BLOCK:(u + 2) * DIL_BLOCK]
            outs, lses = [], []
            for h in range(2):
                s = _dot_nt(q_heads[h][u * DIL_BLOCK:(u + 1) * DIL_BLOCK], k2) + bias_ref[h, 0]
                if u == 0:
                    s = s + first
                m = jnp.max(s, axis=-1, keepdims=True)
                p = jnp.exp(s - m)
                l = jnp.maximum(jnp.sum(p, axis=-1, keepdims=True), 1e-30)
                outs.append(_dot(p.astype(BF16), v2) / l)
                lses.append(m + jnp.log(l))
            if d == 1:
                dst = (0, slice(u * DIL_BLOCK, (u + 1) * DIL_BLOCK), slice(None))
            else:
                dst = (0, pl.ds(u * DIL_BLOCK * d + r, DIL_BLOCK, stride=d), slice(None))
            o_ref[dst] = jnp.where(low, outs[0], outs[1])
            lse_ref[dst] = jnp.where(low, lses[0], lses[1])


def _dilated_group(proj, bias, g, dilation, bsz, seq):
    prev_rows = DIL_BLOCK * dilation
    step_rows = math.gcd(seq, 16 * DIL_BLOCK)
    per_step = step_rows // prev_rows
    assert per_step >= 1 and seq % step_rows == 0
    q_blk = g * N_HEADS // 2
    k_blk = 3 * N_HEADS + g * 2 * PAIRS
    v_blk = k_blk + PAIRS

    def kv_specs(base):
        prev = pl.BlockSpec((1, prev_rows, LANES), lambda hp, b, j: (b, jnp.maximum(per_step * j - 1, 0), base + hp))
        cur = pl.BlockSpec((1, step_rows, LANES), lambda hp, b, j: (b, j, base + hp))
        return [prev, cur]

    out_spec = pl.BlockSpec((1, step_rows, LANES), lambda hp, b, j: (b, j, hp))
    staged = 8 if dilation == 1 else None
    o, lse = pl.pallas_call(
        functools.partial(_dil_kernel, dilation=dilation, per_step=per_step),
        out_shape=[jax.ShapeDtypeStruct((bsz, seq, D_MODEL), F32)] * 2,
        grid=(PAIRS, bsz, seq // step_rows),
        in_specs=[pl.BlockSpec((1, step_rows, 2 * LANES), lambda hp, b, j: (b, j, q_blk + hp))]
        + kv_specs(k_blk) + kv_specs(v_blk)
        + [pl.BlockSpec((2, 1, DIL_BLOCK, 2 * DIL_BLOCK), lambda hp, b, j: (hp, g, 0, 0))],
        out_specs=[out_spec, out_spec],
        scratch_shapes=[pltpu.VMEM((2, staged or step_rows, LANES), F32),
                        pltpu.VMEM((staged or prev_rows, LANES), F32), pltpu.VMEM((staged or step_rows, LANES), F32),
                        pltpu.VMEM((staged or prev_rows, LANES), F32), pltpu.VMEM((staged or step_rows, LANES), F32)],
        compiler_params=_params(("parallel", "parallel", "arbitrary")),
        name="dilated_attn",
    )(proj, proj, proj, proj, proj, bias)
    return o.reshape(bsz * seq, D_MODEL), lse.reshape(bsz * seq, D_MODEL)


def _pad_heads_cols(w, parity_of_head):
    k, n = w.shape
    nh = n // HEAD_DIM
    wh = w.reshape(k, nh, HEAD_DIM)
    z = jnp.zeros_like(wh)
    par = jnp.asarray([parity_of_head(h) for h in range(nh)], jnp.int32)[None, :, None]
    lo = jnp.where(par == 0, wh, z)
    hi = jnp.where(par == 1, wh, z)
    return jnp.concatenate([lo, hi], axis=-1).reshape(k, nh * LANES)


def _pad_heads_rows(w, parity_of_head):
    return _pad_heads_cols(w.T, parity_of_head).T


def _mixer_dilated(x, w_in, w_out, bias_dil, bsz, seq):
    aw = N_HEADS * HEAD_DIM
    par = lambda h: h % 2
    qs, kvs = [], []
    for g in range(len(DILATED_PAIRS)):
        base = g * 3 * aw
        qs.append(_pad_heads_cols(w_in[:, base:base + aw] * ATTN_SCALE, par))
        kvs.append(w_in[:, base + aw:base + 3 * aw])
    w_all = jnp.concatenate(qs + kvs, axis=1).astype(BF16)
    proj = _matmul(x, w_all, BF16).reshape(bsz, seq, -1)
    parts = [_dilated_group(proj, bias_dil, g, d, bsz, seq) for g, (_, d) in enumerate(DILATED_PAIRS)]
    return [p[0] for p in parts] + [p[1] for p in parts], w_out.astype(BF16)


def _sb_kernel(q_ref, k_ref, v_ref, o_ref):
    i = pl.program_id(2)
    t = SB_TILE
    row = lax.broadcasted_iota(jnp.int32, (t, t), 0)
    col = lax.broadcasted_iota(jnp.int32, (t, t), 1)
    suffix = jnp.where(row >= col, 1.0, 0.0).astype(BF16)
    strict = col < row
    q2 = q_ref[0]
    q_heads = [q2[:, h * LANES:(h + 1) * LANES] for h in range(2)]

    def block(qh, kj, masked):
        start = pl.multiple_of(kj * t, t)
        kb = k_ref[0, pl.ds(start, t), :]
        vb = v_ref[0, pl.ds(start, t), :]
        a = jnp.clip(_dot_nt(qh, kb), -SB_CLIP, SB_CLIP) * LOG2_E
        sp = jnp.log2(1.0 + jnp.exp2(a))
        if masked:
            sp = jnp.where(strict, sp, 0.0)
        hi, lo = _split_bf16(sp)
        rr = _dot(hi, suffix) + _dot(lo, suffix)
        att = jnp.exp2(a - rr)
        if masked:
            att = jnp.where(strict, att, 0.0)
        return _dot(att.astype(BF16), vb), rr[:, 0:1]

    def first_two(kj, carry):
        has_prev = jnp.where(kj > 0, 1.0, 0.0)
        out = []
        for qh in q_heads:
            acc_d, r_d = block(qh, kj, True)
            pv, r_p = block(qh, jnp.maximum(kj - 1, 0), False)
            out.append((acc_d + (has_prev * jnp.exp2(-r_d)) * pv, r_d + has_prev * r_p))
        return tuple(out)

    zero = (jnp.zeros((t, LANES), F32), jnp.zeros((t, 1), F32))
    (acc0, r0), (acc1, r1) = lax.fori_loop(i, i + 1, first_two, (zero, zero))

    def cond(carry):
        n, alive = carry[0], carry[1]
        return (n < i - 1) & alive

    def step(carry):
        n = carry[0]
        new, tops = [], []
        for qh, (r_sum, w, acc) in zip(q_heads, carry[2:]):
            pv, r_blk = block(qh, i - 2 - n, False)
            r_new = r_sum + r_blk
            w_new = jnp.exp2(-r_new)
            new.append((r_new, w_new, acc + w * pv))
            tops.append(jnp.max(w_new, axis=0, keepdims=True))
        alive = jnp.maximum(tops[0], tops[1])[0, 0] > 0.0
        return (n + 1, alive) + tuple(new)

    w0, w1 = jnp.exp2(-r0), jnp.exp2(-r1)
    alive0 = jnp.maximum(jnp.max(w0, axis=0, keepdims=True), jnp.max(w1, axis=0, keepdims=True))[0, 0] > 0.0
    init = (jnp.int32(0), alive0, (r0, w0, acc0), (r1, w1, acc1))
    res = lax.while_loop(cond, step, init)
    lane = lax.broadcasted_iota(jnp.int32, (t, LANES), 1)
    o_ref[0] = jnp.where(lane < HEAD_DIM, res[2][2], res[3][2]).astype(o_ref.dtype)


def _mixer_stick_breaking(x, w_in, w_out, bsz, seq):
    aw = N_HEADS * HEAD_DIM
    wq = _pad_heads_cols(w_in[:, :aw] * ATTN_SCALE, lambda h: h % 2)
    w_all = jnp.concatenate([wq, w_in[:, aw:]], axis=1).astype(BF16)
    proj = _matmul(x, w_all, BF16).reshape(bsz, seq, -1)
    kb = N_HEADS
    o = pl.pallas_call(
        _sb_kernel,
        out_shape=jax.ShapeDtypeStruct((bsz, seq, D_MODEL), BF16),
        grid=(bsz, PAIRS, seq // SB_TILE),
        in_specs=[pl.BlockSpec((1, SB_TILE, 2 * LANES), lambda b, hp, i: (b, i, hp)),
                  pl.BlockSpec((1, seq, LANES), lambda b, hp, i: (b, 0, kb + hp)),
                  pl.BlockSpec((1, seq, LANES), lambda b, hp, i: (b, 0, kb + PAIRS + hp))],
        out_specs=pl.BlockSpec((1, SB_TILE, LANES), lambda b, hp, i: (b, i, hp)),
        compiler_params=_params(("parallel", "parallel", "arbitrary")),
        name="stick_breaking_attn",
    )(proj, proj, proj)
    return [o.reshape(bsz * seq, D_MODEL)], w_out.astype(BF16)


def _flash_chunks(q_heads, k_ref, k_cols, v_ref, v_rows, bias_ref, mask_rows, lo, hi, i, blocks_per_chunk, n_tiles,
                  fold_mask=False, pipelined=False, far_width=1, near_width=1):
    nh = len(q_heads)

    def logits(ch, near, width):
        rows = width * TILE
        start = pl.multiple_of(ch * TILE, TILE)
        kbs = {c: k_ref[0, pl.ds(start, rows), c:c + LANES] for c in sorted(set(k_cols))}
        add = None
        if fold_mask:
            key = lax.broadcasted_iota(jnp.int32, (rows, LANES), 0)
            blk = lax.broadcasted_iota(jnp.int32, (rows, LANES), 1)
            blk_of_key = ch * blocks_per_chunk + key // (TILE // blocks_per_chunk)
            ind = jnp.where(blk == blk_of_key, NEG_INF, 0.0).astype(BF16)
            kbs = {c: jnp.concatenate([kb, ind], axis=1) for c, kb in kbs.items()}
        elif mask_rows is not None:
            nb = width * blocks_per_chunk
            per = TILE // blocks_per_chunk
            blk_rows = mask_rows(ch * blocks_per_chunk, nb)
            add = jnp.concatenate([jnp.broadcast_to(blk_rows[r:r + 1, :], (per, TILE)) for r in range(nb)], axis=0)
        out = []
        for h in range(nh):
            s = _dot_nt(kbs[k_cols[h]], q_heads[h])
            if near:
                tiles = [bias_ref[h, jnp.maximum(i - ch - w, 0)] for w in range(width)]
                s = s + (tiles[0] if width == 1 else jnp.concatenate(tiles, axis=0))
            if add is not None:
                s = s + add
            out.append(s)
        return tuple(out)

    def update(ch, s_heads, state, width):
        start = pl.multiple_of(ch * TILE, TILE)
        vts = {}
        new = []
        for h in range(nh):
            r0 = v_rows[h]
            if r0 not in vts:
                vts[r0] = v_ref[r0:r0 + LANES, pl.ds(start, width * TILE)]
            m, acc = state[h]
            s = s_heads[h]
            m_new = jnp.maximum(m, jnp.max(s, axis=0, keepdims=True))
            alpha = jnp.exp2(m - m_new)
            p = jnp.exp2(s - m_new)
            acc = alpha * acc + _dot(vts[r0], p.astype(BF16))
            new.append((m_new, acc))
        return tuple(new)

    def run(ch0, steps, near, width, state, pipelined=pipelined):
        if not pipelined:
            return lax.fori_loop(
                0, steps, lambda t, st: update(ch0 + t * width, logits(ch0 + t * width, near, width), st, width), state)

        def body(t, carry):
            s_cur, st = carry
            s_next = logits(ch0 + jnp.minimum(t + 1, steps - 1) * width, near, width)
            return s_next, update(ch0 + t * width, s_cur, st, width)

        zeros = tuple(jnp.zeros((width * TILE, TILE), F32) for _ in range(nh))
        first = jnp.minimum(ch0, i)
        s0 = lax.fori_loop(first, first + 1, lambda c, _: logits(c, near, width), zeros)
        return lax.fori_loop(0, steps, body, (s0, state))[1]

    state = tuple((jnp.full((1, TILE), NEG_INF, F32), jnp.zeros((LANES, TILE), F32)) for _ in range(nh))
    n_near = n_tiles - (far_width - 1)
    far_steps = jnp.maximum(hi - n_near - lo, 0) // far_width
    split = lo + far_steps * far_width
    state = run(lo, far_steps, False, far_width, state)
    near_pairs = (hi - split) // near_width if near_width > 1 else 0
    if near_width > 1:
        state = run(split, near_pairs, True, near_width, state)
    single = split + near_pairs * near_width
    state = run(single, hi - single, True, 1, state, pipelined=pipelined and near_width == 1)
    outs = []
    for _, acc in state:
        o = (acc / acc[HEAD_DIM:HEAD_DIM + 1, :]).T
        lane = lax.broadcasted_iota(jnp.int32, o.shape, 1)
        outs.append(jnp.where(lane < HEAD_DIM, o, 0.0))
    return outs


def _kmean_kernel(k_ref, o_ref):
    rows = k_ref.shape[1]
    kf = k_ref[0].astype(F32).reshape(rows // MOBA_BLOCK, MOBA_BLOCK, k_ref.shape[2])
    o_ref[0] = jnp.sum(kf, axis=1) * (1.0 / MOBA_BLOCK)


def _moba_kernel(q_ref, k_ref, v_ref, km_ref, bias_ref, o_ref):
    i = pl.program_id(2)
    nblk = km_ref.shape[1]
    km_hi, km_lo = _split_bf16(km_ref[0])
    qs = q_ref[0]
    blk = lax.broadcasted_iota(jnp.int32, (nblk, TILE), 0)
    past = blk < i
    q_heads = []
    for h in range(MOBA_HEADS):
        qh = qs[:, h * LANES:(h + 1) * LANES]
        pair = slice((h // 2) * LANES, (h // 2 + 1) * LANES)
        gate = jnp.where(past, _dot_nt(km_hi[:, pair], qh) + _dot_nt(km_lo[:, pair], qh), NEG_INF)
        allowed = blk == i
        for _ in range(MOBA_TOP_K):
            mx = jnp.max(gate, axis=0, keepdims=True)
            first = jnp.min(jnp.where(gate == mx, blk, nblk), axis=0, keepdims=True)
            pick = blk == first
            allowed = allowed | (pick & past)
            gate = jnp.where(pick, -jnp.inf, gate)
        nsel = jnp.where(allowed, 0.0, 1.0).T.astype(BF16)
        q_heads.append(jnp.concatenate([qh, nsel], axis=1))
    outs = _flash_chunks(q_heads, k_ref, [(h // 2) * LANES for h in range(MOBA_HEADS)],
                         v_ref, [h * LANES for h in range(MOBA_HEADS)], bias_ref, None, 0, i + 1, i, 1,
                         BIASED_TILES + MOBA_FAR_WIDTH - 1, fold_mask=True, pipelined=False, far_width=MOBA_FAR_WIDTH,
                         near_width=2)
    for h in range(MOBA_HEADS):
        o_ref[0, :, h * LANES:(h + 1) * LANES] = outs[h].astype(o_ref.dtype)


def _mixer_moba(x, w_in, w_out, bias_main, bsz, seq):
    aw = N_HEADS * HEAD_DIM
    nblk = seq // MOBA_BLOCK
    assert nblk - 1 >= MOBA_TOP_K
    wq = _pad_heads_cols(w_in[:, :aw] * (ATTN_SCALE * LOG2_E), lambda h: h % 2)
    w_all = jnp.concatenate([wq, w_in[:, aw:2 * aw]], axis=1).astype(BF16)
    proj = _matmul(x, w_all, BF16).reshape(bsz, seq, -1)
    v_t = _matmul_t(x, _pad_heads_cols(w_in[:, 2 * aw:], lambda h: 0).T.astype(BF16))
    kb = N_HEADS
    rows = 8 * MOBA_BLOCK
    kmean = pl.pallas_call(
        _kmean_kernel,
        out_shape=jax.ShapeDtypeStruct((bsz, nblk, D_MODEL), F32),
        grid=(bsz, seq // rows),
        in_specs=[pl.BlockSpec((1, rows, D_MODEL), lambda b, i: (b, i, kb * LANES // D_MODEL))],
        out_specs=pl.BlockSpec((1, 8, D_MODEL), lambda b, i: (b, i, 0)),
        compiler_params=_params(("parallel", "parallel")),
        name="moba_kmean",
    )(proj)
    assert nblk <= LANES
    kmean = jnp.pad(kmean, ((0, 0), (0, LANES - nblk), (0, 0)))
    nh = MOBA_HEADS
    kw = nh // 2 * LANES
    once = pl.Buffered(1)
    o = pl.pallas_call(
        _moba_kernel,
        out_shape=jax.ShapeDtypeStruct((bsz, seq, N_HEADS * LANES), BF16),
        grid=(bsz, N_HEADS // nh, seq // TILE),
        in_specs=[pl.BlockSpec((1, TILE, nh * LANES), lambda b, hq, i: (b, i, hq)),
                  pl.BlockSpec((1, seq, kw), lambda b, hq, i: (b, 0, kb * LANES // kw + hq), pipeline_mode=once),
                  pl.BlockSpec((nh * LANES, seq), lambda b, hq, i: (hq, b), pipeline_mode=once),
                  pl.BlockSpec((1, LANES, kw), lambda b, hq, i: (b, 0, hq)),
                  pl.BlockSpec((nh, BIASED_TILES + MOBA_FAR_WIDTH - 1, TILE, TILE), lambda b, hq, i: (hq, 0, 0, 0),
                               pipeline_mode=once)],
        out_specs=pl.BlockSpec((1, TILE, nh * LANES), lambda b, hq, i: (b, i, hq)),
        compiler_params=_params(("parallel", "parallel", "arbitrary")),
        name="moba_attn",
    )(proj, proj, v_t, kmean, bias_main)
    return [o.reshape(bsz * seq, N_HEADS * LANES)], _pad_heads_rows(w_out, lambda h: 0).astype(BF16)


def _gelu_tanh(x):
    return 0.5 * x * (1.0 + jnp.tanh(math.sqrt(2.0 / math.pi) * (x + 0.044715 * (x * x * x))))


def _compress_kernel(a_ref, pos_ref, w1_ref, w2_ref, o_ref, *, transpose_out):
    nc = a_ref.shape[2]
    half = NSA_CMP_STRIDE * HEAD_DIM
    a = a_ref[0, 0].astype(F32)
    top = _dot((a + pos_ref[0:1, :]).astype(BF16), w1_ref[0:half, :])
    bot = _dot((a + pos_ref[1:2, :]).astype(BF16), w1_ref[half:2 * half, :])
    hid = top + pltpu.roll(bot, nc - 1, 0)
    out = _dot(_gelu_tanh(hid).astype(BF16), w2_ref[0])
    rowi = lax.broadcasted_iota(jnp.int32, out.shape, 0)
    out = jnp.where(rowi < nc - 1, out, 0.0)
    o_ref[0, 0] = (out.T if transpose_out else out).astype(o_ref.dtype)


def _compress(tok, first, pos, w1, w2, bsz, seq, by_parity):
    nc = seq // NSA_CMP_STRIDE
    half = NSA_CMP_STRIDE * HEAD_DIM
    pos2 = pos.reshape(2, half)
    z = jnp.zeros_like(w2)
    low = jnp.concatenate([w2, z], axis=1)
    w2p = jnp.stack([low, jnp.concatenate([z, w2], axis=1) if by_parity else low]).astype(BF16)
    out_dims = (nc, LANES) if by_parity else (LANES, nc)
    return pl.pallas_call(
        functools.partial(_compress_kernel, transpose_out=not by_parity),
        out_shape=jax.ShapeDtypeStruct((bsz, NSA_KV_HEADS) + out_dims, BF16),
        grid=(bsz, NSA_KV_HEADS),
        in_specs=[pl.BlockSpec((1, 1, nc, half), lambda b, g: (b, first + g, 0, 0)),
                  pl.BlockSpec((2, half), lambda b, g: (0, 0)),
                  pl.BlockSpec((2 * half, w1.shape[1]), lambda b, g: (0, 0)),
                  pl.BlockSpec((1, w2.shape[0], LANES), lambda b, g: (g % 2, 0, 0))],
        out_specs=pl.BlockSpec((1, 1) + out_dims, lambda b, g: (b, g, 0, 0)),
        compiler_params=_params(("parallel", "parallel")),
        name="nsa_compress",
    )(tok, pos2, w1.astype(BF16), w2p)


def _gate_columns(gate_ref, branch):
    tile = gate_ref[...]
    lane = lax.broadcasted_iota(jnp.int32, tile.shape, 1)
    base = branch * N_HEADS + pl.program_id(1) * NSA_GROUP
    return [jnp.sum(jnp.where(lane == base + r, tile, 0.0), axis=-1, keepdims=True) for r in range(NSA_GROUP)]


def _nsa_cmp_kernel(q_ref, kc_ref, vc_ref, bias_ref, gate_ref, o_ref, sel_ref, *, n_top):
    i = pl.program_id(2)
    nc = kc_ref.shape[2]
    n_chunks = nc // CMP_CHUNK
    nsb = sel_ref.shape[2]
    q4 = q_ref[0]
    gates = _gate_columns(gate_ref, 0)
    keep = lax.broadcasted_iota(jnp.int32, (TILE, LANES), 1) < HEAD_DIM
    psum = [jnp.zeros((CMP_CHUNK, TILE), F32) for _ in range(n_chunks)]
    for r in range(NSA_GROUP):
        qh = q4[:, r * LANES:(r + 1) * LANES]
        ss = []
        for c in range(n_chunks):
            tile = jnp.clip(i - (CMP_CHUNK * NSA_CMP_STRIDE // TILE) * c, -1, CMP_TILES - 2) + 1
            ss.append(_dot_nt(kc_ref[0, 0, c * CMP_CHUNK:(c + 1) * CMP_CHUNK, :], qh) + bias_ref[r, tile])
        m = ss[0].max(axis=0, keepdims=True)
        for c in range(1, n_chunks):
            m = jnp.maximum(m, ss[c].max(axis=0, keepdims=True))
        ps = [jnp.exp2(s - m) for s in ss]
        l = ps[0].sum(axis=0, keepdims=True)
        for c in range(1, n_chunks):
            l = l + ps[c].sum(axis=0, keepdims=True)
        inv = jnp.where(m > 0.5 * NEG_INF, 1.0 / jnp.maximum(l, 1e-30), 0.0)
        acc = jnp.zeros((LANES, TILE), F32)
        for c in range(n_chunks):
            pc = ps[c] * inv
            psum[c] = psum[c] + pc
            acc = acc + _dot(vc_ref[0, 0, :, c * CMP_CHUNK:(c + 1) * CMP_CHUNK], pc.astype(BF16))
        o_ref[0, :, r * LANES:(r + 1) * LANES] = jnp.where(keep, acc.T * gates[r], 0.0).astype(o_ref.dtype)
    imp = jnp.zeros((nsb, TILE), F32)
    per_sel = NSA_SEL_BLOCK // NSA_CMP_STRIDE
    for c in range(n_chunks):
        jb = lax.broadcasted_iota(jnp.int32, (nsb, CMP_CHUNK), 0)
        ci = lax.broadcasted_iota(jnp.int32, (nsb, CMP_CHUNK), 1) + c * CMP_CHUNK
        rel = ci - per_sel * jb
        over = (rel >= 1 - NSA_CMP_LEN // NSA_CMP_STRIDE) & (rel < per_sel) & (ci < nc - 1)
        ov = jnp.where(over, 1.0, 0.0).astype(BF16)
        hi, lo = _split_bf16(psum[c])
        imp = imp + _dot(ov, hi) + _dot(ov, lo)
    jb = lax.broadcasted_iota(jnp.int32, (nsb, TILE), 0)
    qpos = lax.broadcasted_iota(jnp.int32, (nsb, TILE), 1) + i * TILE
    cur = qpos // NSA_SEL_BLOCK
    forced = (jb == 0) | (jb == cur) | (jb == cur - 1)
    causal = jb * NSA_SEL_BLOCK <= qpos
    score = jnp.where(forced, FORCED_SCORE, jnp.where(causal, imp, NEG_INF))
    chosen = jnp.zeros((nsb, TILE), F32)
    for _ in range(n_top):
        mx = jnp.max(score, axis=0, keepdims=True)
        first = jnp.min(jnp.where(score == mx, jb, nsb), axis=0, keepdims=True)
        pick = jb == first
        chosen = jnp.where(pick, 1.0, chosen)
        score = jnp.where(pick, -jnp.inf, score)
    sel_ref[0, 0] = jnp.where((chosen > 0.0) & causal, 0.0, NEG_INF)


def _nsa_sparse_kernel(q_ref, k_ref, v_ref, bias_ref, gate_ref, *rest, window):
    if window:
        (o_ref,) = rest
        mask_rows = None
    else:
        sel_ref, o_ref = rest

        def mask_rows(b0, n):
            if n % 8 == 0:
                return sel_ref[0, 0, pl.ds(pl.multiple_of(b0, 8), n), :]
            assert n == 4
            rows = sel_ref[0, 0, pl.ds(pl.multiple_of(b0 // 8 * 8, 8), 8), :]
            return jnp.where(b0 % 8 == 0, rows[0:4], rows[4:8])

    i = pl.program_id(2)
    q4 = q_ref[0]
    gates = _gate_columns(gate_ref, 2 if window else 1)
    q_heads = [q4[:, r * LANES:(r + 1) * LANES] for r in range(NSA_GROUP)]
    if window:
        n_tiles = NSA_WINDOW // TILE + 1
        lo = jnp.maximum(i - (n_tiles - 1), 0)
        outs = _flash_chunks(q_heads, k_ref, [0] * NSA_GROUP, v_ref, [0] * NSA_GROUP, bias_ref, None,
                             lo, i + 1, i, 1, n_tiles, near_width=2)
    else:
        outs = _flash_chunks(q_heads, k_ref, [0] * NSA_GROUP, v_ref, [0] * NSA_GROUP, bias_ref, mask_rows,
                             0, i + 1, i,
                             TILE // NSA_SEL_BLOCK, BIASED_TILES + NSA_FAR_WIDTH - 1, far_width=NSA_FAR_WIDTH,
                             near_width=2)
    for r in range(NSA_GROUP):
        o_ref[0, :, r * LANES:(r + 1) * LANES] = (outs[r] * gates[r]).astype(o_ref.dtype)


def _mixer_nsa(x, w_in, w_out, table_bias, pos_k, w1_k, w2_k, pos_v, w1_v, w2_v, bsz, seq):
    bias_main, bias_win, bias_cmp = table_bias
    aw = N_HEADS * HEAD_DIM
    kvw = NSA_KV_HEADS * HEAD_DIM
    G, R = NSA_KV_HEADS, NSA_GROUP
    nc = seq // NSA_CMP_STRIDE
    nsb = seq // NSA_SEL_BLOCK
    n_top = min(NSA_TOP_N, nsb)
    nq = seq // TILE
    assert nc % CMP_CHUNK == 0
    par = lambda h: (h // R) % 2
    wq = _pad_heads_cols(w_in[:, :aw] * (ATTN_SCALE * LOG2_E), par).astype(BF16)
    w_cmp = w_in[:, aw:aw + 2 * kvw].astype(BF16)
    c0 = aw + 2 * kvw
    wk = jnp.concatenate([w_in[:, c0:c0 + kvw], w_in[:, c0 + 2 * kvw:c0 + 3 * kvw]], axis=1).astype(BF16)
    wv = jnp.concatenate([w_in[:, c0 + kvw:c0 + 2 * kvw], w_in[:, c0 + 3 * kvw:c0 + 4 * kvw]], axis=1)
    wv_t = _pad_heads_cols(wv, lambda h: 0).T.astype(BF16)
    wg = jnp.pad(w_in[:, aw + 6 * kvw:], ((0, 0), (0, LANES - 3 * N_HEADS))).astype(BF16)
    q = _matmul(x, wq, BF16).reshape(bsz, seq, N_HEADS * LANES)
    kk = _matmul(x, wk, BF16).reshape(bsz, seq, 2 * kvw)
    v_t = _matmul_t(x, wv_t)
    gates = _matmul(x, wg, F32, act='sigmoid')
    tok = _matmul_heads(x, w_cmp, bsz, seq).reshape(bsz, 2 * G, nc, NSA_CMP_STRIDE * HEAD_DIM)
    kc = _compress(tok, 0, pos_k, w1_k, w2_k, bsz, seq, True)
    vc = _compress(tok, G, pos_v, w1_v, w2_v, bsz, seq, False)

    q_spec = pl.BlockSpec((1, TILE, R * LANES), lambda b, g, i: (b, i, g))
    o_spec = pl.BlockSpec((1, TILE, R * LANES), lambda b, g, i: (b, i, g))
    gate_spec = pl.BlockSpec((TILE, LANES), lambda b, g, i: (b * nq + i, 0))
    o_shape = jax.ShapeDtypeStruct((bsz, seq, N_HEADS * LANES), BF16)
    sem = ("parallel", "parallel", "arbitrary")

    o_cmp, nsel = pl.pallas_call(
        functools.partial(_nsa_cmp_kernel, n_top=n_top),
        out_shape=[o_shape, jax.ShapeDtypeStruct((bsz, G, nsb, seq), F32)],
        grid=(bsz, G, nq),
        in_specs=[q_spec,
                  pl.BlockSpec((1, 1, nc, LANES), lambda b, g, i: (b, g, 0, 0)),
                  pl.BlockSpec((1, 1, LANES, nc), lambda b, g, i: (b, g, 0, 0)),
                  pl.BlockSpec((R, CMP_TILES, CMP_CHUNK, TILE), lambda b, g, i: (g, 0, 0, 0)),
                  gate_spec],
        out_specs=[o_spec, pl.BlockSpec((1, 1, nsb, TILE), lambda b, g, i: (b, g, 0, i))],
        compiler_params=_params(sem),
        name="nsa_compressed_attn",
    )(q, kc, vc, bias_cmp, gates)

    once = pl.Buffered(1)

    def k_spec(first_blk):
        return pl.BlockSpec((1, seq, LANES), lambda b, g, i: (b, 0, first_blk + g // 2), pipeline_mode=once)

    def vt_spec(first_blk):
        return pl.BlockSpec((LANES, seq), lambda b, g, i: (first_blk + g, b), pipeline_mode=once)

    o_sel = pl.pallas_call(
        functools.partial(_nsa_sparse_kernel, window=False),
        out_shape=o_shape,
        grid=(bsz, G, nq),
        in_specs=[q_spec, k_spec(0), vt_spec(0),
                  pl.BlockSpec((R, BIASED_TILES + NSA_FAR_WIDTH - 1, TILE, TILE), lambda b, g, i: (g, 0, 0, 0),
                               pipeline_mode=once),
                  gate_spec,
                  pl.BlockSpec((1, 1, nsb, TILE), lambda b, g, i: (b, g, 0, i))],
        out_specs=o_spec,
        compiler_params=_params(sem),
        name="nsa_selected_attn",
    )(q, kk, v_t, bias_main, gates, nsel)

    o_win = pl.pallas_call(
        functools.partial(_nsa_sparse_kernel, window=True),
        out_shape=o_shape,
        grid=(bsz, G, nq),
        in_specs=[q_spec, k_spec(2), vt_spec(G),
                  pl.BlockSpec((R, NSA_WINDOW // TILE + 1, TILE, TILE), lambda b, g, i: (g, 0, 0, 0)),
                  gate_spec],
        out_specs=o_spec,
        compiler_params=_params(sem),
        name="nsa_window_attn",
    )(q, kk, v_t, bias_win, gates)

    m = bsz * seq
    parts = [t.reshape(m, N_HEADS * LANES) for t in (o_cmp, o_sel, o_win)]
    return parts, _pad_heads_rows(w_out, lambda h: 0).astype(BF16)


def _router_kernel(x_ref, w_ref, b_ref, o_ref):
    logits = lax.dot_general(w_ref[...], x_ref[...], (((1,), (1,)), ((), ())),
                             preferred_element_type=F32, precision=lax.Precision.HIGHEST)
    scores = 1.0 / (1.0 + jnp.exp(-logits))
    biased = scores + b_ref[...]
    rows = [biased[e:e + 1, :] for e in range(N_EXPERTS)]
    group_score = []
    for g in range(N_GROUPS):
        r = rows[g * EXPERTS_PER_GROUP:(g + 1) * EXPERTS_PER_GROUP]
        best = None
        for a in range(EXPERTS_PER_GROUP):
            for c in range(a + 1, EXPERTS_PER_GROUP):
                pair = r[a] + r[c]
                best = pair if best is None else jnp.maximum(best, pair)
        group_score.append(best)
    best_val = group_score[0]
    best_grp = jnp.zeros_like(best_val, dtype=jnp.int32)
    for g in range(1, N_GROUPS):
        better = group_score[g] > best_val
        best_val = jnp.where(better, group_score[g], best_val)
        best_grp = jnp.where(better, g, best_grp)
    picked = []
    for e in range(N_EXPERTS):
        g, a = divmod(e, EXPERTS_PER_GROUP)
        rank = jnp.zeros_like(best_grp)
        for c in range(EXPERTS_PER_GROUP):
            if c == a:
                continue
            other = rows[g * EXPERTS_PER_GROUP + c]
            ahead = (other > rows[e]) | ((other == rows[e]) & (c < a))
            rank = rank + jnp.where(ahead, 1, 0)
        picked.append((best_grp == g) & (rank < 2))
    raw = [jnp.where(picked[e], scores[e:e + 1, :], 0.0) for e in range(N_EXPERTS)]
    total = raw[0]
    for e in range(1, N_EXPERTS):
        total = total + raw[e]
    o_ref[...] = (jnp.concatenate(raw, axis=0) / total).T


def _moe_kernel(x_ref, gate_ref, wg_ref, wu_ref, wd_ref, g_ref, b_ref, o_ref, xb_ref, acc_ref):
    e = pl.program_id(1)

    @pl.when(e == 0)
    def _():
        xb_ref[...] = x_ref[...].astype(BF16)
        acc_ref[...] = jnp.zeros_like(acc_ref)

    xb = xb_ref[...]
    lane = lax.broadcasted_iota(jnp.int32, gate_ref.shape, 1)
    gcol = jnp.sum(jnp.where(lane == e, gate_ref[...], 0.0), axis=-1, keepdims=True)
    a = _dot(xb, wg_ref[0])
    h = a / (1.0 + jnp.exp(-a)) * _dot(xb, wu_ref[0])
    acc_ref[...] += gcol * _dot(h.astype(BF16), wd_ref[0])

    @pl.when(e == N_EXPERTS - 1)
    def _():
        o_ref[...] = _layer_norm(DEEPNORM_ALPHA * x_ref[...] + acc_ref[...], g_ref[...], b_ref[...])


def _moe_ln(x, router_w, router_b, w_gate, w_up, w_down, g, b):
    m, d = x.shape
    tm = 1024
    gates = pl.pallas_call(
        _router_kernel,
        out_shape=jax.ShapeDtypeStruct((m, N_EXPERTS), F32),
        grid=(m // tm,),
        in_specs=[pl.BlockSpec((tm, d), lambda i: (i, 0)),
                  pl.BlockSpec((N_EXPERTS, d), lambda i: (0, 0)),
                  pl.BlockSpec((N_EXPERTS, 1), lambda i: (0, 0))],
        out_specs=pl.BlockSpec((tm, N_EXPERTS), lambda i: (i, 0)),
        compiler_params=_params(("parallel",)),
        name="moe_router",
    )(x, router_w.T, router_b.reshape(N_EXPERTS, 1))
    de = w_gate.shape[-1]
    return pl.pallas_call(
        _moe_kernel,
        out_shape=jax.ShapeDtypeStruct((m, d), F32),
        grid=(m // tm, N_EXPERTS),
        in_specs=[pl.BlockSpec((tm, d), lambda i, e: (i, 0)),
                  pl.BlockSpec((tm, N_EXPERTS), lambda i, e: (i, 0)),
                  pl.BlockSpec((1, d, de), lambda i, e: (e, 0, 0)),
                  pl.BlockSpec((1, d, de), lambda i, e: (e, 0, 0)),
                  pl.BlockSpec((1, de, d), lambda i, e: (e, 0, 0)),
                  pl.BlockSpec((1, d), lambda i, e: (0, 0)),
                  pl.BlockSpec((1, d), lambda i, e: (0, 0))],
        out_specs=pl.BlockSpec((tm, d), lambda i, e: (i, 0)),
        scratch_shapes=[pltpu.VMEM((tm, d), BF16), pltpu.VMEM((tm, d), F32)],
        compiler_params=_params(("parallel", "arbitrary")),
        name="moe_experts_ln",
    )(x, gates, w_gate.astype(BF16), w_up.astype(BF16), w_down.astype(BF16), g.reshape(1, d), b.reshape(1, d))


def kernel(x, rel_table, router_w, router_b, ln1_g, ln1_b, ln2_g, ln2_b, exp_w_gate, exp_w_up, exp_w_down, dil_w_in, dil_w_out, sb_w_in, sb_w_out, nsa_w_in, nsa_w_out, nsa_cmp_pos_k, nsa_cmp_w1_k, nsa_cmp_w2_k, nsa_cmp_pos_v, nsa_cmp_w1_v, nsa_cmp_w2_v, moba_w_in, moba_w_out):
    bsz, seq, d = x.shape
    assert d == D_MODEL and seq % (DILATED_PAIRS[-1][1] * DIL_BLOCK) == 0
    depth = ln1_g.shape[0]
    n_mixers = 4
    table = rel_table.astype(F32)
    span = DIL_BLOCK
    big = 1 << 30
    bias_dil = _bias_tiles(table, len(DILATED_PAIRS), DIL_BLOCK, 2 * DIL_BLOCK, DIL_BLOCK, 0, 1, 0, span,
                           [dl for _, dl in DILATED_PAIRS])
    assert BIASED_TILES * TILE - (TILE - 1) >= BUCKET_THRESHOLDS[-1]
    bias_main = _bias_tiles(table, MAIN_TILES, TILE, TILE, 0, TILE, 1, 0, big, [1], shift=True, transposed=True,
                            scale=LOG2_E)
    bias_win = _bias_tiles(table, NSA_WINDOW // TILE + 1, TILE, TILE, 0, TILE, 1, 0, NSA_WINDOW - 1, [1],
                           transposed=True, scale=LOG2_E)
    bias_cmp = _bias_tiles(table, CMP_TILES, CMP_CHUNK, TILE, -TILE - (NSA_CMP_LEN - 1), TILE, NSA_CMP_STRIDE,
                           0, big, [1], transposed=True, scale=LOG2_E)
    xf = x.reshape(bsz * seq, d)
    for layer in range(depth):
        kind, occ = layer % n_mixers, layer // n_mixers
        merge = None
        if kind == 0:
            parts, w_out = _mixer_dilated(xf, dil_w_in[occ], dil_w_out[occ], bias_dil, bsz, seq)
            merge = 'lse3'
        elif kind == 1:
            parts, w_out = _mixer_stick_breaking(xf, sb_w_in[occ], sb_w_out[occ], bsz, seq)
        elif kind == 2:
            parts, w_out = _mixer_nsa(xf, nsa_w_in[occ], nsa_w_out[occ], (bias_main, bias_win, bias_cmp),
                                      nsa_cmp_pos_k[occ], nsa_cmp_w1_k[occ], nsa_cmp_w2_k[occ],
                                      nsa_cmp_pos_v[occ], nsa_cmp_w1_v[occ], nsa_cmp_w2_v[occ], bsz, seq)
            merge = 'sum3'
        else:
            parts, w_out = _mixer_moba(xf, moba_w_in[occ], moba_w_out[occ], bias_main, bsz, seq)
        xf = _outproj_ln(parts, w_out, xf, ln1_g[layer], ln1_b[layer], merge=merge)
        xf = _moe_ln(xf, router_w, router_b, exp_w_gate[layer], exp_w_up[layer], exp_w_down[layer],
                     ln2_g[layer], ln2_b[layer])
    return xf.reshape(bsz, seq, d)
```

```python
import functools
import math

import numpy as np
import jax
import jax.numpy as jnp
from jax import lax
from jax.experimental import pallas as pl
from jax.experimental.pallas import tpu as pltpu

F32 = jnp.float32
BF16 = jnp.bfloat16

D_MODEL = 1024
HEAD_DIM = 64
N_HEADS = 16
LANES = 128
PAIRS = N_HEADS // 2
ATTN_SCALE = HEAD_DIM ** -0.5
REL_BUCKETS = 32
REL_MAX_DIST = 2048
DILATED_PAIRS = ((128, 1), (512, 4), (2048, 16))
DIL_BLOCK = 128
SB_CLIP = 60.0
SB_TILE = 256
LOG2_E = math.log2(math.e)
NSA_KV_HEADS = 4
NSA_GROUP = N_HEADS // NSA_KV_HEADS
NSA_CMP_LEN = 32
NSA_CMP_STRIDE = 16
NSA_SEL_BLOCK = 64
NSA_TOP_N = 16
NSA_WINDOW = 512
MOBA_BLOCK = 256
MOBA_TOP_K = 3
N_EXPERTS = 16
N_GROUPS = 4
EXPERTS_PER_GROUP = N_EXPERTS // N_GROUPS
DEPTH = 4
DEEPNORM_ALPHA = (2 * DEPTH) ** 0.25
LN_EPS = 1e-5
NEG_INF = -1e30
FORCED_SCORE = 1e9
TILE = 256
MOBA_FAR_WIDTH = 8
MOBA_HEADS = 4
NSA_FAR_WIDTH = 4
CMP_CHUNK = 128
VMEM_LIMIT = 56 * 1024 * 1024


def _bucket_thresholds():
    n = np.arange(0, 2 * REL_MAX_DIST)
    exact = REL_BUCKETS // 2
    logf = np.log(np.maximum(n, 1).astype(np.float64) / exact) / math.log(REL_MAX_DIST / exact)
    large = np.minimum(exact + (logf * (REL_BUCKETS - exact)).astype(np.int64), REL_BUCKETS - 1)
    bucket = np.where(n < exact, n, large)
    return tuple(int(np.argmax(bucket >= k)) for k in range(1, REL_BUCKETS))


BUCKET_THRESHOLDS = _bucket_thresholds()
BIASED_TILES = -(-(BUCKET_THRESHOLDS[-1] + TILE - 1) // TILE)
MAIN_TILES = BIASED_TILES + max(MOBA_FAR_WIDTH, NSA_FAR_WIDTH) - 1
CMP_BIASED_TILES = -(-(BUCKET_THRESHOLDS[-1] + NSA_CMP_STRIDE * (CMP_CHUNK - 1) + NSA_CMP_LEN - 1) // TILE)
CMP_TILES = CMP_BIASED_TILES + 2


def _params(sem, vmem=VMEM_LIMIT):
    return pltpu.CompilerParams(dimension_semantics=sem, vmem_limit_bytes=vmem)


def _dot(a, b):
    return jnp.dot(a, b, preferred_element_type=F32)


def _dot_nt(a, b):
    return lax.dot_general(a, b, (((1,), (1,)), ((), ())), preferred_element_type=F32)


def _split_bf16(x):
    hi = x.astype(BF16)
    lo = (x - hi.astype(F32)).astype(BF16)
    return hi, lo


def _matmul_kernel(x_ref, w_ref, o_ref, *, act):
    y = _dot(x_ref[...].astype(BF16), w_ref[...])
    if act == 'sigmoid':
        y = 1.0 / (1.0 + jnp.exp(-y))
    o_ref[...] = y.astype(o_ref.dtype)


def _matmul(x, w, out_dtype, act=None):
    m, k = x.shape
    n = w.shape[1]
    tm = 1024
    tn = next(t for t in (1024, 512, 128) if n % t == 0)
    return pl.pallas_call(
        functools.partial(_matmul_kernel, act=act),
        out_shape=jax.ShapeDtypeStruct((m, n), out_dtype),
        grid=(m // tm, n // tn),
        in_specs=[pl.BlockSpec((tm, k), lambda i, j: (i, 0)),
                  pl.BlockSpec((k, tn), lambda i, j: (0, j))],
        out_specs=pl.BlockSpec((tm, tn), lambda i, j: (i, j)),
        compiler_params=_params(("parallel", "arbitrary")),
        name="proj_matmul",
    )(x, w)


def _matmul_t_kernel(x_ref, w_ref, o_ref):
    y = _dot_nt(w_ref[...], x_ref[...].astype(BF16))
    row = lax.broadcasted_iota(jnp.int32, y.shape, 0)
    o_ref[...] = jnp.where(row % LANES == HEAD_DIM, 1.0, y).astype(o_ref.dtype)


def _matmul_t(x, wt):
    m, k = x.shape
    n = wt.shape[0]
    tm = 512
    tn = 512 if n % 512 == 0 else 256
    return pl.pallas_call(
        _matmul_t_kernel,
        out_shape=jax.ShapeDtypeStruct((n, m), BF16),
        grid=(m // tm, n // tn),
        in_specs=[pl.BlockSpec((tm, k), lambda i, j: (i, 0)),
                  pl.BlockSpec((tn, k), lambda i, j: (j, 0))],
        out_specs=pl.BlockSpec((tn, tm), lambda i, j: (j, i)),
        compiler_params=_params(("parallel", "arbitrary")),
        name="proj_matmul_t",
    )(x, wt)


def _matmul_heads_kernel(x_ref, w_ref, o_ref):
    y = _dot(x_ref[...].astype(BF16), w_ref[...]).astype(o_ref.dtype)
    for j in range(o_ref.shape[1]):
        o_ref[0, j] = y[:, j * HEAD_DIM:(j + 1) * HEAD_DIM]


def _matmul_heads(x, w, bsz, seq):
    m, k = x.shape
    n = w.shape[1] // HEAD_DIM
    tm = 512
    per_b = seq // tm
    return pl.pallas_call(
        _matmul_heads_kernel,
        out_shape=jax.ShapeDtypeStruct((bsz, n, seq, HEAD_DIM), BF16),
        grid=(m // tm,),
        in_specs=[pl.BlockSpec((tm, k), lambda i: (i, 0)),
                  pl.BlockSpec((k, n * HEAD_DIM), lambda i: (0, 0))],
        out_specs=pl.BlockSpec((1, n, tm, HEAD_DIM), lambda i: (i // per_b, 0, i % per_b, 0)),
        compiler_params=_params(("parallel",)),
        name="proj_heads",
    )(x, w)


def _layer_norm(z, g, b):
    mu = jnp.mean(z, axis=-1, keepdims=True)
    zc = z - mu
    var = jnp.mean(zc * zc, axis=-1, keepdims=True)
    return zc * lax.rsqrt(var + LN_EPS) * g + b


def _outproj_kernel(*refs, n_in, merge):
    ins = refs[:n_in]
    w_ref, x_ref, g_ref, b_ref, o_ref = refs[n_in:]
    if merge == 'lse3':
        o1, o2, o3, l1, l2, l3 = [r[...] for r in ins]
        mx = jnp.maximum(jnp.maximum(l1, l2), l3)
        e1, e2, e3 = jnp.exp(l1 - mx), jnp.exp(l2 - mx), jnp.exp(l3 - mx)
        a = (e1 * o1 + e2 * o2 + e3 * o3) / (e1 + e2 + e3)
    elif merge == 'sum3':
        a = ins[0][...].astype(F32) + ins[1][...].astype(F32) + ins[2][...].astype(F32)
    else:
        a = ins[0][...]
    y = _dot(a.astype(BF16), w_ref[...])
    o_ref[...] = _layer_norm(DEEPNORM_ALPHA * x_ref[...] + y, g_ref[...], b_ref[...])


def _outproj_ln(ins, w, x, g, b, merge=None):
    m, d = x.shape
    ka = ins[0].shape[1]
    tm = 256
    n_in = len(ins)
    return pl.pallas_call(
        functools.partial(_outproj_kernel, n_in=n_in, merge=merge),
        out_shape=jax.ShapeDtypeStruct((m, d), F32),
        grid=(m // tm,),
        in_specs=[pl.BlockSpec((tm, ka), lambda i: (i, 0)) for _ in ins]
        + [pl.BlockSpec((ka, d), lambda i: (0, 0)),
           pl.BlockSpec((tm, d), lambda i: (i, 0)),
           pl.BlockSpec((1, d), lambda i: (0, 0)),
           pl.BlockSpec((1, d), lambda i: (0, 0))],
        out_specs=pl.BlockSpec((tm, d), lambda i: (i, 0)),
        compiler_params=_params(("parallel",)),
        name="outproj_ln",
    )(*ins, w, x, g.reshape(1, d), b.reshape(1, d))


def _bias_kernel(tbl_ref, o_ref, *, rows, cols, off0, tstride, cmul, lo, hi, mults, shift, transposed, scale):
    t = pl.program_id(0)
    if len(set(mults)) == 1:
        mult = mults[0]
    else:
        mult = jnp.int32(mults[-1])
        for idx in range(len(mults) - 2, -1, -1):
            mult = jnp.where(t == idx, jnp.int32(mults[idx]), mult)
    base = off0 + tstride * t

    def strip(i, carry):
        r0 = pl.multiple_of(i * 8, 8)
        a = lax.broadcasted_iota(jnp.int32, (8, cols), 0) + r0
        c = lax.broadcasted_iota(jnp.int32, (8, cols), 1)
        steps = base + c - cmul * a if transposed else base + a - cmul * c
        valid = (steps >= lo) & (steps <= hi)
        dist = steps * mult
        for h in range(N_HEADS):
            acc = jnp.full((8, cols), tbl_ref[0, h], F32)
            for k, thr in enumerate(BUCKET_THRESHOLDS):
                acc = jnp.where(dist >= thr, tbl_ref[k + 1, h], acc)
            if shift:
                acc = acc - tbl_ref[REL_BUCKETS - 1, h]
            if scale != 1.0:
                acc = acc * scale
            o_ref[h, 0, pl.ds(r0, 8), :] = jnp.where(valid, acc, NEG_INF)
        return carry

    lax.fori_loop(0, rows // 8, strip, 0)


def _bias_tiles(table, n_tiles, rows, cols, off0, tstride, cmul, lo, hi, mults, shift=False, transposed=False,
                scale=1.0):
    return pl.pallas_call(
        functools.partial(_bias_kernel, rows=rows, cols=cols, off0=off0, tstride=tstride, cmul=cmul,
                          lo=lo, hi=hi, mults=tuple(mults), shift=shift, transposed=transposed, scale=scale),
        out_shape=jax.ShapeDtypeStruct((N_HEADS, n_tiles, rows, cols), F32),
        grid=(n_tiles,),
        in_specs=[pl.BlockSpec(memory_space=pltpu.SMEM)],
        out_specs=pl.BlockSpec((N_HEADS, 1, rows, cols), lambda t: (0, t, 0, 0)),
        compiler_params=_params(("arbitrary",)),
        name="bias_tiles",
    )(table)


def _dil_kernel(q_ref, kp_ref, kc_ref, vp_ref, vc_ref, bias_ref, o_ref, lse_ref,
                q_scr, kp_scr, kc_scr, vp_scr, vc_scr, *, dilation, per_step):
    j = pl.program_id(2)
    d = dilation
    n_cur = per_step * DIL_BLOCK

    def rows_of(ref, r, n, lead=()):
        if d == 1:
            return ref[lead + (slice(0, n), slice(None))]
        return ref[lead + (pl.ds(r, n, stride=d), slice(None))]

    if d > 1:
        for h in range(2):
            q_scr[h] = q_ref[0, :, h * LANES:(h + 1) * LANES].astype(F32)
        kp_scr[...] = kp_ref[0].astype(F32)
        kc_scr[...] = kc_ref[0].astype(F32)
        vp_scr[...] = vp_ref[0].astype(F32)
        vc_scr[...] = vc_ref[0].astype(F32)
    colk = lax.broadcasted_iota(jnp.int32, (DIL_BLOCK, 2 * DIL_BLOCK), 1)
    first = jnp.where((colk < DIL_BLOCK) & (j == 0), NEG_INF, 0.0)
    low = lax.broadcasted_iota(jnp.int32, (DIL_BLOCK, LANES), 1) < HEAD_DIM
    for r in range(d):
        if d == 1:
            q_heads = [q_ref[0, :, h * LANES:(h + 1) * LANES] for h in range(2)]
            k_all = jnp.concatenate([kp_ref[0], kc_ref[0]], axis=0)
            v_all = jnp.concatenate([vp_ref[0], vc_ref[0]], axis=0)
        else:
            q_heads = [rows_of(q_scr, r, n_cur, (h,)).astype(BF16) for h in range(2)]
            k_all = jnp.concatenate([rows_of(kp_scr, r, DIL_BLOCK), rows_of(kc_scr, r, n_cur)], axis=0).astype(BF16)
            v_all = jnp.concatenate([rows_of(vp_scr, r, DIL_BLOCK), rows_of(vc_scr, r, n_cur)], axis=0).astype(BF16)
        for u in range(per_step):
            k2 = k_all[u * DIL_BLOCK:(u + 2) * DIL_BLOCK]
            v2 = v_all[u * DIL_BLOCK:(u + 2) * DIL_BLOCK]
            outs, lses = [], []
            for h in range(2):
                s = _dot_nt(q_heads[h][u * DIL_BLOCK:(u + 1) * DIL_BLOCK], k2) + bias_ref[h, 0]
                if u == 0:
                    s = s + first
                m = jnp.max(s, axis=-1, keepdims=True)
                p = jnp.exp(s - m)
                l = jnp.maximum(jnp.sum(p, axis=-1, keepdims=True), 1e-30)
                outs.append(_dot(p.astype(BF16), v2) / l)
                lses.append(m + jnp.log(l))
            if d == 1:
                dst = (0, slice(u * DIL_BLOCK, (u + 1) * DIL_BLOCK), slice(None))
            else:
                dst = (0, pl.ds(u * DIL_BLOCK * d + r, DIL_BLOCK, stride=d), slice(None))
            o_ref[dst] = jnp.where(low, outs[0], outs[1])
            lse_ref[dst] = jnp.where(low, lses[0], lses[1])


def _dilated_group(proj, bias, g, dilation, bsz, seq):
    prev_rows = DIL_BLOCK * dilation
    step_rows = math.gcd(seq, 16 * DIL_BLOCK)
    per_step = step_rows // prev_rows
    assert per_step >= 1 and seq % step_rows == 0
    q_blk = g * N_HEADS // 2
    k_blk = 3 * N_HEADS + g * 2 * PAIRS
    v_blk = k_blk + PAIRS

    def kv_specs(base):
        prev = pl.BlockSpec((1, prev_rows, LANES), lambda hp, b, j: (b, jnp.maximum(per_step * j - 1, 0), base + hp))
        cur = pl.BlockSpec((1, step_rows, LANES), lambda hp, b, j: (b, j, base + hp))
        return [prev, cur]

    out_spec = pl.BlockSpec((1, step_rows, LANES), lambda hp, b, j: (b, j, hp))
    staged = 8 if dilation == 1 else None
    o, lse = pl.pallas_call(
        functools.partial(_dil_kernel, dilation=dilation, per_step=per_step),
        out_shape=[jax.ShapeDtypeStruct((bsz, seq, D_MODEL), F32)] * 2,
        grid=(PAIRS, bsz, seq // step_rows),
        in_specs=[pl.BlockSpec((1, step_rows, 2 * LANES), lambda hp, b, j: (b, j, q_blk + hp))]
        + kv_specs(k_blk) + kv_specs(v_blk)
        + [pl.BlockSpec((2, 1, DIL_BLOCK, 2 * DIL_BLOCK), lambda hp, b, j: (hp, g, 0, 0))],
        out_specs=[out_spec, out_spec],
        scratch_shapes=[pltpu.VMEM((2, staged or step_rows, LANES), F32),
                        pltpu.VMEM((staged or prev_rows, LANES), F32), pltpu.VMEM((staged or step_rows, LANES), F32),
                        pltpu.VMEM((staged or prev_rows, LANES), F32), pltpu.VMEM((staged or step_rows, LANES), F32)],
        compiler_params=_params(("parallel", "parallel", "arbitrary")),
        name="dilated_attn",
    )(proj, proj, proj, proj, proj, bias)
    return o.reshape(bsz * seq, D_MODEL), lse.reshape(bsz * seq, D_MODEL)


def _pad_heads_cols(w, parity_of_head):
    k, n = w.shape
    nh = n // HEAD_DIM
    wh = w.reshape(k, nh, HEAD_DIM)
    z = jnp.zeros_like(wh)
    par = jnp.asarray([parity_of_head(h) for h in range(nh)], jnp.int32)[None, :, None]
    lo = jnp.where(par == 0, wh, z)
    hi = jnp.where(par == 1, wh, z)
    return jnp.concatenate([lo, hi], axis=-1).reshape(k, nh * LANES)


def _pad_heads_rows(w, parity_of_head):
    return _pad_heads_cols(w.T, parity_of_head).T


def _mixer_dilated(x, w_in, w_out, bias_dil, bsz, seq):
    aw = N_HEADS * HEAD_DIM
    par = lambda h: h % 2
    qs, kvs = [], []
    for g in range(len(DILATED_PAIRS)):
        base = g * 3 * aw
        qs.append(_pad_heads_cols(w_in[:, base:base + aw] * ATTN_SCALE, par))
        kvs.append(w_in[:, base + aw:base + 3 * aw])
    w_all = jnp.concatenate(qs + kvs, axis=1).astype(BF16)
    proj = _matmul(x, w_all, BF16).reshape(bsz, seq, -1)
    parts = [_dilated_group(proj, bias_dil, g, d, bsz, seq) for g, (_, d) in enumerate(DILATED_PAIRS)]
    return [p[0] for p in parts] + [p[1] for p in parts], w_out.astype(BF16)


def _sb_kernel(q_ref, k_ref, v_ref, o_ref):
    i = pl.program_id(2)
    t = SB_TILE
    row = lax.broadcasted_iota(jnp.int32, (t, t), 0)
    col = lax.broadcasted_iota(jnp.int32, (t, t), 1)
    suffix = jnp.where(row >= col, 1.0, 0.0).astype(BF16)
    strict = col < row
    q2 = q_ref[0]
    q_heads = [q2[:, h * LANES:(h + 1) * LANES] for h in range(2)]

    def block(qh, kj, masked):
        start = pl.multiple_of(kj * t, t)
        kb = k_ref[0, pl.ds(start, t), :]
        vb = v_ref[0, pl.ds(start, t), :]
        a = jnp.clip(_dot_nt(qh, kb), -SB_CLIP, SB_CLIP) * LOG2_E
        sp = jnp.log2(1.0 + jnp.exp2(a))
        if masked:
            sp = jnp.where(strict, sp, 0.0)
        hi, lo = _split_bf16(sp)
        rr = _dot(hi, suffix) + _dot(lo, suffix)
        att = jnp.exp2(a - rr)
        if masked:
            att = jnp.where(strict, att, 0.0)
        return _dot(att.astype(BF16), vb), rr[:, 0:1]

    def first_two(kj, carry):
        has_prev = jnp.where(kj > 0, 1.0, 0.0)
        out = []
        for qh in q_heads:
            acc_d, r_d = block(qh, kj, True)
            pv, r_p = block(qh, jnp.maximum(kj - 1, 0), False)
            out.append((acc_d + (has_prev * jnp.exp2(-r_d)) * pv, r_d + has_prev * r_p))
        return tuple(out)

    zero = (jnp.zeros((t, LANES), F32), jnp.zeros((t, 1), F32))
    (acc0, r0), (acc1, r1) = lax.fori_loop(i, i + 1, first_two, (zero, zero))

    def cond(carry):
        n, alive = carry[0], carry[1]
        return (n < i - 1) & alive

    def step(carry):
        n = carry[0]
        new, tops = [], []
        for qh, (r_sum, w, acc) in zip(q_heads, carry[2:]):
            pv, r_blk = block(qh, i - 2 - n, False)
            r_new = r_sum + r_blk
            w_new = jnp.exp2(-r_new)
            new.append((r_new, w_new, acc + w * pv))
            tops.append(jnp.max(w_new, axis=0, keepdims=True))
        alive = jnp.maximum(tops[0], tops[1])[0, 0] > 0.0
        return (n + 1, alive) + tuple(new)

    w0, w1 = jnp.exp2(-r0), jnp.exp2(-r1)
    alive0 = jnp.maximum(jnp.max(w0, axis=0, keepdims=True), jnp.max(w1, axis=0, keepdims=True))[0, 0] > 0.0
    init = (jnp.int32(0), alive0, (r0, w0, acc0), (r1, w1, acc1))
    res = lax.while_loop(cond, step, init)
    lane = lax.broadcasted_iota(jnp.int32, (t, LANES), 1)
    o_ref[0] = jnp.where(lane < HEAD_DIM, res[2][2], res[3][2]).astype(o_ref.dtype)


def _mixer_stick_breaking(x, w_in, w_out, bsz, seq):
    aw = N_HEADS * HEAD_DIM
    wq = _pad_heads_cols(w_in[:, :aw] * ATTN_SCALE, lambda h: h % 2)
    w_all = jnp.concatenate([wq, w_in[:, aw:]], axis=1).astype(BF16)
    proj = _matmul(x, w_all, BF16).reshape(bsz, seq, -1)
    kb = N_HEADS
    o = pl.pallas_call(
        _sb_kernel,
        out_shape=jax.ShapeDtypeStruct((bsz, seq, D_MODEL), BF16),
        grid=(bsz, PAIRS, seq // SB_TILE),
        in_specs=[pl.BlockSpec((1, SB_TILE, 2 * LANES), lambda b, hp, i: (b, i, hp)),
                  pl.BlockSpec((1, seq, LANES), lambda b, hp, i: (b, 0, kb + hp)),
                  pl.BlockSpec((1, seq, LANES), lambda b, hp, i: (b, 0, kb + PAIRS + hp))],
        out_specs=pl.BlockSpec((1, SB_TILE, LANES), lambda b, hp, i: (b, i, hp)),
        compiler_params=_params(("parallel", "parallel", "arbitrary")),
        name="stick_breaking_attn",
    )(proj, proj, proj)
    return [o.reshape(bsz * seq, D_MODEL)], w_out.astype(BF16)


def _flash_chunks(q_heads, k_ref, k_cols, v_ref, v_rows, bias_ref, mask_rows, lo, hi, i, blocks_per_chunk, n_tiles,
                  fold_mask=False, pipelined=False, far_width=1, near_width=1):
    nh = len(q_heads)

    def logits(ch, near, width):
        rows = width * TILE
        start = pl.multiple_of(ch * TILE, TILE)
        kbs = {c: k_ref[0, pl.ds(start, rows), c:c + LANES] for c in sorted(set(k_cols))}
        add = None
        if fold_mask:
            key = lax.broadcasted_iota(jnp.int32, (rows, LANES), 0)
            blk = lax.broadcasted_iota(jnp.int32, (rows, LANES), 1)
            blk_of_key = ch * blocks_per_chunk + key // (TILE // blocks_per_chunk)
            ind = jnp.where(blk == blk_of_key, NEG_INF, 0.0).astype(BF16)
            kbs = {c: jnp.concatenate([kb, ind], axis=1) for c, kb in kbs.items()}
        elif mask_rows is not None:
            nb = width * blocks_per_chunk
            per = TILE // blocks_per_chunk
            blk_rows = mask_rows(ch * blocks_per_chunk, nb)
            add = jnp.concatenate([jnp.broadcast_to(blk_rows[r:r + 1, :], (per, TILE)) for r in range(nb)], axis=0)
        out = []
        for h in range(nh):
            s = _dot_nt(kbs[k_cols[h]], q_heads[h])
            if near:
                tiles = [bias_ref[h, jnp.maximum(i - ch - w, 0)] for w in range(width)]
                s = s + (tiles[0] if width == 1 else jnp.concatenate(tiles, axis=0))
            if add is not None:
                s = s + add
            out.append(s)
        return tuple(out)

    def update(ch, s_heads, state, width):
        start = pl.multiple_of(ch * TILE, TILE)
        vts = {}
        new = []
        for h in range(nh):
            r0 = v_rows[h]
            if r0 not in vts:
                vts[r0] = v_ref[r0:r0 + LANES, pl.ds(start, width * TILE)]
            m, acc = state[h]
            s = s_heads[h]
            m_new = jnp.maximum(m, jnp.max(s, axis=0, keepdims=True))
            alpha = jnp.exp2(m - m_new)
            p = jnp.exp2(s - m_new)
            acc = alpha * acc + _dot(vts[r0], p.astype(BF16))
            new.append((m_new, acc))
        return tuple(new)

    def run(ch0, steps, near, width, state, pipelined=pipelined):
        if not pipelined:
            return lax.fori_loop(
                0, steps, lambda t, st: update(ch0 + t * width, logits(ch0 + t * width, near, width), st, width), state)

        def body(t, carry):
            s_cur, st = carry
            s_next = logits(ch0 + jnp.minimum(t + 1, steps - 1) * width, near, width)
            return s_next, update(ch0 + t * width, s_cur, st, width)

        zeros = tuple(jnp.zeros((width * TILE, TILE), F32) for _ in range(nh))
        first = jnp.minimum(ch0, i)
        s0 = lax.fori_loop(first, first + 1, lambda c, _: logits(c, near, width), zeros)
        return lax.fori_loop(0, steps, body, (s0, state))[1]

    state = tuple((jnp.full((1, TILE), NEG_INF, F32), jnp.zeros((LANES, TILE), F32)) for _ in range(nh))
    n_near = n_tiles - (far_width - 1)
    far_steps = jnp.maximum(hi - n_near - lo, 0) // far_width
    split = lo + far_steps * far_width
    state = run(lo, far_steps, False, far_width, state)
    near_pairs = (hi - split) // near_width if near_width > 1 else 0
    if near_width > 1:
        state = run(split, near_pairs, True, near_width, state)
    single = split + near_pairs * near_width
    state = run(single, hi - single, True, 1, state, pipelined=pipelined and near_width == 1)
    outs = []
    for _, acc in state:
        o = (acc / acc[HEAD_DIM:HEAD_DIM + 1, :]).T
        lane = lax.broadcasted_iota(jnp.int32, o.shape, 1)
        outs.append(jnp.where(lane < HEAD_DIM, o, 0.0))
    return outs


def _kmean_kernel(k_ref, o_ref):
    rows = k_ref.shape[1]
    kf = k_ref[0].astype(F32).reshape(rows // MOBA_BLOCK, MOBA_BLOCK, k_ref.shape[2])
    o_ref[0] = jnp.sum(kf, axis=1) * (1.0 / MOBA_BLOCK)


def _moba_kernel(q_ref, k_ref, v_ref, km_ref, bias_ref, o_ref):
    i = pl.program_id(2)
    nblk = km_ref.shape[1]
    km_hi, km_lo = _split_bf16(km_ref[0])
    qs = q_ref[0]
    blk = lax.broadcasted_iota(jnp.int32, (nblk, TILE), 0)
    past = blk < i
    q_heads = []
    for h in range(MOBA_HEADS):
        qh = qs[:, h * LANES:(h + 1) * LANES]
        pair = slice((h // 2) * LANES, (h // 2 + 1) * LANES)
        gate = jnp.where(past, _dot_nt(km_hi[:, pair], qh) + _dot_nt(km_lo[:, pair], qh), NEG_INF)
        allowed = blk == i
        for _ in range(MOBA_TOP_K):
            mx = jnp.max(gate, axis=0, keepdims=True)
            first = jnp.min(jnp.where(gate == mx, blk, nblk), axis=0, keepdims=True)
            pick = blk == first
            allowed = allowed | (pick & past)
            gate = jnp.where(pick, -jnp.inf, gate)
        nsel = jnp.where(allowed, 0.0, 1.0).T.astype(BF16)
        q_heads.append(jnp.concatenate([qh, nsel], axis=1))
    outs = _flash_chunks(q_heads, k_ref, [(h // 2) * LANES for h in range(MOBA_HEADS)],
                         v_ref, [h * LANES for h in range(MOBA_HEADS)], bias_ref, None, 0, i + 1, i, 1,
                         BIASED_TILES + MOBA_FAR_WIDTH - 1, fold_mask=True, pipelined=False, far_width=MOBA_FAR_WIDTH,
                         near_width=2)
    for h in range(MOBA_HEADS):
        o_ref[0, :, h * LANES:(h + 1) * LANES] = outs[h].astype(o_ref.dtype)


def _mixer_moba(x, w_in, w_out, bias_main, bsz, seq):
    aw = N_HEADS * HEAD_DIM
    nblk = seq // MOBA_BLOCK
    assert nblk - 1 >= MOBA_TOP_K
    wq = _pad_heads_cols(w_in[:, :aw] * (ATTN_SCALE * LOG2_E), lambda h: h % 2)
    w_all = jnp.concatenate([wq, w_in[:, aw:2 * aw]], axis=1).astype(BF16)
    proj = _matmul(x, w_all, BF16).reshape(bsz, seq, -1)
    v_t = _matmul_t(x, _pad_heads_cols(w_in[:, 2 * aw:], lambda h: 0).T.astype(BF16))
    kb = N_HEADS
    rows = 8 * MOBA_BLOCK
    kmean = pl.pallas_call(
        _kmean_kernel,
        out_shape=jax.ShapeDtypeStruct((bsz, nblk, D_MODEL), F32),
        grid=(bsz, seq // rows),
        in_specs=[pl.BlockSpec((1, rows, D_MODEL), lambda b, i: (b, i, kb * LANES // D_MODEL))],
        out_specs=pl.BlockSpec((1, 8, D_MODEL), lambda b, i: (b, i, 0)),
        compiler_params=_params(("parallel", "parallel")),
        name="moba_kmean",
    )(proj)
    assert nblk <= LANES
    kmean = jnp.pad(kmean, ((0, 0), (0, LANES - nblk), (0, 0)))
    nh = MOBA_HEADS
    kw = nh // 2 * LANES
    once = pl.Buffered(1)
    o = pl.pallas_call(
        _moba_kernel,
        out_shape=jax.ShapeDtypeStruct((bsz, seq, N_HEADS * LANES), BF16),
        grid=(bsz, N_HEADS // nh, seq // TILE),
        in_specs=[pl.BlockSpec((1, TILE, nh * LANES), lambda b, hq, i: (b, i, hq)),
                  pl.BlockSpec((1, seq, kw), lambda b, hq, i: (b, 0, kb * LANES // kw + hq), pipeline_mode=once),
                  pl.BlockSpec((nh * LANES, seq), lambda b, hq, i: (hq, b), pipeline_mode=once),
                  pl.BlockSpec((1, LANES, kw), lambda b, hq, i: (b, 0, hq)),
                  pl.BlockSpec((nh, BIASED_TILES + MOBA_FAR_WIDTH - 1, TILE, TILE), lambda b, hq, i: (hq, 0, 0, 0),
                               pipeline_mode=once)],
        out_specs=pl.BlockSpec((1, TILE, nh * LANES), lambda b, hq, i: (b, i, hq)),
        compiler_params=_params(("parallel", "parallel", "arbitrary")),
        name="moba_attn",
    )(proj, proj, v_t, kmean, bias_main)
    return [o.reshape(bsz * seq, N_HEADS * LANES)], _pad_heads_rows(w_out, lambda h: 0).astype(BF16)


def _gelu_tanh(x):
    return 0.5 * x * (1.0 + jnp.tanh(math.sqrt(2.0 / math.pi) * (x + 0.044715 * (x * x * x))))


def _compress_kernel(a_ref, pos_ref, w1_ref, w2_ref, o_ref, *, transpose_out):
    nc = a_ref.shape[2]
    half = NSA_CMP_STRIDE * HEAD_DIM
    a = a_ref[0, 0].astype(F32)
    top = _dot((a + pos_ref[0:1, :]).astype(BF16), w1_ref[0:half, :])
    bot = _dot((a + pos_ref[1:2, :]).astype(BF16), w1_ref[half:2 * half, :])
    hid = top + pltpu.roll(bot, nc - 1, 0)
    out = _dot(_gelu_tanh(hid).astype(BF16), w2_ref[0])
    rowi = lax.broadcasted_iota(jnp.int32, out.shape, 0)
    out = jnp.where(rowi < nc - 1, out, 0.0)
    o_ref[0, 0] = (out.T if transpose_out else out).astype(o_ref.dtype)


def _compress(tok, first, pos, w1, w2, bsz, seq, by_parity):
    nc = seq // NSA_CMP_STRIDE
    half = NSA_CMP_STRIDE * HEAD_DIM
    pos2 = pos.reshape(2, half)
    z = jnp.zeros_like(w2)
    low = jnp.concatenate([w2, z], axis=1)
    w2p = jnp.stack([low, jnp.concatenate([z, w2], axis=1) if by_parity else low]).astype(BF16)
    out_dims = (nc, LANES) if by_parity else (LANES, nc)
    return pl.pallas_call(
        functools.partial(_compress_kernel, transpose_out=not by_parity),
        out_shape=jax.ShapeDtypeStruct((bsz, NSA_KV_HEADS) + out_dims, BF16),
        grid=(bsz, NSA_KV_HEADS),
        in_specs=[pl.BlockSpec((1, 1, nc, half), lambda b, g: (b, first + g, 0, 0)),
                  pl.BlockSpec((2, half), lambda b, g: (0, 0)),
                  pl.BlockSpec((2 * half, w1.shape[1]), lambda b, g: (0, 0)),
                  pl.BlockSpec((1, w2.shape[0], LANES), lambda b, g: (g % 2, 0, 0))],
        out_specs=pl.BlockSpec((1, 1) + out_dims, lambda b, g: (b, g, 0, 0)),
        compiler_params=_params(("parallel", "parallel")),
        name="nsa_compress",
    )(tok, pos2, w1.astype(BF16), w2p)


def _gate_columns(gate_ref, branch):
    tile = gate_ref[...]
    lane = lax.broadcasted_iota(jnp.int32, tile.shape, 1)
    base = branch * N_HEADS + pl.program_id(1) * NSA_GROUP
    return [jnp.sum(jnp.where(lane == base + r, tile, 0.0), axis=-1, keepdims=True) for r in range(NSA_GROUP)]


def _nsa_cmp_kernel(q_ref, kc_ref, vc_ref, bias_ref, gate_ref, o_ref, sel_ref, *, n_top):
    i = pl.program_id(2)
    nc = kc_ref.shape[2]
    n_chunks = nc // CMP_CHUNK
    nsb = sel_ref.shape[2]
    q4 = q_ref[0]
    gates = _gate_columns(gate_ref, 0)
    keep = lax.broadcasted_iota(jnp.int32, (TILE, LANES), 1) < HEAD_DIM
    psum = [jnp.zeros((CMP_CHUNK, TILE), F32) for _ in range(n_chunks)]
    for r in range(NSA_GROUP):
        qh = q4[:, r * LANES:(r + 1) * LANES]
        ss = []
        for c in range(n_chunks):
            tile = jnp.clip(i - (CMP_CHUNK * NSA_CMP_STRIDE // TILE) * c, -1, CMP_TILES - 2) + 1
            ss.append(_dot_nt(kc_ref[0, 0, c * CMP_CHUNK:(c + 1) * CMP_CHUNK, :], qh) + bias_ref[r, tile])
        m = ss[0].max(axis=0, keepdims=True)
        for c in range(1, n_chunks):
            m = jnp.maximum(m, ss[c].max(axis=0, keepdims=True))
        ps = [jnp.exp2(s - m) for s in ss]
        l = ps[0].sum(axis=0, keepdims=True)
        for c in range(1, n_chunks):
            l = l + ps[c].sum(axis=0, keepdims=True)
        inv = jnp.where(m > 0.5 * NEG_INF, 1.0 / jnp.maximum(l, 1e-30), 0.0)
        acc = jnp.zeros((LANES, TILE), F32)
        for c in range(n_chunks):
            pc = ps[c] * inv
            psum[c] = psum[c] + pc
            acc = acc + _dot(vc_ref[0, 0, :, c * CMP_CHUNK:(c + 1) * CMP_CHUNK], pc.astype(BF16))
        o_ref[0, :, r * LANES:(r + 1) * LANES] = jnp.where(keep, acc.T * gates[r], 0.0).astype(o_ref.dtype)
    imp = jnp.zeros((nsb, TILE), F32)
    per_sel = NSA_SEL_BLOCK // NSA_CMP_STRIDE
    for c in range(n_chunks):
        jb = lax.broadcasted_iota(jnp.int32, (nsb, CMP_CHUNK), 0)
        ci = lax.broadcasted_iota(jnp.int32, (nsb, CMP_CHUNK), 1) + c * CMP_CHUNK
        rel = ci - per_sel * jb
        over = (rel >= 1 - NSA_CMP_LEN // NSA_CMP_STRIDE) & (rel < per_sel) & (ci < nc - 1)
        ov = jnp.where(over, 1.0, 0.0).astype(BF16)
        hi, lo = _split_bf16(psum[c])
        imp = imp + _dot(ov, hi) + _dot(ov, lo)
    jb = lax.broadcasted_iota(jnp.int32, (nsb, TILE), 0)
    qpos = lax.broadcasted_iota(jnp.int32, (nsb, TILE), 1) + i * TILE
    cur = qpos // NSA_SEL_BLOCK
    forced = (jb == 0) | (jb == cur) | (jb == cur - 1)
    causal = jb * NSA_SEL_BLOCK <= qpos
    score = jnp.where(forced, FORCED_SCORE, jnp.where(causal, imp, NEG_INF))
    chosen = jnp.zeros((nsb, TILE), F32)
    for _ in range(n_top):
        mx = jnp.max(score, axis=0, keepdims=True)
        first = jnp.min(jnp.where(score == mx, jb, nsb), axis=0, keepdims=True)
        pick = jb == first
        chosen = jnp.where(pick, 1.0, chosen)
        score = jnp.where(pick, -jnp.inf, score)
    sel_ref[0, 0] = jnp.where((chosen > 0.0) & causal, 0.0, NEG_INF)


def _nsa_sparse_kernel(q_ref, k_ref, v_ref, bias_ref, gate_ref, *rest, window):
    if window:
        (o_ref,) = rest
        mask_rows = None
    else:
        sel_ref, o_ref = rest

        def mask_rows(b0, n):
            if n % 8 == 0:
                return sel_ref[0, 0, pl.ds(pl.multiple_of(b0, 8), n), :]
            assert n == 4
            rows = sel_ref[0, 0, pl.ds(pl.multiple_of(b0 // 8 * 8, 8), 8), :]
            return jnp.where(b0 % 8 == 0, rows[0:4], rows[4:8])

    i = pl.program_id(2)
    q4 = q_ref[0]
    gates = _gate_columns(gate_ref, 2 if window else 1)
    q_heads = [q4[:, r * LANES:(r + 1) * LANES] for r in range(NSA_GROUP)]
    if window:
        n_tiles = NSA_WINDOW // TILE + 1
        lo = jnp.maximum(i - (n_tiles - 1), 0)
        outs = _flash_chunks(q_heads, k_ref, [0] * NSA_GROUP, v_ref, [0] * NSA_GROUP, bias_ref, None,
                             lo, i + 1, i, 1, n_tiles, near_width=2)
    else:
        outs = _flash_chunks(q_heads, k_ref, [0] * NSA_GROUP, v_ref, [0] * NSA_GROUP, bias_ref, mask_rows,
                             0, i + 1, i,
                             TILE // NSA_SEL_BLOCK, BIASED_TILES + NSA_FAR_WIDTH - 1, far_width=NSA_FAR_WIDTH,
                             near_width=2)
    for r in range(NSA_GROUP):
        o_ref[0, :, r * LANES:(r + 1) * LANES] = (outs[r] * gates[r]).astype(o_ref.dtype)


def _mixer_nsa(x, w_in, w_out, table_bias, pos_k, w1_k, w2_k, pos_v, w1_v, w2_v, bsz, seq):
    bias_main, bias_win, bias_cmp = table_bias
    aw = N_HEADS * HEAD_DIM
    kvw = NSA_KV_HEADS * HEAD_DIM
    G, R = NSA_KV_HEADS, NSA_GROUP
    nc = seq // NSA_CMP_STRIDE
    nsb = seq // NSA_SEL_BLOCK
    n_top = min(NSA_TOP_N, nsb)
    nq = seq // TILE
    assert nc % CMP_CHUNK == 0
    par = lambda h: (h // R) % 2
    wq = _pad_heads_cols(w_in[:, :aw] * (ATTN_SCALE * LOG2_E), par).astype(BF16)
    w_cmp = w_in[:, aw:aw + 2 * kvw].astype(BF16)
    c0 = aw + 2 * kvw
    wk = jnp.concatenate([w_in[:, c0:c0 + kvw], w_in[:, c0 + 2 * kvw:c0 + 3 * kvw]], axis=1).astype(BF16)
    wv = jnp.concatenate([w_in[:, c0 + kvw:c0 + 2 * kvw], w_in[:, c0 + 3 * kvw:c0 + 4 * kvw]], axis=1)
    wv_t = _pad_heads_cols(wv, lambda h: 0).T.astype(BF16)
    wg = jnp.pad(w_in[:, aw + 6 * kvw:], ((0, 0), (0, LANES - 3 * N_HEADS))).astype(BF16)
    q = _matmul(x, wq, BF16).reshape(bsz, seq, N_HEADS * LANES)
    kk = _matmul(x, wk, BF16).reshape(bsz, seq, 2 * kvw)
    v_t = _matmul_t(x, wv_t)
    gates = _matmul(x, wg, F32, act='sigmoid')
    tok = _matmul_heads(x, w_cmp, bsz, seq).reshape(bsz, 2 * G, nc, NSA_CMP_STRIDE * HEAD_DIM)
    kc = _compress(tok, 0, pos_k, w1_k, w2_k, bsz, seq, True)
    vc = _compress(tok, G, pos_v, w1_v, w2_v, bsz, seq, False)

    q_spec = pl.BlockSpec((1, TILE, R * LANES), lambda b, g, i: (b, i, g))
    o_spec = pl.BlockSpec((1, TILE, R * LANES), lambda b, g, i: (b, i, g))
    gate_spec = pl.BlockSpec((TILE, LANES), lambda b, g, i: (b * nq + i, 0))
    o_shape = jax.ShapeDtypeStruct((bsz, seq, N_HEADS * LANES), BF16)
    sem = ("parallel", "parallel", "arbitrary")

    o_cmp, nsel = pl.pallas_call(
        functools.partial(_nsa_cmp_kernel, n_top=n_top),
        out_shape=[o_shape, jax.ShapeDtypeStruct((bsz, G, nsb, seq), F32)],
        grid=(bsz, G, nq),
        in_specs=[q_spec,
                  pl.BlockSpec((1, 1, nc, LANES), lambda b, g, i: (b, g, 0, 0)),
                  pl.BlockSpec((1, 1, LANES, nc), lambda b, g, i: (b, g, 0, 0)),
                  pl.BlockSpec((R, CMP_TILES, CMP_CHUNK, TILE), lambda b, g, i: (g, 0, 0, 0)),
                  gate_spec],
        out_specs=[o_spec, pl.BlockSpec((1, 1, nsb, TILE), lambda b, g, i: (b, g, 0, i))],
        compiler_params=_params(sem),
        name="nsa_compressed_attn",
    )(q, kc, vc, bias_cmp, gates)

    once = pl.Buffered(1)

    def k_spec(first_blk):
        return pl.BlockSpec((1, seq, LANES), lambda b, g, i: (b, 0, first_blk + g // 2), pipeline_mode=once)

    def vt_spec(first_blk):
        return pl.BlockSpec((LANES, seq), lambda b, g, i: (first_blk + g, b), pipeline_mode=once)

    o_sel = pl.pallas_call(
        functools.partial(_nsa_sparse_kernel, window=False),
        out_shape=o_shape,
        grid=(bsz, G, nq),
        in_specs=[q_spec, k_spec(0), vt_spec(0),
                  pl.BlockSpec((R, BIASED_TILES + NSA_FAR_WIDTH - 1, TILE, TILE), lambda b, g, i: (g, 0, 0, 0),
                               pipeline_mode=once),
                  gate_spec,
                  pl.BlockSpec((1, 1, nsb, TILE), lambda b, g, i: (b, g, 0, i))],
        out_specs=o_spec,
        compiler_params=_params(sem),
        name="nsa_selected_attn",
    )(q, kk, v_t, bias_main, gates, nsel)

    o_win = pl.pallas_call(
        functools.partial(_nsa_sparse_kernel, window=True),
        out_shape=o_shape,
        grid=(bsz, G, nq),
        in_specs=[q_spec, k_spec(2), vt_spec(G),
                  pl.BlockSpec((R, NSA_WINDOW // TILE + 1, TILE, TILE), lambda b, g, i: (g, 0, 0, 0)),
                  gate_spec],
        out_specs=o_spec,
        compiler_params=_params(sem),
        name="nsa_window_attn",
    )(q, kk, v_t, bias_win, gates)

    m = bsz * seq
    parts = [t.reshape(m, N_HEADS * LANES) for t in (o_cmp, o_sel, o_win)]
    return parts, _pad_heads_rows(w_out, lambda h: 0).astype(BF16)


def _router_kernel(x_ref, w_ref, b_ref, o_ref):
    logits = lax.dot_general(w_ref[...], x_ref[...], (((1,), (1,)), ((), ())),
                             preferred_element_type=F32, precision=lax.Precision.HIGHEST)
    scores = 1.0 / (1.0 + jnp.exp(-logits))
    biased = scores + b_ref[...]
    rows = [biased[e:e + 1, :] for e in range(N_EXPERTS)]
    group_score = []
    for g in range(N_GROUPS):
        r = rows[g * EXPERTS_PER_GROUP:(g + 1) * EXPERTS_PER_GROUP]
        best = None
        for a in range(EXPERTS_PER_GROUP):
            for c in range(a + 1, EXPERTS_PER_GROUP):
                pair = r[a] + r[c]
                best = pair if best is None else jnp.maximum(best, pair)
        group_score.append(best)
    best_val = group_score[0]
    best_grp = jnp.zeros_like(best_val, dtype=jnp.int32)
    for g in range(1, N_GROUPS):
        better = group_score[g] > best_val
        best_val = jnp.where(better, group_score[g], best_val)
        best_grp = jnp.where(better, g, best_grp)
    picked = []
    for e in range(N_EXPERTS):
        g, a = divmod(e, EXPERTS_PER_GROUP)
        rank = jnp.zeros_like(best_grp)
        for c in range(EXPERTS_PER_GROUP):
            if c == a:
                continue
            other = rows[g * EXPERTS_PER_GROUP + c]
            ahead = (other > rows[e]) | ((other == rows[e]) & (c < a))
            rank = rank + jnp.where(ahead, 1, 0)
        picked.append((best_grp == g) & (rank < 2))
    raw = [jnp.where(picked[e], scores[e:e + 1, :], 0.0) for e in range(N_EXPERTS)]
    total = raw[0]
    for e in range(1, N_EXPERTS):
        total = total + raw[e]
    o_ref[...] = (jnp.concatenate(raw, axis=0) / total).T


def _moe_kernel(x_ref, gate_ref, wg_ref, wu_ref, wd_ref, g_ref, b_ref, o_ref, xb_ref, acc_ref):
    e = pl.program_id(1)

    @pl.when(e == 0)
    def _():
        xb_ref[...] = x_ref[...].astype(BF16)
        acc_ref[...] = jnp.zeros_like(acc_ref)

    xb = xb_ref[...]
    lane = lax.broadcasted_iota(jnp.int32, gate_ref.shape, 1)
    gcol = jnp.sum(jnp.where(lane == e, gate_ref[...], 0.0), axis=-1, keepdims=True)
    a = _dot(xb, wg_ref[0])
    h = a / (1.0 + jnp.exp(-a)) * _dot(xb, wu_ref[0])
    acc_ref[...] += gcol * _dot(h.astype(BF16), wd_ref[0])

    @pl.when(e == N_EXPERTS - 1)
    def _():
        o_ref[...] = _layer_norm(DEEPNORM_ALPHA * x_ref[...] + acc_ref[...], g_ref[...], b_ref[...])


def _moe_ln(x, router_w, router_b, w_gate, w_up, w_down, g, b):
    m, d = x.shape
    tm = 1024
    gates = pl.pallas_call(
        _router_kernel,
        out_shape=jax.ShapeDtypeStruct((m, N_EXPERTS), F32),
        grid=(m // tm,),
        in_specs=[pl.BlockSpec((tm, d), lambda i: (i, 0)),
                  pl.BlockSpec((N_EXPERTS, d), lambda i: (0, 0)),
                  pl.BlockSpec((N_EXPERTS, 1), lambda i: (0, 0))],
        out_specs=pl.BlockSpec((tm, N_EXPERTS), lambda i: (i, 0)),
        compiler_params=_params(("parallel",)),
        name="moe_router",
    )(x, router_w.T, router_b.reshape(N_EXPERTS, 1))
    de = w_gate.shape[-1]
    return pl.pallas_call(
        _moe_kernel,
        out_shape=jax.ShapeDtypeStruct((m, d), F32),
        grid=(m // tm, N_EXPERTS),
        in_specs=[pl.BlockSpec((tm, d), lambda i, e: (i, 0)),
                  pl.BlockSpec((tm, N_EXPERTS), lambda i, e: (i, 0)),
                  pl.BlockSpec((1, d, de), lambda i, e: (e, 0, 0)),
                  pl.BlockSpec((1, d, de), lambda i, e: (e, 0, 0)),
                  pl.BlockSpec((1, de, d), lambda i, e: (e, 0, 0)),
                  pl.BlockSpec((1, d), lambda i, e: (0, 0)),
                  pl.BlockSpec((1, d), lambda i, e: (0, 0))],
        out_specs=pl.BlockSpec((tm, d), lambda i, e: (i, 0)),
        scratch_shapes=[pltpu.VMEM((tm, d), BF16), pltpu.VMEM((tm, d), F32)],
        compiler_params=_params(("parallel", "arbitrary")),
        name="moe_experts_ln",
    )(x, gates, w_gate.astype(BF16), w_up.astype(BF16), w_down.astype(BF16), g.reshape(1, d), b.reshape(1, d))


def kernel(x, rel_table, router_w, router_b, ln1_g, ln1_b, ln2_g, ln2_b, exp_w_gate, exp_w_up, exp_w_down, dil_w_in, dil_w_out, sb_w_in, sb_w_out, nsa_w_in, nsa_w_out, nsa_cmp_pos_k, nsa_cmp_w1_k, nsa_cmp_w2_k, nsa_cmp_pos_v, nsa_cmp_w1_v, nsa_cmp_w2_v, moba_w_in, moba_w_out):
    bsz, seq, d = x.shape
    assert d == D_MODEL and seq % (DILATED_PAIRS[-1][1] * DIL_BLOCK) == 0
    depth = ln1_g.shape[0]
    assert depth == DEPTH
    n_mixers = 4
    table = rel_table.astype(F32)
    span = DIL_BLOCK
    big = 1 << 30
    bias_dil = _bias_tiles(table, len(DILATED_PAIRS), DIL_BLOCK, 2 * DIL_BLOCK, DIL_BLOCK, 0, 1, 0, span,
                           [dl for _, dl in DILATED_PAIRS])
    assert BIASED_TILES * TILE - (TILE - 1) >= BUCKET_THRESHOLDS[-1]
    bias_main = _bias_tiles(table, MAIN_TILES, TILE, TILE, 0, TILE, 1, 0, big, [1], shift=True, transposed=True,
                            scale=LOG2_E)
    bias_win = _bias_tiles(table, NSA_WINDOW // TILE + 1, TILE, TILE, 0, TILE, 1, 0, NSA_WINDOW - 1, [1],
                           transposed=True, scale=LOG2_E)
    bias_cmp = _bias_tiles(table, CMP_TILES, CMP_CHUNK, TILE, -TILE - (NSA_CMP_LEN - 1), TILE, NSA_CMP_STRIDE,
                           0, big, [1], transposed=True, scale=LOG2_E)
    xf = x.reshape(bsz * seq, d)
    for layer in range(depth):
        kind, occ = layer % n_mixers, layer // n_mixers
        merge = None
        if kind == 0:
            parts, w_out = _mixer_dilated(xf, dil_w_in[occ], dil_w_out[occ], bias_dil, bsz, seq)
            merge = 'lse3'
        elif kind == 1:
            parts, w_out = _mixer_stick_breaking(xf, sb_w_in[occ], sb_w_out[occ], bsz, seq)
        elif kind == 2:
            parts, w_out = _mixer_nsa(xf, nsa_w_in[occ], nsa_w_out[occ], (bias_main, bias_win, bias_cmp),
                                      nsa_cmp_pos_k[occ], nsa_cmp_w1_k[occ], nsa_cmp_w2_k[occ],
                                      nsa_cmp_pos_v[occ], nsa_cmp_w1_v[occ], nsa_cmp_w2_v[occ], bsz, seq)
            merge = 'sum3'
        else:
            parts, w_out = _mixer_moba(xf, moba_w_in[occ], moba_w_out[occ], bias_main, bsz, seq)
        xf = _outproj_ln(parts, w_out, xf, ln1_g[layer], ln1_b[layer], merge=merge)
        xf = _moe_ln(xf, router_w, router_b, exp_w_gate[layer], exp_w_up[layer], exp_w_down[layer],
                     ln2_g[layer], ln2_b[layer])
    return xf.reshape(bsz, seq, d)
```

```python
import functools
import math

import numpy as np
import jax
import jax.numpy as jnp
from jax import lax
from jax.experimental import pallas as pl
from jax.experimental.pallas import tpu as pltpu

F32 = jnp.float32
BF16 = jnp.bfloat16

D_MODEL = 1024
HEAD_DIM = 64
N_HEADS = 16
LANES = 128
PAIRS = N_HEADS // 2
ATTN_SCALE = HEAD_DIM ** -0.5
REL_BUCKETS = 32
REL_MAX_DIST = 2048
DILATED_PAIRS = ((128, 1), (512, 4), (2048, 16))
DIL_BLOCK = 128
SB_CLIP = 60.0
SB_TILE = 256
LOG2_E = math.log2(math.e)
NSA_KV_HEADS = 4
NSA_GROUP = N_HEADS // NSA_KV_HEADS
NSA_CMP_LEN = 32
NSA_CMP_STRIDE = 16
NSA_SEL_BLOCK = 64
NSA_TOP_N = 16
NSA_WINDOW = 512
MOBA_BLOCK = 256
MOBA_TOP_K = 3
N_EXPERTS = 16
N_GROUPS = 4
EXPERTS_PER_GROUP = N_EXPERTS // N_GROUPS
DEPTH = 4
DEEPNORM_ALPHA = (2 * DEPTH) ** 0.25
LN_EPS = 1e-5
NEG_INF = -1e30
FORCED_SCORE = 1e9
TILE = 256
MOBA_FAR_WIDTH = 8
MOBA_HEADS = 4
NSA_FAR_WIDTH = 4
CMP_CHUNK = 128
VMEM_LIMIT = 56 * 1024 * 1024


def _bucket_thresholds():
    n = np.arange(0, 2 * REL_MAX_DIST)
    exact = REL_BUCKETS // 2
    logf = np.log(np.maximum(n, 1).astype(np.float64) / exact) / math.log(REL_MAX_DIST / exact)
    large = np.minimum(exact + (logf * (REL_BUCKETS - exact)).astype(np.int64), REL_BUCKETS - 1)
    bucket = np.where(n < exact, n, large)
    return tuple(int(np.argmax(bucket >= k)) for k in range(1, REL_BUCKETS))


BUCKET_THRESHOLDS = _bucket_thresholds()
BIASED_TILES = -(-(BUCKET_THRESHOLDS[-1] + TILE - 1) // TILE)
MAIN_TILES = BIASED_TILES + max(MOBA_FAR_WIDTH, NSA_FAR_WIDTH) - 1
CMP_BIASED_TILES = -(-(BUCKET_THRESHOLDS[-1] + NSA_CMP_STRIDE * (CMP_CHUNK - 1) + NSA_CMP_LEN - 1) // TILE)
CMP_TILES = CMP_BIASED_TILES + 2


def _params(sem, vmem=VMEM_LIMIT):
    return pltpu.CompilerParams(dimension_semantics=sem, vmem_limit_bytes=vmem)


def _dot(a, b):
    return jnp.dot(a, b, preferred_element_type=F32)


def _dot_nt(a, b):
    return lax.dot_general(a, b, (((1,), (1,)), ((), ())), preferred_element_type=F32)


def _split_bf16(x):
    hi = x.astype(BF16)
    lo = (x - hi.astype(F32)).astype(BF16)
    return hi, lo


def _matmul_kernel(x_ref, w_ref, o_ref, *, act):
    y = _dot(x_ref[...].astype(BF16), w_ref[...])
    if act == 'sigmoid':
        y = 1.0 / (1.0 + jnp.exp(-y))
    o_ref[...] = y.astype(o_ref.dtype)


def _matmul(x, w, out_dtype, act=None):
    m, k = x.shape
    n = w.shape[1]
    tm = 1024
    tn = next(t for t in (1024, 512, 128) if n % t == 0)
    return pl.pallas_call(
        functools.partial(_matmul_kernel, act=act),
        out_shape=jax.ShapeDtypeStruct((m, n), out_dtype),
        grid=(m // tm, n // tn),
        in_specs=[pl.BlockSpec((tm, k), lambda i, j: (i, 0)),
                  pl.BlockSpec((k, tn), lambda i, j: (0, j))],
        out_specs=pl.BlockSpec((tm, tn), lambda i, j: (i, j)),
        compiler_params=_params(("parallel", "arbitrary")),
        name="proj_matmul",
    )(x, w)


def _matmul_t_kernel(x_ref, w_ref, o_ref):
    y = _dot_nt(w_ref[...], x_ref[...].astype(BF16))
    row = lax.broadcasted_iota(jnp.int32, y.shape, 0)
    o_ref[...] = jnp.where(row % LANES == HEAD_DIM, 1.0, y).astype(o_ref.dtype)


def _matmul_t(x, wt):
    m, k = x.shape
    n = wt.shape[0]
    tm = 1024
    tn = 512 if n % 512 == 0 else 256
    return pl.pallas_call(
        _matmul_t_kernel,
        out_shape=jax.ShapeDtypeStruct((n, m), BF16),
        grid=(m // tm, n // tn),
        in_specs=[pl.BlockSpec((tm, k), lambda i, j: (i, 0)),
                  pl.BlockSpec((tn, k), lambda i, j: (j, 0))],
        out_specs=pl.BlockSpec((tn, tm), lambda i, j: (j, i)),
        compiler_params=_params(("parallel", "arbitrary")),
        name="proj_matmul_t",
    )(x, wt)


def _matmul_heads_kernel(x_ref, w_ref, o_ref):
    y = _dot(x_ref[...].astype(BF16), w_ref[...]).astype(o_ref.dtype)
    for j in range(o_ref.shape[1]):
        o_ref[0, j] = y[:, j * HEAD_DIM:(j + 1) * HEAD_DIM]


def _matmul_heads(x, w, bsz, seq):
    m, k = x.shape
    n = w.shape[1] // HEAD_DIM
    tm = 512
    per_b = seq // tm
    return pl.pallas_call(
        _matmul_heads_kernel,
        out_shape=jax.ShapeDtypeStruct((bsz, n, seq, HEAD_DIM), BF16),
        grid=(m // tm,),
        in_specs=[pl.BlockSpec((tm, k), lambda i: (i, 0)),
                  pl.BlockSpec((k, n * HEAD_DIM), lambda i: (0, 0))],
        out_specs=pl.BlockSpec((1, n, tm, HEAD_DIM), lambda i: (i // per_b, 0, i % per_b, 0)),
        compiler_params=_params(("parallel",)),
        name="proj_heads",
    )(x, w)


def _layer_norm(z, g, b):
    mu = jnp.mean(z, axis=-1, keepdims=True)
    zc = z - mu
    var = jnp.mean(zc * zc, axis=-1, keepdims=True)
    return zc * lax.rsqrt(var + LN_EPS) * g + b


def _outproj_kernel(*refs, n_in, merge):
    ins = refs[:n_in]
    w_ref, x_ref, g_ref, b_ref, o_ref = refs[n_in:]
    if merge == 'lse3':
        o1, o2, o3, l1, l2, l3 = [r[...] for r in ins]
        mx = jnp.maximum(jnp.maximum(l1, l2), l3)
        e1, e2, e3 = jnp.exp(l1 - mx), jnp.exp(l2 - mx), jnp.exp(l3 - mx)
        a = (e1 * o1 + e2 * o2 + e3 * o3) / (e1 + e2 + e3)
    elif merge == 'sum3':
        a = ins[0][...].astype(F32) + ins[1][...].astype(F32) + ins[2][...].astype(F32)
    else:
        a = ins[0][...]
    y = _dot(a.astype(BF16), w_ref[...])
    o_ref[...] = _layer_norm(DEEPNORM_ALPHA * x_ref[...] + y, g_ref[...], b_ref[...])


def _outproj_ln(ins, w, x, g, b, merge=None):
    m, d = x.shape
    ka = ins[0].shape[1]
    tm = 256
    n_in = len(ins)
    return pl.pallas_call(
        functools.partial(_outproj_kernel, n_in=n_in, merge=merge),
        out_shape=jax.ShapeDtypeStruct((m, d), F32),
        grid=(m // tm,),
        in_specs=[pl.BlockSpec((tm, ka), lambda i: (i, 0)) for _ in ins]
        + [pl.BlockSpec((ka, d), lambda i: (0, 0)),
           pl.BlockSpec((tm, d), lambda i: (i, 0)),
           pl.BlockSpec((1, d), lambda i: (0, 0)),
           pl.BlockSpec((1, d), lambda i: (0, 0))],
        out_specs=pl.BlockSpec((tm, d), lambda i: (i, 0)),
        compiler_params=_params(("parallel",)),
        name="outproj_ln",
    )(*ins, w, x, g.reshape(1, d), b.reshape(1, d))


def _bias_kernel(tbl_ref, o_ref, *, rows, cols, off0, tstride, cmul, lo, hi, mults, shift, transposed, scale):
    t = pl.program_id(0)
    if len(set(mults)) == 1:
        mult = mults[0]
    else:
        mult = jnp.int32(mults[-1])
        for idx in range(len(mults) - 2, -1, -1):
            mult = jnp.where(t == idx, jnp.int32(mults[idx]), mult)
    base = off0 + tstride * t

    def strip(i, carry):
        r0 = pl.multiple_of(i * 8, 8)
        a = lax.broadcasted_iota(jnp.int32, (8, cols), 0) + r0
        c = lax.broadcasted_iota(jnp.int32, (8, cols), 1)
        steps = base + c - cmul * a if transposed else base + a - cmul * c
        valid = (steps >= lo) & (steps <= hi)
        dist = steps * mult
        for h in range(N_HEADS):
            acc = jnp.full((8, cols), tbl_ref[0, h], F32)
            for k, thr in enumerate(BUCKET_THRESHOLDS):
                acc = jnp.where(dist >= thr, tbl_ref[k + 1, h], acc)
            if shift:
                acc = acc - tbl_ref[REL_BUCKETS - 1, h]
            if scale != 1.0:
                acc = acc * scale
            o_ref[h, 0, pl.ds(r0, 8), :] = jnp.where(valid, acc, NEG_INF)
        return carry

    lax.fori_loop(0, rows // 8, strip, 0)


def _bias_tiles(table, n_tiles, rows, cols, off0, tstride, cmul, lo, hi, mults, shift=False, transposed=False,
                scale=1.0):
    return pl.pallas_call(
        functools.partial(_bias_kernel, rows=rows, cols=cols, off0=off0, tstride=tstride, cmul=cmul,
                          lo=lo, hi=hi, mults=tuple(mults), shift=shift, transposed=transposed, scale=scale),
        out_shape=jax.ShapeDtypeStruct((N_HEADS, n_tiles, rows, cols), F32),
        grid=(n_tiles,),
        in_specs=[pl.BlockSpec(memory_space=pltpu.SMEM)],
        out_specs=pl.BlockSpec((N_HEADS, 1, rows, cols), lambda t: (0, t, 0, 0)),
        compiler_params=_params(("arbitrary",)),
        name="bias_tiles",
    )(table)


def _dil_kernel(q_ref, kp_ref, kc_ref, vp_ref, vc_ref, bias_ref, o_ref, lse_ref,
                q_scr, kp_scr, kc_scr, vp_scr, vc_scr, *, dilation, per_step):
    j = pl.program_id(2)
    d = dilation
    n_cur = per_step * DIL_BLOCK

    def rows_of(ref, r, n, lead=()):
        if d == 1:
            return ref[lead + (slice(0, n), slice(None))]
        return ref[lead + (pl.ds(r, n, stride=d), slice(None))]

    if d > 1:
        for h in range(2):
            q_scr[h] = q_ref[0, :, h * LANES:(h + 1) * LANES].astype(F32)
        kp_scr[...] = kp_ref[0].astype(F32)
        kc_scr[...] = kc_ref[0].astype(F32)
        vp_scr[...] = vp_ref[0].astype(F32)
        vc_scr[...] = vc_ref[0].astype(F32)
    colk = lax.broadcasted_iota(jnp.int32, (DIL_BLOCK, 2 * DIL_BLOCK), 1)
    first = jnp.where((colk < DIL_BLOCK) & (j == 0), NEG_INF, 0.0)
    low = lax.broadcasted_iota(jnp.int32, (DIL_BLOCK, LANES), 1) < HEAD_DIM
    for r in range(d):
        if d == 1:
            q_heads = [q_ref[0, :, h * LANES:(h + 1) * LANES] for h in range(2)]
            k_all = jnp.concatenate([kp_ref[0], kc_ref[0]], axis=0)
            v_all = jnp.concatenate([vp_ref[0], vc_ref[0]], axis=0)
        else:
            q_heads = [rows_of(q_scr, r, n_cur, (h,)).astype(BF16) for h in range(2)]
            k_all = jnp.concatenate([rows_of(kp_scr, r, DIL_BLOCK), rows_of(kc_scr, r, n_cur)], axis=0).astype(BF16)
            v_all = jnp.concatenate([rows_of(vp_scr, r, DIL_BLOCK), rows_of(vc_scr, r, n_cur)], axis=0).astype(BF16)
        for u in range(per_step):
            k2 = k_all[u * DIL_BLOCK:(u + 2) * DIL_BLOCK]
            v2 = v_all[u * DIL_BLOCK:(u + 2) * DIL_BLOCK]
            outs, lses = [], []
            for h in range(2):
                s = _dot_nt(q_heads[h][u * DIL_BLOCK:(u + 1) * DIL_BLOCK], k2) + bias_ref[h, 0]
                if u == 0:
                    s = s + first
                m = jnp.max(s, axis=-1, keepdims=True)
                p = jnp.exp(s - m)
                l = jnp.maximum(jnp.sum(p, axis=-1, keepdims=True), 1e-30)
                outs.append(_dot(p.astype(BF16), v2) / l)
                lses.append(m + jnp.log(l))
            if d == 1:
                dst = (0, slice(u * DIL_BLOCK, (u + 1) * DIL_BLOCK), slice(None))
            else:
                dst = (0, pl.ds(u * DIL_BLOCK * d + r, DIL_BLOCK, stride=d), slice(None))
            o_ref[dst] = jnp.where(low, outs[0], outs[1])
            lse_ref[dst] = jnp.where(low, lses[0], lses[1])


def _dilated_group(proj, bias, g, dilation, bsz, seq):
    prev_rows = DIL_BLOCK * dilation
    step_rows = math.gcd(seq, 16 * DIL_BLOCK)
    per_step = step_rows // prev_rows
    assert per_step >= 1 and seq % step_rows == 0
    q_blk = g * N_HEADS // 2
    k_blk = 3 * N_HEADS + g * 2 * PAIRS
    v_blk = k_blk + PAIRS

    def kv_specs(base):
        prev = pl.BlockSpec((1, prev_rows, LANES), lambda hp, b, j: (b, jnp.maximum(per_step * j - 1, 0), base + hp))
        cur = pl.BlockSpec((1, step_rows, LANES), lambda hp, b, j: (b, j, base + hp))
        return [prev, cur]

    out_spec = pl.BlockSpec((1, step_rows, LANES), lambda hp, b, j: (b, j, hp))
    staged = 8 if dilation == 1 else None
    o, lse = pl.pallas_call(
        functools.partial(_dil_kernel, dilation=dilation, per_step=per_step),
        out_shape=[jax.ShapeDtypeStruct((bsz, seq, D_MODEL), F32)] * 2,
        grid=(PAIRS, bsz, seq // step_rows),
        in_specs=[pl.BlockSpec((1, step_rows, 2 * LANES), lambda hp, b, j: (b, j, q_blk + hp))]
        + kv_specs(k_blk) + kv_specs(v_blk)
        + [pl.BlockSpec((2, 1, DIL_BLOCK, 2 * DIL_BLOCK), lambda hp, b, j: (hp, g, 0, 0))],
        out_specs=[out_spec, out_spec],
        scratch_shapes=[pltpu.VMEM((2, staged or step_rows, LANES), F32),
                        pltpu.VMEM((staged or prev_rows, LANES), F32), pltpu.VMEM((staged or step_rows, LANES), F32),
                        pltpu.VMEM((staged or prev_rows, LANES), F32), pltpu.VMEM((staged or step_rows, LANES), F32)],
        compiler_params=_params(("parallel", "parallel", "arbitrary")),
        name="dilated_attn",
    )(proj, proj, proj, proj, proj, bias)
    return o.reshape(bsz * seq, D_MODEL), lse.reshape(bsz * seq, D_MODEL)


def _pad_heads_cols(w, parity_of_head):
    k, n = w.shape
    nh = n // HEAD_DIM
    wh = w.reshape(k, nh, HEAD_DIM)
    z = jnp.zeros_like(wh)
    par = jnp.asarray([parity_of_head(h) for h in range(nh)], jnp.int32)[None, :, None]
    lo = jnp.where(par == 0, wh, z)
    hi = jnp.where(par == 1, wh, z)
    return jnp.concatenate([lo, hi], axis=-1).reshape(k, nh * LANES)


def _pad_heads_rows(w, parity_of_head):
    return _pad_heads_cols(w.T, parity_of_head).T


def _mixer_dilated(x, w_in, w_out, bias_dil, bsz, seq):
    aw = N_HEADS * HEAD_DIM
    par = lambda h: h % 2
    qs, kvs = [], []
    for g in range(len(DILATED_PAIRS)):
        base = g * 3 * aw
        qs.append(_pad_heads_cols(w_in[:, base:base + aw] * ATTN_SCALE, par))
        kvs.append(w_in[:, base + aw:base + 3 * aw])
    w_all = jnp.concatenate(qs + kvs, axis=1).astype(BF16)
    proj = _matmul(x, w_all, BF16).reshape(bsz, seq, -1)
    parts = [_dilated_group(proj, bias_dil, g, d, bsz, seq) for g, (_, d) in enumerate(DILATED_PAIRS)]
    return [p[0] for p in parts] + [p[1] for p in parts], w_out.astype(BF16)


def _sb_kernel(q_ref, k_ref, v_ref, o_ref):
    i = pl.program_id(2)
    t = SB_TILE
    row = lax.broadcasted_iota(jnp.int32, (t, t), 0)
    col = lax.broadcasted_iota(jnp.int32, (t, t), 1)
    suffix = jnp.where(row >= col, 1.0, 0.0).astype(BF16)
    strict = col < row
    q2 = q_ref[0]
    q_heads = [q2[:, h * LANES:(h + 1) * LANES] for h in range(2)]

    def block(qh, kj, masked):
        start = pl.multiple_of(kj * t, t)
        kb = k_ref[0, pl.ds(start, t), :]
        vb = v_ref[0, pl.ds(start, t), :]
        a = jnp.clip(_dot_nt(qh, kb), -SB_CLIP, SB_CLIP) * LOG2_E
        sp = jnp.log2(1.0 + jnp.exp2(a))
        if masked:
            sp = jnp.where(strict, sp, 0.0)
        hi, lo = _split_bf16(sp)
        rr = _dot(hi, suffix) + _dot(lo, suffix)
        att = jnp.exp2(a - rr)
        if masked:
            att = jnp.where(strict, att, 0.0)
        return _dot(att.astype(BF16), vb), rr[:, 0:1]

    def first_two(kj, carry):
        has_prev = jnp.where(kj > 0, 1.0, 0.0)
        out = []
        for qh in q_heads:
            acc_d, r_d = block(qh, kj, True)
            pv, r_p = block(qh, jnp.maximum(kj - 1, 0), False)
            out.append((acc_d + (has_prev * jnp.exp2(-r_d)) * pv, r_d + has_prev * r_p))
        return tuple(out)

    zero = (jnp.zeros((t, LANES), F32), jnp.zeros((t, 1), F32))
    (acc0, r0), (acc1, r1) = lax.fori_loop(i, i + 1, first_two, (zero, zero))

    def cond(carry):
        n, alive = carry[0], carry[1]
        return (n < i - 1) & alive

    def step(carry):
        n = carry[0]
        new, tops = [], []
        for qh, (r_sum, w, acc) in zip(q_heads, carry[2:]):
            pv, r_blk = block(qh, i - 2 - n, False)
            r_new = r_sum + r_blk
            w_new = jnp.exp2(-r_new)
            new.append((r_new, w_new, acc + w * pv))
            tops.append(jnp.max(w_new, axis=0, keepdims=True))
        alive = jnp.maximum(tops[0], tops[1])[0, 0] > 0.0
        return (n + 1, alive) + tuple(new)

    w0, w1 = jnp.exp2(-r0), jnp.exp2(-r1)
    alive0 = jnp.maximum(jnp.max(w0, axis=0, keepdims=True), jnp.max(w1, axis=0, keepdims=True))[0, 0] > 0.0
    init = (jnp.int32(0), alive0, (r0, w0, acc0), (r1, w1, acc1))
    res = lax.while_loop(cond, step, init)
    lane = lax.broadcasted_iota(jnp.int32, (t, LANES), 1)
    o_ref[0] = jnp.where(lane < HEAD_DIM, res[2][2], res[3][2]).astype(o_ref.dtype)


def _mixer_stick_breaking(x, w_in, w_out, bsz, seq):
    aw = N_HEADS * HEAD_DIM
    wq = _pad_heads_cols(w_in[:, :aw] * ATTN_SCALE, lambda h: h % 2)
    w_all = jnp.concatenate([wq, w_in[:, aw:]], axis=1).astype(BF16)
    proj = _matmul(x, w_all, BF16).reshape(bsz, seq, -1)
    kb = N_HEADS
    o = pl.pallas_call(
        _sb_kernel,
        out_shape=jax.ShapeDtypeStruct((bsz, seq, D_MODEL), BF16),
        grid=(bsz, PAIRS, seq // SB_TILE),
        in_specs=[pl.BlockSpec((1, SB_TILE, 2 * LANES), lambda b, hp, i: (b, i, hp)),
                  pl.BlockSpec((1, seq, LANES), lambda b, hp, i: (b, 0, kb + hp)),
                  pl.BlockSpec((1, seq, LANES), lambda b, hp, i: (b, 0, kb + PAIRS + hp))],
        out_specs=pl.BlockSpec((1, SB_TILE, LANES), lambda b, hp, i: (b, i, hp)),
        compiler_params=_params(("parallel", "parallel", "arbitrary")),
        name="stick_breaking_attn",
    )(proj, proj, proj)
    return [o.reshape(bsz * seq, D_MODEL)], w_out.astype(BF16)


def _flash_chunks(q_heads, k_ref, k_cols, v_ref, v_rows, bias_ref, mask_rows, lo, hi, i, blocks_per_chunk, n_tiles,
                  fold_mask=False, pipelined=False, far_width=1, near_width=1):
    nh = len(q_heads)

    def logits(ch, near, width):
        rows = width * TILE
        start = pl.multiple_of(ch * TILE, TILE)
        kbs = {c: k_ref[0, pl.ds(start, rows), c:c + LANES] for c in sorted(set(k_cols))}
        add = None
        if fold_mask:
            key = lax.broadcasted_iota(jnp.int32, (rows, LANES), 0)
            blk = lax.broadcasted_iota(jnp.int32, (rows, LANES), 1)
            blk_of_key = ch * blocks_per_chunk + key // (TILE // blocks_per_chunk)
            ind = jnp.where(blk == blk_of_key, NEG_INF, 0.0).astype(BF16)
            kbs = {c: jnp.concatenate([kb, ind], axis=1) for c, kb in kbs.items()}
        elif mask_rows is not None:
            nb = width * blocks_per_chunk
            per = TILE // blocks_per_chunk
            blk_rows = mask_rows(ch * blocks_per_chunk, nb)
            add = jnp.concatenate([jnp.broadcast_to(blk_rows[r:r + 1, :], (per, TILE)) for r in range(nb)], axis=0)
        out = []
        for h in range(nh):
            s = _dot_nt(kbs[k_cols[h]], q_heads[h])
            if near:
                tiles = [bias_ref[h, jnp.maximum(i - ch - w, 0)] for w in range(width)]
                s = s + (tiles[0] if width == 1 else jnp.concatenate(tiles, axis=0))
            if add is not None:
                s = s + add
            out.append(s)
        return tuple(out)

    def update(ch, s_heads, state, width):
        start = pl.multiple_of(ch * TILE, TILE)
        vts = {}
        new = []
        for h in range(nh):
            r0 = v_rows[h]
            if r0 not in vts:
                vts[r0] = v_ref[r0:r0 + LANES, pl.ds(start, width * TILE)]
            m, acc = state[h]
            s = s_heads[h]
            m_new = jnp.maximum(m, jnp.max(s, axis=0, keepdims=True))
            alpha = jnp.exp2(m - m_new)
            p = jnp.exp2(s - m_new)
            acc = alpha * acc + _dot(vts[r0], p.astype(BF16))
            new.append((m_new, acc))
        return tuple(new)

    def run(ch0, steps, near, width, state, pipelined=pipelined):
        if not pipelined:
            return lax.fori_loop(
                0, steps, lambda t, st: update(ch0 + t * width, logits(ch0 + t * width, near, width), st, width), state)

        def body(t, carry):
            s_cur, st = carry
            s_next = logits(ch0 + jnp.minimum(t + 1, steps - 1) * width, near, width)
            return s_next, update(ch0 + t * width, s_cur, st, width)

        zeros = tuple(jnp.zeros((width * TILE, TILE), F32) for _ in range(nh))
        first = jnp.minimum(ch0, i)
        s0 = lax.fori_loop(first, first + 1, lambda c, _: logits(c, near, width), zeros)
        return lax.fori_loop(0, steps, body, (s0, state))[1]

    state = tuple((jnp.full((1, TILE), NEG_INF, F32), jnp.zeros((LANES, TILE), F32)) for _ in range(nh))
    n_near = n_tiles - (far_width - 1)
    far_steps = jnp.maximum(hi - n_near - lo, 0) // far_width
    split = lo + far_steps * far_width
    state = run(lo, far_steps, False, far_width, state)
    near_pairs = (hi - split) // near_width if near_width > 1 else 0
    if near_width > 1:
        state = run(split, near_pairs, True, near_width, state)
    single = split + near_pairs * near_width
    state = run(single, hi - single, True, 1, state, pipelined=pipelined and near_width == 1)
    outs = []
    for _, acc in state:
        o = (acc / acc[HEAD_DIM:HEAD_DIM + 1, :]).T
        lane = lax.broadcasted_iota(jnp.int32, o.shape, 1)
        outs.append(jnp.where(lane < HEAD_DIM, o, 0.0))
    return outs


def _kmean_kernel(k_ref, o_ref):
    rows = k_ref.shape[1]
    kf = k_ref[0].astype(F32).reshape(rows // MOBA_BLOCK, MOBA_BLOCK, k_ref.shape[2])
    o_ref[0] = jnp.sum(kf, axis=1) * (1.0 / MOBA_BLOCK)


def _moba_kernel(q_ref, k_ref, v_ref, km_ref, bias_ref, o_ref):
    i = pl.program_id(2)
    nblk = km_ref.shape[1]
    km_hi, km_lo = _split_bf16(km_ref[0])
    qs = q_ref[0]
    blk = lax.broadcasted_iota(jnp.int32, (nblk, TILE), 0)
    past = blk < i
    q_heads = []
    for h in range(MOBA_HEADS):
        qh = qs[:, h * LANES:(h + 1) * LANES]
        pair = slice((h // 2) * LANES, (h // 2 + 1) * LANES)
        gate = jnp.where(past, _dot_nt(km_hi[:, pair], qh) + _dot_nt(km_lo[:, pair], qh), NEG_INF)
        allowed = blk == i
        for _ in range(MOBA_TOP_K):
            mx = jnp.max(gate, axis=0, keepdims=True)
            first = jnp.min(jnp.where(gate == mx, blk, nblk), axis=0, keepdims=True)
            pick = blk == first
            allowed = allowed | (pick & past)
            gate = jnp.where(pick, -jnp.inf, gate)
        nsel = jnp.where(allowed, 0.0, 1.0).T.astype(BF16)
        q_heads.append(jnp.concatenate([qh, nsel], axis=1))
    outs = _flash_chunks(q_heads, k_ref, [(h // 2) * LANES for h in range(MOBA_HEADS)],
                         v_ref, [h * LANES for h in range(MOBA_HEADS)], bias_ref, None, 0, i + 1, i, 1,
                         BIASED_TILES + MOBA_FAR_WIDTH - 1, fold_mask=True, pipelined=False, far_width=MOBA_FAR_WIDTH,
                         near_width=2)
    for h in range(MOBA_HEADS):
        o_ref[0, :, h * LANES:(h + 1) * LANES] = outs[h].astype(o_ref.dtype)


def _mixer_moba(x, w_in, w_out, bias_main, bsz, seq):
    aw = N_HEADS * HEAD_DIM
    nblk = seq // MOBA_BLOCK
    assert nblk - 1 >= MOBA_TOP_K
    wq = _pad_heads_cols(w_in[:, :aw] * (ATTN_SCALE * LOG2_E), lambda h: h % 2)
    w_all = jnp.concatenate([wq, w_in[:, aw:2 * aw]], axis=1).astype(BF16)
    proj = _matmul(x, w_all, BF16).reshape(bsz, seq, -1)
    v_t = _matmul_t(x, _pad_heads_cols(w_in[:, 2 * aw:], lambda h: 0).T.astype(BF16))
    kb = N_HEADS
    rows = 8 * MOBA_BLOCK
    kmean = pl.pallas_call(
        _kmean_kernel,
        out_shape=jax.ShapeDtypeStruct((bsz, nblk, D_MODEL), F32),
        grid=(bsz, seq // rows),
        in_specs=[pl.BlockSpec((1, rows, D_MODEL), lambda b, i: (b, i, kb * LANES // D_MODEL))],
        out_specs=pl.BlockSpec((1, 8, D_MODEL), lambda b, i: (b, i, 0)),
        compiler_params=_params(("parallel", "parallel")),
        name="moba_kmean",
    )(proj)
    assert nblk <= LANES
    kmean = jnp.pad(kmean, ((0, 0), (0, LANES - nblk), (0, 0)))
    nh = MOBA_HEADS
    kw = nh // 2 * LANES
    once = pl.Buffered(1)
    o = pl.pallas_call(
        _moba_kernel,
        out_shape=jax.ShapeDtypeStruct((bsz, seq, N_HEADS * LANES), BF16),
        grid=(bsz, N_HEADS // nh, seq // TILE),
        in_specs=[pl.BlockSpec((1, TILE, nh * LANES), lambda b, hq, i: (b, i, hq)),
                  pl.BlockSpec((1, seq, kw), lambda b, hq, i: (b, 0, kb * LANES // kw + hq), pipeline_mode=once),
                  pl.BlockSpec((nh * LANES, seq), lambda b, hq, i: (hq, b), pipeline_mode=once),
                  pl.BlockSpec((1, LANES, kw), lambda b, hq, i: (b, 0, hq)),
                  pl.BlockSpec((nh, BIASED_TILES + MOBA_FAR_WIDTH - 1, TILE, TILE), lambda b, hq, i: (hq, 0, 0, 0),
                               pipeline_mode=once)],
        out_specs=pl.BlockSpec((1, TILE, nh * LANES), lambda b, hq, i: (b, i, hq)),
        compiler_params=_params(("parallel", "parallel", "arbitrary")),
        name="moba_attn",
    )(proj, proj, v_t, kmean, bias_main)
    return [o.reshape(bsz * seq, N_HEADS * LANES)], _pad_heads_rows(w_out, lambda h: 0).astype(BF16)


def _gelu_tanh(x):
    return 0.5 * x * (1.0 + jnp.tanh(math.sqrt(2.0 / math.pi) * (x + 0.044715 * (x * x * x))))


def _compress_kernel(a_ref, pos_ref, w1_ref, w2_ref, o_ref, *, transpose_out):
    nc = a_ref.shape[2]
    half = NSA_CMP_STRIDE * HEAD_DIM
    a = a_ref[0, 0].astype(F32)
    top = _dot((a + pos_ref[0:1, :]).astype(BF16), w1_ref[0:half, :])
    bot = _dot((a + pos_ref[1:2, :]).astype(BF16), w1_ref[half:2 * half, :])
    hid = top + pltpu.roll(bot, nc - 1, 0)
    out = _dot(_gelu_tanh(hid).astype(BF16), w2_ref[0])
    rowi = lax.broadcasted_iota(jnp.int32, out.shape, 0)
    out = jnp.where(rowi < nc - 1, out, 0.0)
    o_ref[0, 0] = (out.T if transpose_out else out).astype(o_ref.dtype)


def _compress(tok, first, pos, w1, w2, bsz, seq, by_parity):
    nc = seq // NSA_CMP_STRIDE
    half = NSA_CMP_STRIDE * HEAD_DIM
    pos2 = pos.reshape(2, half)
    z = jnp.zeros_like(w2)
    low = jnp.concatenate([w2, z], axis=1)
    w2p = jnp.stack([low, jnp.concatenate([z, w2], axis=1) if by_parity else low]).astype(BF16)
    out_dims = (nc, LANES) if by_parity else (LANES, nc)
    return pl.pallas_call(
        functools.partial(_compress_kernel, transpose_out=not by_parity),
        out_shape=jax.ShapeDtypeStruct((bsz, NSA_KV_HEADS) + out_dims, BF16),
        grid=(bsz, NSA_KV_HEADS),
        in_specs=[pl.BlockSpec((1, 1, nc, half), lambda b, g: (b, first + g, 0, 0)),
                  pl.BlockSpec((2, half), lambda b, g: (0, 0)),
                  pl.BlockSpec((2 * half, w1.shape[1]), lambda b, g: (0, 0)),
                  pl.BlockSpec((1, w2.shape[0], LANES), lambda b, g: (g % 2, 0, 0))],
        out_specs=pl.BlockSpec((1, 1) + out_dims, lambda b, g: (b, g, 0, 0)),
        compiler_params=_params(("parallel", "parallel")),
        name="nsa_compress",
    )(tok, pos2, w1.astype(BF16), w2p)


def _gate_columns(gate_ref, branch):
    tile = gate_ref[...]
    lane = lax.broadcasted_iota(jnp.int32, tile.shape, 1)
    base = branch * N_HEADS + pl.program_id(1) * NSA_GROUP
    return [jnp.sum(jnp.where(lane == base + r, tile, 0.0), axis=-1, keepdims=True) for r in range(NSA_GROUP)]


def _nsa_cmp_kernel(q_ref, kc_ref, vc_ref, bias_ref, gate_ref, o_ref, sel_ref, *, n_top):
    i = pl.program_id(2)
    nc = kc_ref.shape[2]
    n_chunks = nc // CMP_CHUNK
    nsb = sel_ref.shape[2]
    q4 = q_ref[0]
    gates = _gate_columns(gate_ref, 0)
    keep = lax.broadcasted_iota(jnp.int32, (TILE, LANES), 1) < HEAD_DIM
    psum = [jnp.zeros((CMP_CHUNK, TILE), F32) for _ in range(n_chunks)]
    for r in range(NSA_GROUP):
        qh = q4[:, r * LANES:(r + 1) * LANES]
        ss = []
        for c in range(n_chunks):
            tile = jnp.clip(i - (CMP_CHUNK * NSA_CMP_STRIDE // TILE) * c, -1, CMP_TILES - 2) + 1
            ss.append(_dot_nt(kc_ref[0, 0, c * CMP_CHUNK:(c + 1) * CMP_CHUNK, :], qh) + bias_ref[r, tile])
        m = ss[0].max(axis=0, keepdims=True)
        for c in range(1, n_chunks):
            m = jnp.maximum(m, ss[c].max(axis=0, keepdims=True))
        ps = [jnp.exp2(s - m) for s in ss]
        l = ps[0].sum(axis=0, keepdims=True)
        for c in range(1, n_chunks):
            l = l + ps[c].sum(axis=0, keepdims=True)
        inv = jnp.where(m > 0.5 * NEG_INF, 1.0 / jnp.maximum(l, 1e-30), 0.0)
        acc = jnp.zeros((LANES, TILE), F32)
        for c in range(n_chunks):
            pc = ps[c] * inv
            psum[c] = psum[c] + pc
            acc = acc + _dot(vc_ref[0, 0, :, c * CMP_CHUNK:(c + 1) * CMP_CHUNK], pc.astype(BF16))
        o_ref[0, :, r * LANES:(r + 1) * LANES] = jnp.where(keep, acc.T * gates[r], 0.0).astype(o_ref.dtype)
    imp = jnp.zeros((nsb, TILE), F32)
    per_sel = NSA_SEL_BLOCK // NSA_CMP_STRIDE
    for c in range(n_chunks):
        jb = lax.broadcasted_iota(jnp.int32, (nsb, CMP_CHUNK), 0)
        ci = lax.broadcasted_iota(jnp.int32, (nsb, CMP_CHUNK), 1) + c * CMP_CHUNK
        rel = ci - per_sel * jb
        over = (rel >= 1 - NSA_CMP_LEN // NSA_CMP_STRIDE) & (rel < per_sel) & (ci < nc - 1)
        ov = jnp.where(over, 1.0, 0.0).astype(BF16)
        hi, lo = _split_bf16(psum[c])
        imp = imp + _dot(ov, hi) + _dot(ov, lo)
    jb = lax.broadcasted_iota(jnp.int32, (nsb, TILE), 0)
    qpos = lax.broadcasted_iota(jnp.int32, (nsb, TILE), 1) + i * TILE
    cur = qpos // NSA_SEL_BLOCK
    forced = (jb == 0) | (jb == cur) | (jb == cur - 1)
    causal = jb * NSA_SEL_BLOCK <= qpos
    score = jnp.where(forced, FORCED_SCORE, jnp.where(causal, imp, NEG_INF))
    chosen = jnp.zeros((nsb, TILE), F32)
    for _ in range(n_top):
        mx = jnp.max(score, axis=0, keepdims=True)
        first = jnp.min(jnp.where(score == mx, jb, nsb), axis=0, keepdims=True)
        pick = jb == first
        chosen = jnp.where(pick, 1.0, chosen)
        score = jnp.where(pick, -jnp.inf, score)
    sel_ref[0, 0] = jnp.where((chosen > 0.0) & causal, 0.0, NEG_INF)


def _nsa_sparse_kernel(q_ref, k_ref, v_ref, bias_ref, gate_ref, *rest, window):
    if window:
        (o_ref,) = rest
        mask_rows = None
    else:
        sel_ref, o_ref = rest

        def mask_rows(b0, n):
            if n % 8 == 0:
                return sel_ref[0, 0, pl.ds(pl.multiple_of(b0, 8), n), :]
            assert n == 4
            rows = sel_ref[0, 0, pl.ds(pl.multiple_of(b0 // 8 * 8, 8), 8), :]
            return jnp.where(b0 % 8 == 0, rows[0:4], rows[4:8])

    i = pl.program_id(2)
    q4 = q_ref[0]
    gates = _gate_columns(gate_ref, 2 if window else 1)
    q_heads = [q4[:, r * LANES:(r + 1) * LANES] for r in range(NSA_GROUP)]
    if window:
        n_tiles = NSA_WINDOW // TILE + 1
        lo = jnp.maximum(i - (n_tiles - 1), 0)
        outs = _flash_chunks(q_heads, k_ref, [0] * NSA_GROUP, v_ref, [0] * NSA_GROUP, bias_ref, None,
                             lo, i + 1, i, 1, n_tiles, near_width=n_tiles)
    else:
        outs = _flash_chunks(q_heads, k_ref, [0] * NSA_GROUP, v_ref, [0] * NSA_GROUP, bias_ref, mask_rows,
                             0, i + 1, i,
                             TILE // NSA_SEL_BLOCK, BIASED_TILES + NSA_FAR_WIDTH - 1, far_width=NSA_FAR_WIDTH,
                             near_width=2)
    for r in range(NSA_GROUP):
        o_ref[0, :, r * LANES:(r + 1) * LANES] = (outs[r] * gates[r]).astype(o_ref.dtype)


def _mixer_nsa(x, w_in, w_out, table_bias, pos_k, w1_k, w2_k, pos_v, w1_v, w2_v, bsz, seq):
    bias_main, bias_win, bias_cmp = table_bias
    aw = N_HEADS * HEAD_DIM
    kvw = NSA_KV_HEADS * HEAD_DIM
    G, R = NSA_KV_HEADS, NSA_GROUP
    nc = seq // NSA_CMP_STRIDE
    nsb = seq // NSA_SEL_BLOCK
    n_top = min(NSA_TOP_N, nsb)
    nq = seq // TILE
    assert nc % CMP_CHUNK == 0
    par = lambda h: (h // R) % 2
    wq = _pad_heads_cols(w_in[:, :aw] * (ATTN_SCALE * LOG2_E), par).astype(BF16)
    w_cmp = w_in[:, aw:aw + 2 * kvw].astype(BF16)
    c0 = aw + 2 * kvw
    wk = jnp.concatenate([w_in[:, c0:c0 + kvw], w_in[:, c0 + 2 * kvw:c0 + 3 * kvw]], axis=1).astype(BF16)
    wv = jnp.concatenate([w_in[:, c0 + kvw:c0 + 2 * kvw], w_in[:, c0 + 3 * kvw:c0 + 4 * kvw]], axis=1)
    wv_t = _pad_heads_cols(wv, lambda h: 0).T.astype(BF16)
    wg = jnp.pad(w_in[:, aw + 6 * kvw:], ((0, 0), (0, LANES - 3 * N_HEADS))).astype(BF16)
    q = _matmul(x, wq, BF16).reshape(bsz, seq, N_HEADS * LANES)
    kk = _matmul(x, wk, BF16).reshape(bsz, seq, 2 * kvw)
    v_t = _matmul_t(x, wv_t)
    gates = _matmul(x, wg, F32, act='sigmoid')
    tok = _matmul_heads(x, w_cmp, bsz, seq).reshape(bsz, 2 * G, nc, NSA_CMP_STRIDE * HEAD_DIM)
    kc = _compress(tok, 0, pos_k, w1_k, w2_k, bsz, seq, True)
    vc = _compress(tok, G, pos_v, w1_v, w2_v, bsz, seq, False)

    q_spec = pl.BlockSpec((1, TILE, R * LANES), lambda b, g, i: (b, i, g))
    o_spec = pl.BlockSpec((1, TILE, R * LANES), lambda b, g, i: (b, i, g))
    gate_spec = pl.BlockSpec((TILE, LANES), lambda b, g, i: (b * nq + i, 0))
    o_shape = jax.ShapeDtypeStruct((bsz, seq, N_HEADS * LANES), BF16)
    sem = ("parallel", "parallel", "arbitrary")

    o_cmp, nsel = pl.pallas_call(
        functools.partial(_nsa_cmp_kernel, n_top=n_top),
        out_shape=[o_shape, jax.ShapeDtypeStruct((bsz, G, nsb, seq), F32)],
        grid=(bsz, G, nq),
        in_specs=[q_spec,
                  pl.BlockSpec((1, 1, nc, LANES), lambda b, g, i: (b, g, 0, 0)),
                  pl.BlockSpec((1, 1, LANES, nc), lambda b, g, i: (b, g, 0, 0)),
                  pl.BlockSpec((R, CMP_TILES, CMP_CHUNK, TILE), lambda b, g, i: (g, 0, 0, 0)),
                  gate_spec],
        out_specs=[o_spec, pl.BlockSpec((1, 1, nsb, TILE), lambda b, g, i: (b, g, 0, i))],
        compiler_params=_params(sem),
        name="nsa_compressed_attn",
    )(q, kc, vc, bias_cmp, gates)

    once = pl.Buffered(1)

    def k_spec(first_blk):
        return pl.BlockSpec((1, seq, LANES), lambda b, g, i: (b, 0, first_blk + g // 2), pipeline_mode=once)

    def vt_spec(first_blk):
        return pl.BlockSpec((LANES, seq), lambda b, g, i: (first_blk + g, b), pipeline_mode=once)

    o_sel = pl.pallas_call(
        functools.partial(_nsa_sparse_kernel, window=False),
        out_shape=o_shape,
        grid=(bsz, G, nq),
        in_specs=[q_spec, k_spec(0), vt_spec(0),
                  pl.BlockSpec((R, BIASED_TILES + NSA_FAR_WIDTH - 1, TILE, TILE), lambda b, g, i: (g, 0, 0, 0),
                               pipeline_mode=once),
                  gate_spec,
                  pl.BlockSpec((1, 1, nsb, TILE), lambda b, g, i: (b, g, 0, i))],
        out_specs=o_spec,
        compiler_params=_params(sem),
        name="nsa_selected_attn",
    )(q, kk, v_t, bias_main, gates, nsel)

    o_win = pl.pallas_call(
        functools.partial(_nsa_sparse_kernel, window=True),
        out_shape=o_shape,
        grid=(bsz, G, nq),
        in_specs=[q_spec, k_spec(2), vt_spec(G),
                  pl.BlockSpec((R, NSA_WINDOW // TILE + 1, TILE, TILE), lambda b, g, i: (g, 0, 0, 0)),
                  gate_spec],
        out_specs=o_spec,
        compiler_params=_params(sem),
        name="nsa_window_attn",
    )(q, kk, v_t, bias_win, gates)

    m = bsz * seq
    parts = [t.reshape(m, N_HEADS * LANES) for t in (o_cmp, o_sel, o_win)]
    return parts, _pad_heads_rows(w_out, lambda h: 0).astype(BF16)


def _router_kernel(x_ref, w_ref, b_ref, o_ref):
    logits = lax.dot_general(w_ref[...], x_ref[...], (((1,), (1,)), ((), ())),
                             preferred_element_type=F32, precision=lax.Precision.HIGHEST)
    scores = 1.0 / (1.0 + jnp.exp(-logits))
    biased = scores + b_ref[...]
    rows = [biased[e:e + 1, :] for e in range(N_EXPERTS)]
    group_score = []
    for g in range(N_GROUPS):
        r = rows[g * EXPERTS_PER_GROUP:(g + 1) * EXPERTS_PER_GROUP]
        best = None
        for a in range(EXPERTS_PER_GROUP):
            for c in range(a + 1, EXPERTS_PER_GROUP):
                pair = r[a] + r[c]
                best = pair if best is None else jnp.maximum(best, pair)
        group_score.append(best)
    best_val = group_score[0]
    best_grp = jnp.zeros_like(best_val, dtype=jnp.int32)
    for g in range(1, N_GROUPS):
        better = group_score[g] > best_val
        best_val = jnp.where(better, group_score[g], best_val)
        best_grp = jnp.where(better, g, best_grp)
    picked = []
    for e in range(N_EXPERTS):
        g, a = divmod(e, EXPERTS_PER_GROUP)
        rank = jnp.zeros_like(best_grp)
        for c in range(EXPERTS_PER_GROUP):
            if c == a:
                continue
            other = rows[g * EXPERTS_PER_GROUP + c]
            ahead = (other > rows[e]) | ((other == rows[e]) & (c < a))
            rank = rank + jnp.where(ahead, 1, 0)
        picked.append((best_grp == g) & (rank < 2))
    raw = [jnp.where(picked[e], scores[e:e + 1, :], 0.0) for e in range(N_EXPERTS)]
    total = raw[0]
    for e in range(1, N_EXPERTS):
        total = total + raw[e]
    o_ref[...] = (jnp.concatenate(raw, axis=0) / total).T


def _moe_kernel(x_ref, gate_ref, wg_ref, wu_ref, wd_ref, g_ref, b_ref, o_ref, xb_ref, acc_ref):
    e = pl.program_id(1)

    @pl.when(e == 0)
    def _():
        xb_ref[...] = x_ref[...].astype(BF16)
        acc_ref[...] = jnp.zeros_like(acc_ref)

    xb = xb_ref[...]
    lane = lax.broadcasted_iota(jnp.int32, gate_ref.shape, 1)
    gcol = jnp.sum(jnp.where(lane == e, gate_ref[...], 0.0), axis=-1, keepdims=True)
    a = _dot(xb, wg_ref[0])
    h = a / (1.0 + jnp.exp(-a)) * _dot(xb, wu_ref[0])
    acc_ref[...] += gcol * _dot(h.astype(BF16), wd_ref[0])

    @pl.when(e == N_EXPERTS - 1)
    def _():
        o_ref[...] = _layer_norm(DEEPNORM_ALPHA * x_ref[...] + acc_ref[...], g_ref[...], b_ref[...])


def _moe_ln(x, router_w, router_b, w_gate, w_up, w_down, g, b):
    m, d = x.shape
    tm = 1024
    gates = pl.pallas_call(
        _router_kernel,
        out_shape=jax.ShapeDtypeStruct((m, N_EXPERTS), F32),
        grid=(m // tm,),
        in_specs=[pl.BlockSpec((tm, d), lambda i: (i, 0)),
                  pl.BlockSpec((N_EXPERTS, d), lambda i: (0, 0)),
                  pl.BlockSpec((N_EXPERTS, 1), lambda i: (0, 0))],
        out_specs=pl.BlockSpec((tm, N_EXPERTS), lambda i: (i, 0)),
        compiler_params=_params(("parallel",)),
        name="moe_router",
    )(x, router_w.T, router_b.reshape(N_EXPERTS, 1))
    de = w_gate.shape[-1]
    return pl.pallas_call(
        _moe_kernel,
        out_shape=jax.ShapeDtypeStruct((m, d), F32),
        grid=(m // tm, N_EXPERTS),
        in_specs=[pl.BlockSpec((tm, d), lambda i, e: (i, 0)),
                  pl.BlockSpec((tm, N_EXPERTS), lambda i, e: (i, 0)),
                  pl.BlockSpec((1, d, de), lambda i, e: (e, 0, 0)),
                  pl.BlockSpec((1, d, de), lambda i, e: (e, 0, 0)),
                  pl.BlockSpec((1, de, d), lambda i, e: (e, 0, 0)),
                  pl.BlockSpec((1, d), lambda i, e: (0, 0)),
                  pl.BlockSpec((1, d), lambda i, e: (0, 0))],
        out_specs=pl.BlockSpec((tm, d), lambda i, e: (i, 0)),
        scratch_shapes=[pltpu.VMEM((tm, d), BF16), pltpu.VMEM((tm, d), F32)],
        compiler_params=_params(("parallel", "arbitrary")),
        name="moe_experts_ln",
    )(x, gates, w_gate.astype(BF16), w_up.astype(BF16), w_down.astype(BF16), g.reshape(1, d), b.reshape(1, d))


def kernel(x, rel_table, router_w, router_b, ln1_g, ln1_b, ln2_g, ln2_b, exp_w_gate, exp_w_up, exp_w_down, dil_w_in, dil_w_out, sb_w_in, sb_w_out, nsa_w_in, nsa_w_out, nsa_cmp_pos_k, nsa_cmp_w1_k, nsa_cmp_w2_k, nsa_cmp_pos_v, nsa_cmp_w1_v, nsa_cmp_w2_v, moba_w_in, moba_w_out):
    bsz, seq, d = x.shape
    assert d == D_MODEL and seq % (DILATED_PAIRS[-1][1] * DIL_BLOCK) == 0
    depth = ln1_g.shape[0]
    assert depth == DEPTH
    n_mixers = 4
    table = rel_table.astype(F32)
    span = DIL_BLOCK
    big = 1 << 30
    bias_dil = _bias_tiles(table, len(DILATED_PAIRS), DIL_BLOCK, 2 * DIL_BLOCK, DIL_BLOCK, 0, 1, 0, span,
                           [dl for _, dl in DILATED_PAIRS])
    assert BIASED_TILES * TILE - (TILE - 1) >= BUCKET_THRESHOLDS[-1]
    bias_main = _bias_tiles(table, MAIN_TILES, TILE, TILE, 0, TILE, 1, 0, big, [1], shift=True, transposed=True,
                            scale=LOG2_E)
    bias_win = _bias_tiles(table, NSA_WINDOW // TILE + 1, TILE, TILE, 0, TILE, 1, 0, NSA_WINDOW - 1, [1],
                           transposed=True, scale=LOG2_E)
    bias_cmp = _bias_tiles(table, CMP_TILES, CMP_CHUNK, TILE, -TILE - (NSA_CMP_LEN - 1), TILE, NSA_CMP_STRIDE,
                           0, big, [1], transposed=True, scale=LOG2_E)
    xf = x.reshape(bsz * seq, d)
    for layer in range(depth):
        kind, occ = layer % n_mixers, layer // n_mixers
        merge = None
        if kind == 0:
            parts, w_out = _mixer_dilated(xf, dil_w_in[occ], dil_w_out[occ], bias_dil, bsz, seq)
            merge = 'lse3'
        elif kind == 1:
            parts, w_out = _mixer_stick_breaking(xf, sb_w_in[occ], sb_w_out[occ], bsz, seq)
        elif kind == 2:
            parts, w_out = _mixer_nsa(xf, nsa_w_in[occ], nsa_w_out[occ], (bias_main, bias_win, bias_cmp),
                                      nsa_cmp_pos_k[occ], nsa_cmp_w1_k[occ], nsa_cmp_w2_k[occ],
                                      nsa_cmp_pos_v[occ], nsa_cmp_w1_v[occ], nsa_cmp_w2_v[occ], bsz, seq)
            merge = 'sum3'
        else:
            parts, w_out = _mixer_moba(xf, moba_w_in[occ], moba_w_out[occ], bias_main, bsz, seq)
        xf = _outproj_ln(parts, w_out, xf, ln1_g[layer], ln1_b[layer], merge=merge)
        xf = _moe_ln(xf, router_w, router_b, exp_w_gate[layer], exp_w_up[layer], exp_w_down[layer],
                     ln2_g[layer], ln2_b[layer])
    return xf.reshape(bsz, seq, d)
```

```python
import functools
import math

import numpy as np
import jax
import jax.numpy as jnp
from jax import lax
from jax.experimental import pallas as pl
from jax.experimental.pallas import tpu as pltpu

F32 = jnp.float32
BF16 = jnp.bfloat16

D_MODEL = 1024
HEAD_DIM = 64
N_HEADS = 16
LANES = 128
PAIRS = N_HEADS // 2
ATTN_SCALE = HEAD_DIM ** -0.5
REL_BUCKETS = 32
REL_MAX_DIST = 2048
DILATED_PAIRS = ((128, 1), (512, 4), (2048, 16))
DIL_BLOCK = 128
SB_CLIP = 60.0
SB_TILE = 256
LOG2_E = math.log2(math.e)
NSA_KV_HEADS = 4
NSA_GROUP = N_HEADS // NSA_KV_HEADS
NSA_CMP_LEN = 32
NSA_CMP_STRIDE = 16
NSA_SEL_BLOCK = 64
NSA_TOP_N = 16
NSA_WINDOW = 512
MOBA_BLOCK = 256
MOBA_TOP_K = 3
N_EXPERTS = 16
N_GROUPS = 4
EXPERTS_PER_GROUP = N_EXPERTS // N_GROUPS
DEPTH = 4
DEEPNORM_ALPHA = (2 * DEPTH) ** 0.25
LN_EPS = 1e-5
NEG_INF = -1e30
FORCED_SCORE = 1e9
TILE = 256
MOBA_FAR_WIDTH = 8
MOBA_HEADS = 4
NSA_FAR_WIDTH = 4
CMP_CHUNK = 128
VMEM_LIMIT = 56 * 1024 * 1024


def _bucket_thresholds():
    n = np.arange(0, 2 * REL_MAX_DIST)
    exact = REL_BUCKETS // 2
    logf = np.log(np.maximum(n, 1).astype(np.float64) / exact) / math.log(REL_MAX_DIST / exact)
    large = np.minimum(exact + (logf * (REL_BUCKETS - exact)).astype(np.int64), REL_BUCKETS - 1)
    bucket = np.where(n < exact, n, large)
    return tuple(int(np.argmax(bucket >= k)) for k in range(1, REL_BUCKETS))


BUCKET_THRESHOLDS = _bucket_thresholds()
BIASED_TILES = -(-(BUCKET_THRESHOLDS[-1] + TILE - 1) // TILE)
MAIN_TILES = BIASED_TILES + max(MOBA_FAR_WIDTH, NSA_FAR_WIDTH) - 1
CMP_BIASED_TILES = -(-(BUCKET_THRESHOLDS[-1] + NSA_CMP_STRIDE * (CMP_CHUNK - 1) + NSA_CMP_LEN - 1) // TILE)
CMP_TILES = CMP_BIASED_TILES + 2


def _params(sem, vmem=VMEM_LIMIT):
    return pltpu.CompilerParams(dimension_semantics=sem, vmem_limit_bytes=vmem)


def _dot(a, b):
    return jnp.dot(a, b, preferred_element_type=F32)


def _dot_nt(a, b):
    return lax.dot_general(a, b, (((1,), (1,)), ((), ())), preferred_element_type=F32)


def _split_bf16(x):
    hi = x.astype(BF16)
    lo = (x - hi.astype(F32)).astype(BF16)
    return hi, lo


def _matmul_kernel(x_ref, w_ref, o_ref, *, act):
    y = _dot(x_ref[...].astype(BF16), w_ref[...])
    if act == 'sigmoid':
        y = 1.0 / (1.0 + jnp.exp(-y))
    o_ref[...] = y.astype(o_ref.dtype)


def _matmul(x, w, out_dtype, act=None):
    m, k = x.shape
    n = w.shape[1]
    tm = 1024
    tn = next(t for t in (1024, 512, 128) if n % t == 0)
    return pl.pallas_call(
        functools.partial(_matmul_kernel, act=act),
        out_shape=jax.ShapeDtypeStruct((m, n), out_dtype),
        grid=(m // tm, n // tn),
        in_specs=[pl.BlockSpec((tm, k), lambda i, j: (i, 0)),
                  pl.BlockSpec((k, tn), lambda i, j: (0, j))],
        out_specs=pl.BlockSpec((tm, tn), lambda i, j: (i, j)),
        compiler_params=_params(("parallel", "arbitrary")),
        name="proj_matmul",
    )(x, w)


def _matmul_t_kernel(x_ref, w_ref, o_ref):
    y = _dot_nt(w_ref[...], x_ref[...].astype(BF16))
    row = lax.broadcasted_iota(jnp.int32, y.shape, 0)
    o_ref[...] = jnp.where(row % LANES == HEAD_DIM, 1.0, y).astype(o_ref.dtype)


def _matmul_t(x, wt):
    m, k = x.shape
    n = wt.shape[0]
    tm = 1024
    tn = 512 if n % 512 == 0 else 256
    return pl.pallas_call(
        _matmul_t_kernel,
        out_shape=jax.ShapeDtypeStruct((n, m), BF16),
        grid=(m // tm, n // tn),
        in_specs=[pl.BlockSpec((tm, k), lambda i, j: (i, 0)),
                  pl.BlockSpec((tn, k), lambda i, j: (j, 0))],
        out_specs=pl.BlockSpec((tn, tm), lambda i, j: (j, i)),
        compiler_params=_params(("parallel", "arbitrary")),
        name="proj_matmul_t",
    )(x, wt)


def _matmul_heads_kernel(x_ref, w_ref, o_ref):
    y = _dot(x_ref[...].astype(BF16), w_ref[...]).astype(o_ref.dtype)
    for j in range(o_ref.shape[1]):
        o_ref[0, j] = y[:, j * HEAD_DIM:(j + 1) * HEAD_DIM]


def _matmul_heads(x, w, bsz, seq):
    m, k = x.shape
    n = w.shape[1] // HEAD_DIM
    tm = 512
    per_b = seq // tm
    return pl.pallas_call(
        _matmul_heads_kernel,
        out_shape=jax.ShapeDtypeStruct((bsz, n, seq, HEAD_DIM), BF16),
        grid=(m // tm,),
        in_specs=[pl.BlockSpec((tm, k), lambda i: (i, 0)),
                  pl.BlockSpec((k, n * HEAD_DIM), lambda i: (0, 0))],
        out_specs=pl.BlockSpec((1, n, tm, HEAD_DIM), lambda i: (i // per_b, 0, i % per_b, 0)),
        compiler_params=_params(("parallel",)),
        name="proj_heads",
    )(x, w)


def _layer_norm(z, g, b):
    mu = jnp.mean(z, axis=-1, keepdims=True)
    zc = z - mu
    var = jnp.mean(zc * zc, axis=-1, keepdims=True)
    return zc * lax.rsqrt(var + LN_EPS) * g + b


def _outproj_kernel(*refs, n_in, merge):
    ins = refs[:n_in]
    w_ref, x_ref, g_ref, b_ref, o_ref = refs[n_in:]
    if merge == 'lse3':
        o1, o2, o3, l1, l2, l3 = [r[...] for r in ins]
        mx = jnp.maximum(jnp.maximum(l1, l2), l3)
        e1, e2, e3 = jnp.exp(l1 - mx), jnp.exp(l2 - mx), jnp.exp(l3 - mx)
        a = (e1 * o1 + e2 * o2 + e3 * o3) / (e1 + e2 + e3)
    elif merge == 'sum3':
        a = ins[0][...].astype(F32) + ins[1][...].astype(F32) + ins[2][...].astype(F32)
    else:
        a = ins[0][...]
    y = _dot(a.astype(BF16), w_ref[...])
    o_ref[...] = _layer_norm(DEEPNORM_ALPHA * x_ref[...] + y, g_ref[...], b_ref[...])


def _outproj_ln(ins, w, x, g, b, merge=None):
    m, d = x.shape
    ka = ins[0].shape[1]
    tm = 256
    n_in = len(ins)
    return pl.pallas_call(
        functools.partial(_outproj_kernel, n_in=n_in, merge=merge),
        out_shape=jax.ShapeDtypeStruct((m, d), F32),
        grid=(m // tm,),
        in_specs=[pl.BlockSpec((tm, ka), lambda i: (i, 0)) for _ in ins]
        + [pl.BlockSpec((ka, d), lambda i: (0, 0)),
           pl.BlockSpec((tm, d), lambda i: (i, 0)),
           pl.BlockSpec((1, d), lambda i: (0, 0)),
           pl.BlockSpec((1, d), lambda i: (0, 0))],
        out_specs=pl.BlockSpec((tm, d), lambda i: (i, 0)),
        compiler_params=_params(("parallel",)),
        name="outproj_ln",
    )(*ins, w, x, g.reshape(1, d), b.reshape(1, d))


def _bias_kernel(tbl_ref, o_ref, *, rows, cols, off0, tstride, cmul, lo, hi, mults, shift, transposed, scale):
    t = pl.program_id(0)
    if len(set(mults)) == 1:
        mult = mults[0]
    else:
        mult = jnp.int32(mults[-1])
        for idx in range(len(mults) - 2, -1, -1):
            mult = jnp.where(t == idx, jnp.int32(mults[idx]), mult)
    base = off0 + tstride * t

    def strip(i, carry):
        r0 = pl.multiple_of(i * 8, 8)
        a = lax.broadcasted_iota(jnp.int32, (8, cols), 0) + r0
        c = lax.broadcasted_iota(jnp.int32, (8, cols), 1)
        steps = base + c - cmul * a if transposed else base + a - cmul * c
        valid = (steps >= lo) & (steps <= hi)
        dist = steps * mult
        for h in range(N_HEADS):
            acc = jnp.full((8, cols), tbl_ref[0, h], F32)
            for k, thr in enumerate(BUCKET_THRESHOLDS):
                acc = jnp.where(dist >= thr, tbl_ref[k + 1, h], acc)
            if shift:
                acc = acc - tbl_ref[REL_BUCKETS - 1, h]
            if scale != 1.0:
                acc = acc * scale
            o_ref[h, 0, pl.ds(r0, 8), :] = jnp.where(valid, acc, NEG_INF)
        return carry

    lax.fori_loop(0, rows // 8, strip, 0)


def _bias_tiles(table, n_tiles, rows, cols, off0, tstride, cmul, lo, hi, mults, shift=False, transposed=False,
                scale=1.0):
    return pl.pallas_call(
        functools.partial(_bias_kernel, rows=rows, cols=cols, off0=off0, tstride=tstride, cmul=cmul,
                          lo=lo, hi=hi, mults=tuple(mults), shift=shift, transposed=transposed, scale=scale),
        out_shape=jax.ShapeDtypeStruct((N_HEADS, n_tiles, rows, cols), F32),
        grid=(n_tiles,),
        in_specs=[pl.BlockSpec(memory_space=pltpu.SMEM)],
        out_specs=pl.BlockSpec((N_HEADS, 1, rows, cols), lambda t: (0, t, 0, 0)),
        compiler_params=_params(("arbitrary",)),
        name="bias_tiles",
    )(table)


def _dil_kernel(q_ref, kp_ref, kc_ref, vp_ref, vc_ref, bias_ref, o_ref, lse_ref,
                q_scr, kp_scr, kc_scr, vp_scr, vc_scr, *, dilation, per_step):
    j = pl.program_id(2)
    d = dilation
    n_cur = per_step * DIL_BLOCK

    def rows_of(ref, r, n, lead=()):
        if d == 1:
            return ref[lead + (slice(0, n), slice(None))]
        return ref[lead + (pl.ds(r, n, stride=d), slice(None))]

    if d > 1:
        for h in range(2):
            q_scr[h] = q_ref[0, :, h * LANES:(h + 1) * LANES].astype(F32)
        kp_scr[...] = kp_ref[0].astype(F32)
        kc_scr[...] = kc_ref[0].astype(F32)
        vp_scr[...] = vp_ref[0].astype(F32)
        vc_scr[...] = vc_ref[0].astype(F32)
    colk = lax.broadcasted_iota(jnp.int32, (DIL_BLOCK, 2 * DIL_BLOCK), 1)
    first = jnp.where((colk < DIL_BLOCK) & (j == 0), NEG_INF, 0.0)
    low = lax.broadcasted_iota(jnp.int32, (DIL_BLOCK, LANES), 1) < HEAD_DIM
    for r in range(d):
        if d == 1:
            q_heads = [q_ref[0, :, h * LANES:(h + 1) * LANES] for h in range(2)]
            k_all = jnp.concatenate([kp_ref[0], kc_ref[0]], axis=0)
            v_all = jnp.concatenate([vp_ref[0], vc_ref[0]], axis=0)
        else:
            q_heads = [rows_of(q_scr, r, n_cur, (h,)).astype(BF16) for h in range(2)]
            k_all = jnp.concatenate([rows_of(kp_scr, r, DIL_BLOCK), rows_of(kc_scr, r, n_cur)], axis=0).astype(BF16)
            v_all = jnp.concatenate([rows_of(vp_scr, r, DIL_BLOCK), rows_of(vc_scr, r, n_cur)], axis=0).astype(BF16)
        for u in range(per_step):
            k2 = k_all[u * DIL_BLOCK:(u + 2) * DIL_BLOCK]
            v2 = v_all[u * DIL_BLOCK:(u + 2) * DIL_BLOCK]
            outs, lses = [], []
            for h in range(2):
                s = _dot_nt(q_heads[h][u * DIL_BLOCK:(u + 1) * DIL_BLOCK], k2) + bias_ref[h, 0]
                if u == 0:
                    s = s + first
                m = jnp.max(s, axis=-1, keepdims=True)
                p = jnp.exp(s - m)
                l = jnp.maximum(jnp.sum(p, axis=-1, keepdims=True), 1e-30)
                outs.append(_dot(p.astype(BF16), v2) / l)
                lses.append(m + jnp.log(l))
            if d == 1:
                dst = (0, slice(u * DIL_BLOCK, (u + 1) * DIL_BLOCK), slice(None))
            else:
                dst = (0, pl.ds(u * DIL_BLOCK * d + r, DIL_BLOCK, stride=d), slice(None))
            o_ref[dst] = jnp.where(low, outs[0], outs[1])
            lse_ref[dst] = jnp.where(low, lses[0], lses[1])


def _dilated_group(proj, bias, g, dilation, bsz, seq):
    prev_rows = DIL_BLOCK * dilation
    step_rows = math.gcd(seq, 16 * DIL_BLOCK)
    per_step = step_rows // prev_rows
    assert per_step >= 1 and seq % step_rows == 0
    q_blk = g * N_HEADS // 2
    k_blk = 3 * N_HEADS + g * 2 * PAIRS
    v_blk = k_blk + PAIRS

    def kv_specs(base):
        prev = pl.BlockSpec((1, prev_rows, LANES), lambda hp, b, j: (b, jnp.maximum(per_step * j - 1, 0), base + hp))
        cur = pl.BlockSpec((1, step_rows, LANES), lambda hp, b, j: (b, j, base + hp))
        return [prev, cur]

    out_spec = pl.BlockSpec((1, step_rows, LANES), lambda hp, b, j: (b, j, hp))
    staged = 8 if dilation == 1 else None
    o, lse = pl.pallas_call(
        functools.partial(_dil_kernel, dilation=dilation, per_step=per_step),
        out_shape=[jax.ShapeDtypeStruct((bsz, seq, D_MODEL), F32)] * 2,
        grid=(PAIRS, bsz, seq // step_rows),
        in_specs=[pl.BlockSpec((1, step_rows, 2 * LANES), lambda hp, b, j: (b, j, q_blk + hp))]
        + kv_specs(k_blk) + kv_specs(v_blk)
        + [pl.BlockSpec((2, 1, DIL_BLOCK, 2 * DIL_BLOCK), lambda hp, b, j: (hp, g, 0, 0))],
        out_specs=[out_spec, out_spec],
        scratch_shapes=[pltpu.VMEM((2, staged or step_rows, LANES), F32),
                        pltpu.VMEM((staged or prev_rows, LANES), F32), pltpu.VMEM((staged or step_rows, LANES), F32),
                        pltpu.VMEM((staged or prev_rows, LANES), F32), pltpu.VMEM((staged or step_rows, LANES), F32)],
        compiler_params=_params(("parallel", "parallel", "arbitrary")),
        name="dilated_attn",
    )(proj, proj, proj, proj, proj, bias)
    return o.reshape(bsz * seq, D_MODEL), lse.reshape(bsz * seq, D_MODEL)


def _pad_heads_cols(w, parity_of_head):
    k, n = w.shape
    nh = n // HEAD_DIM
    wh = w.reshape(k, nh, HEAD_DIM)
    z = jnp.zeros_like(wh)
    par = jnp.asarray([parity_of_head(h) for h in range(nh)], jnp.int32)[None, :, None]
    lo = jnp.where(par == 0, wh, z)
    hi = jnp.where(par == 1, wh, z)
    return jnp.concatenate([lo, hi], axis=-1).reshape(k, nh * LANES)


def _pad_heads_rows(w, parity_of_head):
    return _pad_heads_cols(w.T, parity_of_head).T


def _mixer_dilated(x, w_in, w_out, bias_dil, bsz, seq):
    aw = N_HEADS * HEAD_DIM
    par = lambda h: h % 2
    qs, kvs = [], []
    for g in range(len(DILATED_PAIRS)):
        base = g * 3 * aw
        qs.append(_pad_heads_cols(w_in[:, base:base + aw] * ATTN_SCALE, par))
        kvs.append(w_in[:, base + aw:base + 3 * aw])
    w_all = jnp.concatenate(qs + kvs, axis=1).astype(BF16)
    proj = _matmul(x, w_all, BF16).reshape(bsz, seq, -1)
    parts = [_dilated_group(proj, bias_dil, g, d, bsz, seq) for g, (_, d) in enumerate(DILATED_PAIRS)]
    return [p[0] for p in parts] + [p[1] for p in parts], w_out.astype(BF16)


def _sb_kernel(q_ref, k_ref, v_ref, o_ref):
    i = pl.program_id(2)
    t = SB_TILE
    row = lax.broadcasted_iota(jnp.int32, (t, t), 0)
    col = lax.broadcasted_iota(jnp.int32, (t, t), 1)
    suffix = jnp.where(row >= col, 1.0, 0.0).astype(BF16)
    strict = col < row
    q2 = q_ref[0]
    q_heads = [q2[:, h * LANES:(h + 1) * LANES] for h in range(2)]

    def block(qh, kj, masked):
        start = pl.multiple_of(kj * t, t)
        kb = k_ref[0, pl.ds(start, t), :]
        vb = v_ref[0, pl.ds(start, t), :]
        a = jnp.clip(_dot_nt(qh, kb), -SB_CLIP, SB_CLIP) * LOG2_E
        sp = jnp.log2(1.0 + jnp.exp2(a))
        if masked:
            sp = jnp.where(strict, sp, 0.0)
        hi, lo = _split_bf16(sp)
        rr = _dot(hi, suffix) + _dot(lo, suffix)
        att = jnp.exp2(a - rr)
        if masked:
            att = jnp.where(strict, att, 0.0)
        return _dot(att.astype(BF16), vb), rr[:, 0:1]

    def first_two(kj, carry):
        has_prev = jnp.where(kj > 0, 1.0, 0.0)
        out = []
        for qh in q_heads:
            acc_d, r_d = block(qh, kj, True)
            pv, r_p = block(qh, jnp.maximum(kj - 1, 0), False)
            out.append((acc_d + (has_prev * jnp.exp2(-r_d)) * pv, r_d + has_prev * r_p))
        return tuple(out)

    zero = (jnp.zeros((t, LANES), F32), jnp.zeros((t, 1), F32))
    (acc0, r0), (acc1, r1) = lax.fori_loop(i, i + 1, first_two, (zero, zero))

    def cond(carry):
        n, alive = carry[0], carry[1]
        return (n < i - 1) & alive

    def step(carry):
        n = carry[0]
        new, tops = [], []
        for qh, (r_sum, w, acc) in zip(q_heads, carry[2:]):
            pv, r_blk = block(qh, i - 2 - n, False)
            r_new = r_sum + r_blk
            w_new = jnp.exp2(-r_new)
            new.append((r_new, w_new, acc + w * pv))
            tops.append(jnp.max(w_new, axis=0, keepdims=True))
        alive = jnp.maximum(tops[0], tops[1])[0, 0] > 0.0
        return (n + 1, alive) + tuple(new)

    w0, w1 = jnp.exp2(-r0), jnp.exp2(-r1)
    alive0 = jnp.maximum(jnp.max(w0, axis=0, keepdims=True), jnp.max(w1, axis=0, keepdims=True))[0, 0] > 0.0
    init = (jnp.int32(0), alive0, (r0, w0, acc0), (r1, w1, acc1))
    res = lax.while_loop(cond, step, init)
    lane = lax.broadcasted_iota(jnp.int32, (t, LANES), 1)
    o_ref[0] = jnp.where(lane < HEAD_DIM, res[2][2], res[3][2]).astype(o_ref.dtype)


def _mixer_stick_breaking(x, w_in, w_out, bsz, seq):
    aw = N_HEADS * HEAD_DIM
    wq = _pad_heads_cols(w_in[:, :aw] * ATTN_SCALE, lambda h: h % 2)
    w_all = jnp.concatenate([wq, w_in[:, aw:]], axis=1).astype(BF16)
    proj = _matmul(x, w_all, BF16).reshape(bsz, seq, -1)
    kb = N_HEADS
    o = pl.pallas_call(
        _sb_kernel,
        out_shape=jax.ShapeDtypeStruct((bsz, seq, D_MODEL), BF16),
        grid=(bsz, PAIRS, seq // SB_TILE),
        in_specs=[pl.BlockSpec((1, SB_TILE, 2 * LANES), lambda b, hp, i: (b, i, hp)),
                  pl.BlockSpec((1, seq, LANES), lambda b, hp, i: (b, 0, kb + hp)),
                  pl.BlockSpec((1, seq, LANES), lambda b, hp, i: (b, 0, kb + PAIRS + hp))],
        out_specs=pl.BlockSpec((1, SB_TILE, LANES), lambda b, hp, i: (b, i, hp)),
        compiler_params=_params(("parallel", "parallel", "arbitrary")),
        name="stick_breaking_attn",
    )(proj, proj, proj)
    return [o.reshape(bsz * seq, D_MODEL)], w_out.astype(BF16)


def _flash_chunks(q_heads, k_ref, k_cols, v_ref, v_rows, bias_ref, mask_rows, lo, hi, i, blocks_per_chunk, n_tiles,
                  fold_mask=False, pipelined=False, far_width=1, near_width=1):
    nh = len(q_heads)

    def logits(ch, near, width):
        rows = width * TILE
        start = pl.multiple_of(ch * TILE, TILE)
        kbs = {c: k_ref[0, pl.ds(start, rows), c:c + LANES] for c in sorted(set(k_cols))}
        add = None
        if fold_mask:
            key = lax.broadcasted_iota(jnp.int32, (rows, LANES), 0)
            blk = lax.broadcasted_iota(jnp.int32, (rows, LANES), 1)
            blk_of_key = ch * blocks_per_chunk + key // (TILE // blocks_per_chunk)
            ind = jnp.where(blk == blk_of_key, NEG_INF, 0.0).astype(BF16)
            kbs = {c: jnp.concatenate([kb, ind], axis=1) for c, kb in kbs.items()}
        elif mask_rows is not None:
            nb = width * blocks_per_chunk
            per = TILE // blocks_per_chunk
            blk_rows = mask_rows(ch * blocks_per_chunk, nb)
            add = jnp.concatenate([jnp.broadcast_to(blk_rows[r:r + 1, :], (per, TILE)) for r in range(nb)], axis=0)
        out = []
        for h in range(nh):
            s = _dot_nt(kbs[k_cols[h]], q_heads[h])
            if near:
                tiles = [bias_ref[h, jnp.maximum(i - ch - w, 0)] for w in range(width)]
                s = s + (tiles[0] if width == 1 else jnp.concatenate(tiles, axis=0))
            if add is not None:
                s = s + add
            out.append(s)
        return tuple(out)

    def update(ch, s_heads, state, width):
        start = pl.multiple_of(ch * TILE, TILE)
        vts = {}
        new = []
        for h in range(nh):
            r0 = v_rows[h]
            if r0 not in vts:
                vts[r0] = v_ref[r0:r0 + LANES, pl.ds(start, width * TILE)]
            m, acc = state[h]
            s = s_heads[h]
            m_new = jnp.maximum(m, jnp.max(s, axis=0, keepdims=True))
            alpha = jnp.exp2(m - m_new)
            p = jnp.exp2(s - m_new)
            acc = alpha * acc + _dot(vts[r0], p.astype(BF16))
            new.append((m_new, acc))
        return tuple(new)

    def run(ch0, steps, near, width, state, pipelined=pipelined):
        if not pipelined:
            return lax.fori_loop(
                0, steps, lambda t, st: update(ch0 + t * width, logits(ch0 + t * width, near, width), st, width), state)

        def body(t, carry):
            s_cur, st = carry
            s_next = logits(ch0 + jnp.minimum(t + 1, steps - 1) * width, near, width)
            return s_next, update(ch0 + t * width, s_cur, st, width)

        zeros = tuple(jnp.zeros((width * TILE, TILE), F32) for _ in range(nh))
        first = jnp.minimum(ch0, i)
        s0 = lax.fori_loop(first, first + 1, lambda c, _: logits(c, near, width), zeros)
        return lax.fori_loop(0, steps, body, (s0, state))[1]

    state = tuple((jnp.full((1, TILE), NEG_INF, F32), jnp.zeros((LANES, TILE), F32)) for _ in range(nh))
    n_near = n_tiles - (far_width - 1)
    far_steps = jnp.maximum(hi - n_near - lo, 0) // far_width
    split = lo + far_steps * far_width
    state = run(lo, far_steps, False, far_width, state)
    start, width = split, near_width
    while width >= 1:
        steps = (hi - start) // width
        state = run(start, steps, True, width, state, pipelined=pipelined and near_width == 1)
        start = start + steps * width
        width = width // 2 if width > 1 else 0
    outs = []
    for _, acc in state:
        o = (acc / acc[HEAD_DIM:HEAD_DIM + 1, :]).T
        lane = lax.broadcasted_iota(jnp.int32, o.shape, 1)
        outs.append(jnp.where(lane < HEAD_DIM, o, 0.0))
    return outs


def _kmean_kernel(k_ref, o_ref):
    rows = k_ref.shape[1]
    kf = k_ref[0].astype(F32).reshape(rows // MOBA_BLOCK, MOBA_BLOCK, k_ref.shape[2])
    o_ref[0] = jnp.sum(kf, axis=1) * (1.0 / MOBA_BLOCK)


def _moba_kernel(q_ref, k_ref, v_ref, km_ref, bias_ref, o_ref):
    i = pl.program_id(2)
    nblk = km_ref.shape[1]
    km_hi, km_lo = _split_bf16(km_ref[0])
    qs = q_ref[0]
    blk = lax.broadcasted_iota(jnp.int32, (nblk, TILE), 0)
    past = blk < i
    q_heads = []
    for h in range(MOBA_HEADS):
        qh = qs[:, h * LANES:(h + 1) * LANES]
        pair = slice((h // 2) * LANES, (h // 2 + 1) * LANES)
        gate = jnp.where(past, _dot_nt(km_hi[:, pair], qh) + _dot_nt(km_lo[:, pair], qh), NEG_INF)
        allowed = blk == i
        for _ in range(MOBA_TOP_K):
            mx = jnp.max(gate, axis=0, keepdims=True)
            first = jnp.min(jnp.where(gate == mx, blk, nblk), axis=0, keepdims=True)
            pick = blk == first
            allowed = allowed | (pick & past)
            gate = jnp.where(pick, -jnp.inf, gate)
        nsel = jnp.where(allowed, 0.0, 1.0).T.astype(BF16)
        q_heads.append(jnp.concatenate([qh, nsel], axis=1))
    outs = _flash_chunks(q_heads, k_ref, [(h // 2) * LANES for h in range(MOBA_HEADS)],
                         v_ref, [h * LANES for h in range(MOBA_HEADS)], bias_ref, None, 0, i + 1, i, 1,
                         BIASED_TILES + MOBA_FAR_WIDTH - 1, fold_mask=True, pipelined=False, far_width=MOBA_FAR_WIDTH,
                         near_width=4)
    for h in range(MOBA_HEADS):
        o_ref[0, :, h * LANES:(h + 1) * LANES] = outs[h].astype(o_ref.dtype)


def _mixer_moba(x, w_in, w_out, bias_main, bsz, seq):
    aw = N_HEADS * HEAD_DIM
    nblk = seq // MOBA_BLOCK
    assert nblk - 1 >= MOBA_TOP_K
    wq = _pad_heads_cols(w_in[:, :aw] * (ATTN_SCALE * LOG2_E), lambda h: h % 2)
    w_all = jnp.concatenate([wq, w_in[:, aw:2 * aw]], axis=1).astype(BF16)
    proj = _matmul(x, w_all, BF16).reshape(bsz, seq, -1)
    v_t = _matmul_t(x, _pad_heads_cols(w_in[:, 2 * aw:], lambda h: 0).T.astype(BF16))
    kb = N_HEADS
    rows = 8 * MOBA_BLOCK
    kmean = pl.pallas_call(
        _kmean_kernel,
        out_shape=jax.ShapeDtypeStruct((bsz, nblk, D_MODEL), F32),
        grid=(bsz, seq // rows),
        in_specs=[pl.BlockSpec((1, rows, D_MODEL), lambda b, i: (b, i, kb * LANES // D_MODEL))],
        out_specs=pl.BlockSpec((1, 8, D_MODEL), lambda b, i: (b, i, 0)),
        compiler_params=_params(("parallel", "parallel")),
        name="moba_kmean",
    )(proj)
    assert nblk <= LANES
    kmean = jnp.pad(kmean, ((0, 0), (0, LANES - nblk), (0, 0)))
    nh = MOBA_HEADS
    kw = nh // 2 * LANES
    once = pl.Buffered(1)
    o = pl.pallas_call(
        _moba_kernel,
        out_shape=jax.ShapeDtypeStruct((bsz, seq, N_HEADS * LANES), BF16),
        grid=(bsz, N_HEADS // nh, seq // TILE),
        in_specs=[pl.BlockSpec((1, TILE, nh * LANES), lambda b, hq, i: (b, i, hq)),
                  pl.BlockSpec((1, seq, kw), lambda b, hq, i: (b, 0, kb * LANES // kw + hq), pipeline_mode=once),
                  pl.BlockSpec((nh * LANES, seq), lambda b, hq, i: (hq, b), pipeline_mode=once),
                  pl.BlockSpec((1, LANES, kw), lambda b, hq, i: (b, 0, hq)),
                  pl.BlockSpec((nh, BIASED_TILES + MOBA_FAR_WIDTH - 1, TILE, TILE), lambda b, hq, i: (hq, 0, 0, 0),
                               pipeline_mode=once)],
        out_specs=pl.BlockSpec((1, TILE, nh * LANES), lambda b, hq, i: (b, i, hq)),
        compiler_params=_params(("parallel", "parallel", "arbitrary")),
        name="moba_attn",
    )(proj, proj, v_t, kmean, bias_main)
    return [o.reshape(bsz * seq, N_HEADS * LANES)], _pad_heads_rows(w_out, lambda h: 0).astype(BF16)


def _gelu_tanh(x):
    return 0.5 * x * (1.0 + jnp.tanh(math.sqrt(2.0 / math.pi) * (x + 0.044715 * (x * x * x))))


def _compress_kernel(a_ref, pos_ref, w1_ref, w2_ref, o_ref, *, transpose_out):
    nc = a_ref.shape[2]
    half = NSA_CMP_STRIDE * HEAD_DIM
    a = a_ref[0, 0].astype(F32)
    top = _dot((a + pos_ref[0:1, :]).astype(BF16), w1_ref[0:half, :])
    bot = _dot((a + pos_ref[1:2, :]).astype(BF16), w1_ref[half:2 * half, :])
    hid = top + pltpu.roll(bot, nc - 1, 0)
    out = _dot(_gelu_tanh(hid).astype(BF16), w2_ref[0])
    rowi = lax.broadcasted_iota(jnp.int32, out.shape, 0)
    out = jnp.where(rowi < nc - 1, out, 0.0)
    o_ref[0, 0] = (out.T if transpose_out else out).astype(o_ref.dtype)


def _compress(tok, first, pos, w1, w2, bsz, seq, by_parity):
    nc = seq // NSA_CMP_STRIDE
    half = NSA_CMP_STRIDE * HEAD_DIM
    pos2 = pos.reshape(2, half)
    z = jnp.zeros_like(w2)
    low = jnp.concatenate([w2, z], axis=1)
    w2p = jnp.stack([low, jnp.concatenate([z, w2], axis=1) if by_parity else low]).astype(BF16)
    out_dims = (nc, LANES) if by_parity else (LANES, nc)
    return pl.pallas_call(
        functools.partial(_compress_kernel, transpose_out=not by_parity),
        out_shape=jax.ShapeDtypeStruct((bsz, NSA_KV_HEADS) + out_dims, BF16),
        grid=(bsz, NSA_KV_HEADS),
        in_specs=[pl.BlockSpec((1, 1, nc, half), lambda b, g: (b, first + g, 0, 0)),
                  pl.BlockSpec((2, half), lambda b, g: (0, 0)),
                  pl.BlockSpec((2 * half, w1.shape[1]), lambda b, g: (0, 0)),
                  pl.BlockSpec((1, w2.shape[0], LANES), lambda b, g: (g % 2, 0, 0))],
        out_specs=pl.BlockSpec((1, 1) + out_dims, lambda b, g: (b, g, 0, 0)),
        compiler_params=_params(("parallel", "parallel")),
        name="nsa_compress",
    )(tok, pos2, w1.astype(BF16), w2p)


def _gate_columns(gate_ref, branch):
    tile = gate_ref[...]
    lane = lax.broadcasted_iota(jnp.int32, tile.shape, 1)
    base = branch * N_HEADS + pl.program_id(1) * NSA_GROUP
    return [jnp.sum(jnp.where(lane == base + r, tile, 0.0), axis=-1, keepdims=True) for r in range(NSA_GROUP)]


def _nsa_cmp_kernel(q_ref, kc_ref, vc_ref, bias_ref, gate_ref, o_ref, sel_ref, *, n_top):
    i = pl.program_id(2)
    nc = kc_ref.shape[2]
    n_chunks = nc // CMP_CHUNK
    nsb = sel_ref.shape[2]
    q4 = q_ref[0]
    gates = _gate_columns(gate_ref, 0)
    keep = lax.broadcasted_iota(jnp.int32, (TILE, LANES), 1) < HEAD_DIM
    psum = [jnp.zeros((CMP_CHUNK, TILE), F32) for _ in range(n_chunks)]
    for r in range(NSA_GROUP):
        qh = q4[:, r * LANES:(r + 1) * LANES]
        ss = []
        for c in range(n_chunks):
            tile = jnp.clip(i - (CMP_CHUNK * NSA_CMP_STRIDE // TILE) * c, -1, CMP_TILES - 2) + 1
            ss.append(_dot_nt(kc_ref[0, 0, c * CMP_CHUNK:(c + 1) * CMP_CHUNK, :], qh) + bias_ref[r, tile])
        m = ss[0].max(axis=0, keepdims=True)
        for c in range(1, n_chunks):
            m = jnp.maximum(m, ss[c].max(axis=0, keepdims=True))
        ps = [jnp.exp2(s - m) for s in ss]
        l = ps[0].sum(axis=0, keepdims=True)
        for c in range(1, n_chunks):
            l = l + ps[c].sum(axis=0, keepdims=True)
        inv = jnp.where(m > 0.5 * NEG_INF, 1.0 / jnp.maximum(l, 1e-30), 0.0)
        acc = jnp.zeros((LANES, TILE), F32)
        for c in range(n_chunks):
            pc = ps[c] * inv
            psum[c] = psum[c] + pc
            acc = acc + _dot(vc_ref[0, 0, :, c * CMP_CHUNK:(c + 1) * CMP_CHUNK], pc.astype(BF16))
        o_ref[0, :, r * LANES:(r + 1) * LANES] = jnp.where(keep, acc.T * gates[r], 0.0).astype(o_ref.dtype)
    imp = jnp.zeros((nsb, TILE), F32)
    per_sel = NSA_SEL_BLOCK // NSA_CMP_STRIDE
    for c in range(n_chunks):
        jb = lax.broadcasted_iota(jnp.int32, (nsb, CMP_CHUNK), 0)
        ci = lax.broadcasted_iota(jnp.int32, (nsb, CMP_CHUNK), 1) + c * CMP_CHUNK
        rel = ci - per_sel * jb
        over = (rel >= 1 - NSA_CMP_LEN // NSA_CMP_STRIDE) & (rel < per_sel) & (ci < nc - 1)
        ov = jnp.where(over, 1.0, 0.0).astype(BF16)
        hi, lo = _split_bf16(psum[c])
        imp = imp + _dot(ov, hi) + _dot(ov, lo)
    jb = lax.broadcasted_iota(jnp.int32, (nsb, TILE), 0)
    qpos = lax.broadcasted_iota(jnp.int32, (nsb, TILE), 1) + i * TILE
    cur = qpos // NSA_SEL_BLOCK
    forced = (jb == 0) | (jb == cur) | (jb == cur - 1)
    causal = jb * NSA_SEL_BLOCK <= qpos
    score = jnp.where(forced, FORCED_SCORE, jnp.where(causal, imp, NEG_INF))
    chosen = jnp.zeros((nsb, TILE), F32)
    for _ in range(n_top):
        mx = jnp.max(score, axis=0, keepdims=True)
        first = jnp.min(jnp.where(score == mx, jb, nsb), axis=0, keepdims=True)
        pick = jb == first
        chosen = jnp.where(pick, 1.0, chosen)
        score = jnp.where(pick, -jnp.inf, score)
    sel_ref[0, 0] = jnp.where((chosen > 0.0) & causal, 0.0, NEG_INF)


def _nsa_sparse_kernel(q_ref, k_ref, v_ref, bias_ref, gate_ref, *rest, window):
    if window:
        (o_ref,) = rest
        mask_rows = None
    else:
        sel_ref, o_ref = rest

        def mask_rows(b0, n):
            if n % 8 == 0:
                return sel_ref[0, 0, pl.ds(pl.multiple_of(b0, 8), n), :]
            assert n == 4
            rows = sel_ref[0, 0, pl.ds(pl.multiple_of(b0 // 8 * 8, 8), 8), :]
            return jnp.where(b0 % 8 == 0, rows[0:4], rows[4:8])

    i = pl.program_id(2)
    q4 = q_ref[0]
    gates = _gate_columns(gate_ref, 2 if window else 1)
    q_heads = [q4[:, r * LANES:(r + 1) * LANES] for r in range(NSA_GROUP)]
    if window:
        n_tiles = NSA_WINDOW // TILE + 1
        lo = jnp.maximum(i - (n_tiles - 1), 0)
        outs = _flash_chunks(q_heads, k_ref, [0] * NSA_GROUP, v_ref, [0] * NSA_GROUP, bias_ref, None,
                             lo, i + 1, i, 1, n_tiles, near_width=n_tiles)
    else:
        outs = _flash_chunks(q_heads, k_ref, [0] * NSA_GROUP, v_ref, [0] * NSA_GROUP, bias_ref, mask_rows,
                             0, i + 1, i,
                             TILE // NSA_SEL_BLOCK, BIASED_TILES + NSA_FAR_WIDTH - 1, far_width=NSA_FAR_WIDTH,
                             near_width=4)
    for r in range(NSA_GROUP):
        o_ref[0, :, r * LANES:(r + 1) * LANES] = (outs[r] * gates[r]).astype(o_ref.dtype)


def _mixer_nsa(x, w_in, w_out, table_bias, pos_k, w1_k, w2_k, pos_v, w1_v, w2_v, bsz, seq):
    bias_main, bias_win, bias_cmp = table_bias
    aw = N_HEADS * HEAD_DIM
    kvw = NSA_KV_HEADS * HEAD_DIM
    G, R = NSA_KV_HEADS, NSA_GROUP
    nc = seq // NSA_CMP_STRIDE
    nsb = seq // NSA_SEL_BLOCK
    n_top = min(NSA_TOP_N, nsb)
    nq = seq // TILE
    assert nc % CMP_CHUNK == 0
    par = lambda h: (h // R) % 2
    wq = _pad_heads_cols(w_in[:, :aw] * (ATTN_SCALE * LOG2_E), par).astype(BF16)
    w_cmp = w_in[:, aw:aw + 2 * kvw].astype(BF16)
    c0 = aw + 2 * kvw
    wk = jnp.concatenate([w_in[:, c0:c0 + kvw], w_in[:, c0 + 2 * kvw:c0 + 3 * kvw]], axis=1).astype(BF16)
    wv = jnp.concatenate([w_in[:, c0 + kvw:c0 + 2 * kvw], w_in[:, c0 + 3 * kvw:c0 + 4 * kvw]], axis=1)
    wv_t = _pad_heads_cols(wv, lambda h: 0).T.astype(BF16)
    wg = jnp.pad(w_in[:, aw + 6 * kvw:], ((0, 0), (0, LANES - 3 * N_HEADS))).astype(BF16)
    q = _matmul(x, wq, BF16).reshape(bsz, seq, N_HEADS * LANES)
    kk = _matmul(x, wk, BF16).reshape(bsz, seq, 2 * kvw)
    v_t = _matmul_t(x, wv_t)
    gates = _matmul(x, wg, F32, act='sigmoid')
    tok = _matmul_heads(x, w_cmp, bsz, seq).reshape(bsz, 2 * G, nc, NSA_CMP_STRIDE * HEAD_DIM)
    kc = _compress(tok, 0, pos_k, w1_k, w2_k, bsz, seq, True)
    vc = _compress(tok, G, pos_v, w1_v, w2_v, bsz, seq, False)

    q_spec = pl.BlockSpec((1, TILE, R * LANES), lambda b, g, i: (b, i, g))
    o_spec = pl.BlockSpec((1, TILE, R * LANES), lambda b, g, i: (b, i, g))
    gate_spec = pl.BlockSpec((TILE, LANES), lambda b, g, i: (b * nq + i, 0))
    o_shape = jax.ShapeDtypeStruct((bsz, seq, N_HEADS * LANES), BF16)
    sem = ("parallel", "parallel", "arbitrary")

    o_cmp, nsel = pl.pallas_call(
        functools.partial(_nsa_cmp_kernel, n_top=n_top),
        out_shape=[o_shape, jax.ShapeDtypeStruct((bsz, G, nsb, seq), F32)],
        grid=(bsz, G, nq),
        in_specs=[q_spec,
                  pl.BlockSpec((1, 1, nc, LANES), lambda b, g, i: (b, g, 0, 0)),
                  pl.BlockSpec((1, 1, LANES, nc), lambda b, g, i: (b, g, 0, 0)),
                  pl.BlockSpec((R, CMP_TILES, CMP_CHUNK, TILE), lambda b, g, i: (g, 0, 0, 0)),
                  gate_spec],
        out_specs=[o_spec, pl.BlockSpec((1, 1, nsb, TILE), lambda b, g, i: (b, g, 0, i))],
        compiler_params=_params(sem),
        name="nsa_compressed_attn",
    )(q, kc, vc, bias_cmp, gates)

    once = pl.Buffered(1)

    def k_spec(first_blk):
        return pl.BlockSpec((1, seq, LANES), lambda b, g, i: (b, 0, first_blk + g // 2), pipeline_mode=once)

    def vt_spec(first_blk):
        return pl.BlockSpec((LANES, seq), lambda b, g, i: (first_blk + g, b), pipeline_mode=once)

    o_sel = pl.pallas_call(
        functools.partial(_nsa_sparse_kernel, window=False),
        out_shape=o_shape,
        grid=(bsz, G, nq),
        in_specs=[q_spec, k_spec(0), vt_spec(0),
                  pl.BlockSpec((R, BIASED_TILES + NSA_FAR_WIDTH - 1, TILE, TILE), lambda b, g, i: (g, 0, 0, 0),
                               pipeline_mode=once),
                  gate_spec,
                  pl.BlockSpec((1, 1, nsb, TILE), lambda b, g, i: (b, g, 0, i))],
        out_specs=o_spec,
        compiler_params=_params(sem),
        name="nsa_selected_attn",
    )(q, kk, v_t, bias_main, gates, nsel)

    o_win = pl.pallas_call(
        functools.partial(_nsa_sparse_kernel, window=True),
        out_shape=o_shape,
        grid=(bsz, G, nq),
        in_specs=[q_spec, k_spec(2), vt_spec(G),
                  pl.BlockSpec((R, NSA_WINDOW // TILE + 1, TILE, TILE), lambda b, g, i: (g, 0, 0, 0)),
                  gate_spec],
        out_specs=o_spec,
        compiler_params=_params(sem),
        name="nsa_window_attn",
    )(q, kk, v_t, bias_win, gates)

    m = bsz * seq
    parts = [t.reshape(m, N_HEADS * LANES) for t in (o_cmp, o_sel, o_win)]
    return parts, _pad_heads_rows(w_out, lambda h: 0).astype(BF16)


def _router_kernel(x_ref, w_ref, b_ref, o_ref):
    logits = lax.dot_general(w_ref[...], x_ref[...], (((1,), (1,)), ((), ())),
                             preferred_element_type=F32, precision=lax.Precision.HIGHEST)
    scores = 1.0 / (1.0 + jnp.exp(-logits))
    biased = scores + b_ref[...]
    rows = [biased[e:e + 1, :] for e in range(N_EXPERTS)]
    group_score = []
    for g in range(N_GROUPS):
        r = rows[g * EXPERTS_PER_GROUP:(g + 1) * EXPERTS_PER_GROUP]
        best = None
        for a in range(EXPERTS_PER_GROUP):
            for c in range(a + 1, EXPERTS_PER_GROUP):
                pair = r[a] + r[c]
                best = pair if best is None else jnp.maximum(best, pair)
        group_score.append(best)
    best_val = group_score[0]
    best_grp = jnp.zeros_like(best_val, dtype=jnp.int32)
    for g in range(1, N_GROUPS):
        better = group_score[g] > best_val
        best_val = jnp.where(better, group_score[g], best_val)
        best_grp = jnp.where(better, g, best_grp)
    picked = []
    for e in range(N_EXPERTS):
        g, a = divmod(e, EXPERTS_PER_GROUP)
        rank = jnp.zeros_like(best_grp)
        for c in range(EXPERTS_PER_GROUP):
            if c == a:
                continue
            other = rows[g * EXPERTS_PER_GROUP + c]
            ahead = (other > rows[e]) | ((other == rows[e]) & (c < a))
            rank = rank + jnp.where(ahead, 1, 0)
        picked.append((best_grp == g) & (rank < 2))
    raw = [jnp.where(picked[e], scores[e:e + 1, :], 0.0) for e in range(N_EXPERTS)]
    total = raw[0]
    for e in range(1, N_EXPERTS):
        total = total + raw[e]
    o_ref[...] = (jnp.concatenate(raw, axis=0) / total).T


def _moe_kernel(x_ref, gate_ref, wg_ref, wu_ref, wd_ref, g_ref, b_ref, o_ref, xb_ref, acc_ref):
    e = pl.program_id(1)

    @pl.when(e == 0)
    def _():
        xb_ref[...] = x_ref[...].astype(BF16)
        acc_ref[...] = jnp.zeros_like(acc_ref)

    xb = xb_ref[...]
    lane = lax.broadcasted_iota(jnp.int32, gate_ref.shape, 1)
    gcol = jnp.sum(jnp.where(lane == e, gate_ref[...], 0.0), axis=-1, keepdims=True)
    a = _dot(xb, wg_ref[0])
    h = a / (1.0 + jnp.exp(-a)) * _dot(xb, wu_ref[0])
    acc_ref[...] += gcol * _dot(h.astype(BF16), wd_ref[0])

    @pl.when(e == N_EXPERTS - 1)
    def _():
        o_ref[...] = _layer_norm(DEEPNORM_ALPHA * x_ref[...] + acc_ref[...], g_ref[...], b_ref[...])


def _moe_ln(x, router_w, router_b, w_gate, w_up, w_down, g, b):
    m, d = x.shape
    tm = 1024
    gates = pl.pallas_call(
        _router_kernel,
        out_shape=jax.ShapeDtypeStruct((m, N_EXPERTS), F32),
        grid=(m // tm,),
        in_specs=[pl.BlockSpec((tm, d), lambda i: (i, 0)),
                  pl.BlockSpec((N_EXPERTS, d), lambda i: (0, 0)),
                  pl.BlockSpec((N_EXPERTS, 1), lambda i: (0, 0))],
        out_specs=pl.BlockSpec((tm, N_EXPERTS), lambda i: (i, 0)),
        compiler_params=_params(("parallel",)),
        name="moe_router",
    )(x, router_w.T, router_b.reshape(N_EXPERTS, 1))
    de = w_gate.shape[-1]
    return pl.pallas_call(
        _moe_kernel,
        out_shape=jax.ShapeDtypeStruct((m, d), F32),
        grid=(m // tm, N_EXPERTS),
        in_specs=[pl.BlockSpec((tm, d), lambda i, e: (i, 0)),
                  pl.BlockSpec((tm, N_EXPERTS), lambda i, e: (i, 0)),
                  pl.BlockSpec((1, d, de), lambda i, e: (e, 0, 0)),
                  pl.BlockSpec((1, d, de), lambda i, e: (e, 0, 0)),
                  pl.BlockSpec((1, de, d), lambda i, e: (e, 0, 0)),
                  pl.BlockSpec((1, d), lambda i, e: (0, 0)),
                  pl.BlockSpec((1, d), lambda i, e: (0, 0))],
        out_specs=pl.BlockSpec((tm, d), lambda i, e: (i, 0)),
        scratch_shapes=[pltpu.VMEM((tm, d), BF16), pltpu.VMEM((tm, d), F32)],
        compiler_params=_params(("parallel", "arbitrary")),
        name="moe_experts_ln",
    )(x, gates, w_gate.astype(BF16), w_up.astype(BF16), w_down.astype(BF16), g.reshape(1, d), b.reshape(1, d))


def kernel(x, rel_table, router_w, router_b, ln1_g, ln1_b, ln2_g, ln2_b, exp_w_gate, exp_w_up, exp_w_down, dil_w_in, dil_w_out, sb_w_in, sb_w_out, nsa_w_in, nsa_w_out, nsa_cmp_pos_k, nsa_cmp_w1_k, nsa_cmp_w2_k, nsa_cmp_pos_v, nsa_cmp_w1_v, nsa_cmp_w2_v, moba_w_in, moba_w_out):
    bsz, seq, d = x.shape
    assert d == D_MODEL and seq % (DILATED_PAIRS[-1][1] * DIL_BLOCK) == 0
    depth = ln1_g.shape[0]
    assert depth == DEPTH
    n_mixers = 4
    table = rel_table.astype(F32)
    span = DIL_BLOCK
    big = 1 << 30
    bias_dil = _bias_tiles(table, len(DILATED_PAIRS), DIL_BLOCK, 2 * DIL_BLOCK, DIL_BLOCK, 0, 1, 0, span,
                           [dl for _, dl in DILATED_PAIRS])
    assert BIASED_TILES * TILE - (TILE - 1) >= BUCKET_THRESHOLDS[-1]
    bias_main = _bias_tiles(table, MAIN_TILES, TILE, TILE, 0, TILE, 1, 0, big, [1], shift=True, transposed=True,
                            scale=LOG2_E)
    bias_win = _bias_tiles(table, NSA_WINDOW // TILE + 1, TILE, TILE, 0, TILE, 1, 0, NSA_WINDOW - 1, [1],
                           transposed=True, scale=LOG2_E)
    bias_cmp = _bias_tiles(table, CMP_TILES, CMP_CHUNK, TILE, -TILE - (NSA_CMP_LEN - 1), TILE, NSA_CMP_STRIDE,
                           0, big, [1], transposed=True, scale=LOG2_E)
    xf = x.reshape(bsz * seq, d)
    for layer in range(depth):
        kind, occ = layer % n_mixers, layer // n_mixers
        merge = None
        if kind == 0:
            parts, w_out = _mixer_dilated(xf, dil_w_in[occ], dil_w_out[occ], bias_dil, bsz, seq)
            merge = 'lse3'
        elif kind == 1:
            parts, w_out = _mixer_stick_breaking(xf, sb_w_in[occ], sb_w_out[occ], bsz, seq)
        elif kind == 2:
            parts, w_out = _mixer_nsa(xf, nsa_w_in[occ], nsa_w_out[occ], (bias_main, bias_win, bias_cmp),
                                      nsa_cmp_pos_k[occ], nsa_cmp_w1_k[occ], nsa_cmp_w2_k[occ],
                                      nsa_cmp_pos_v[occ], nsa_cmp_w1_v[occ], nsa_cmp_w2_v[occ], bsz, seq)
            merge = 'sum3'
        else:
            parts, w_out = _mixer_moba(xf, moba_w_in[occ], moba_w_out[occ], bias_main, bsz, seq)
        xf = _outproj_ln(parts, w_out, xf, ln1_g[layer], ln1_b[layer], merge=merge)
        xf = _moe_ln(xf, router_w, router_b, exp_w_gate[layer], exp_w_up[layer], exp_w_down[layer],
                     ln2_g[layer], ln2_b[layer])
    return xf.reshape(bsz, seq, d)
```

```python
import functools
import math

import numpy as np
import jax
import jax.numpy as jnp
from jax import lax
from jax.experimental import pallas as pl
from jax.experimental.pallas import tpu as pltpu

F32 = jnp.float32
BF16 = jnp.bfloat16

D_MODEL = 1024
HEAD_DIM = 64
N_HEADS = 16
LANES = 128
PAIRS = N_HEADS // 2
ATTN_SCALE = HEAD_DIM ** -0.5
REL_BUCKETS = 32
REL_MAX_DIST = 2048
DILATED_PAIRS = ((128, 1), (512, 4), (2048, 16))
DIL_BLOCK = 128
SB_CLIP = 60.0
SB_TILE = 256
LOG2_E = math.log2(math.e)
NSA_KV_HEADS = 4
NSA_GROUP = N_HEADS // NSA_KV_HEADS
NSA_CMP_LEN = 32
NSA_CMP_STRIDE = 16
NSA_SEL_BLOCK = 64
NSA_TOP_N = 16
NSA_WINDOW = 512
MOBA_BLOCK = 256
MOBA_TOP_K = 3
N_EXPERTS = 16
N_GROUPS = 4
EXPERTS_PER_GROUP = N_EXPERTS // N_GROUPS
DEPTH = 4
DEEPNORM_ALPHA = (2 * DEPTH) ** 0.25
LN_EPS = 1e-5
NEG_INF = -1e30
FORCED_SCORE = 1e9
TILE = 256
MOBA_FAR_WIDTH = 8
MOBA_HEADS = 4
NSA_FAR_WIDTH = 4
CMP_CHUNK = 128
VMEM_LIMIT = 56 * 1024 * 1024


def _bucket_thresholds():
    n = np.arange(0, 2 * REL_MAX_DIST)
    exact = REL_BUCKETS // 2
    logf = np.log(np.maximum(n, 1).astype(np.float64) / exact) / math.log(REL_MAX_DIST / exact)
    large = np.minimum(exact + (logf * (REL_BUCKETS - exact)).astype(np.int64), REL_BUCKETS - 1)
    bucket = np.where(n < exact, n, large)
    return tuple(int(np.argmax(bucket >= k)) for k in range(1, REL_BUCKETS))


BUCKET_THRESHOLDS = _bucket_thresholds()
BIASED_TILES = -(-(BUCKET_THRESHOLDS[-1] + TILE - 1) // TILE)
MAIN_TILES = BIASED_TILES + max(MOBA_FAR_WIDTH, NSA_FAR_WIDTH) - 1
CMP_BIASED_TILES = -(-(BUCKET_THRESHOLDS[-1] + NSA_CMP_STRIDE * (CMP_CHUNK - 1) + NSA_CMP_LEN - 1) // TILE)
CMP_TILES = CMP_BIASED_TILES + 2


def _params(sem, vmem=VMEM_LIMIT):
    return pltpu.CompilerParams(dimension_semantics=sem, vmem_limit_bytes=vmem)


def _dot(a, b):
    return jnp.dot(a, b, preferred_element_type=F32)


def _dot_nt(a, b):
    return lax.dot_general(a, b, (((1,), (1,)), ((), ())), preferred_element_type=F32)


def _split_bf16(x):
    hi = x.astype(BF16)
    lo = (x - hi.astype(F32)).astype(BF16)
    return hi, lo


def _matmul_kernel(x_ref, w_ref, o_ref, *, act):
    y = _dot(x_ref[...].astype(BF16), w_ref[...])
    if act == 'sigmoid':
        y = 1.0 / (1.0 + jnp.exp(-y))
    o_ref[...] = y.astype(o_ref.dtype)


def _matmul(x, w, out_dtype, act=None):
    m, k = x.shape
    n = w.shape[1]
    tm = 1024
    tn = next(t for t in (1024, 512, 128) if n % t == 0)
    return pl.pallas_call(
        functools.partial(_matmul_kernel, act=act),
        out_shape=jax.ShapeDtypeStruct((m, n), out_dtype),
        grid=(m // tm, n // tn),
        in_specs=[pl.BlockSpec((tm, k), lambda i, j: (i, 0)),
                  pl.BlockSpec((k, tn), lambda i, j: (0, j))],
        out_specs=pl.BlockSpec((tm, tn), lambda i, j: (i, j)),
        compiler_params=_params(("parallel", "arbitrary")),
        name="proj_matmul",
    )(x, w)


def _matmul_t_kernel(x_ref, w_ref, o_ref):
    y = _dot_nt(w_ref[...], x_ref[...].astype(BF16))
    row = lax.broadcasted_iota(jnp.int32, y.shape, 0)
    o_ref[...] = jnp.where(row % LANES == HEAD_DIM, 1.0, y).astype(o_ref.dtype)


def _matmul_t(x, wt):
    m, k = x.shape
    n = wt.shape[0]
    tm = 1024
    tn = 512 if n % 512 == 0 else 256
    return pl.pallas_call(
        _matmul_t_kernel,
        out_shape=jax.ShapeDtypeStruct((n, m), BF16),
        grid=(m // tm, n // tn),
        in_specs=[pl.BlockSpec((tm, k), lambda i, j: (i, 0)),
                  pl.BlockSpec((tn, k), lambda i, j: (j, 0))],
        out_specs=pl.BlockSpec((tn, tm), lambda i, j: (j, i)),
        compiler_params=_params(("parallel", "arbitrary")),
        name="proj_matmul_t",
    )(x, wt)


def _matmul_heads_kernel(x_ref, w_ref, o_ref):
    y = _dot(x_ref[...].astype(BF16), w_ref[...]).astype(o_ref.dtype)
    for j in range(o_ref.shape[1]):
        o_ref[0, j] = y[:, j * HEAD_DIM:(j + 1) * HEAD_DIM]


def _matmul_heads(x, w, bsz, seq):
    m, k = x.shape
    n = w.shape[1] // HEAD_DIM
    tm = 512
    per_b = seq // tm
    return pl.pallas_call(
        _matmul_heads_kernel,
        out_shape=jax.ShapeDtypeStruct((bsz, n, seq, HEAD_DIM), BF16),
        grid=(m // tm,),
        in_specs=[pl.BlockSpec((tm, k), lambda i: (i, 0)),
                  pl.BlockSpec((k, n * HEAD_DIM), lambda i: (0, 0))],
        out_specs=pl.BlockSpec((1, n, tm, HEAD_DIM), lambda i: (i // per_b, 0, i % per_b, 0)),
        compiler_params=_params(("parallel",)),
        name="proj_heads",
    )(x, w)


def _layer_norm(z, g, b):
    mu = jnp.mean(z, axis=-1, keepdims=True)
    zc = z - mu
    var = jnp.mean(zc * zc, axis=-1, keepdims=True)
    return zc * lax.rsqrt(var + LN_EPS) * g + b


def _outproj_kernel(*refs, n_in, merge):
    ins = refs[:n_in]
    w_ref, x_ref, g_ref, b_ref, rw_ref, rb_ref, o_ref, gate_ref = refs[n_in:]
    if merge == 'lse3':
        o1, o2, o3, l1, l2, l3 = [r[...] for r in ins]
        mx = jnp.maximum(jnp.maximum(l1, l2), l3)
        e1, e2, e3 = jnp.exp(l1 - mx), jnp.exp(l2 - mx), jnp.exp(l3 - mx)
        a = (e1 * o1 + e2 * o2 + e3 * o3) / (e1 + e2 + e3)
    elif merge == 'sum3':
        a = ins[0][...].astype(F32) + ins[1][...].astype(F32) + ins[2][...].astype(F32)
    else:
        a = ins[0][...]
    y = _dot(a.astype(BF16), w_ref[...])
    z = _layer_norm(DEEPNORM_ALPHA * x_ref[...] + y, g_ref[...], b_ref[...])
    o_ref[...] = z
    gate_ref[...] = _route(z, rw_ref[...], rb_ref[...])


def _outproj_ln(ins, w, x, g, b, router_w, router_b, merge=None):
    m, d = x.shape
    ka = ins[0].shape[1]
    tm = 256
    n_in = len(ins)
    return pl.pallas_call(
        functools.partial(_outproj_kernel, n_in=n_in, merge=merge),
        out_shape=[jax.ShapeDtypeStruct((m, d), F32), jax.ShapeDtypeStruct((m, N_EXPERTS), F32)],
        grid=(m // tm,),
        in_specs=[pl.BlockSpec((tm, ka), lambda i: (i, 0)) for _ in ins]
        + [pl.BlockSpec((ka, d), lambda i: (0, 0)),
           pl.BlockSpec((tm, d), lambda i: (i, 0)),
           pl.BlockSpec((1, d), lambda i: (0, 0)),
           pl.BlockSpec((1, d), lambda i: (0, 0)),
           pl.BlockSpec((N_EXPERTS, d), lambda i: (0, 0)),
           pl.BlockSpec((N_EXPERTS, 1), lambda i: (0, 0))],
        out_specs=[pl.BlockSpec((tm, d), lambda i: (i, 0)), pl.BlockSpec((tm, N_EXPERTS), lambda i: (i, 0))],
        compiler_params=_params(("parallel",)),
        name="outproj_ln",
    )(*ins, w, x, g.reshape(1, d), b.reshape(1, d), router_w.T, router_b.reshape(N_EXPERTS, 1))


def _bias_kernel(tbl_ref, o_ref, *, rows, cols, off0, tstride, cmul, lo, hi, mults, shift, transposed, scale):
    t = pl.program_id(0)
    if len(set(mults)) == 1:
        mult = mults[0]
    else:
        mult = jnp.int32(mults[-1])
        for idx in range(len(mults) - 2, -1, -1):
            mult = jnp.where(t == idx, jnp.int32(mults[idx]), mult)
    base = off0 + tstride * t

    def strip(i, carry):
        r0 = pl.multiple_of(i * 8, 8)
        a = lax.broadcasted_iota(jnp.int32, (8, cols), 0) + r0
        c = lax.broadcasted_iota(jnp.int32, (8, cols), 1)
        steps = base + c - cmul * a if transposed else base + a - cmul * c
        valid = (steps >= lo) & (steps <= hi)
        dist = steps * mult
        for h in range(N_HEADS):
            acc = jnp.full((8, cols), tbl_ref[0, h], F32)
            for k, thr in enumerate(BUCKET_THRESHOLDS):
                acc = jnp.where(dist >= thr, tbl_ref[k + 1, h], acc)
            if shift:
                acc = acc - tbl_ref[REL_BUCKETS - 1, h]
            if scale != 1.0:
                acc = acc * scale
            o_ref[h, 0, pl.ds(r0, 8), :] = jnp.where(valid, acc, NEG_INF)
        return carry

    lax.fori_loop(0, rows // 8, strip, 0)


def _bias_tiles(table, n_tiles, rows, cols, off0, tstride, cmul, lo, hi, mults, shift=False, transposed=False,
                scale=1.0):
    return pl.pallas_call(
        functools.partial(_bias_kernel, rows=rows, cols=cols, off0=off0, tstride=tstride, cmul=cmul,
                          lo=lo, hi=hi, mults=tuple(mults), shift=shift, transposed=transposed, scale=scale),
        out_shape=jax.ShapeDtypeStruct((N_HEADS, n_tiles, rows, cols), F32),
        grid=(n_tiles,),
        in_specs=[pl.BlockSpec(memory_space=pltpu.SMEM)],
        out_specs=pl.BlockSpec((N_HEADS, 1, rows, cols), lambda t: (0, t, 0, 0)),
        compiler_params=_params(("arbitrary",)),
        name="bias_tiles",
    )(table)


def _dil_kernel(q_ref, kp_ref, kc_ref, vp_ref, vc_ref, bias_ref, o_ref, lse_ref,
                q_scr, kp_scr, kc_scr, vp_scr, vc_scr, *, dilation, per_step):
    j = pl.program_id(2)
    d = dilation
    n_cur = per_step * DIL_BLOCK

    def rows_of(ref, r, n, lead=()):
        if d == 1:
            return ref[lead + (slice(0, n), slice(None))]
        return ref[lead + (pl.ds(r, n, stride=d), slice(None))]

    if d > 1:
        for h in range(2):
            q_scr[h] = q_ref[0, :, h * LANES:(h + 1) * LANES].astype(F32)
        kp_scr[...] = kp_ref[0].astype(F32)
        kc_scr[...] = kc_ref[0].astype(F32)
        vp_scr[...] = vp_ref[0].astype(F32)
        vc_scr[...] = vc_ref[0].astype(F32)
    colk = lax.broadcasted_iota(jnp.int32, (DIL_BLOCK, 2 * DIL_BLOCK), 1)
    first = jnp.where((colk < DIL_BLOCK) & (j == 0), NEG_INF, 0.0)
    low = lax.broadcasted_iota(jnp.int32, (DIL_BLOCK, LANES), 1) < HEAD_DIM
    for r in range(d):
        if d == 1:
            q_heads = [q_ref[0, :, h * LANES:(h + 1) * LANES] for h in range(2)]
            k_all = jnp.concatenate([kp_ref[0], kc_ref[0]], axis=0)
            v_all = jnp.concatenate([vp_ref[0], vc_ref[0]], axis=0)
        else:
            q_heads = [rows_of(q_scr, r, n_cur, (h,)).astype(BF16) for h in range(2)]
            k_all = jnp.concatenate([rows_of(kp_scr, r, DIL_BLOCK), rows_of(kc_scr, r, n_cur)], axis=0).astype(BF16)
            v_all = jnp.concatenate([rows_of(vp_scr, r, DIL_BLOCK), rows_of(vc_scr, r, n_cur)], axis=0).astype(BF16)
        for u in range(per_step):
            k2 = k_all[u * DIL_BLOCK:(u + 2) * DIL_BLOCK]
            v2 = v_all[u * DIL_BLOCK:(u + 2) * DIL_BLOCK]
            outs, lses = [], []
            for h in range(2):
                s = _dot_nt(q_heads[h][u * DIL_BLOCK:(u + 1) * DIL_BLOCK], k2) + bias_ref[h, 0]
                if u == 0:
                    s = s + first
                m = jnp.max(s, axis=-1, keepdims=True)
                p = jnp.exp(s - m)
                l = jnp.maximum(jnp.sum(p, axis=-1, keepdims=True), 1e-30)
                outs.append(_dot(p.astype(BF16), v2) / l)
                lses.append(m + jnp.log(l))
            if d == 1:
                dst = (0, slice(u * DIL_BLOCK, (u + 1) * DIL_BLOCK), slice(None))
            else:
                dst = (0, pl.ds(u * DIL_BLOCK * d + r, DIL_BLOCK, stride=d), slice(None))
            o_ref[dst] = jnp.where(low, outs[0], outs[1])
            lse_ref[dst] = jnp.where(low, lses[0], lses[1])


def _dilated_group(proj, bias, g, dilation, bsz, seq):
    prev_rows = DIL_BLOCK * dilation
    step_rows = math.gcd(seq, 16 * DIL_BLOCK)
    per_step = step_rows // prev_rows
    assert per_step >= 1 and seq % step_rows == 0
    q_blk = g * N_HEADS // 2
    k_blk = 3 * N_HEADS + g * 2 * PAIRS
    v_blk = k_blk + PAIRS

    def kv_specs(base):
        prev = pl.BlockSpec((1, prev_rows, LANES), lambda hp, b, j: (b, jnp.maximum(per_step * j - 1, 0), base + hp))
        cur = pl.BlockSpec((1, step_rows, LANES), lambda hp, b, j: (b, j, base + hp))
        return [prev, cur]

    out_spec = pl.BlockSpec((1, step_rows, LANES), lambda hp, b, j: (b, j, hp))
    staged = 8 if dilation == 1 else None
    o, lse = pl.pallas_call(
        functools.partial(_dil_kernel, dilation=dilation, per_step=per_step),
        out_shape=[jax.ShapeDtypeStruct((bsz, seq, D_MODEL), F32)] * 2,
        grid=(PAIRS, bsz, seq // step_rows),
        in_specs=[pl.BlockSpec((1, step_rows, 2 * LANES), lambda hp, b, j: (b, j, q_blk + hp))]
        + kv_specs(k_blk) + kv_specs(v_blk)
        + [pl.BlockSpec((2, 1, DIL_BLOCK, 2 * DIL_BLOCK), lambda hp, b, j: (hp, g, 0, 0))],
        out_specs=[out_spec, out_spec],
        scratch_shapes=[pltpu.VMEM((2, staged or step_rows, LANES), F32),
                        pltpu.VMEM((staged or prev_rows, LANES), F32), pltpu.VMEM((staged or step_rows, LANES), F32),
                        pltpu.VMEM((staged or prev_rows, LANES), F32), pltpu.VMEM((staged or step_rows, LANES), F32)],
        compiler_params=_params(("parallel", "parallel", "arbitrary")),
        name="dilated_attn",
    )(proj, proj, proj, proj, proj, bias)
    return o.reshape(bsz * seq, D_MODEL), lse.reshape(bsz * seq, D_MODEL)


def _pad_heads_cols(w, parity_of_head):
    k, n = w.shape
    nh = n // HEAD_DIM
    wh = w.reshape(k, nh, HEAD_DIM)
    z = jnp.zeros_like(wh)
    par = jnp.asarray([parity_of_head(h) for h in range(nh)], jnp.int32)[None, :, None]
    lo = jnp.where(par == 0, wh, z)
    hi = jnp.where(par == 1, wh, z)
    return jnp.concatenate([lo, hi], axis=-1).reshape(k, nh * LANES)


def _pad_heads_rows(w, parity_of_head):
    return _pad_heads_cols(w.T, parity_of_head).T


def _mixer_dilated(x, w_in, w_out, bias_dil, bsz, seq):
    aw = N_HEADS * HEAD_DIM
    par = lambda h: h % 2
    qs, kvs = [], []
    for g in range(len(DILATED_PAIRS)):
        base = g * 3 * aw
        qs.append(_pad_heads_cols(w_in[:, base:base + aw] * ATTN_SCALE, par))
        kvs.append(w_in[:, base + aw:base + 3 * aw])
    w_all = jnp.concatenate(qs + kvs, axis=1).astype(BF16)
    proj = _matmul(x, w_all, BF16).reshape(bsz, seq, -1)
    parts = [_dilated_group(proj, bias_dil, g, d, bsz, seq) for g, (_, d) in enumerate(DILATED_PAIRS)]
    return [p[0] for p in parts] + [p[1] for p in parts], w_out.astype(BF16)


def _sb_kernel(q_ref, k_ref, v_ref, o_ref):
    i = pl.program_id(2)
    t = SB_TILE
    row = lax.broadcasted_iota(jnp.int32, (t, t), 0)
    col = lax.broadcasted_iota(jnp.int32, (t, t), 1)
    suffix = jnp.where(row >= col, 1.0, 0.0).astype(BF16)
    strict = col < row
    q2 = q_ref[0]
    q_heads = [q2[:, h * LANES:(h + 1) * LANES] for h in range(2)]

    def block(qh, kj, masked):
        start = pl.multiple_of(kj * t, t)
        kb = k_ref[0, pl.ds(start, t), :]
        vb = v_ref[0, pl.ds(start, t), :]
        a = jnp.clip(_dot_nt(qh, kb), -SB_CLIP, SB_CLIP) * LOG2_E
        sp = jnp.log2(1.0 + jnp.exp2(a))
        if masked:
            sp = jnp.where(strict, sp, 0.0)
        hi, lo = _split_bf16(sp)
        rr = _dot(hi, suffix) + _dot(lo, suffix)
        att = jnp.exp2(a - rr)
        if masked:
            att = jnp.where(strict, att, 0.0)
        return _dot(att.astype(BF16), vb), rr[:, 0:1]

    def first_two(kj, carry):
        has_prev = jnp.where(kj > 0, 1.0, 0.0)
        out = []
        for qh in q_heads:
            acc_d, r_d = block(qh, kj, True)
            pv, r_p = block(qh, jnp.maximum(kj - 1, 0), False)
            out.append((acc_d + (has_prev * jnp.exp2(-r_d)) * pv, r_d + has_prev * r_p))
        return tuple(out)

    zero = (jnp.zeros((t, LANES), F32), jnp.zeros((t, 1), F32))
    (acc0, r0), (acc1, r1) = lax.fori_loop(i, i + 1, first_two, (zero, zero))

    def cond(carry):
        n, alive = carry[0], carry[1]
        return (n < i - 1) & alive

    def step(carry):
        n = carry[0]
        new, tops = [], []
        for qh, (r_sum, w, acc) in zip(q_heads, carry[2:]):
            pv, r_blk = block(qh, i - 2 - n, False)
            r_new = r_sum + r_blk
            w_new = jnp.exp2(-r_new)
            new.append((r_new, w_new, acc + w * pv))
            tops.append(jnp.max(w_new, axis=0, keepdims=True))
        alive = jnp.maximum(tops[0], tops[1])[0, 0] > 0.0
        return (n + 1, alive) + tuple(new)

    w0, w1 = jnp.exp2(-r0), jnp.exp2(-r1)
    alive0 = jnp.maximum(jnp.max(w0, axis=0, keepdims=True), jnp.max(w1, axis=0, keepdims=True))[0, 0] > 0.0
    init = (jnp.int32(0), alive0, (r0, w0, acc0), (r1, w1, acc1))
    res = lax.while_loop(cond, step, init)
    lane = lax.broadcasted_iota(jnp.int32, (t, LANES), 1)
    o_ref[0] = jnp.where(lane < HEAD_DIM, res[2][2], res[3][2]).astype(o_ref.dtype)


def _mixer_stick_breaking(x, w_in, w_out, bsz, seq):
    aw = N_HEADS * HEAD_DIM
    wq = _pad_heads_cols(w_in[:, :aw] * ATTN_SCALE, lambda h: h % 2)
    w_all = jnp.concatenate([wq, w_in[:, aw:]], axis=1).astype(BF16)
    proj = _matmul(x, w_all, BF16).reshape(bsz, seq, -1)
    kb = N_HEADS
    o = pl.pallas_call(
        _sb_kernel,
        out_shape=jax.ShapeDtypeStruct((bsz, seq, D_MODEL), BF16),
        grid=(bsz, PAIRS, seq // SB_TILE),
        in_specs=[pl.BlockSpec((1, SB_TILE, 2 * LANES), lambda b, hp, i: (b, i, hp)),
                  pl.BlockSpec((1, seq, LANES), lambda b, hp, i: (b, 0, kb + hp)),
                  pl.BlockSpec((1, seq, LANES), lambda b, hp, i: (b, 0, kb + PAIRS + hp))],
        out_specs=pl.BlockSpec((1, SB_TILE, LANES), lambda b, hp, i: (b, i, hp)),
        compiler_params=_params(("parallel", "parallel", "arbitrary")),
        name="stick_breaking_attn",
    )(proj, proj, proj)
    return [o.reshape(bsz * seq, D_MODEL)], w_out.astype(BF16)


def _flash_chunks(q_heads, k_ref, k_cols, v_ref, v_rows, bias_ref, mask_rows, lo, hi, i, blocks_per_chunk, n_tiles,
                  fold_mask=False, pipelined=False, far_width=1, near_width=1):
    nh = len(q_heads)

    def logits(ch, near, width):
        rows = width * TILE
        start = pl.multiple_of(ch * TILE, TILE)
        kbs = {c: k_ref[0, pl.ds(start, rows), c:c + LANES] for c in sorted(set(k_cols))}
        add = None
        if fold_mask:
            key = lax.broadcasted_iota(jnp.int32, (rows, LANES), 0)
            blk = lax.broadcasted_iota(jnp.int32, (rows, LANES), 1)
            blk_of_key = ch * blocks_per_chunk + key // (TILE // blocks_per_chunk)
            ind = jnp.where(blk == blk_of_key, NEG_INF, 0.0).astype(BF16)
            kbs = {c: jnp.concatenate([kb, ind], axis=1) for c, kb in kbs.items()}
        elif mask_rows is not None:
            nb = width * blocks_per_chunk
            per = TILE // blocks_per_chunk
            blk_rows = mask_rows(ch * blocks_per_chunk, nb)
            add = jnp.concatenate([jnp.broadcast_to(blk_rows[r:r + 1, :], (per, TILE)) for r in range(nb)], axis=0)
        out = []
        for h in range(nh):
            s = _dot_nt(kbs[k_cols[h]], q_heads[h])
            if near:
                tiles = [bias_ref[h, jnp.maximum(i - ch - w, 0)] for w in range(width)]
                s = s + (tiles[0] if width == 1 else jnp.concatenate(tiles, axis=0))
            if add is not None:
                s = s + add
            out.append(s)
        return tuple(out)

    def update(ch, s_heads, state, width):
        start = pl.multiple_of(ch * TILE, TILE)
        vts = {}
        new = []
        for h in range(nh):
            r0 = v_rows[h]
            if r0 not in vts:
                vts[r0] = v_ref[r0:r0 + LANES, pl.ds(start, width * TILE)]
            m, acc = state[h]
            s = s_heads[h]
            m_new = jnp.maximum(m, jnp.max(s, axis=0, keepdims=True))
            alpha = jnp.exp2(m - m_new)
            p = jnp.exp2(s - m_new)
            acc = alpha * acc + _dot(vts[r0], p.astype(BF16))
            new.append((m_new, acc))
        return tuple(new)

    def run(ch0, steps, near, width, state, pipelined=pipelined):
        if not pipelined:
            return lax.fori_loop(
                0, steps, lambda t, st: update(ch0 + t * width, logits(ch0 + t * width, near, width), st, width), state)

        def body(t, carry):
            s_cur, st = carry
            s_next = logits(ch0 + jnp.minimum(t + 1, steps - 1) * width, near, width)
            return s_next, update(ch0 + t * width, s_cur, st, width)

        zeros = tuple(jnp.zeros((width * TILE, TILE), F32) for _ in range(nh))
        first = jnp.minimum(ch0, i)
        s0 = lax.fori_loop(first, first + 1, lambda c, _: logits(c, near, width), zeros)
        return lax.fori_loop(0, steps, body, (s0, state))[1]

    state = tuple((jnp.full((1, TILE), NEG_INF, F32), jnp.zeros((LANES, TILE), F32)) for _ in range(nh))
    n_near = n_tiles - (far_width - 1)
    far_steps = jnp.maximum(hi - n_near - lo, 0) // far_width
    split = lo + far_steps * far_width
    state = run(lo, far_steps, False, far_width, state)
    start, width = split, near_width
    while width >= 1:
        steps = (hi - start) // width
        state = run(start, steps, True, width, state, pipelined=pipelined and near_width == 1)
        start = start + steps * width
        width = width // 2 if width > 1 else 0
    outs = []
    for _, acc in state:
        o = (acc / acc[HEAD_DIM:HEAD_DIM + 1, :]).T
        lane = lax.broadcasted_iota(jnp.int32, o.shape, 1)
        outs.append(jnp.where(lane < HEAD_DIM, o, 0.0))
    return outs


def _kmean_kernel(k_ref, o_ref):
    rows = k_ref.shape[1]
    kf = k_ref[0].astype(F32).reshape(rows // MOBA_BLOCK, MOBA_BLOCK, k_ref.shape[2])
    o_ref[0] = jnp.sum(kf, axis=1) * (1.0 / MOBA_BLOCK)


def _moba_kernel(q_ref, k_ref, v_ref, km_ref, bias_ref, o_ref):
    i = pl.program_id(2)
    nblk = km_ref.shape[1]
    km_hi, km_lo = _split_bf16(km_ref[0])
    qs = q_ref[0]
    blk = lax.broadcasted_iota(jnp.int32, (nblk, TILE), 0)
    past = blk < i
    q_heads = []
    for h in range(MOBA_HEADS):
        qh = qs[:, h * LANES:(h + 1) * LANES]
        pair = slice((h // 2) * LANES, (h // 2 + 1) * LANES)
        gate = jnp.where(past, _dot_nt(km_hi[:, pair], qh) + _dot_nt(km_lo[:, pair], qh), NEG_INF)
        allowed = blk == i
        for _ in range(MOBA_TOP_K):
            mx = jnp.max(gate, axis=0, keepdims=True)
            first = jnp.min(jnp.where(gate == mx, blk, nblk), axis=0, keepdims=True)
            pick = blk == first
            allowed = allowed | (pick & past)
            gate = jnp.where(pick, -jnp.inf, gate)
        nsel = jnp.where(allowed, 0.0, 1.0).T.astype(BF16)
        q_heads.append(jnp.concatenate([qh, nsel], axis=1))
    outs = _flash_chunks(q_heads, k_ref, [(h // 2) * LANES for h in range(MOBA_HEADS)],
                         v_ref, [h * LANES for h in range(MOBA_HEADS)], bias_ref, None, 0, i + 1, i, 1,
                         BIASED_TILES + MOBA_FAR_WIDTH - 1, fold_mask=True, pipelined=False, far_width=MOBA_FAR_WIDTH,
                         near_width=4)
    for h in range(MOBA_HEADS):
        o_ref[0, :, h * LANES:(h + 1) * LANES] = outs[h].astype(o_ref.dtype)


def _mixer_moba(x, w_in, w_out, bias_main, bsz, seq):
    aw = N_HEADS * HEAD_DIM
    nblk = seq // MOBA_BLOCK
    assert nblk - 1 >= MOBA_TOP_K
    wq = _pad_heads_cols(w_in[:, :aw] * (ATTN_SCALE * LOG2_E), lambda h: h % 2)
    w_all = jnp.concatenate([wq, w_in[:, aw:2 * aw]], axis=1).astype(BF16)
    proj = _matmul(x, w_all, BF16).reshape(bsz, seq, -1)
    v_t = _matmul_t(x, _pad_heads_cols(w_in[:, 2 * aw:], lambda h: 0).T.astype(BF16))
    kb = N_HEADS
    rows = 8 * MOBA_BLOCK
    kmean = pl.pallas_call(
        _kmean_kernel,
        out_shape=jax.ShapeDtypeStruct((bsz, nblk, D_MODEL), F32),
        grid=(bsz, seq // rows),
        in_specs=[pl.BlockSpec((1, rows, D_MODEL), lambda b, i: (b, i, kb * LANES // D_MODEL))],
        out_specs=pl.BlockSpec((1, 8, D_MODEL), lambda b, i: (b, i, 0)),
        compiler_params=_params(("parallel", "parallel")),
        name="moba_kmean",
    )(proj)
    assert nblk <= LANES
    kmean = jnp.pad(kmean, ((0, 0), (0, LANES - nblk), (0, 0)))
    nh = MOBA_HEADS
    kw = nh // 2 * LANES
    once = pl.Buffered(1)
    o = pl.pallas_call(
        _moba_kernel,
        out_shape=jax.ShapeDtypeStruct((bsz, seq, N_HEADS * LANES), BF16),
        grid=(bsz, N_HEADS // nh, seq // TILE),
        in_specs=[pl.BlockSpec((1, TILE, nh * LANES), lambda b, hq, i: (b, i, hq)),
                  pl.BlockSpec((1, seq, kw), lambda b, hq, i: (b, 0, kb * LANES // kw + hq), pipeline_mode=once),
                  pl.BlockSpec((nh * LANES, seq), lambda b, hq, i: (hq, b), pipeline_mode=once),
                  pl.BlockSpec((1, LANES, kw), lambda b, hq, i: (b, 0, hq)),
                  pl.BlockSpec((nh, BIASED_TILES + MOBA_FAR_WIDTH - 1, TILE, TILE), lambda b, hq, i: (hq, 0, 0, 0),
                               pipeline_mode=once)],
        out_specs=pl.BlockSpec((1, TILE, nh * LANES), lambda b, hq, i: (b, i, hq)),
        compiler_params=_params(("parallel", "parallel", "arbitrary")),
        name="moba_attn",
    )(proj, proj, v_t, kmean, bias_main)
    return [o.reshape(bsz * seq, N_HEADS * LANES)], _pad_heads_rows(w_out, lambda h: 0).astype(BF16)


def _gelu_tanh(x):
    return 0.5 * x * (1.0 + jnp.tanh(math.sqrt(2.0 / math.pi) * (x + 0.044715 * (x * x * x))))


def _compress_kernel(a_ref, pos_ref, w1_ref, w2_ref, o_ref, *, transpose_out):
    nc = a_ref.shape[2]
    half = NSA_CMP_STRIDE * HEAD_DIM
    a = a_ref[0, 0].astype(F32)
    top = _dot((a + pos_ref[0:1, :]).astype(BF16), w1_ref[0:half, :])
    bot = _dot((a + pos_ref[1:2, :]).astype(BF16), w1_ref[half:2 * half, :])
    hid = top + pltpu.roll(bot, nc - 1, 0)
    out = _dot(_gelu_tanh(hid).astype(BF16), w2_ref[0])
    rowi = lax.broadcasted_iota(jnp.int32, out.shape, 0)
    out = jnp.where(rowi < nc - 1, out, 0.0)
    o_ref[0, 0] = (out.T if transpose_out else out).astype(o_ref.dtype)


def _compress(tok, first, pos, w1, w2, bsz, seq, by_parity):
    nc = seq // NSA_CMP_STRIDE
    half = NSA_CMP_STRIDE * HEAD_DIM
    pos2 = pos.reshape(2, half)
    z = jnp.zeros_like(w2)
    low = jnp.concatenate([w2, z], axis=1)
    w2p = jnp.stack([low, jnp.concatenate([z, w2], axis=1) if by_parity else low]).astype(BF16)
    out_dims = (nc, LANES) if by_parity else (LANES, nc)
    return pl.pallas_call(
        functools.partial(_compress_kernel, transpose_out=not by_parity),
        out_shape=jax.ShapeDtypeStruct((bsz, NSA_KV_HEADS) + out_dims, BF16),
        grid=(bsz, NSA_KV_HEADS),
        in_specs=[pl.BlockSpec((1, 1, nc, half), lambda b, g: (b, first + g, 0, 0)),
                  pl.BlockSpec((2, half), lambda b, g: (0, 0)),
                  pl.BlockSpec((2 * half, w1.shape[1]), lambda b, g: (0, 0)),
                  pl.BlockSpec((1, w2.shape[0], LANES), lambda b, g: (g % 2, 0, 0))],
        out_specs=pl.BlockSpec((1, 1) + out_dims, lambda b, g: (b, g, 0, 0)),
        compiler_params=_params(("parallel", "parallel")),
        name="nsa_compress",
    )(tok, pos2, w1.astype(BF16), w2p)


def _gate_columns(gate_ref, branch):
    tile = gate_ref[...]
    lane = lax.broadcasted_iota(jnp.int32, tile.shape, 1)
    base = branch * N_HEADS + pl.program_id(1) * NSA_GROUP
    return [jnp.sum(jnp.where(lane == base + r, tile, 0.0), axis=-1, keepdims=True) for r in range(NSA_GROUP)]


def _nsa_cmp_kernel(q_ref, kc_ref, vc_ref, bias_ref, gate_ref, o_ref, sel_ref, *, n_top):
    i = pl.program_id(2)
    nc = kc_ref.shape[2]
    n_chunks = nc // CMP_CHUNK
    nsb = sel_ref.shape[2]
    q4 = q_ref[0]
    gates = _gate_columns(gate_ref, 0)
    keep = lax.broadcasted_iota(jnp.int32, (TILE, LANES), 1) < HEAD_DIM
    psum = [jnp.zeros((CMP_CHUNK, TILE), F32) for _ in range(n_chunks)]
    for r in range(NSA_GROUP):
        qh = q4[:, r * LANES:(r + 1) * LANES]
        ss = []
        for c in range(n_chunks):
            tile = jnp.clip(i - (CMP_CHUNK * NSA_CMP_STRIDE // TILE) * c, -1, CMP_TILES - 2) + 1
            ss.append(_dot_nt(kc_ref[0, 0, c * CMP_CHUNK:(c + 1) * CMP_CHUNK, :], qh) + bias_ref[r, tile])
        m = ss[0].max(axis=0, keepdims=True)
        for c in range(1, n_chunks):
            m = jnp.maximum(m, ss[c].max(axis=0, keepdims=True))
        ps = [jnp.exp2(s - m) for s in ss]
        l = ps[0].sum(axis=0, keepdims=True)
        for c in range(1, n_chunks):
            l = l + ps[c].sum(axis=0, keepdims=True)
        inv = jnp.where(m > 0.5 * NEG_INF, 1.0 / jnp.maximum(l, 1e-30), 0.0)
        acc = jnp.zeros((LANES, TILE), F32)
        for c in range(n_chunks):
            pc = ps[c] * inv
            psum[c] = psum[c] + pc
            acc = acc + _dot(vc_ref[0, 0, :, c * CMP_CHUNK:(c + 1) * CMP_CHUNK], pc.astype(BF16))
        o_ref[0, :, r * LANES:(r + 1) * LANES] = jnp.where(keep, acc.T * gates[r], 0.0).astype(o_ref.dtype)
    imp = jnp.zeros((nsb, TILE), F32)
    per_sel = NSA_SEL_BLOCK // NSA_CMP_STRIDE
    for c in range(n_chunks):
        jb = lax.broadcasted_iota(jnp.int32, (nsb, CMP_CHUNK), 0)
        ci = lax.broadcasted_iota(jnp.int32, (nsb, CMP_CHUNK), 1) + c * CMP_CHUNK
        rel = ci - per_sel * jb
        over = (rel >= 1 - NSA_CMP_LEN // NSA_CMP_STRIDE) & (rel < per_sel) & (ci < nc - 1)
        ov = jnp.where(over, 1.0, 0.0).astype(BF16)
        hi, lo = _split_bf16(psum[c])
        imp = imp + _dot(ov, hi) + _dot(ov, lo)
    jb = lax.broadcasted_iota(jnp.int32, (nsb, TILE), 0)
    qpos = lax.broadcasted_iota(jnp.int32, (nsb, TILE), 1) + i * TILE
    cur = qpos // NSA_SEL_BLOCK
    forced = (jb == 0) | (jb == cur) | (jb == cur - 1)
    causal = jb * NSA_SEL_BLOCK <= qpos
    score = jnp.where(forced, FORCED_SCORE, jnp.where(causal, imp, NEG_INF))
    chosen = jnp.zeros((nsb, TILE), F32)
    for _ in range(n_top):
        mx = jnp.max(score, axis=0, keepdims=True)
        first = jnp.min(jnp.where(score == mx, jb, nsb), axis=0, keepdims=True)
        pick = jb == first
        chosen = jnp.where(pick, 1.0, chosen)
        score = jnp.where(pick, -jnp.inf, score)
    sel_ref[0, 0] = jnp.where((chosen > 0.0) & causal, 0.0, NEG_INF)


def _nsa_sparse_kernel(q_ref, k_ref, v_ref, bias_ref, gate_ref, *rest, window):
    if window:
        (o_ref,) = rest
        mask_rows = None
    else:
        sel_ref, o_ref = rest

        def mask_rows(b0, n):
            if n % 8 == 0:
                return sel_ref[0, 0, pl.ds(pl.multiple_of(b0, 8), n), :]
            assert n == 4
            rows = sel_ref[0, 0, pl.ds(pl.multiple_of(b0 // 8 * 8, 8), 8), :]
            return jnp.where(b0 % 8 == 0, rows[0:4], rows[4:8])

    i = pl.program_id(2)
    q4 = q_ref[0]
    gates = _gate_columns(gate_ref, 2 if window else 1)
    q_heads = [q4[:, r * LANES:(r + 1) * LANES] for r in range(NSA_GROUP)]
    if window:
        n_tiles = NSA_WINDOW // TILE + 1
        lo = jnp.maximum(i - (n_tiles - 1), 0)
        outs = _flash_chunks(q_heads, k_ref, [0] * NSA_GROUP, v_ref, [0] * NSA_GROUP, bias_ref, None,
                             lo, i + 1, i, 1, n_tiles, near_width=n_tiles)
    else:
        outs = _flash_chunks(q_heads, k_ref, [0] * NSA_GROUP, v_ref, [0] * NSA_GROUP, bias_ref, mask_rows,
                             0, i + 1, i,
                             TILE // NSA_SEL_BLOCK, BIASED_TILES + NSA_FAR_WIDTH - 1, far_width=NSA_FAR_WIDTH,
                             near_width=4)
    for r in range(NSA_GROUP):
        o_ref[0, :, r * LANES:(r + 1) * LANES] = (outs[r] * gates[r]).astype(o_ref.dtype)


def _mixer_nsa(x, w_in, w_out, table_bias, pos_k, w1_k, w2_k, pos_v, w1_v, w2_v, bsz, seq):
    bias_main, bias_win, bias_cmp = table_bias
    aw = N_HEADS * HEAD_DIM
    kvw = NSA_KV_HEADS * HEAD_DIM
    G, R = NSA_KV_HEADS, NSA_GROUP
    nc = seq // NSA_CMP_STRIDE
    nsb = seq // NSA_SEL_BLOCK
    n_top = min(NSA_TOP_N, nsb)
    nq = seq // TILE
    assert nc % CMP_CHUNK == 0
    par = lambda h: (h // R) % 2
    wq = _pad_heads_cols(w_in[:, :aw] * (ATTN_SCALE * LOG2_E), par).astype(BF16)
    w_cmp = w_in[:, aw:aw + 2 * kvw].astype(BF16)
    c0 = aw + 2 * kvw
    wk = jnp.concatenate([w_in[:, c0:c0 + kvw], w_in[:, c0 + 2 * kvw:c0 + 3 * kvw]], axis=1).astype(BF16)
    wv = jnp.concatenate([w_in[:, c0 + kvw:c0 + 2 * kvw], w_in[:, c0 + 3 * kvw:c0 + 4 * kvw]], axis=1)
    wv_t = _pad_heads_cols(wv, lambda h: 0).T.astype(BF16)
    wg = jnp.pad(w_in[:, aw + 6 * kvw:], ((0, 0), (0, LANES - 3 * N_HEADS))).astype(BF16)
    q = _matmul(x, wq, BF16).reshape(bsz, seq, N_HEADS * LANES)
    kk = _matmul(x, wk, BF16).reshape(bsz, seq, 2 * kvw)
    v_t = _matmul_t(x, wv_t)
    gates = _matmul(x, wg, F32, act='sigmoid')
    tok = _matmul_heads(x, w_cmp, bsz, seq).reshape(bsz, 2 * G, nc, NSA_CMP_STRIDE * HEAD_DIM)
    kc = _compress(tok, 0, pos_k, w1_k, w2_k, bsz, seq, True)
    vc = _compress(tok, G, pos_v, w1_v, w2_v, bsz, seq, False)

    q_spec = pl.BlockSpec((1, TILE, R * LANES), lambda b, g, i: (b, i, g))
    o_spec = pl.BlockSpec((1, TILE, R * LANES), lambda b, g, i: (b, i, g))
    gate_spec = pl.BlockSpec((TILE, LANES), lambda b, g, i: (b * nq + i, 0))
    o_shape = jax.ShapeDtypeStruct((bsz, seq, N_HEADS * LANES), BF16)
    sem = ("parallel", "parallel", "arbitrary")

    o_cmp, nsel = pl.pallas_call(
        functools.partial(_nsa_cmp_kernel, n_top=n_top),
        out_shape=[o_shape, jax.ShapeDtypeStruct((bsz, G, nsb, seq), F32)],
        grid=(bsz, G, nq),
        in_specs=[q_spec,
                  pl.BlockSpec((1, 1, nc, LANES), lambda b, g, i: (b, g, 0, 0)),
                  pl.BlockSpec((1, 1, LANES, nc), lambda b, g, i: (b, g, 0, 0)),
                  pl.BlockSpec((R, CMP_TILES, CMP_CHUNK, TILE), lambda b, g, i: (g, 0, 0, 0)),
                  gate_spec],
        out_specs=[o_spec, pl.BlockSpec((1, 1, nsb, TILE), lambda b, g, i: (b, g, 0, i))],
        compiler_params=_params(sem),
        name="nsa_compressed_attn",
    )(q, kc, vc, bias_cmp, gates)

    once = pl.Buffered(1)

    def k_spec(first_blk):
        return pl.BlockSpec((1, seq, LANES), lambda b, g, i: (b, 0, first_blk + g // 2), pipeline_mode=once)

    def vt_spec(first_blk):
        return pl.BlockSpec((LANES, seq), lambda b, g, i: (first_blk + g, b), pipeline_mode=once)

    o_sel = pl.pallas_call(
        functools.partial(_nsa_sparse_kernel, window=False),
        out_shape=o_shape,
        grid=(bsz, G, nq),
        in_specs=[q_spec, k_spec(0), vt_spec(0),
                  pl.BlockSpec((R, BIASED_TILES + NSA_FAR_WIDTH - 1, TILE, TILE), lambda b, g, i: (g, 0, 0, 0),
                               pipeline_mode=once),
                  gate_spec,
                  pl.BlockSpec((1, 1, nsb, TILE), lambda b, g, i: (b, g, 0, i))],
        out_specs=o_spec,
        compiler_params=_params(sem),
        name="nsa_selected_attn",
    )(q, kk, v_t, bias_main, gates, nsel)

    o_win = pl.pallas_call(
        functools.partial(_nsa_sparse_kernel, window=True),
        out_shape=o_shape,
        grid=(bsz, G, nq),
        in_specs=[q_spec, k_spec(2), vt_spec(G),
                  pl.BlockSpec((R, NSA_WINDOW // TILE + 1, TILE, TILE), lambda b, g, i: (g, 0, 0, 0)),
                  gate_spec],
        out_specs=o_spec,
        compiler_params=_params(sem),
        name="nsa_window_attn",
    )(q, kk, v_t, bias_win, gates)

    m = bsz * seq
    parts = [t.reshape(m, N_HEADS * LANES) for t in (o_cmp, o_sel, o_win)]
    return parts, _pad_heads_rows(w_out, lambda h: 0).astype(BF16)


def _route(x, w_t, bias):
    logits = lax.dot_general(w_t, x, (((1,), (1,)), ((), ())),
                             preferred_element_type=F32, precision=lax.Precision.HIGHEST)
    scores = 1.0 / (1.0 + jnp.exp(-logits))
    biased = scores + bias
    rows = [biased[e:e + 1, :] for e in range(N_EXPERTS)]
    group_score = []
    for g in range(N_GROUPS):
        r = rows[g * EXPERTS_PER_GROUP:(g + 1) * EXPERTS_PER_GROUP]
        best = None
        for a in range(EXPERTS_PER_GROUP):
            for c in range(a + 1, EXPERTS_PER_GROUP):
                pair = r[a] + r[c]
                best = pair if best is None else jnp.maximum(best, pair)
        group_score.append(best)
    best_val = group_score[0]
    best_grp = jnp.zeros_like(best_val, dtype=jnp.int32)
    for g in range(1, N_GROUPS):
        better = group_score[g] > best_val
        best_val = jnp.where(better, group_score[g], best_val)
        best_grp = jnp.where(better, g, best_grp)
    picked = []
    for e in range(N_EXPERTS):
        g, a = divmod(e, EXPERTS_PER_GROUP)
        rank = jnp.zeros_like(best_grp)
        for c in range(EXPERTS_PER_GROUP):
            if c == a:
                continue
            other = rows[g * EXPERTS_PER_GROUP + c]
            ahead = (other > rows[e]) | ((other == rows[e]) & (c < a))
            rank = rank + jnp.where(ahead, 1, 0)
        picked.append((best_grp == g) & (rank < 2))
    raw = [jnp.where(picked[e], scores[e:e + 1, :], 0.0) for e in range(N_EXPERTS)]
    total = raw[0]
    for e in range(1, N_EXPERTS):
        total = total + raw[e]
    return (jnp.concatenate(raw, axis=0) / total).T


def _moe_kernel(x_ref, gate_ref, wg_ref, wu_ref, wd_ref, g_ref, b_ref, o_ref, xb_ref, acc_ref):
    e = pl.program_id(1)

    @pl.when(e == 0)
    def _():
        xb_ref[...] = x_ref[...].astype(BF16)
        acc_ref[...] = jnp.zeros_like(acc_ref)

    xb = xb_ref[...]
    lane = lax.broadcasted_iota(jnp.int32, gate_ref.shape, 1)
    gcol = jnp.sum(jnp.where(lane == e, gate_ref[...], 0.0), axis=-1, keepdims=True)
    a = _dot(xb, wg_ref[0])
    h = a / (1.0 + jnp.exp(-a)) * _dot(xb, wu_ref[0])
    acc_ref[...] += gcol * _dot(h.astype(BF16), wd_ref[0])

    @pl.when(e == N_EXPERTS - 1)
    def _():
        o_ref[...] = _layer_norm(DEEPNORM_ALPHA * x_ref[...] + acc_ref[...], g_ref[...], b_ref[...])


def _moe_ln(x, gates, w_gate, w_up, w_down, g, b):
    m, d = x.shape
    tm = 1024
    de = w_gate.shape[-1]
    return pl.pallas_call(
        _moe_kernel,
        out_shape=jax.ShapeDtypeStruct((m, d), F32),
        grid=(m // tm, N_EXPERTS),
        in_specs=[pl.BlockSpec((tm, d), lambda i, e: (i, 0)),
                  pl.BlockSpec((tm, N_EXPERTS), lambda i, e: (i, 0)),
                  pl.BlockSpec((1, d, de), lambda i, e: (e, 0, 0)),
                  pl.BlockSpec((1, d, de), lambda i, e: (e, 0, 0)),
                  pl.BlockSpec((1, de, d), lambda i, e: (e, 0, 0)),
                  pl.BlockSpec((1, d), lambda i, e: (0, 0)),
                  pl.BlockSpec((1, d), lambda i, e: (0, 0))],
        out_specs=pl.BlockSpec((tm, d), lambda i, e: (i, 0)),
        scratch_shapes=[pltpu.VMEM((tm, d), BF16), pltpu.VMEM((tm, d), F32)],
        compiler_params=_params(("parallel", "arbitrary")),
        name="moe_experts_ln",
    )(x, gates, w_gate.astype(BF16), w_up.astype(BF16), w_down.astype(BF16), g.reshape(1, d), b.reshape(1, d))


def kernel(x, rel_table, router_w, router_b, ln1_g, ln1_b, ln2_g, ln2_b, exp_w_gate, exp_w_up, exp_w_down, dil_w_in, dil_w_out, sb_w_in, sb_w_out, nsa_w_in, nsa_w_out, nsa_cmp_pos_k, nsa_cmp_w1_k, nsa_cmp_w2_k, nsa_cmp_pos_v, nsa_cmp_w1_v, nsa_cmp_w2_v, moba_w_in, moba_w_out):
    bsz, seq, d = x.shape
    assert d == D_MODEL and seq % (DILATED_PAIRS[-1][1] * DIL_BLOCK) == 0
    depth = ln1_g.shape[0]
    assert depth == DEPTH
    n_mixers = 4
    table = rel_table.astype(F32)
    span = DIL_BLOCK
    big = 1 << 30
    bias_dil = _bias_tiles(table, len(DILATED_PAIRS), DIL_BLOCK, 2 * DIL_BLOCK, DIL_BLOCK, 0, 1, 0, span,
                           [dl for _, dl in DILATED_PAIRS])
    assert BIASED_TILES * TILE - (TILE - 1) >= BUCKET_THRESHOLDS[-1]
    bias_main = _bias_tiles(table, MAIN_TILES, TILE, TILE, 0, TILE, 1, 0, big, [1], shift=True, transposed=True,
                            scale=LOG2_E)
    bias_win = _bias_tiles(table, NSA_WINDOW // TILE + 1, TILE, TILE, 0, TILE, 1, 0, NSA_WINDOW - 1, [1],
                           transposed=True, scale=LOG2_E)
    bias_cmp = _bias_tiles(table, CMP_TILES, CMP_CHUNK, TILE, -TILE - (NSA_CMP_LEN - 1), TILE, NSA_CMP_STRIDE,
                           0, big, [1], transposed=True, scale=LOG2_E)
    xf = x.reshape(bsz * seq, d)
    for layer in range(depth):
        kind, occ = layer % n_mixers, layer // n_mixers
        merge = None
        if kind == 0:
            parts, w_out = _mixer_dilated(xf, dil_w_in[occ], dil_w_out[occ], bias_dil, bsz, seq)
            merge = 'lse3'
        elif kind == 1:
            parts, w_out = _mixer_stick_breaking(xf, sb_w_in[occ], sb_w_out[occ], bsz, seq)
        elif kind == 2:
            parts, w_out = _mixer_nsa(xf, nsa_w_in[occ], nsa_w_out[occ], (bias_main, bias_win, bias_cmp),
                                      nsa_cmp_pos_k[occ], nsa_cmp_w1_k[occ], nsa_cmp_w2_k[occ],
                                      nsa_cmp_pos_v[occ], nsa_cmp_w1_v[occ], nsa_cmp_w2_v[occ], bsz, seq)
            merge = 'sum3'
        else:
            parts, w_out = _mixer_moba(xf, moba_w_in[occ], moba_w_out[occ], bias_main, bsz, seq)
        xf, gates = _outproj_ln(parts, w_out, xf, ln1_g[layer], ln1_b[layer], router_w, router_b, merge=merge)
        xf = _moe_ln(xf, gates, exp_w_gate[layer], exp_w_up[layer], exp_w_down[layer],
                     ln2_g[layer], ln2_b[layer])
    return xf.reshape(bsz, seq, d)
```
